```python
import jax
import jax.numpy as jnp
from jax import lax
import numpy as np

D_MODEL = 2048
BATCH = 16
SEQ = 256
DEPTH = 2
DEC_BATCH = 2
DEC_SEQ = 1024
PAST_LEN = 256

GRID_W = 64
HEAD_DIM = 128
N_HEADS_A = D_MODEL // (2 * HEAD_DIM)
N_HEADS_B = D_MODEL // (2 * HEAD_DIM)
N_KV_B = max(1, N_HEADS_B // 4)
G_B = N_HEADS_B // N_KV_B
NA_ROWS = 8
NA_COLS = 16
WIN_B = 128
BLK = 128
D_RNN = D_MODEL
N_RG_BLOCKS = 16
RG_BLOCK = D_RNN // N_RG_BLOCKS
CONV_W = 4
CONV_PAD_L = 2
RG_C = 8.0
D_FF = 4 * D_MODEL
ROPE_BASE = 10000.0
EPS = 1e-6
NEG = -1e30
N_ATT_LAYERS = (DEPTH + 1) // 2
N_REC_LAYERS = DEPTH // 2
QA_W = N_HEADS_A * HEAD_DIM
QB_W = N_HEADS_B * HEAD_DIM
KVB_W = N_KV_B * HEAD_DIM
D_ATT_IN = 3 * QA_W + QB_W + 2 * KVB_W
D_ATT_OUT = QA_W + QB_W

kernel_name = "hybrid_diffusion_na_swa_rglru_step"


def rmsnorm(x, g):
    xf = x.astype(jnp.float32)
    y = xf * lax.rsqrt(jnp.mean(xf * xf, axis=-1, keepdims=True) + EPS)
    return (y * g.astype(jnp.float32)).astype(x.dtype)


def adaln(cond, w, b):
    m = jax.nn.silu(cond) @ w + b
    return [t[:, None, :] for t in jnp.split(m, 6, axis=-1)]


def _rope_axis(x, pos):
    d = x.shape[-1]
    inv = ROPE_BASE ** (-jnp.arange(0, d, 2, dtype=jnp.float32) / d)
    ang = pos.astype(jnp.float32)[:, None] * inv[None, :]
    cos = jnp.cos(ang)[None, :, None, :]
    sin = jnp.sin(ang)[None, :, None, :]
    x1, x2 = jnp.split(x.astype(jnp.float32), 2, axis=-1)
    return jnp.concatenate([x1 * cos - x2 * sin, x2 * cos + x1 * sin], axis=-1).astype(x.dtype)


def rope_2d(x):
    t = jnp.arange(x.shape[1])
    half = x.shape[-1] // 2
    return jnp.concatenate([_rope_axis(x[..., :half], t // GRID_W),
                            _rope_axis(x[..., half:], t % GRID_W)], axis=-1)


def dense_ctx_attention(q, k, v, sink):
    s = jnp.einsum('bqhgd,bkhd->bhgqk', q, k).astype(jnp.float32) * (HEAD_DIM ** -0.5)
    if sink is not None:
        B, H, G, Lq, _ = s.shape
        s_sink = jnp.broadcast_to(sink.astype(jnp.float32)[None, :, :, None, None], (B, H, G, Lq, 1))
        p = jax.nn.softmax(jnp.concatenate([s_sink, s], axis=-1), axis=-1)[..., 1:]
    else:
        p = jax.nn.softmax(s, axis=-1)
    return jnp.einsum('bhgqk,bkhd->bqhgd', p.astype(v.dtype), v)


def neighbourhood_attention(q, k, v, kc, vc, rpb):
    B, T, H, D = q.shape
    R = T // GRID_W
    kh = min(NA_ROWS, R)
    r = jnp.arange(R)
    row_idx = jnp.clip(r - kh // 2, 0, R - kh)[:, None] + jnp.arange(kh)[None, :]
    j = jnp.arange(GRID_W)
    start_c = jnp.clip(j - NA_COLS // 2, 0, GRID_W - NA_COLS)
    qg = q.reshape(B, R, GRID_W, H, D)
    kb = k.reshape(B, R, GRID_W, H, D)[:, row_idx]
    vb = v.reshape(B, R, GRID_W, H, D)[:, row_idx]
    s = jnp.einsum('brqhd,brikhd->bhrqik', qg, kb).astype(jnp.float32) * (HEAD_DIM ** -0.5)
    dr = row_idx - r[:, None] + (NA_ROWS - 1)
    dc = jnp.clip(j[None, :] - j[:, None] + (NA_COLS - 1), 0, 2 * NA_COLS - 2)
    in_win = (j[None, :] >= start_c[:, None]) & (j[None, :] < start_c[:, None] + NA_COLS)
    bias = rpb[:, dr[:, None, :, None], dc[None, :, None, :]].astype(jnp.float32)
    s = jnp.where(in_win[None, None, None, :, None, :], s + bias[None], NEG)
    sc = jnp.einsum('brqhd,bchd->bhrqc', qg, kc).astype(jnp.float32) * (HEAD_DIM ** -0.5)
    nk = kh * GRID_W
    p = jax.nn.softmax(jnp.concatenate([s.reshape(B, H, R, GRID_W, nk), sc], axis=-1), axis=-1)
    p_nb = p[..., :nk].reshape(B, H, R, GRID_W, kh, GRID_W).astype(v.dtype)
    p_c = p[..., nk:].astype(v.dtype)
    o = jnp.einsum('bhrqik,brikhd->brqhd', p_nb, vb) + jnp.einsum('bhrqc,bchd->brqhd', p_c, vc)
    return o.reshape(B, T, H, D)


def window_attention(q, k, v, kc, vc, sink):
    B, T, Hkv, G, D = q.shape
    nb = T // BLK
    qb = q.reshape(B, nb, BLK, Hkv, G, D)
    pad = jnp.zeros((B, BLK, Hkv, D), k.dtype)
    kp = jnp.concatenate([pad, k, pad], axis=1)
    vp = jnp.concatenate([pad, v, pad], axis=1)
    idx = jnp.arange(nb)[:, None] * BLK + jnp.arange(3 * BLK)[None, :]
    kb = kp[:, idx]
    vb = vp[:, idx]
    qpos = jnp.arange(nb)[:, None] * BLK + jnp.arange(BLK)[None, :]
    kpos = idx - BLK
    valid = ((jnp.abs(qpos[:, :, None] - kpos[:, None, :]) <= WIN_B)
             & (kpos >= 0)[:, None, :] & (kpos < T)[:, None, :])
    s = jnp.einsum('bnqhgd,bnkhd->bhgnqk', qb, kb).astype(jnp.float32) * (HEAD_DIM ** -0.5)
    s = jnp.where(valid[None, None, None], s, NEG)
    sc = jnp.einsum('bnqhgd,bchd->bhgnqc', qb, kc).astype(jnp.float32) * (HEAD_DIM ** -0.5)
    s_sink = jnp.broadcast_to(sink.astype(jnp.float32)[None, :, :, None, None, None], (B, Hkv, G, nb, BLK, 1))
    p = jax.nn.softmax(jnp.concatenate([s_sink, s, sc], axis=-1), axis=-1)
    p_band = p[..., 1:1 + 3 * BLK].astype(v.dtype)
    p_ctx = p[..., 1 + 3 * BLK:].astype(v.dtype)
    o = jnp.einsum('bhgnqk,bnkhd->bnqhgd', p_band, vb) + jnp.einsum('bhgnqc,bchd->bnqhgd', p_ctx, vc)
    return o.reshape(B, T, Hkv * G, D)


def att_project(h, w_in):
    B, L, _ = h.shape
    cuts = [QA_W, 2 * QA_W, 3 * QA_W, 3 * QA_W + QB_W, 3 * QA_W + QB_W + KVB_W]
    qa, ka, va, qb, kb, vb = jnp.split(h @ w_in, cuts, axis=-1)
    sa = (B, L, N_HEADS_A, HEAD_DIM)
    skv = (B, L, N_KV_B, HEAD_DIM)
    return (qa.reshape(sa), ka.reshape(sa), va.reshape(sa),
            qb.reshape(B, L, N_HEADS_B, HEAD_DIM), kb.reshape(skv), vb.reshape(skv))


def att_mixer_ctx(h, w_in, w_out, sink):
    B, L, _ = h.shape
    qa, ka, va, qb, kb, vb = att_project(h, w_in)
    oa = dense_ctx_attention(qa[:, :, :, None, :], ka, va, None)
    ob = dense_ctx_attention(qb.reshape(B, L, N_KV_B, G_B, HEAD_DIM), kb, vb, sink)
    o = jnp.concatenate([oa.reshape(B, L, QA_W), ob.reshape(B, L, QB_W)], axis=-1) @ w_out
    return o, ka, va, kb, vb


def att_mixer_lat(h, w_in, w_out, sink, rpb, ak, av, bk, bv):
    B, T, _ = h.shape
    qa, ka, va, qb, kb, vb = att_project(h, w_in)
    qb = rope_2d(qb)
    kb = rope_2d(kb)
    oa = neighbourhood_attention(qa, ka, va, ak, av, rpb)
    ob = window_attention(qb.reshape(B, T, N_KV_B, G_B, HEAD_DIM), kb, vb, bk, bv, sink)
    return jnp.concatenate([oa.reshape(B, T, QA_W), ob.reshape(B, T, QB_W)], axis=-1) @ w_out


def depthwise_conv(x, w, b):
    y = lax.conv_general_dilated(x, w[:, None, :].astype(x.dtype), window_strides=(1,),
                                 padding=[(CONV_PAD_L, CONV_W - 1 - CONV_PAD_L)],
                                 dimension_numbers=('NWC', 'WIO', 'NWC'),
                                 feature_group_count=x.shape[-1])
    return y + b


def _lin_combine(left, right):
    a_l, b_l = left
    a_r, b_r = right
    return a_l * a_r, a_r * b_l + b_r


def rglru(x, w_a, b_a, w_x, b_x, lam, h0, reverse):
    B, T, _ = x.shape
    xb = x.reshape(B, T, N_RG_BLOCKS, RG_BLOCK)
    gate_a = jnp.einsum('btnk,nkj->btnj', xb, w_a).reshape(B, T, D_RNN) + b_a
    gate_x = jnp.einsum('btnk,nkj->btnj', xb, w_x).reshape(B, T, D_RNN) + b_x
    r = jax.nn.sigmoid(gate_a.astype(jnp.float32))
    i = jax.nn.sigmoid(gate_x.astype(jnp.float32))
    log_a = -RG_C * r * jax.nn.softplus(-lam.astype(jnp.float32))
    a = jnp.exp(log_a)
    bcoef = jnp.sqrt(-jnp.expm1(2.0 * log_a)) * i * x.astype(jnp.float32)
    h0 = h0.astype(jnp.float32)
    if reverse:
        bcoef = bcoef.at[:, -1].add(a[:, -1] * h0)
    else:
        bcoef = bcoef.at[:, 0].add(a[:, 0] * h0)
    _, h = lax.associative_scan(_lin_combine, (a, bcoef), reverse=reverse, axis=1)
    return h


def rec_mixer(h, w_in, cw, cb, w_a, b_a, w_x, b_x, lam, w_out, h0f, h0b):
    xr, g = jnp.split(h @ w_in, 2, axis=-1)
    xr = depthwise_conv(xr, cw, cb)
    hf = rglru(xr, w_a[0], b_a[0], w_x[0], b_x[0], lam[0], h0f, False)
    hb = rglru(xr, w_a[1], b_a[1], w_x[1], b_x[1], lam[1], h0b, True)
    y = ((hf + hb) * jax.nn.gelu(g.astype(jnp.float32))).astype(h.dtype)
    return y @ w_out, hf, hb


def ffn(h, w1, w2):
    a = jnp.maximum(h @ w1, 0)
    return (a * a) @ w2


def setup_inputs(seed: int = 0) -> dict:
    key = jax.random.key(seed)
    ks = iter(jax.random.split(key, 40))

    def nrm(shape, scale):
        return jax.random.normal(next(ks), shape, jnp.float32) * scale

    u = jax.random.uniform(next(ks), (N_REC_LAYERS, 2, D_RNN), jnp.float32, minval=0.9, maxval=0.999)
    a0 = u ** (1.0 / RG_C)
    rg_lambda = jnp.log(a0) - jnp.log1p(-a0)
    return {
        'x_prompt': nrm((BATCH, SEQ, D_MODEL), 1.0),
        'x_sample': nrm((DEC_BATCH, DEC_SEQ, D_MODEL), 1.0),
        'c': nrm((DEC_BATCH, D_MODEL), 1.0),
        'cache_a_k': nrm((DEC_BATCH, N_ATT_LAYERS, PAST_LEN, N_HEADS_A, HEAD_DIM), 1.0),
        'cache_a_v': nrm((DEC_BATCH, N_ATT_LAYERS, PAST_LEN, N_HEADS_A, HEAD_DIM), 1.0),
        'cache_b_k': nrm((DEC_BATCH, N_ATT_LAYERS, PAST_LEN, N_KV_B, HEAD_DIM), 1.0),
        'cache_b_v': nrm((DEC_BATCH, N_ATT_LAYERS, PAST_LEN, N_KV_B, HEAD_DIM), 1.0),
        'state_rg_fwd': nrm((DEC_BATCH, N_REC_LAYERS, D_RNN), 0.5),
        'state_rg_bwd': nrm((DEC_BATCH, N_REC_LAYERS, D_RNN), 0.5),
        'c_ctx': nrm((D_MODEL,), 1.0),
        'w_ada': nrm((DEPTH, D_MODEL, 6 * D_MODEL), 0.5 * D_MODEL ** -0.5),
        'b_ada': nrm((DEPTH, 6 * D_MODEL), 0.02),
        'g_pre_mix': 1.0 + nrm((DEPTH, D_MODEL), 0.05),
        'g_post_mix': 1.0 + nrm((DEPTH, D_MODEL), 0.05),
        'g_pre_ffn': 1.0 + nrm((DEPTH, D_MODEL), 0.05),
        'g_post_ffn': 1.0 + nrm((DEPTH, D_MODEL), 0.05),
        'w_att_in': nrm((N_ATT_LAYERS, D_MODEL, D_ATT_IN), D_MODEL ** -0.5),
        'w_att_out': nrm((N_ATT_LAYERS, D_ATT_OUT, D_MODEL), D_ATT_OUT ** -0.5),
        'sink_b': nrm((N_ATT_LAYERS, N_KV_B, G_B), 0.5),
        'rpb_a': nrm((N_ATT_LAYERS, N_HEADS_A, 2 * NA_ROWS - 1, 2 * NA_COLS - 1), 0.1),
        'w_rec_in': nrm((N_REC_LAYERS, D_MODEL, 2 * D_RNN), D_MODEL ** -0.5),
        'conv_w': nrm((N_REC_LAYERS, CONV_W, D_RNN), CONV_W ** -0.5),
        'conv_b': nrm((N_REC_LAYERS, D_RNN), 0.02),
        'w_rg_a': nrm((N_REC_LAYERS, 2, N_RG_BLOCKS, RG_BLOCK, RG_BLOCK), RG_BLOCK ** -0.5),
        'b_rg_a': nrm((N_REC_LAYERS, 2, D_RNN), 0.02),
        'w_rg_x': nrm((N_REC_LAYERS, 2, N_RG_BLOCKS, RG_BLOCK, RG_BLOCK), RG_BLOCK ** -0.5),
        'b_rg_x': nrm((N_REC_LAYERS, 2, D_RNN), 0.02),
        'rg_lambda': rg_lambda,
        'w_rec_out': nrm((N_REC_LAYERS, D_RNN, D_MODEL), D_RNN ** -0.5),
        'w_ff1': nrm((DEPTH, D_MODEL, D_FF), D_MODEL ** -0.5),
        'w_ff2': nrm((DEPTH, D_FF, D_MODEL), D_FF ** -0.5),
    }


def reference(x_prompt, x_sample, c, cache_a_k, cache_a_v, cache_b_k, cache_b_v, state_rg_fwd,
              state_rg_bwd, c_ctx, w_ada, b_ada, g_pre_mix, g_post_mix, g_pre_ffn, g_post_ffn,
              w_att_in, w_att_out, sink_b, rpb_a, w_rec_in, conv_w, conv_b, w_rg_a, b_rg_a,
              w_rg_x, b_rg_x, rg_lambda, w_rec_out, w_ff1, w_ff2):
    yp, ys = x_prompt, x_sample
    ctx_cond = c_ctx[None, :]
    ak_l, av_l, bk_l, bv_l, sf_l, sb_l = [], [], [], [], [], []
    for layer in range(DEPTH):
        shm_p, scm_p, gm_p, shf_p, scf_p, gf_p = adaln(ctx_cond, w_ada[layer], b_ada[layer])
        shm_s, scm_s, gm_s, shf_s, scf_s, gf_s = adaln(c, w_ada[layer], b_ada[layer])
        hp = rmsnorm(yp, g_pre_mix[layer]) * (1 + scm_p) + shm_p
        hs = rmsnorm(ys, g_pre_mix[layer]) * (1 + scm_s) + shm_s
        li = layer // 2
        if layer % 2 == 0:
            op, ka, va, kb, vb = att_mixer_ctx(hp, w_att_in[li], w_att_out[li], sink_b[li])
            os_ = att_mixer_lat(hs, w_att_in[li], w_att_out[li], sink_b[li], rpb_a[li],
                                cache_a_k[:, li], cache_a_v[:, li], cache_b_k[:, li], cache_b_v[:, li])
            ak_l.append(ka)
            av_l.append(va)
            bk_l.append(kb)
            bv_l.append(vb)
        else:
            zeros = jnp.zeros((yp.shape[0], D_RNN), yp.dtype)
            op, hf, hb = rec_mixer(hp, w_rec_in[li], conv_w[li], conv_b[li], w_rg_a[li], b_rg_a[li],
                                   w_rg_x[li], b_rg_x[li], rg_lambda[li], w_rec_out[li], zeros, zeros)
            os_, _, _ = rec_mixer(hs, w_rec_in[li], conv_w[li], conv_b[li], w_rg_a[li], b_rg_a[li],
                                  w_rg_x[li], b_rg_x[li], rg_lambda[li], w_rec_out[li],
                                  state_rg_fwd[:, li], state_rg_bwd[:, li])
            sf_l.append(hf[:, -1].astype(yp.dtype))
            sb_l.append(hb[:, 0].astype(yp.dtype))
        yp = yp + gm_p * rmsnorm(op, g_post_mix[layer])
        ys = ys + gm_s * rmsnorm(os_, g_post_mix[layer])
        hp = rmsnorm(yp, g_pre_ffn[layer]) * (1 + scf_p) + shf_p
        hs = rmsnorm(ys, g_pre_ffn[layer]) * (1 + scf_s) + shf_s
        yp = yp + gf_p * rmsnorm(ffn(hp, w_ff1[layer], w_ff2[layer]), g_post_ffn[layer])
        ys = ys + gf_s * rmsnorm(ffn(hs, w_ff1[layer], w_ff2[layer]), g_post_ffn[layer])
    return (yp, ys, jnp.stack(ak_l, axis=1), jnp.stack(av_l, axis=1), jnp.stack(bk_l, axis=1),
            jnp.stack(bv_l, axis=1), jnp.stack(sf_l, axis=1), jnp.stack(sb_l, axis=1))
```

```python
import functools

import jax
import jax.numpy as jnp
import numpy as np
from jax import lax
from jax.experimental import pallas as pl
from jax.experimental.pallas import tpu as pltpu

D_MODEL = 2048
BATCH = 16
SEQ = 256
DEC_BATCH = 2
DEC_SEQ = 1024
PAST_LEN = 256
GRID_W = 64
GRID_R = DEC_SEQ // GRID_W
HEAD_DIM = 128
N_HEADS_A = 8
N_HEADS_B = 8
N_KV_B = 2
G_B = N_HEADS_B // N_KV_B
NA_ROWS = 8
NA_COLS = 16
WIN_B = 128
D_RNN = D_MODEL
N_RG_BLOCKS = 16
RG_BLOCK = D_RNN // N_RG_BLOCKS
CONV_W = 4
CONV_PAD_L = 2
RG_C = 8.0
D_FF = 4 * D_MODEL
ROPE_BASE = 10000.0
EPS = 1e-6
NEG = -1e30
QA_W = N_HEADS_A * HEAD_DIM
QB_W = N_HEADS_B * HEAD_DIM
KVB_W = N_KV_B * HEAD_DIM
D_ATT_IN = 3 * QA_W + QB_W + 2 * KVB_W
SCALE = HEAD_DIM ** -0.5

N_PROMPT = BATCH * SEQ
N_SAMPLE = DEC_BATCH * DEC_SEQ
N_TOK = N_PROMPT + N_SAMPLE
N_SEG = 1 + DEC_BATCH
MOD_ROWS = 8

COL_QA, COL_KA, COL_VA = 0, QA_W, 2 * QA_W
COL_QB = 3 * QA_W
COL_KB = COL_QB + QB_W
COL_VB = COL_KB + KVB_W

V7X_VMEM_BYTES = 64 * 1024 * 1024
VMEM_LIMIT = V7X_VMEM_BYTES - 8 * 1024 * 1024

TM = 1024
ROW_CHUNK = 128

F32 = jnp.float32
BF16 = jnp.bfloat16


def _params(*sem):
    return pltpu.CompilerParams(dimension_semantics=sem, vmem_limit_bytes=VMEM_LIMIT)


def _seg_of_tile(i, tm):
    n_prompt_tiles = N_PROMPT // tm
    tiles_per_sample = DEC_SEQ // tm
    return jnp.where(i < n_prompt_tiles, 0, 1 + (i - n_prompt_tiles) // tiles_per_sample)


def _rms_scale(x):
    return lax.rsqrt(jnp.mean(x * x, axis=-1, keepdims=True) + EPS)


def _norm_mod_rows(y_ref, g_ref, mod_ref, h_ref, shift_row, tm):
    g = g_ref[...]
    shift = mod_ref[shift_row:shift_row + 1, :]
    scale1 = 1.0 + mod_ref[shift_row + 1:shift_row + 2, :]

    def body(r, carry):
        rows = pl.ds(pl.multiple_of(r * ROW_CHUNK, ROW_CHUNK), ROW_CHUNK)
        y = y_ref[rows, :]
        h = (y * _rms_scale(y) * g) * scale1 + shift
        h_ref[rows, :] = h.astype(BF16)
        return carry

    lax.fori_loop(0, tm // ROW_CHUNK, body, 0)


def _adaln_kernel(cond_ref, w_ref, b_ref, o_ref):
    c = cond_ref[...]
    s = c / (1.0 + jnp.exp(-c))
    o_ref[...] = jnp.dot(s.astype(BF16), w_ref[...].astype(BF16),
                         preferred_element_type=F32) + b_ref[...]


def _adaln(cond8, w_ada, b_ada):
    depth = w_ada.shape[0]
    n = w_ada.shape[2]
    tn = 1024
    return pl.pallas_call(
        _adaln_kernel,
        grid=(depth, n // tn),
        in_specs=[
            pl.BlockSpec((MOD_ROWS, D_MODEL), lambda l, j: (0, 0)),
            pl.BlockSpec((None, D_MODEL, tn), lambda l, j: (l, 0, j)),
            pl.BlockSpec((None, 1, tn), lambda l, j: (l, 0, j)),
        ],
        out_specs=pl.BlockSpec((None, MOD_ROWS, tn), lambda l, j: (l, 0, j)),
        out_shape=jax.ShapeDtypeStruct((depth, MOD_ROWS, n), F32),
        compiler_params=_params("arbitrary", "arbitrary"),
        name="adaln",
    )(cond8, w_ada, b_ada.reshape(depth, 1, n))


def _proj_kernel(y_ref, g_ref, mod_ref, w_ref, o_ref, h_ref, *, tm):
    @pl.when(pl.program_id(1) == 0)
    def _():
        _norm_mod_rows(y_ref, g_ref, mod_ref, h_ref, 0, tm)

    o_ref[...] = jnp.dot(h_ref[...], w_ref[...].astype(BF16), preferred_element_type=F32)


def _proj(y, g, mod, w, layer, tn):
    n = w.shape[2]
    tm = TM
    return pl.pallas_call(
        functools.partial(_proj_kernel, tm=tm),
        grid=(N_TOK // tm, n // tn),
        in_specs=[
            pl.BlockSpec((tm, D_MODEL), lambda i, j: (i, 0)),
            pl.BlockSpec((1, D_MODEL), lambda i, j: (0, 0)),
            pl.BlockSpec((None, MOD_ROWS, D_MODEL), lambda i, j: (_seg_of_tile(i, tm), 0, 0)),
            pl.BlockSpec((None, D_MODEL, tn), lambda i, j: (layer, 0, j)),
        ],
        out_specs=pl.BlockSpec((tm, tn), lambda i, j: (i, j)),
        out_shape=jax.ShapeDtypeStruct((N_TOK, n), F32),
        scratch_shapes=[pltpu.VMEM((tm, D_MODEL), BF16)],
        compiler_params=_params("arbitrary", "arbitrary"),
        name="proj",
    )(y, g.reshape(1, D_MODEL), mod, w)


def _mixout_kernel(a_ref, w_ref, y_ref, g_ref, mod_ref, o_ref, *, tm, nj, tn):
    j = pl.program_id(1)

    @pl.when(j == 0)
    def _():
        o_ref[...] = jnp.zeros_like(o_ref)

    a = a_ref[...]
    for c in range(D_MODEL // tn):
        cols = slice(c * tn, (c + 1) * tn)
        o_ref[:, cols] += jnp.dot(a, w_ref[:, cols].astype(BF16), preferred_element_type=F32)

    @pl.when(j == nj - 1)
    def _():
        g = g_ref[...]
        gate = mod_ref[2:3, :]

        def body(r, carry):
            rows = pl.ds(pl.multiple_of(r * ROW_CHUNK, ROW_CHUNK), ROW_CHUNK)
            o = o_ref[rows, :]
            o_ref[rows, :] = y_ref[rows, :] + gate * (o * _rms_scale(o) * g)
            return carry

        lax.fori_loop(0, tm // ROW_CHUNK, body, 0)


def _mixout(a, w, layer, y, g, mod):
    tm, tk, tn = TM, 512, 512
    k = a.shape[1]
    nj = k // tk
    return pl.pallas_call(
        functools.partial(_mixout_kernel, tm=tm, nj=nj, tn=tn),
        grid=(N_TOK // tm, nj),
        in_specs=[
            pl.BlockSpec((tm, tk), lambda i, j: (i, j)),
            pl.BlockSpec((None, tk, D_MODEL), lambda i, j: (layer, j, 0)),
            pl.BlockSpec((tm, D_MODEL), lambda i, j: (i, 0)),
            pl.BlockSpec((1, D_MODEL), lambda i, j: (0, 0)),
            pl.BlockSpec((None, MOD_ROWS, D_MODEL), lambda i, j: (_seg_of_tile(i, tm), 0, 0)),
        ],
        out_specs=pl.BlockSpec((tm, D_MODEL), lambda i, j: (i, 0)),
        out_shape=jax.ShapeDtypeStruct((N_TOK, D_MODEL), F32),
        compiler_params=_params("arbitrary", "arbitrary"),
        name="mixout",
    )(a, w, y, g.reshape(1, D_MODEL), mod)


def _ffn_kernel(y_ref, g1_ref, g2_ref, mod_ref, w1_ref, w2_ref, o_ref, h_ref, *, tm, nk, tn2):
    k = pl.program_id(1)

    @pl.when(k == 0)
    def _():
        _norm_mod_rows(y_ref, g1_ref, mod_ref, h_ref, 3, tm)
        o_ref[...] = jnp.zeros_like(o_ref)

    a = jnp.dot(h_ref[...], w1_ref[...].astype(BF16), preferred_element_type=F32)
    a = jnp.maximum(a, 0.0)
    a = (a * a).astype(BF16)
    for c in range(D_MODEL // tn2):
        cols = slice(c * tn2, (c + 1) * tn2)
        o_ref[:, cols] += jnp.dot(a, w2_ref[:, cols].astype(BF16), preferred_element_type=F32)

    @pl.when(k == nk - 1)
    def _():
        g2 = g2_ref[...]
        gate = mod_ref[5:6, :]

        def body(r, carry):
            rows = pl.ds(pl.multiple_of(r * ROW_CHUNK, ROW_CHUNK), ROW_CHUNK)
            o = o_ref[rows, :]
            o_ref[rows, :] = y_ref[rows, :] + gate * (o * _rms_scale(o) * g2)
            return carry

        lax.fori_loop(0, tm // ROW_CHUNK, body, 0)


def _ffn(y, g1, g2, mod, w1, w2, layer):
    tm, tf, tn2 = TM, 256, 512
    nk = D_FF // tf
    return pl.pallas_call(
        functools.partial(_ffn_kernel, tm=tm, nk=nk, tn2=tn2),
        grid=(N_TOK // tm, nk),
        in_specs=[
            pl.BlockSpec((tm, D_MODEL), lambda i, k: (i, 0)),
            pl.BlockSpec((1, D_MODEL), lambda i, k: (0, 0)),
            pl.BlockSpec((1, D_MODEL), lambda i, k: (0, 0)),
            pl.BlockSpec((None, MOD_ROWS, D_MODEL), lambda i, k: (_seg_of_tile(i, tm), 0, 0)),
            pl.BlockSpec((None, D_MODEL, tf), lambda i, k: (layer, 0, k)),
            pl.BlockSpec((None, tf, D_MODEL), lambda i, k: (layer, k, 0)),
        ],
        out_specs=pl.BlockSpec((tm, D_MODEL), lambda i, k: (i, 0)),
        out_shape=jax.ShapeDtypeStruct((N_TOK, D_MODEL), F32),
        scratch_shapes=[pltpu.VMEM((tm, D_MODEL), BF16)],
        compiler_params=_params("arbitrary", "arbitrary"),
        name="ffn",
    )(y, g1.reshape(1, D_MODEL), g2.reshape(1, D_MODEL), mod, w1, w2)


def _qkt(q, k):
    return lax.dot_general(q, k, (((1,), (1,)), ((), ())), preferred_element_type=F32)


def _head(ref, col, rows=slice(None)):
    return ref[rows, col:col + HEAD_DIM].astype(BF16)


def _attn_ctx_kernel(sink_ref, qkv_ref, o_ref):
    for h in range(N_HEADS_A):
        q = _head(qkv_ref, COL_QA + h * HEAD_DIM)
        k = _head(qkv_ref, COL_KA + h * HEAD_DIM)
        v = _head(qkv_ref, COL_VA + h * HEAD_DIM)
        s = _qkt(q, k) * SCALE
        m = jnp.max(s, axis=-1, keepdims=True)
        p = jnp.exp(s - m)
        l = jnp.sum(p, axis=-1, keepdims=True)
        o = jnp.dot(p.astype(BF16), v, preferred_element_type=F32) / l
        o_ref[:, h * HEAD_DIM:(h + 1) * HEAD_DIM] = o.astype(BF16)
    for j in range(N_KV_B):
        k = _head(qkv_ref, COL_KB + j * HEAD_DIM)
        v = _head(qkv_ref, COL_VB + j * HEAD_DIM)
        for g in range(G_B):
            hq = j * G_B + g
            q = _head(qkv_ref, COL_QB + hq * HEAD_DIM)
            sink = sink_ref[j, g]
            s = _qkt(q, k) * SCALE
            m = jnp.maximum(jnp.max(s, axis=-1, keepdims=True), sink)
            p = jnp.exp(s - m)
            l = jnp.sum(p, axis=-1, keepdims=True) + jnp.exp(sink - m)
            o = jnp.dot(p.astype(BF16), v, preferred_element_type=F32) / l
            o_ref[:, QA_W + hq * HEAD_DIM:QA_W + (hq + 1) * HEAD_DIM] = o.astype(BF16)


def _attn_ctx(sink, qkv):
    return pl.pallas_call(
        _attn_ctx_kernel,
        grid=(BATCH,),
        in_specs=[
            pl.BlockSpec(memory_space=pltpu.SMEM),
            pl.BlockSpec((SEQ, D_ATT_IN), lambda b: (b, 0)),
        ],
        out_specs=pl.BlockSpec((SEQ, D_MODEL), lambda b: (b, 0)),
        out_shape=jax.ShapeDtypeStruct((N_PROMPT, D_MODEL), BF16),
        compiler_params=_params("arbitrary"),
        name="attn_ctx",
    )(sink, qkv)


NA_Q_CHUNK = 4 * GRID_W
NA_K_SPAN = 12 * GRID_W
NA_K_START = (0, 0, 4 * GRID_W, 4 * GRID_W)


def _attn_na_kernel(q_ref, k_ref, v_ref, bias_ref, kc_ref, vc_ref, o_ref):
    kc = kc_ref[...].astype(BF16)
    vc = vc_ref[...].astype(BF16)
    for c in range(DEC_SEQ // NA_Q_CHUNK):
        rows = slice(c * NA_Q_CHUNK, (c + 1) * NA_Q_CHUNK)
        keys = slice(NA_K_START[c], NA_K_START[c] + NA_K_SPAN)
        q = q_ref[rows, :].astype(BF16)
        k = k_ref[keys, :].astype(BF16)
        v = v_ref[keys, :].astype(BF16)
        s = _qkt(q, k) * SCALE + bias_ref[rows, keys]
        sc = _qkt(q, kc) * SCALE
        m = jnp.maximum(jnp.max(s, axis=-1, keepdims=True), jnp.max(sc, axis=-1, keepdims=True))
        p = jnp.exp(s - m)
        pc = jnp.exp(sc - m)
        l = jnp.sum(p, axis=-1, keepdims=True) + jnp.sum(pc, axis=-1, keepdims=True)
        o = (jnp.dot(p.astype(BF16), v, preferred_element_type=F32)
             + jnp.dot(pc.astype(BF16), vc, preferred_element_type=F32)) / l
        o_ref[rows, :] = o.astype(BF16)


def _attn_na(qkv, bias, cache_k, cache_v):
    row0 = N_PROMPT // DEC_SEQ

    def col(c0):
        return lambda h, b: (row0 + b, c0 // HEAD_DIM + h)

    ctx_spec = pl.BlockSpec((None, PAST_LEN, HEAD_DIM), lambda h, b: (b, 0, h))
    return pl.pallas_call(
        _attn_na_kernel,
        grid=(N_HEADS_A, DEC_BATCH),
        in_specs=[
            pl.BlockSpec((DEC_SEQ, HEAD_DIM), col(COL_QA)),
            pl.BlockSpec((DEC_SEQ, HEAD_DIM), col(COL_KA)),
            pl.BlockSpec((DEC_SEQ, HEAD_DIM), col(COL_VA)),
            pl.BlockSpec((None, DEC_SEQ, DEC_SEQ), lambda h, b: (h, 0, 0)),
            ctx_spec,
            ctx_spec,
        ],
        out_specs=pl.BlockSpec((DEC_SEQ, HEAD_DIM), lambda h, b: (b, h)),
        out_shape=jax.ShapeDtypeStruct((N_SAMPLE, QA_W), BF16),
        compiler_params=_params("arbitrary", "arbitrary"),
        name="attn_na",
    )(qkv, qkv, qkv, bias, cache_k, cache_v)


def _na_bias(rpb):
    j = np.arange(GRID_W)
    start_c = np.clip(j - NA_COLS // 2, 0, GRID_W - NA_COLS)
    dc = np.clip(j[None, :] - j[:, None] + (NA_COLS - 1), 0, 2 * NA_COLS - 2)
    in_win = (j[None, :] >= start_c[:, None]) & (j[None, :] < start_c[:, None] + NA_COLS)
    tiles = jnp.where(in_win[None, None], rpb[:, :, dc], NEG)
    neg_tile = jnp.full((N_HEADS_A, 1, GRID_W, GRID_W), NEG, F32)
    tiles = jnp.concatenate([tiles.astype(F32), neg_tile], axis=1)
    r = np.arange(GRID_R)
    row_start = np.clip(r - NA_ROWS // 2, 0, GRID_R - NA_ROWS)
    dr = r[None, :] - r[:, None] + (NA_ROWS - 1)
    valid = (r[None, :] >= row_start[:, None]) & (r[None, :] < row_start[:, None] + NA_ROWS)
    tile_idx = np.where(valid, dr, 2 * NA_ROWS - 1).reshape(-1)
    dense = jnp.take(tiles, tile_idx, axis=1)
    dense = dense.reshape(N_HEADS_A, GRID_R, GRID_R, GRID_W, GRID_W).transpose(0, 1, 3, 2, 4)
    return dense.reshape(N_HEADS_A, DEC_SEQ, DEC_SEQ)


WIN_Q_CHUNK = 256
WIN_K_SPAN = WIN_Q_CHUNK + 2 * WIN_B


def _rope_tables():
    t = np.arange(DEC_SEQ)
    half = HEAD_DIM // 2
    inv = ROPE_BASE ** (-np.arange(0, half, 2, dtype=np.float64) / half)
    ang_r = (t // GRID_W)[:, None] * inv[None, :]
    ang_c = (t % GRID_W)[:, None] * inv[None, :]
    cos = np.concatenate([np.cos(ang_r), np.cos(ang_r), np.cos(ang_c), np.cos(ang_c)], axis=1)
    sin = np.concatenate([-np.sin(ang_r), np.sin(ang_r), -np.sin(ang_c), np.sin(ang_c)], axis=1)
    return jnp.asarray(cos, F32), jnp.asarray(sin, F32)


def _rope(x, cos, sin_signed):
    quarter = HEAD_DIM // 4
    lane = lax.broadcasted_iota(jnp.int32, x.shape, 1)
    first = (lane % (2 * quarter)) < quarter
    partner = jnp.where(first, pltpu.roll(x, HEAD_DIM - quarter, 1), pltpu.roll(x, quarter, 1))
    return x * cos + partner * sin_signed


def _attn_win_kernel(sink_ref, q_ref, k_ref, v_ref, kc_ref, vc_ref, cos_ref, sin_ref, o_ref, qr_ref, kr_ref):
    j = pl.program_id(1)
    cos = cos_ref[...]
    sin = sin_ref[...]
    kr_ref[...] = _rope(k_ref[...], cos, sin).astype(BF16)
    for g in range(G_B):
        qr_ref[g] = _rope(q_ref[:, g * HEAD_DIM:(g + 1) * HEAD_DIM], cos, sin).astype(BF16)
    kc = kc_ref[...].astype(BF16)
    vc = vc_ref[...].astype(BF16)
    n_rows = G_B * WIN_Q_CHUNK
    row = lax.broadcasted_iota(jnp.int32, (n_rows, 1), 0)
    grp = row // WIN_Q_CHUNK
    sink = jnp.zeros((n_rows, 1), F32)
    for g in range(G_B):
        sink = jnp.where(grp == g, sink_ref[j, g], sink)
    for c in range(DEC_SEQ // WIN_Q_CHUNK):
        q0 = c * WIN_Q_CHUNK
        k0 = min(max(q0 - WIN_B, 0), DEC_SEQ - WIN_K_SPAN)
        rows = slice(q0, q0 + WIN_Q_CHUNK)
        keys = slice(k0, k0 + WIN_K_SPAN)
        q = jnp.concatenate([qr_ref[g, rows, :] for g in range(G_B)], axis=0)
        k = kr_ref[keys, :]
        v = v_ref[keys, :].astype(BF16)
        qpos = q0 + lax.broadcasted_iota(jnp.int32, (n_rows, WIN_K_SPAN), 0) % WIN_Q_CHUNK
        kpos = k0 + lax.broadcasted_iota(jnp.int32, (n_rows, WIN_K_SPAN), 1)
        s = jnp.where(jnp.abs(qpos - kpos) <= WIN_B, _qkt(q, k) * SCALE, NEG)
        sc = _qkt(q, kc) * SCALE
        m = jnp.maximum(jnp.maximum(jnp.max(s, axis=-1, keepdims=True),
                                    jnp.max(sc, axis=-1, keepdims=True)), sink)
        p = jnp.exp(s - m)
        pc = jnp.exp(sc - m)
        l = jnp.sum(p, axis=-1, keepdims=True) + jnp.sum(pc, axis=-1, keepdims=True) + jnp.exp(sink - m)
        o = (jnp.dot(p.astype(BF16), v, preferred_element_type=F32)
             + jnp.dot(pc.astype(BF16), vc, preferred_element_type=F32)) / l
        for g in range(G_B):
            o_ref[rows, g * HEAD_DIM:(g + 1) * HEAD_DIM] = (
                o[g * WIN_Q_CHUNK:(g + 1) * WIN_Q_CHUNK, :].astype(BF16))


def _attn_win(sink, qkv, cache_k, cache_v):
    row0 = N_PROMPT // DEC_SEQ
    gw = G_B * HEAD_DIM
    cos, sin = _rope_tables()
    ctx_spec = pl.BlockSpec((None, PAST_LEN, HEAD_DIM), lambda b, j: (b, 0, j))
    tab_spec = pl.BlockSpec((DEC_SEQ, HEAD_DIM), lambda b, j: (0, 0))
    return pl.pallas_call(
        _attn_win_kernel,
        grid=(DEC_BATCH, N_KV_B),
        in_specs=[
            pl.BlockSpec(memory_space=pltpu.SMEM),
            pl.BlockSpec((DEC_SEQ, gw), lambda b, j: (row0 + b, COL_QB // gw + j)),
            pl.BlockSpec((DEC_SEQ, HEAD_DIM), lambda b, j: (row0 + b, COL_KB // HEAD_DIM + j)),
            pl.BlockSpec((DEC_SEQ, HEAD_DIM), lambda b, j: (row0 + b, COL_VB // HEAD_DIM + j)),
            ctx_spec,
            ctx_spec,
            tab_spec,
            tab_spec,
        ],
        out_specs=pl.BlockSpec((DEC_SEQ, gw), lambda b, j: (b, j)),
        out_shape=jax.ShapeDtypeStruct((N_SAMPLE, QB_W), BF16),
        scratch_shapes=[pltpu.VMEM((G_B, DEC_SEQ, HEAD_DIM), BF16), pltpu.VMEM((DEC_SEQ, HEAD_DIM), BF16)],
        compiler_params=_params("arbitrary", "arbitrary"),
        name="attn_win",
    )(sink, qkv, qkv, qkv, cache_k, cache_v, cos, sin)


REC_CB = 512
REC_PAD = 8
REC_ROWS = 256


def _softplus(x):
    return jnp.maximum(x, 0.0) + jnp.log1p(jnp.exp(-jnp.abs(x)))


def _gelu_tanh(x):
    return 0.5 * x * (1.0 + jnp.tanh(np.sqrt(2.0 / np.pi) * (x + 0.044715 * (x * x * x))))


def _sigmoid(x):
    return 1.0 / (1.0 + jnp.exp(-x))


def _rec_kernel(x_ref, g_ref, cw_ref, cb_ref, wg_ref, bg_ref, lam_ref, h0_ref, y_ref, st_ref,
                xp_ref, xc_ref, af_ref, bf_ref, ab_ref, bb_ref, *, T):
    a_refs = (af_ref, ab_ref)
    b_refs = (bf_ref, bb_ref)
    zeros = jnp.zeros((REC_PAD, REC_CB), F32)
    xp_ref[0:REC_PAD, :] = zeros
    xp_ref[T + REC_PAD:T + 2 * REC_PAD, :] = zeros
    xp_ref[REC_PAD:T + REC_PAD, :] = x_ref[...]
    spl = _softplus(-lam_ref[...])

    def gate_rows(r, carry):
        r0 = pl.multiple_of(r * REC_ROWS, REC_ROWS)
        rows = pl.ds(r0, REC_ROWS)
        n_win = REC_ROWS + 2 * REC_PAD
        win = xp_ref[pl.ds(r0, n_win), :]
        xc = cb_ref[...] + jnp.zeros((REC_ROWS, REC_CB), F32)
        for k in range(CONV_W):
            tap = pltpu.roll(win, n_win - (REC_PAD - CONV_PAD_L + k), 0)[:REC_ROWS, :]
            xc = xc + cw_ref[k:k + 1, :] * tap
        xc_ref[rows, :] = xc
        for n in range(REC_CB // RG_BLOCK):
            cols = slice(n * RG_BLOCK, (n + 1) * RG_BLOCK)
            xn = xc[:, cols]
            gates = jnp.dot(xn.astype(BF16), wg_ref[n].astype(BF16), preferred_element_type=F32)
            for d in range(2):
                ga = gates[:, (2 * d) * RG_BLOCK:(2 * d + 1) * RG_BLOCK] + bg_ref[2 * d:2 * d + 1, cols]
                gx = gates[:, (2 * d + 1) * RG_BLOCK:(2 * d + 2) * RG_BLOCK] + bg_ref[2 * d + 1:2 * d + 2, cols]
                log_a = (-RG_C) * _sigmoid(ga) * spl[d:d + 1, cols]
                a = jnp.exp(log_a)
                a_refs[d][rows, cols] = a
                b_refs[d][rows, cols] = jnp.sqrt(1.0 - a * a) * _sigmoid(gx) * xn
        return carry

    lax.fori_loop(0, T // REC_ROWS, gate_rows, 0)

    def scan_step(t, carry):
        hf, hb = carry
        rf = pl.ds(t, 1)
        hf = af_ref[rf, :] * hf + bf_ref[rf, :]
        bf_ref[rf, :] = hf
        rb = pl.ds(T - 1 - t, 1)
        hb = ab_ref[rb, :] * hb + bb_ref[rb, :]
        bb_ref[rb, :] = hb
        return hf, hb

    hf, hb = lax.fori_loop(0, T, scan_step, (h0_ref[0:1, :], h0_ref[1:2, :]), unroll=8)
    st_ref[0:1, :] = hf
    st_ref[1:2, :] = hb

    def out_rows(r, carry):
        rows = pl.ds(pl.multiple_of(r * REC_ROWS, REC_ROWS), REC_ROWS)
        y_ref[rows, :] = ((bf_ref[rows, :] + bb_ref[rows, :]) * _gelu_tanh(g_ref[rows, :])).astype(BF16)
        return carry

    lax.fori_loop(0, T // REC_ROWS, out_rows, 0)


def _rec(xg, row0, n_seq, T, cw, cb, wg, bg, lam, h0):
    nc = D_RNN // REC_CB
    blk0 = row0 // T
    nb = REC_CB // RG_BLOCK
    vec = lambda rows: pl.BlockSpec((rows, REC_CB), lambda s, c: (0, c))
    return pl.pallas_call(
        functools.partial(_rec_kernel, T=T),
        grid=(n_seq, nc),
        in_specs=[
            pl.BlockSpec((T, REC_CB), lambda s, c: (blk0 + s, c)),
            pl.BlockSpec((T, REC_CB), lambda s, c: (blk0 + s, nc + c)),
            vec(CONV_W),
            vec(1),
            pl.BlockSpec((nb, RG_BLOCK, 4 * RG_BLOCK), lambda s, c: (c, 0, 0)),
            vec(4),
            vec(2),
            pl.BlockSpec((None, 2, REC_CB), lambda s, c: (s, 0, c)),
        ],
        out_specs=[
            pl.BlockSpec((T, REC_CB), lambda s, c: (s, c)),
            pl.BlockSpec((None, 2, REC_CB), lambda s, c: (s, 0, c)),
        ],
        out_shape=[
            jax.ShapeDtypeStruct((n_seq * T, D_RNN), BF16),
            jax.ShapeDtypeStruct((n_seq, 2, D_RNN), F32),
        ],
        scratch_shapes=[pltpu.VMEM((T + 2 * REC_PAD, REC_CB), F32)] + [pltpu.VMEM((T, REC_CB), F32)] * 5,
        compiler_params=_params("arbitrary", "arbitrary"),
        name="rec",
    )(xg, xg, cw, cb.reshape(1, D_RNN), wg, bg, lam, h0)


def kernel(x_prompt, x_sample, c, cache_a_k, cache_a_v, cache_b_k, cache_b_v, state_rg_fwd, state_rg_bwd, c_ctx, w_ada, b_ada, g_pre_mix, g_post_mix, g_pre_ffn, g_post_ffn, w_att_in, w_att_out, sink_b, rpb_a, w_rec_in, conv_w, conv_b, w_rg_a, b_rg_a, w_rg_x, b_rg_x, rg_lambda, w_rec_out, w_ff1, w_ff2):
    depth = w_ada.shape[0]
    y = jnp.concatenate([x_prompt.reshape(N_PROMPT, D_MODEL), x_sample.reshape(N_SAMPLE, D_MODEL)], axis=0)

    cond8 = jnp.concatenate([c_ctx[None, :], c, jnp.zeros((MOD_ROWS - N_SEG, D_MODEL), F32)], axis=0)
    mod_all = _adaln(cond8, w_ada, b_ada)
    mod_all = mod_all[:, :N_SEG, :].reshape(depth, N_SEG, 6, D_MODEL)
    mod_all = jnp.pad(mod_all, ((0, 0), (0, 0), (0, MOD_ROWS - 6), (0, 0)))

    a_k, a_v, b_k, b_v, s_f, s_b = [], [], [], [], [], []
    for layer in range(depth):
        mod = mod_all[layer]
        li = layer // 2
        if layer % 2 == 0:
            qkv = _proj(y, g_pre_mix[layer], mod, w_att_in, li, 512)
            ctx = qkv[:N_PROMPT]
            a_k.append(ctx[:, COL_KA:COL_KA + QA_W].reshape(BATCH, SEQ, N_HEADS_A, HEAD_DIM))
            a_v.append(ctx[:, COL_VA:COL_VA + QA_W].reshape(BATCH, SEQ, N_HEADS_A, HEAD_DIM))
            b_k.append(ctx[:, COL_KB:COL_KB + KVB_W].reshape(BATCH, SEQ, N_KV_B, HEAD_DIM))
            b_v.append(ctx[:, COL_VB:COL_VB + KVB_W].reshape(BATCH, SEQ, N_KV_B, HEAD_DIM))
            att_ctx = _attn_ctx(sink_b[li], qkv)
            att_na = _attn_na(qkv, _na_bias(rpb_a[li]),
                              cache_a_k[:, li].reshape(DEC_BATCH, PAST_LEN, QA_W),
                              cache_a_v[:, li].reshape(DEC_BATCH, PAST_LEN, QA_W))
            att_win = _attn_win(sink_b[li], qkv,
                                cache_b_k[:, li].reshape(DEC_BATCH, PAST_LEN, KVB_W),
                                cache_b_v[:, li].reshape(DEC_BATCH, PAST_LEN, KVB_W))
            mixed = jnp.concatenate([att_ctx, jnp.concatenate([att_na, att_win], axis=1)], axis=0)
            y = _mixout(mixed, w_att_out, li, y, g_post_mix[layer], mod)
        else:
            xg = _proj(y, g_pre_mix[layer], mod, w_rec_in, li, 512)
            wg = jnp.concatenate([w_rg_a[li, 0], w_rg_x[li, 0], w_rg_a[li, 1], w_rg_x[li, 1]], axis=-1)
            bg = jnp.stack([b_rg_a[li, 0], b_rg_x[li, 0], b_rg_a[li, 1], b_rg_x[li, 1]], axis=0)
            rec_args = (conv_w[li], conv_b[li], wg, bg, rg_lambda[li])
            h0_p = jnp.zeros((BATCH, 2, D_RNN), F32)
            h0_s = jnp.stack([state_rg_fwd[:, li], state_rg_bwd[:, li]], axis=1)
            rec_p, st_p = _rec(xg, 0, BATCH, SEQ, *rec_args, h0_p)
            rec_s, _ = _rec(xg, N_PROMPT, DEC_BATCH, DEC_SEQ, *rec_args, h0_s)
            s_f.append(st_p[:, 0])
            s_b.append(st_p[:, 1])
            y = _mixout(jnp.concatenate([rec_p, rec_s], axis=0), w_rec_out, li, y, g_post_mix[layer], mod)
        y = _ffn(y, g_pre_ffn[layer], g_post_ffn[layer], mod, w_ff1, w_ff2, layer)

    return (y[:N_PROMPT].reshape(BATCH, SEQ, D_MODEL), y[N_PROMPT:].reshape(DEC_BATCH, DEC_SEQ, D_MODEL),
            jnp.stack(a_k, axis=1), jnp.stack(a_v, axis=1), jnp.stack(b_k, axis=1), jnp.stack(b_v, axis=1),
            jnp.stack(s_f, axis=1), jnp.stack(s_b, axis=1))
```

```python
import functools

import jax
import jax.numpy as jnp
import numpy as np
from jax import lax
from jax.experimental import pallas as pl
from jax.experimental.pallas import tpu as pltpu

D_MODEL = 2048
BATCH = 16
SEQ = 256
DEC_BATCH = 2
DEC_SEQ = 1024
PAST_LEN = 256
GRID_W = 64
GRID_R = DEC_SEQ // GRID_W
HEAD_DIM = 128
N_HEADS_A = 8
N_HEADS_B = 8
N_KV_B = 2
G_B = N_HEADS_B // N_KV_B
NA_ROWS = 8
NA_COLS = 16
WIN_B = 128
D_RNN = D_MODEL
N_RG_BLOCKS = 16
RG_BLOCK = D_RNN // N_RG_BLOCKS
CONV_W = 4
CONV_PAD_L = 2
RG_C = 8.0
D_FF = 4 * D_MODEL
ROPE_BASE = 10000.0
EPS = 1e-6
NEG = -1e30
QA_W = N_HEADS_A * HEAD_DIM
QB_W = N_HEADS_B * HEAD_DIM
KVB_W = N_KV_B * HEAD_DIM
D_ATT_IN = 3 * QA_W + QB_W + 2 * KVB_W
SCALE = HEAD_DIM ** -0.5

N_PROMPT = BATCH * SEQ
N_SAMPLE = DEC_BATCH * DEC_SEQ
N_SEG = 1 + DEC_BATCH
MOD_ROWS = 8

COL_QA, COL_KA, COL_VA = 0, QA_W, 2 * QA_W
COL_QB = 3 * QA_W
COL_KB = COL_QB + QB_W
COL_VB = COL_KB + KVB_W

V7X_VMEM_BYTES = 64 * 1024 * 1024
VMEM_LIMIT = V7X_VMEM_BYTES - 4 * 1024 * 1024

TM = 1024
TN = 512
ROW_CHUNK = 128

F32 = jnp.float32
BF16 = jnp.bfloat16


def _params(*sem):
    return pltpu.CompilerParams(dimension_semantics=sem, vmem_limit_bytes=VMEM_LIMIT)


class _Rows:
    def __init__(self, n_rows, seg0, seg_rows):
        self.n = n_rows
        self.seg0 = seg0
        self.seg_rows = seg_rows

    def seg(self, i, tm):
        return self.seg0 + (i * tm) // self.seg_rows


PROMPT = _Rows(N_PROMPT, 0, N_PROMPT)
SAMPLE = _Rows(N_SAMPLE, 1, DEC_SEQ)


def _mod_spec(rows, tm):
    return pl.BlockSpec((None, MOD_ROWS, D_MODEL), lambda i, j: (rows.seg(i, tm), 0, 0))


def _vec_spec():
    return pl.BlockSpec((1, D_MODEL), lambda i, j: (0, 0))


def _rms_scale(x):
    return lax.rsqrt(jnp.mean(x * x, axis=-1, keepdims=True) + EPS)


def _norm_mod_rows(y_ref, g_ref, mod_ref, h_ref, shift_row, tm):
    g = g_ref[...]
    shift = mod_ref[shift_row:shift_row + 1, :]
    scale1 = 1.0 + mod_ref[shift_row + 1:shift_row + 2, :]

    def body(r, carry):
        rows = pl.ds(pl.multiple_of(r * ROW_CHUNK, ROW_CHUNK), ROW_CHUNK)
        y = y_ref[rows, :]
        h = (y * _rms_scale(y) * g) * scale1 + shift
        h_ref[rows, :] = h.astype(BF16)
        return carry

    lax.fori_loop(0, tm // ROW_CHUNK, body, 0)


def _gated_residual_rows(y_ref, o_ref, g_ref, gate, tm):
    g = g_ref[...]

    def body(r, carry):
        rows = pl.ds(pl.multiple_of(r * ROW_CHUNK, ROW_CHUNK), ROW_CHUNK)
        o = o_ref[rows, :]
        o_ref[rows, :] = y_ref[rows, :] + gate * (o * _rms_scale(o) * g)
        return carry

    lax.fori_loop(0, tm // ROW_CHUNK, body, 0)


def _adaln_kernel(cond_ref, w_ref, b_ref, o_ref):
    c = cond_ref[...]
    s = c / (1.0 + jnp.exp(-c))
    o_ref[...] = jnp.dot(s.astype(BF16), w_ref[...].astype(BF16),
                         preferred_element_type=F32) + b_ref[...]


def _adaln(cond8, w_ada, b_ada):
    depth = w_ada.shape[0]
    n = w_ada.shape[2]
    tn = 1024
    return pl.pallas_call(
        _adaln_kernel,
        grid=(depth, n // tn),
        in_specs=[
            pl.BlockSpec((MOD_ROWS, D_MODEL), lambda l, j: (0, 0)),
            pl.BlockSpec((None, D_MODEL, tn), lambda l, j: (l, 0, j)),
            pl.BlockSpec((None, 1, tn), lambda l, j: (l, 0, j)),
        ],
        out_specs=pl.BlockSpec((None, MOD_ROWS, tn), lambda l, j: (l, 0, j)),
        out_shape=jax.ShapeDtypeStruct((depth, MOD_ROWS, n), F32),
        compiler_params=_params("arbitrary", "arbitrary"),
        name="adaln",
    )(cond8, w_ada, b_ada.reshape(depth, 1, n))


def _rope_tables():
    t = np.arange(DEC_SEQ)
    half = HEAD_DIM // 2
    inv = ROPE_BASE ** (-np.arange(0, half, 2, dtype=np.float64) / half)
    ang_r = (t // GRID_W)[:, None] * inv[None, :]
    ang_c = (t % GRID_W)[:, None] * inv[None, :]
    cos = np.concatenate([np.cos(ang_r), np.cos(ang_r), np.cos(ang_c), np.cos(ang_c)], axis=1)
    sin = np.concatenate([-np.sin(ang_r), np.sin(ang_r), -np.sin(ang_c), np.sin(ang_c)], axis=1)
    return jnp.asarray(cos, F32), jnp.asarray(sin, F32)


def _rope(x, cos, sin_signed):
    quarter = HEAD_DIM // 4
    lane = lax.broadcasted_iota(jnp.int32, x.shape, 1)
    first = (lane % (2 * quarter)) < quarter
    partner = jnp.where(first, pltpu.roll(x, HEAD_DIM - quarter, 1), pltpu.roll(x, quarter, 1))
    return x * cos + partner * sin_signed


def _proj_kernel(y_ref, g_ref, mod_ref, w_ref, o_ref, h_ref, *, tm):
    @pl.when(pl.program_id(1) == 0)
    def _():
        _norm_mod_rows(y_ref, g_ref, mod_ref, h_ref, 0, tm)

    o_ref[...] = jnp.dot(h_ref[...], w_ref[...].astype(BF16), preferred_element_type=F32)


def _proj(rows, y, g, mod, w, layer):
    n = w.shape[2]
    tm = TM
    return pl.pallas_call(
        functools.partial(_proj_kernel, tm=tm),
        grid=(rows.n // tm, n // TN),
        in_specs=[
            pl.BlockSpec((tm, D_MODEL), lambda i, j: (i, 0)),
            _vec_spec(),
            _mod_spec(rows, tm),
            pl.BlockSpec((None, D_MODEL, TN), lambda i, j: (layer, 0, j)),
        ],
        out_specs=pl.BlockSpec((tm, TN), lambda i, j: (i, j)),
        out_shape=jax.ShapeDtypeStruct((rows.n, n), F32),
        scratch_shapes=[pltpu.VMEM((tm, D_MODEL), BF16)],
        compiler_params=_params("arbitrary", "arbitrary"),
        name="proj",
    )(y, g.reshape(1, D_MODEL), mod, w)


ATT_TILE_KA = COL_KA // TN
ATT_TILE_VA = COL_VA // TN
ATT_TILE_QB = COL_QB // TN
ATT_TILE_KVB = COL_KB // TN
N_ATT_TILES = D_ATT_IN // TN


def _proj_att_prompt_kernel(y_ref, g_ref, mod_ref, w_ref, o_ref, ka_ref, va_ref, kb_ref, vb_ref, h_ref, *, tm):
    j = pl.program_id(1)

    @pl.when(j == 0)
    def _():
        _norm_mod_rows(y_ref, g_ref, mod_ref, h_ref, 0, tm)

    acc = jnp.dot(h_ref[...], w_ref[...].astype(BF16), preferred_element_type=F32)
    o_ref[...] = acc.astype(BF16)

    @pl.when((j >= ATT_TILE_KA) & (j < ATT_TILE_VA))
    def _():
        ka_ref[...] = acc

    @pl.when((j >= ATT_TILE_VA) & (j < ATT_TILE_QB))
    def _():
        va_ref[...] = acc

    @pl.when(j == ATT_TILE_KVB)
    def _():
        kb_ref[...] = acc[:, :KVB_W]
        vb_ref[...] = acc[:, KVB_W:]


def _proj_att_prompt(y, g, mod, w, layer):
    tm = TM
    rows = PROMPT
    per_tile = QA_W // TN

    def kv_map(first):
        return lambda i, j: (i, jnp.clip(j - first, 0, per_tile - 1))

    return pl.pallas_call(
        functools.partial(_proj_att_prompt_kernel, tm=tm),
        grid=(rows.n // tm, N_ATT_TILES),
        in_specs=[
            pl.BlockSpec((tm, D_MODEL), lambda i, j: (i, 0)),
            _vec_spec(),
            _mod_spec(rows, tm),
            pl.BlockSpec((None, D_MODEL, TN), lambda i, j: (layer, 0, j)),
        ],
        out_specs=[
            pl.BlockSpec((tm, TN), lambda i, j: (i, j)),
            pl.BlockSpec((tm, TN), kv_map(ATT_TILE_KA)),
            pl.BlockSpec((tm, TN), kv_map(ATT_TILE_VA)),
            pl.BlockSpec((tm, KVB_W), lambda i, j: (i, 0)),
            pl.BlockSpec((tm, KVB_W), lambda i, j: (i, 0)),
        ],
        out_shape=[
            jax.ShapeDtypeStruct((rows.n, D_ATT_IN), BF16),
            jax.ShapeDtypeStruct((rows.n, QA_W), F32),
            jax.ShapeDtypeStruct((rows.n, QA_W), F32),
            jax.ShapeDtypeStruct((rows.n, KVB_W), F32),
            jax.ShapeDtypeStruct((rows.n, KVB_W), F32),
        ],
        scratch_shapes=[pltpu.VMEM((tm, D_MODEL), BF16)],
        compiler_params=_params("arbitrary", "arbitrary"),
        name="proj_att_prompt",
    )(y, g.reshape(1, D_MODEL), mod, w)


def _proj_att_sample_kernel(y_ref, g_ref, mod_ref, w_ref, cos_ref, sin_ref, o_ref, h_ref, *, tm):
    j = pl.program_id(1)

    @pl.when(j == 0)
    def _():
        _norm_mod_rows(y_ref, g_ref, mod_ref, h_ref, 0, tm)

    acc = jnp.dot(h_ref[...], w_ref[...].astype(BF16), preferred_element_type=F32)

    def store(n_rope_heads):
        cos = cos_ref[...]
        sin = sin_ref[...]
        for hd in range(TN // HEAD_DIM):
            cols = slice(hd * HEAD_DIM, (hd + 1) * HEAD_DIM)
            x = acc[:, cols]
            if hd < n_rope_heads:
                x = _rope(x, cos, sin)
            o_ref[:, cols] = x.astype(BF16)

    @pl.when(j < ATT_TILE_QB)
    def _():
        o_ref[...] = acc.astype(BF16)

    @pl.when((j >= ATT_TILE_QB) & (j < ATT_TILE_KVB))
    def _():
        store(TN // HEAD_DIM)

    @pl.when(j == ATT_TILE_KVB)
    def _():
        store(N_KV_B)


def _proj_att_sample(y, g, mod, w, layer):
    tm = TM
    assert tm == DEC_SEQ
    rows = SAMPLE
    cos, sin = _rope_tables()
    tab_spec = pl.BlockSpec((DEC_SEQ, HEAD_DIM), lambda i, j: (0, 0))
    return pl.pallas_call(
        functools.partial(_proj_att_sample_kernel, tm=tm),
        grid=(rows.n // tm, N_ATT_TILES),
        in_specs=[
            pl.BlockSpec((tm, D_MODEL), lambda i, j: (i, 0)),
            _vec_spec(),
            _mod_spec(rows, tm),
            pl.BlockSpec((None, D_MODEL, TN), lambda i, j: (layer, 0, j)),
            tab_spec,
            tab_spec,
        ],
        out_specs=pl.BlockSpec((tm, TN), lambda i, j: (i, j)),
        out_shape=jax.ShapeDtypeStruct((rows.n, D_ATT_IN), BF16),
        scratch_shapes=[pltpu.VMEM((tm, D_MODEL), BF16)],
        compiler_params=_params("arbitrary", "arbitrary"),
        name="proj_att_sample",
    )(y, g.reshape(1, D_MODEL), mod, w, cos, sin)


def _mixout_kernel(*refs, tm, nj, n_parts):
    a_refs = refs[:n_parts]
    w_ref, y_ref, g_ref, mod_ref, o_ref = refs[n_parts:]
    j = pl.program_id(1)

    @pl.when(j == 0)
    def _():
        o_ref[...] = jnp.zeros_like(o_ref)

    a = a_refs[0][...]
    for p in range(1, n_parts):
        a = jnp.where(j >= p * (nj // n_parts), a_refs[p][...], a)
    for c in range(D_MODEL // TN):
        cols = slice(c * TN, (c + 1) * TN)
        o_ref[:, cols] += jnp.dot(a, w_ref[:, cols].astype(BF16), preferred_element_type=F32)

    @pl.when(j == nj - 1)
    def _():
        _gated_residual_rows(y_ref, o_ref, g_ref, mod_ref[2:3, :], tm)


def _mixout(rows, a_parts, w, layer, y, g, mod):
    tm, tk = TM, 512
    n_parts = len(a_parts)
    per_part = a_parts[0].shape[1] // tk
    nj = n_parts * per_part

    def part_spec(p):
        return pl.BlockSpec((tm, tk), lambda i, j: (i, jnp.clip(j - p * per_part, 0, per_part - 1)))

    return pl.pallas_call(
        functools.partial(_mixout_kernel, tm=tm, nj=nj, n_parts=n_parts),
        grid=(rows.n // tm, nj),
        in_specs=[part_spec(p) for p in range(n_parts)] + [
            pl.BlockSpec((None, tk, D_MODEL), lambda i, j: (layer, j, 0)),
            pl.BlockSpec((tm, D_MODEL), lambda i, j: (i, 0)),
            _vec_spec(),
            _mod_spec(rows, tm),
        ],
        out_specs=pl.BlockSpec((tm, D_MODEL), lambda i, j: (i, 0)),
        out_shape=jax.ShapeDtypeStruct((rows.n, D_MODEL), F32),
        compiler_params=_params("arbitrary", "arbitrary"),
        name="mixout",
    )(*a_parts, w, y, g.reshape(1, D_MODEL), mod)


FFN_TF = 256


def _ffn_kernel(y_ref, g1_ref, g2_ref, mod_ref, w1_ref, w2a_ref, w2b_ref, o_ref, h_ref, a0_ref, a1_ref, *, tm, ns):
    s = pl.program_id(1)

    @pl.when(s == 0)
    def _():
        _norm_mod_rows(y_ref, g1_ref, mod_ref, h_ref, 3, tm)
        o_ref[...] = jnp.zeros_like(o_ref)
        a1_ref[...] = jnp.zeros_like(a1_ref)

    def up(half):
        cols = slice(half * FFN_TF, (half + 1) * FFN_TF)
        a = jnp.dot(h_ref[...], w1_ref[:, cols].astype(BF16), preferred_element_type=F32)
        a = jnp.maximum(a, 0.0)
        return (a * a).astype(BF16)

    def down(a_ref, w_ref):
        a = a_ref[...]
        for c in range(D_MODEL // TN):
            cols = slice(c * TN, (c + 1) * TN)
            o_ref[:, cols] += jnp.dot(a, w_ref[:, cols].astype(BF16), preferred_element_type=F32)

    @pl.when(s < ns)
    def _():
        a0_ref[...] = up(0)
        down(a1_ref, w2a_ref)
        a1_next = up(1)
        down(a0_ref, w2b_ref)
        a1_ref[...] = a1_next

    @pl.when(s == ns)
    def _():
        down(a1_ref, w2a_ref)
        _gated_residual_rows(y_ref, o_ref, g2_ref, mod_ref[5:6, :], tm)


def _ffn(rows, y, g1, g2, mod, w1, w2, layer):
    tm = TM
    ns = D_FF // (2 * FFN_TF)
    last = D_FF // FFN_TF - 1
    return pl.pallas_call(
        functools.partial(_ffn_kernel, tm=tm, ns=ns),
        grid=(rows.n // tm, ns + 1),
        in_specs=[
            pl.BlockSpec((tm, D_MODEL), lambda i, s: (i, 0)),
            _vec_spec(),
            _vec_spec(),
            _mod_spec(rows, tm),
            pl.BlockSpec((None, D_MODEL, 2 * FFN_TF), lambda i, s: (layer, 0, jnp.minimum(s, ns - 1))),
            pl.BlockSpec((None, FFN_TF, D_MODEL), lambda i, s: (layer, jnp.maximum(2 * s - 1, 0), 0)),
            pl.BlockSpec((None, FFN_TF, D_MODEL), lambda i, s: (layer, jnp.minimum(2 * s, last), 0)),
        ],
        out_specs=pl.BlockSpec((tm, D_MODEL), lambda i, s: (i, 0)),
        out_shape=jax.ShapeDtypeStruct((rows.n, D_MODEL), F32),
        scratch_shapes=[pltpu.VMEM((tm, D_MODEL), BF16), pltpu.VMEM((tm, FFN_TF), BF16),
                        pltpu.VMEM((tm, FFN_TF), BF16)],
        compiler_params=_params("arbitrary", "arbitrary"),
        name="ffn",
    )(y, g1.reshape(1, D_MODEL), g2.reshape(1, D_MODEL), mod, w1, w2, w2)


def _qkt(q, k):
    return lax.dot_general(q, k, (((1,), (1,)), ((), ())), preferred_element_type=F32)


def _head(ref, col):
    return ref[:, col:col + HEAD_DIM]


def _attn_ctx_kernel(sink_ref, qkv_ref, o_ref):
    for h in range(N_HEADS_A):
        q = _head(qkv_ref, COL_QA + h * HEAD_DIM)
        k = _head(qkv_ref, COL_KA + h * HEAD_DIM)
        v = _head(qkv_ref, COL_VA + h * HEAD_DIM)
        s = _qkt(q, k) * SCALE
        m = jnp.max(s, axis=-1, keepdims=True)
        p = jnp.exp(s - m)
        l = jnp.sum(p, axis=-1, keepdims=True)
        o = jnp.dot(p.astype(BF16), v, preferred_element_type=F32) / l
        o_ref[:, h * HEAD_DIM:(h + 1) * HEAD_DIM] = o.astype(BF16)
    for j in range(N_KV_B):
        k = _head(qkv_ref, COL_KB + j * HEAD_DIM)
        v = _head(qkv_ref, COL_VB + j * HEAD_DIM)
        for g in range(G_B):
            hq = j * G_B + g
            q = _head(qkv_ref, COL_QB + hq * HEAD_DIM)
            sink = sink_ref[j, g]
            s = _qkt(q, k) * SCALE
            m = jnp.maximum(jnp.max(s, axis=-1, keepdims=True), sink)
            p = jnp.exp(s - m)
            l = jnp.sum(p, axis=-1, keepdims=True) + jnp.exp(sink - m)
            o = jnp.dot(p.astype(BF16), v, preferred_element_type=F32) / l
            o_ref[:, QA_W + hq * HEAD_DIM:QA_W + (hq + 1) * HEAD_DIM] = o.astype(BF16)


def _attn_ctx(sink, qkv):
    return pl.pallas_call(
        _attn_ctx_kernel,
        grid=(BATCH,),
        in_specs=[
            pl.BlockSpec(memory_space=pltpu.SMEM),
            pl.BlockSpec((SEQ, D_ATT_IN), lambda b: (b, 0)),
        ],
        out_specs=pl.BlockSpec((SEQ, D_MODEL), lambda b: (b, 0)),
        out_shape=jax.ShapeDtypeStruct((N_PROMPT, D_MODEL), BF16),
        compiler_params=_params("arbitrary"),
        name="attn_ctx",
    )(sink, qkv)


NA_Q_ROWS = 4
NA_K_ROWS = 12
NA_Q_CHUNK = NA_Q_ROWS * GRID_W
NA_K_SPAN = NA_K_ROWS * GRID_W
NA_K_ROW0 = (0, 0, 4, 4)
N_RPB_ROWS = 2 * NA_ROWS - 1
N_RPB_COLS = 2 * NA_COLS - 1


def _na_row_start(r):
    return min(max(r - NA_ROWS // 2, 0), GRID_R - NA_ROWS)


def _na_build_bias(rpb_ref, tile_ref, bias_ref):
    h = pl.program_id(0)
    shape = (GRID_W, 2 * GRID_W)
    qc = lax.broadcasted_iota(jnp.int32, shape, 0)
    lane = lax.broadcasted_iota(jnp.int32, shape, 1)
    kc = lane % GRID_W
    start_c = jnp.clip(qc - NA_COLS // 2, 0, GRID_W - NA_COLS)
    in_win = (kc >= start_c) & (kc < start_c + NA_COLS)
    dc = jnp.where(in_win, kc - qc + (NA_COLS - 1), -1)
    for dr in range(N_RPB_ROWS):
        t = jnp.full(shape, NEG, F32)
        for c in range(N_RPB_COLS):
            t = jnp.where(dc == c, rpb_ref[h, dr * N_RPB_COLS + c], t)
        tile_ref[dr] = t
    first_half = lane < GRID_W
    neg = jnp.full(shape, NEG, F32)
    for chunk in range(GRID_R // NA_Q_ROWS):
        for qi in range(NA_Q_ROWS):
            qr = chunk * NA_Q_ROWS + qi
            lo = _na_row_start(qr)
            for m in range(NA_K_ROWS // 2):
                kr = NA_K_ROW0[chunk] + 2 * m
                parts = []
                for r in (kr, kr + 1):
                    parts.append(tile_ref[r - qr + NA_ROWS - 1] if lo <= r < lo + NA_ROWS else neg)
                bias_ref[chunk, qi * GRID_W:(qi + 1) * GRID_W, m * 2 * GRID_W:(m + 1) * 2 * GRID_W] = (
                    jnp.where(first_half, parts[0], parts[1]))


def _attn_na_kernel(rpb_ref, q_ref, k_ref, v_ref, kc_ref, vc_ref, o_ref, tile_ref, bias_ref):
    @pl.when(pl.program_id(1) == 0)
    def _():
        _na_build_bias(rpb_ref, tile_ref, bias_ref)

    kc = kc_ref[...].astype(BF16)
    vc = vc_ref[...].astype(BF16)
    for c in range(DEC_SEQ // NA_Q_CHUNK):
        rows = slice(c * NA_Q_CHUNK, (c + 1) * NA_Q_CHUNK)
        keys = slice(NA_K_ROW0[c] * GRID_W, NA_K_ROW0[c] * GRID_W + NA_K_SPAN)
        q = q_ref[rows, :]
        s = _qkt(q, k_ref[keys, :]) * SCALE + bias_ref[c]
        sc = _qkt(q, kc) * SCALE
        m = jnp.maximum(jnp.max(s, axis=-1, keepdims=True), jnp.max(sc, axis=-1, keepdims=True))
        p = jnp.exp(s - m)
        pc = jnp.exp(sc - m)
        l = jnp.sum(p, axis=-1, keepdims=True) + jnp.sum(pc, axis=-1, keepdims=True)
        o = (jnp.dot(p.astype(BF16), v_ref[keys, :], preferred_element_type=F32)
             + jnp.dot(pc.astype(BF16), vc, preferred_element_type=F32)) / l
        o_ref[rows, :] = o.astype(BF16)


def _attn_na(qkv, rpb, cache_k, cache_v):
    def col(c0):
        return lambda h, b: (b, c0 // HEAD_DIM + h)

    ctx_spec = pl.BlockSpec((None, PAST_LEN, HEAD_DIM), lambda h, b: (b, 0, h))
    n_chunks = DEC_SEQ // NA_Q_CHUNK
    return pl.pallas_call(
        _attn_na_kernel,
        grid=(N_HEADS_A, DEC_BATCH),
        in_specs=[
            pl.BlockSpec(memory_space=pltpu.SMEM),
            pl.BlockSpec((DEC_SEQ, HEAD_DIM), col(COL_QA)),
            pl.BlockSpec((DEC_SEQ, HEAD_DIM), col(COL_KA)),
            pl.BlockSpec((DEC_SEQ, HEAD_DIM), col(COL_VA)),
            ctx_spec,
            ctx_spec,
        ],
        out_specs=pl.BlockSpec((DEC_SEQ, HEAD_DIM), lambda h, b: (b, h)),
        out_shape=jax.ShapeDtypeStruct((N_SAMPLE, QA_W), BF16),
        scratch_shapes=[pltpu.VMEM((N_RPB_ROWS, GRID_W, 2 * GRID_W), F32),
                        pltpu.VMEM((n_chunks, NA_Q_CHUNK, NA_K_SPAN), F32)],
        compiler_params=_params("arbitrary", "arbitrary"),
        name="attn_na",
    )(rpb.reshape(N_HEADS_A, N_RPB_ROWS * N_RPB_COLS), qkv, qkv, qkv, cache_k, cache_v)


WIN_Q_CHUNK = 256
WIN_K_SPAN = WIN_Q_CHUNK + 2 * WIN_B


def _attn_win_kernel(sink_ref, q_ref, k_ref, v_ref, kc_ref, vc_ref, o_ref):
    j = pl.program_id(1)
    kc = kc_ref[...].astype(BF16)
    vc = vc_ref[...].astype(BF16)
    n_rows = G_B * WIN_Q_CHUNK
    grp = lax.broadcasted_iota(jnp.int32, (n_rows, 1), 0) // WIN_Q_CHUNK
    sink = jnp.zeros((n_rows, 1), F32)
    for g in range(G_B):
        sink = jnp.where(grp == g, sink_ref[j, g], sink)
    for c in range(DEC_SEQ // WIN_Q_CHUNK):
        q0 = c * WIN_Q_CHUNK
        k0 = min(max(q0 - WIN_B, 0), DEC_SEQ - WIN_K_SPAN)
        rows = slice(q0, q0 + WIN_Q_CHUNK)
        keys = slice(k0, k0 + WIN_K_SPAN)
        q = jnp.concatenate([q_ref[rows, g * HEAD_DIM:(g + 1) * HEAD_DIM] for g in range(G_B)], axis=0)
        qpos = q0 + lax.broadcasted_iota(jnp.int32, (n_rows, WIN_K_SPAN), 0) % WIN_Q_CHUNK
        kpos = k0 + lax.broadcasted_iota(jnp.int32, (n_rows, WIN_K_SPAN), 1)
        s = jnp.where(jnp.abs(qpos - kpos) <= WIN_B, _qkt(q, k_ref[keys, :]) * SCALE, NEG)
        sc = _qkt(q, kc) * SCALE
        m = jnp.maximum(jnp.maximum(jnp.max(s, axis=-1, keepdims=True),
                                    jnp.max(sc, axis=-1, keepdims=True)), sink)
        p = jnp.exp(s - m)
        pc = jnp.exp(sc - m)
        l = jnp.sum(p, axis=-1, keepdims=True) + jnp.sum(pc, axis=-1, keepdims=True) + jnp.exp(sink - m)
        o = (jnp.dot(p.astype(BF16), v_ref[keys, :], preferred_element_type=F32)
             + jnp.dot(pc.astype(BF16), vc, preferred_element_type=F32)) / l
        for g in range(G_B):
            o_ref[rows, g * HEAD_DIM:(g + 1) * HEAD_DIM] = (
                o[g * WIN_Q_CHUNK:(g + 1) * WIN_Q_CHUNK, :].astype(BF16))


def _attn_win(sink, qkv, cache_k, cache_v):
    gw = G_B * HEAD_DIM
    ctx_spec = pl.BlockSpec((None, PAST_LEN, HEAD_DIM), lambda b, j: (b, 0, j))
    return pl.pallas_call(
        _attn_win_kernel,
        grid=(DEC_BATCH, N_KV_B),
        in_specs=[
            pl.BlockSpec(memory_space=pltpu.SMEM),
            pl.BlockSpec((DEC_SEQ, gw), lambda b, j: (b, COL_QB // gw + j)),
            pl.BlockSpec((DEC_SEQ, HEAD_DIM), lambda b, j: (b, COL_KB // HEAD_DIM + j)),
            pl.BlockSpec((DEC_SEQ, HEAD_DIM), lambda b, j: (b, COL_VB // HEAD_DIM + j)),
            ctx_spec,
            ctx_spec,
        ],
        out_specs=pl.BlockSpec((DEC_SEQ, gw), lambda b, j: (b, j)),
        out_shape=jax.ShapeDtypeStruct((N_SAMPLE, QB_W), BF16),
        compiler_params=_params("arbitrary", "arbitrary"),
        name="attn_win",
    )(sink, qkv, qkv, qkv, cache_k, cache_v)


REC_CB = 512
REC_PAD = 8
REC_ROWS = 256


def _softplus(x):
    return jnp.maximum(x, 0.0) + jnp.log1p(jnp.exp(-jnp.abs(x)))


def _gelu_tanh(x):
    return 0.5 * x * (1.0 + jnp.tanh(np.sqrt(2.0 / np.pi) * (x + 0.044715 * (x * x * x))))


def _sigmoid(x):
    return 1.0 / (1.0 + jnp.exp(-x))


def _rec_kernel(x_ref, g_ref, cw_ref, cb_ref, wg_ref, bg_ref, lam_ref, h0_ref, y_ref, st_ref,
                xp_ref, af_ref, bf_ref, ab_ref, bb_ref, *, T):
    a_refs = (af_ref, ab_ref)
    b_refs = (bf_ref, bb_ref)
    zeros = jnp.zeros((REC_PAD, REC_CB), F32)
    xp_ref[0:REC_PAD, :] = zeros
    xp_ref[T + REC_PAD:T + 2 * REC_PAD, :] = zeros
    xp_ref[REC_PAD:T + REC_PAD, :] = x_ref[...]
    spl = _softplus(-lam_ref[...])

    def gate_rows(r, carry):
        r0 = pl.multiple_of(r * REC_ROWS, REC_ROWS)
        rows = pl.ds(r0, REC_ROWS)
        n_win = REC_ROWS + 2 * REC_PAD
        win = xp_ref[pl.ds(r0, n_win), :]
        xc = cb_ref[...] + jnp.zeros((REC_ROWS, REC_CB), F32)
        for k in range(CONV_W):
            tap = pltpu.roll(win, n_win - (REC_PAD - CONV_PAD_L + k), 0)[:REC_ROWS, :]
            xc = xc + cw_ref[k:k + 1, :] * tap
        for n in range(REC_CB // RG_BLOCK):
            cols = slice(n * RG_BLOCK, (n + 1) * RG_BLOCK)
            xn = xc[:, cols]
            gates = jnp.dot(xn.astype(BF16), wg_ref[n].astype(BF16), preferred_element_type=F32)
            for d in range(2):
                ga = gates[:, (2 * d) * RG_BLOCK:(2 * d + 1) * RG_BLOCK] + bg_ref[2 * d:2 * d + 1, cols]
                gx = gates[:, (2 * d + 1) * RG_BLOCK:(2 * d + 2) * RG_BLOCK] + bg_ref[2 * d + 1:2 * d + 2, cols]
                log_a = (-RG_C) * _sigmoid(ga) * spl[d:d + 1, cols]
                a = jnp.exp(log_a)
                a_refs[d][rows, cols] = a
                b_refs[d][rows, cols] = jnp.sqrt(1.0 - a * a) * _sigmoid(gx) * xn
        return carry

    lax.fori_loop(0, T // REC_ROWS, gate_rows, 0)

    def scan_step(t, carry):
        hf, hb = carry
        rf = pl.ds(t, 1)
        hf = af_ref[rf, :] * hf + bf_ref[rf, :]
        bf_ref[rf, :] = hf
        rb = pl.ds(T - 1 - t, 1)
        hb = ab_ref[rb, :] * hb + bb_ref[rb, :]
        bb_ref[rb, :] = hb
        return hf, hb

    hf, hb = lax.fori_loop(0, T, scan_step, (h0_ref[0:1, :], h0_ref[1:2, :]), unroll=8)
    st_ref[0:1, :] = hf
    st_ref[1:2, :] = hb

    def out_rows(r, carry):
        rows = pl.ds(pl.multiple_of(r * REC_ROWS, REC_ROWS), REC_ROWS)
        y_ref[rows, :] = ((bf_ref[rows, :] + bb_ref[rows, :]) * _gelu_tanh(g_ref[rows, :])).astype(BF16)
        return carry

    lax.fori_loop(0, T // REC_ROWS, out_rows, 0)


def _rec(xg, n_seq, T, cw, cb, wg, bg, lam, h0):
    nc = D_RNN // REC_CB
    nb = REC_CB // RG_BLOCK
    vec = lambda rows: pl.BlockSpec((rows, REC_CB), lambda s, c: (0, c))
    return pl.pallas_call(
        functools.partial(_rec_kernel, T=T),
        grid=(n_seq, nc),
        in_specs=[
            pl.BlockSpec((T, REC_CB), lambda s, c: (s, c)),
            pl.BlockSpec((T, REC_CB), lambda s, c: (s, nc + c)),
            vec(CONV_W),
            vec(1),
            pl.BlockSpec((nb, RG_BLOCK, 4 * RG_BLOCK), lambda s, c: (c, 0, 0)),
            vec(4),
            vec(2),
            pl.BlockSpec((None, 2, REC_CB), lambda s, c: (s, 0, c)),
        ],
        out_specs=[
            pl.BlockSpec((T, REC_CB), lambda s, c: (s, c)),
            pl.BlockSpec((None, 2, REC_CB), lambda s, c: (s, 0, c)),
        ],
        out_shape=[
            jax.ShapeDtypeStruct((n_seq * T, D_RNN), BF16),
            jax.ShapeDtypeStruct((n_seq, 2, D_RNN), F32),
        ],
        scratch_shapes=[pltpu.VMEM((T + 2 * REC_PAD, REC_CB), F32)] + [pltpu.VMEM((T, REC_CB), F32)] * 4,
        compiler_params=_params("arbitrary", "arbitrary"),
        name="rec",
    )(xg, xg, cw, cb.reshape(1, D_RNN), wg, bg, lam, h0)


def kernel(x_prompt, x_sample, c, cache_a_k, cache_a_v, cache_b_k, cache_b_v, state_rg_fwd, state_rg_bwd, c_ctx, w_ada, b_ada, g_pre_mix, g_post_mix, g_pre_ffn, g_post_ffn, w_att_in, w_att_out, sink_b, rpb_a, w_rec_in, conv_w, conv_b, w_rg_a, b_rg_a, w_rg_x, b_rg_x, rg_lambda, w_rec_out, w_ff1, w_ff2):
    depth = w_ada.shape[0]
    yp = x_prompt.reshape(N_PROMPT, D_MODEL)
    ys = x_sample.reshape(N_SAMPLE, D_MODEL)

    cond8 = jnp.concatenate([c_ctx[None, :], c, jnp.zeros((MOD_ROWS - N_SEG, D_MODEL), F32)], axis=0)
    mod_all = _adaln(cond8, w_ada, b_ada)
    mod_all = mod_all[:, :N_SEG, :].reshape(depth, N_SEG, 6, D_MODEL)
    mod_all = jnp.pad(mod_all, ((0, 0), (0, 0), (0, MOD_ROWS - 6), (0, 0)))

    a_k, a_v, b_k, b_v, s_f, s_b = [], [], [], [], [], []
    for layer in range(depth):
        mod = mod_all[layer]
        li = layer // 2
        g_pre, g_post = g_pre_mix[layer], g_post_mix[layer]
        if layer % 2 == 0:
            qkv_p, ka, va, kb, vb = _proj_att_prompt(yp, g_pre, mod, w_att_in, li)
            qkv_s = _proj_att_sample(ys, g_pre, mod, w_att_in, li)
            a_k.append(ka.reshape(BATCH, SEQ, N_HEADS_A, HEAD_DIM))
            a_v.append(va.reshape(BATCH, SEQ, N_HEADS_A, HEAD_DIM))
            b_k.append(kb.reshape(BATCH, SEQ, N_KV_B, HEAD_DIM))
            b_v.append(vb.reshape(BATCH, SEQ, N_KV_B, HEAD_DIM))
            mix_p = [_attn_ctx(sink_b[li], qkv_p)]
            mix_s = [_attn_na(qkv_s, rpb_a[li],
                              cache_a_k[:, li].reshape(DEC_BATCH, PAST_LEN, QA_W),
                              cache_a_v[:, li].reshape(DEC_BATCH, PAST_LEN, QA_W)),
                     _attn_win(sink_b[li], qkv_s,
                               cache_b_k[:, li].reshape(DEC_BATCH, PAST_LEN, KVB_W),
                               cache_b_v[:, li].reshape(DEC_BATCH, PAST_LEN, KVB_W))]
            w_out = w_att_out
        else:
            xg_p = _proj(PROMPT, yp, g_pre, mod, w_rec_in, li)
            xg_s = _proj(SAMPLE, ys, g_pre, mod, w_rec_in, li)
            wg = jnp.concatenate([w_rg_a[li, 0], w_rg_x[li, 0], w_rg_a[li, 1], w_rg_x[li, 1]], axis=-1)
            bg = jnp.stack([b_rg_a[li, 0], b_rg_x[li, 0], b_rg_a[li, 1], b_rg_x[li, 1]], axis=0)
            rec_args = (conv_w[li], conv_b[li], wg, bg, rg_lambda[li])
            h0_p = jnp.zeros((BATCH, 2, D_RNN), F32)
            h0_s = jnp.stack([state_rg_fwd[:, li], state_rg_bwd[:, li]], axis=1)
            rec_p, st_p = _rec(xg_p, BATCH, SEQ, *rec_args, h0_p)
            rec_s, _ = _rec(xg_s, DEC_BATCH, DEC_SEQ, *rec_args, h0_s)
            mix_p, mix_s = [rec_p], [rec_s]
            s_f.append(st_p[:, 0])
            s_b.append(st_p[:, 1])
            w_out = w_rec_out
        yp = _mixout(PROMPT, mix_p, w_out, li, yp, g_post, mod)
        ys = _mixout(SAMPLE, mix_s, w_out, li, ys, g_post, mod)
        yp = _ffn(PROMPT, yp, g_pre_ffn[layer], g_post_ffn[layer], mod, w_ff1, w_ff2, layer)
        ys = _ffn(SAMPLE, ys, g_pre_ffn[layer], g_post_ffn[layer], mod, w_ff1, w_ff2, layer)

    return (yp.reshape(BATCH, SEQ, D_MODEL), ys.reshape(DEC_BATCH, DEC_SEQ, D_MODEL),
            jnp.stack(a_k, axis=1), jnp.stack(a_v, axis=1), jnp.stack(b_k, axis=1), jnp.stack(b_v, axis=1),
            jnp.stack(s_f, axis=1), jnp.stack(s_b, axis=1))
```

```python
import functools

import jax
import jax.numpy as jnp
import numpy as np
from jax import lax
from jax.experimental import pallas as pl
from jax.experimental.pallas import tpu as pltpu

D_MODEL = 2048
BATCH = 16
SEQ = 256
DEC_BATCH = 2
DEC_SEQ = 1024
PAST_LEN = 256
GRID_W = 64
GRID_R = DEC_SEQ // GRID_W
HEAD_DIM = 128
N_HEADS_A = 8
N_HEADS_B = 8
N_KV_B = 2
G_B = N_HEADS_B // N_KV_B
NA_ROWS = 8
NA_COLS = 16
WIN_B = 128
D_RNN = D_MODEL
N_RG_BLOCKS = 16
RG_BLOCK = D_RNN // N_RG_BLOCKS
CONV_W = 4
CONV_PAD_L = 2
RG_C = 8.0
D_FF = 4 * D_MODEL
ROPE_BASE = 10000.0
EPS = 1e-6
NEG = -1e30
QA_W = N_HEADS_A * HEAD_DIM
QB_W = N_HEADS_B * HEAD_DIM
KVB_W = N_KV_B * HEAD_DIM
D_ATT_IN = 3 * QA_W + QB_W + 2 * KVB_W
SCALE = HEAD_DIM ** -0.5

N_PROMPT = BATCH * SEQ
N_SAMPLE = DEC_BATCH * DEC_SEQ
N_SEG = 1 + DEC_BATCH
MOD_ROWS = 8

COL_QA, COL_KA, COL_VA = 0, QA_W, 2 * QA_W
COL_QB = 3 * QA_W
COL_KB = COL_QB + QB_W
COL_VB = COL_KB + KVB_W

V7X_VMEM_BYTES = 64 * 1024 * 1024
VMEM_LIMIT = V7X_VMEM_BYTES - 4 * 1024 * 1024

TM = 1024
TN = 512
ROW_CHUNK = 128

F32 = jnp.float32
BF16 = jnp.bfloat16


def _params(*sem):
    return pltpu.CompilerParams(dimension_semantics=sem, vmem_limit_bytes=VMEM_LIMIT)


class _Rows:
    def __init__(self, n_rows, seg0, seg_rows):
        self.n = n_rows
        self.seg0 = seg0
        self.seg_rows = seg_rows

    def seg(self, i, tm):
        return self.seg0 + (i * tm) // self.seg_rows


PROMPT = _Rows(N_PROMPT, 0, N_PROMPT)
SAMPLE = _Rows(N_SAMPLE, 1, DEC_SEQ)


def _mod_spec(rows, tm):
    return pl.BlockSpec((None, MOD_ROWS, D_MODEL), lambda i, j: (rows.seg(i, tm), 0, 0))


def _vec_spec():
    return pl.BlockSpec((1, D_MODEL), lambda i, j: (0, 0))


def _rms_scale(x):
    return lax.rsqrt(jnp.mean(x * x, axis=-1, keepdims=True) + EPS)


def _norm_mod_rows(y_ref, g_ref, mod_ref, h_ref, shift_row, tm, h_row0=0):
    g = g_ref[...]
    shift = mod_ref[shift_row:shift_row + 1, :]
    scale1 = 1.0 + mod_ref[shift_row + 1:shift_row + 2, :]

    def body(r, carry):
        r0 = r * ROW_CHUNK
        y = y_ref[pl.ds(pl.multiple_of(r0, ROW_CHUNK), ROW_CHUNK), :]
        h = (y * _rms_scale(y) * g) * scale1 + shift
        h_ref[pl.ds(pl.multiple_of(h_row0 + r0, ROW_CHUNK), ROW_CHUNK), :] = h.astype(BF16)
        return carry

    lax.fori_loop(0, tm // ROW_CHUNK, body, 0)


def _gated_residual_rows(y_ref, o_ref, g_ref, gate, tm):
    g = g_ref[...]

    def body(r, carry):
        rows = pl.ds(pl.multiple_of(r * ROW_CHUNK, ROW_CHUNK), ROW_CHUNK)
        o = o_ref[rows, :]
        o_ref[rows, :] = y_ref[rows, :] + gate * (o * _rms_scale(o) * g)
        return carry

    lax.fori_loop(0, tm // ROW_CHUNK, body, 0)


def _adaln_kernel(cond_ref, w_ref, b_ref, o_ref):
    c = cond_ref[...]
    s = c / (1.0 + jnp.exp(-c))
    o_ref[...] = jnp.dot(s.astype(BF16), w_ref[...].astype(BF16),
                         preferred_element_type=F32) + b_ref[...]


def _adaln(cond8, w_ada, b_ada):
    depth = w_ada.shape[0]
    n = w_ada.shape[2]
    tn = 1024
    return pl.pallas_call(
        _adaln_kernel,
        grid=(depth, n // tn),
        in_specs=[
            pl.BlockSpec((MOD_ROWS, D_MODEL), lambda l, j: (0, 0)),
            pl.BlockSpec((None, D_MODEL, tn), lambda l, j: (l, 0, j)),
            pl.BlockSpec((None, 1, tn), lambda l, j: (l, 0, j)),
        ],
        out_specs=pl.BlockSpec((None, MOD_ROWS, tn), lambda l, j: (l, 0, j)),
        out_shape=jax.ShapeDtypeStruct((depth, MOD_ROWS, n), F32),
        compiler_params=_params("arbitrary", "arbitrary"),
        name="adaln",
    )(cond8, w_ada, b_ada.reshape(depth, 1, n))


def _rope_tables():
    t = np.arange(DEC_SEQ)
    half = HEAD_DIM // 2
    inv = ROPE_BASE ** (-np.arange(0, half, 2, dtype=np.float64) / half)
    ang_r = (t // GRID_W)[:, None] * inv[None, :]
    ang_c = (t % GRID_W)[:, None] * inv[None, :]
    cos = np.concatenate([np.cos(ang_r), np.cos(ang_r), np.cos(ang_c), np.cos(ang_c)], axis=1)
    sin = np.concatenate([-np.sin(ang_r), np.sin(ang_r), -np.sin(ang_c), np.sin(ang_c)], axis=1)
    return jnp.asarray(cos, F32), jnp.asarray(sin, F32)


def _rope(x, cos, sin_signed):
    quarter = HEAD_DIM // 4
    lane = lax.broadcasted_iota(jnp.int32, x.shape, 1)
    first = (lane % (2 * quarter)) < quarter
    partner = jnp.where(first, pltpu.roll(x, HEAD_DIM - quarter, 1), pltpu.roll(x, quarter, 1))
    return x * cos + partner * sin_signed


def _proj_h_tile(y_ref, g_ref, mod_ref, h_ref, tm):
    row0 = pl.multiple_of(pl.program_id(1) * tm, tm)

    @pl.when(pl.program_id(0) == 0)
    def _():
        _norm_mod_rows(y_ref, g_ref, mod_ref, h_ref, 0, tm, row0)

    return h_ref[pl.ds(row0, tm), :]


def _proj_in_specs(rows, tm, layer):
    last = rows.n // tm - 1

    def tile(j, i):
        return jnp.where(j == 0, i, last)

    return [
        pl.BlockSpec((tm, D_MODEL), lambda j, i: (tile(j, i), 0)),
        pl.BlockSpec((1, D_MODEL), lambda j, i: (0, 0)),
        pl.BlockSpec((None, MOD_ROWS, D_MODEL), lambda j, i: (rows.seg(tile(j, i), tm), 0, 0)),
        pl.BlockSpec((None, D_MODEL, TN), lambda j, i: (layer, 0, j)),
    ]


def _proj_kernel(y_ref, g_ref, mod_ref, w_ref, o_ref, h_ref, *, tm):
    h = _proj_h_tile(y_ref, g_ref, mod_ref, h_ref, tm)
    o_ref[...] = jnp.dot(h, w_ref[...].astype(BF16), preferred_element_type=F32)


def _proj(rows, y, g, mod, w, layer):
    n = w.shape[2]
    tm = TM
    return pl.pallas_call(
        functools.partial(_proj_kernel, tm=tm),
        grid=(n // TN, rows.n // tm),
        in_specs=_proj_in_specs(rows, tm, layer),
        out_specs=pl.BlockSpec((tm, TN), lambda j, i: (i, j)),
        out_shape=jax.ShapeDtypeStruct((rows.n, n), F32),
        scratch_shapes=[pltpu.VMEM((rows.n, D_MODEL), BF16)],
        compiler_params=_params("arbitrary", "arbitrary"),
        name="proj",
    )(y, g.reshape(1, D_MODEL), mod, w)


ATT_TILE_KA = COL_KA // TN
ATT_TILE_VA = COL_VA // TN
ATT_TILE_QB = COL_QB // TN
ATT_TILE_KVB = COL_KB // TN
N_ATT_TILES = D_ATT_IN // TN


def _proj_att_prompt_kernel(y_ref, g_ref, mod_ref, w_ref, o_ref, ka_ref, va_ref, kb_ref, vb_ref, h_ref, *, tm):
    j = pl.program_id(0)
    h = _proj_h_tile(y_ref, g_ref, mod_ref, h_ref, tm)
    acc = jnp.dot(h, w_ref[...].astype(BF16), preferred_element_type=F32)
    o_ref[...] = acc.astype(BF16)

    @pl.when((j >= ATT_TILE_KA) & (j < ATT_TILE_VA))
    def _():
        ka_ref[...] = acc

    @pl.when((j >= ATT_TILE_VA) & (j < ATT_TILE_QB))
    def _():
        va_ref[...] = acc

    @pl.when(j == ATT_TILE_KVB)
    def _():
        kb_ref[...] = acc[:, :KVB_W]
        vb_ref[...] = acc[:, KVB_W:]


def _proj_att_prompt(y, g, mod, w, layer):
    tm = TM
    rows = PROMPT
    last = rows.n // tm - 1

    def kv_map(first, count):
        def index(j, i):
            row = jnp.where(j < first, 0, jnp.where(j >= first + count, last, i))
            return row, jnp.clip(j - first, 0, count - 1)
        return index

    per_head_set = QA_W // TN
    return pl.pallas_call(
        functools.partial(_proj_att_prompt_kernel, tm=tm),
        grid=(N_ATT_TILES, rows.n // tm),
        in_specs=_proj_in_specs(rows, tm, layer),
        out_specs=[
            pl.BlockSpec((tm, TN), lambda j, i: (i, j)),
            pl.BlockSpec((tm, TN), kv_map(ATT_TILE_KA, per_head_set)),
            pl.BlockSpec((tm, TN), kv_map(ATT_TILE_VA, per_head_set)),
            pl.BlockSpec((tm, KVB_W), kv_map(ATT_TILE_KVB, 1)),
            pl.BlockSpec((tm, KVB_W), kv_map(ATT_TILE_KVB, 1)),
        ],
        out_shape=[
            jax.ShapeDtypeStruct((rows.n, D_ATT_IN), BF16),
            jax.ShapeDtypeStruct((rows.n, QA_W), F32),
            jax.ShapeDtypeStruct((rows.n, QA_W), F32),
            jax.ShapeDtypeStruct((rows.n, KVB_W), F32),
            jax.ShapeDtypeStruct((rows.n, KVB_W), F32),
        ],
        scratch_shapes=[pltpu.VMEM((rows.n, D_MODEL), BF16)],
        compiler_params=_params("arbitrary", "arbitrary"),
        name="proj_att_prompt",
    )(y, g.reshape(1, D_MODEL), mod, w)


def _proj_att_sample_kernel(y_ref, g_ref, mod_ref, w_ref, cos_ref, sin_ref, o_ref, h_ref, *, tm):
    j = pl.program_id(0)
    h = _proj_h_tile(y_ref, g_ref, mod_ref, h_ref, tm)
    acc = jnp.dot(h, w_ref[...].astype(BF16), preferred_element_type=F32)

    def store(n_rope_heads):
        cos = cos_ref[...]
        sin = sin_ref[...]
        for hd in range(TN // HEAD_DIM):
            cols = slice(hd * HEAD_DIM, (hd + 1) * HEAD_DIM)
            x = acc[:, cols]
            if hd < n_rope_heads:
                x = _rope(x, cos, sin)
            o_ref[:, cols] = x.astype(BF16)

    @pl.when(j < ATT_TILE_QB)
    def _():
        o_ref[...] = acc.astype(BF16)

    @pl.when((j >= ATT_TILE_QB) & (j < ATT_TILE_KVB))
    def _():
        store(TN // HEAD_DIM)

    @pl.when(j == ATT_TILE_KVB)
    def _():
        store(N_KV_B)


def _proj_att_sample(y, g, mod, w, layer):
    tm = TM
    assert tm == DEC_SEQ
    rows = SAMPLE
    cos, sin = _rope_tables()
    tab_spec = pl.BlockSpec((DEC_SEQ, HEAD_DIM), lambda j, i: (0, 0))
    return pl.pallas_call(
        functools.partial(_proj_att_sample_kernel, tm=tm),
        grid=(N_ATT_TILES, rows.n // tm),
        in_specs=_proj_in_specs(rows, tm, layer) + [tab_spec, tab_spec],
        out_specs=pl.BlockSpec((tm, TN), lambda j, i: (i, j)),
        out_shape=jax.ShapeDtypeStruct((rows.n, D_ATT_IN), BF16),
        scratch_shapes=[pltpu.VMEM((rows.n, D_MODEL), BF16)],
        compiler_params=_params("arbitrary", "arbitrary"),
        name="proj_att_sample",
    )(y, g.reshape(1, D_MODEL), mod, w, cos, sin)


MIXOUT_TM = 512


def _mixout_kernel(*refs, tm, n_parts):
    a_refs = refs[:n_parts]
    w_ref, y_ref, g_ref, mod_ref, o_ref = refs[n_parts:]
    kp = a_refs[0].shape[1]
    for c in range(D_MODEL // TN):
        cols = slice(c * TN, (c + 1) * TN)
        acc = None
        for p in range(n_parts):
            part = jnp.dot(a_refs[p][...], w_ref[p * kp:(p + 1) * kp, cols].astype(BF16),
                           preferred_element_type=F32)
            acc = part if acc is None else acc + part
        o_ref[:, cols] = acc
    _gated_residual_rows(y_ref, o_ref, g_ref, mod_ref[2:3, :], tm)


def _mixout(rows, a_parts, w, layer, y, g, mod):
    tm = MIXOUT_TM
    n_parts = len(a_parts)
    kp = a_parts[0].shape[1]
    assert n_parts * kp == w.shape[1]
    return pl.pallas_call(
        functools.partial(_mixout_kernel, tm=tm, n_parts=n_parts),
        grid=(rows.n // tm,),
        in_specs=[pl.BlockSpec((tm, kp), lambda i: (i, 0)) for _ in range(n_parts)] + [
            pl.BlockSpec((None, w.shape[1], D_MODEL), lambda i: (layer, 0, 0), pipeline_mode=pl.Buffered(1)),
            pl.BlockSpec((tm, D_MODEL), lambda i: (i, 0)),
            pl.BlockSpec((1, D_MODEL), lambda i: (0, 0)),
            pl.BlockSpec((None, MOD_ROWS, D_MODEL), lambda i: (rows.seg(i, tm), 0, 0)),
        ],
        out_specs=pl.BlockSpec((tm, D_MODEL), lambda i: (i, 0)),
        out_shape=jax.ShapeDtypeStruct((rows.n, D_MODEL), F32),
        compiler_params=_params("arbitrary"),
        name="mixout",
    )(*a_parts, w, y, g.reshape(1, D_MODEL), mod)


FFN_TF = 256


def _ffn_kernel(y_ref, g1_ref, g2_ref, mod_ref, w1_ref, w2a_ref, w2b_ref, o_ref, h_ref, a0_ref, a1_ref, *, tm, ns):
    s = pl.program_id(1)

    @pl.when(s == 0)
    def _():
        _norm_mod_rows(y_ref, g1_ref, mod_ref, h_ref, 3, tm)
        o_ref[...] = jnp.zeros_like(o_ref)
        a1_ref[...] = jnp.zeros_like(a1_ref)

    def up(half):
        cols = slice(half * FFN_TF, (half + 1) * FFN_TF)
        a = jnp.dot(h_ref[...], w1_ref[:, cols].astype(BF16), preferred_element_type=F32)
        a = jnp.maximum(a, 0.0)
        return (a * a).astype(BF16)

    def down(a_ref, w_ref):
        a = a_ref[...]
        for c in range(D_MODEL // TN):
            cols = slice(c * TN, (c + 1) * TN)
            o_ref[:, cols] += jnp.dot(a, w_ref[:, cols].astype(BF16), preferred_element_type=F32)

    @pl.when(s < ns)
    def _():
        a0_ref[...] = up(0)
        down(a1_ref, w2a_ref)
        a1_next = up(1)
        down(a0_ref, w2b_ref)
        a1_ref[...] = a1_next

    @pl.when(s == ns)
    def _():
        down(a1_ref, w2a_ref)
        _gated_residual_rows(y_ref, o_ref, g2_ref, mod_ref[5:6, :], tm)


def _ffn(rows, y, g1, g2, mod, w1, w2, layer):
    tm = TM
    ns = D_FF // (2 * FFN_TF)
    last = D_FF // FFN_TF - 1
    return pl.pallas_call(
        functools.partial(_ffn_kernel, tm=tm, ns=ns),
        grid=(rows.n // tm, ns + 1),
        in_specs=[
            pl.BlockSpec((tm, D_MODEL), lambda i, s: (i, 0)),
            _vec_spec(),
            _vec_spec(),
            _mod_spec(rows, tm),
            pl.BlockSpec((None, D_MODEL, 2 * FFN_TF), lambda i, s: (layer, 0, jnp.minimum(s, ns - 1))),
            pl.BlockSpec((None, FFN_TF, D_MODEL), lambda i, s: (layer, jnp.maximum(2 * s - 1, 0), 0)),
            pl.BlockSpec((None, FFN_TF, D_MODEL), lambda i, s: (layer, jnp.minimum(2 * s, last), 0)),
        ],
        out_specs=pl.BlockSpec((tm, D_MODEL), lambda i, s: (i, 0)),
        out_shape=jax.ShapeDtypeStruct((rows.n, D_MODEL), F32),
        scratch_shapes=[pltpu.VMEM((tm, D_MODEL), BF16), pltpu.VMEM((tm, FFN_TF), BF16),
                        pltpu.VMEM((tm, FFN_TF), BF16)],
        compiler_params=_params("arbitrary", "arbitrary"),
        name="ffn",
    )(y, g1.reshape(1, D_MODEL), g2.reshape(1, D_MODEL), mod, w1, w2, w2)


def _qkt(q, k):
    return lax.dot_general(q, k, (((1,), (1,)), ((), ())), preferred_element_type=F32)


def _head(ref, col):
    return ref[:, col:col + HEAD_DIM]


def _attn_ctx_kernel(sink_ref, qkv_ref, o_ref):
    for h in range(N_HEADS_A):
        q = _head(qkv_ref, COL_QA + h * HEAD_DIM)
        k = _head(qkv_ref, COL_KA + h * HEAD_DIM)
        v = _head(qkv_ref, COL_VA + h * HEAD_DIM)
        s = _qkt(q, k) * SCALE
        m = jnp.max(s, axis=-1, keepdims=True)
        p = jnp.exp(s - m)
        l = jnp.sum(p, axis=-1, keepdims=True)
        o = jnp.dot(p.astype(BF16), v, preferred_element_type=F32) / l
        o_ref[:, h * HEAD_DIM:(h + 1) * HEAD_DIM] = o.astype(BF16)
    for j in range(N_KV_B):
        k = _head(qkv_ref, COL_KB + j * HEAD_DIM)
        v = _head(qkv_ref, COL_VB + j * HEAD_DIM)
        for g in range(G_B):
            hq = j * G_B + g
            q = _head(qkv_ref, COL_QB + hq * HEAD_DIM)
            sink = sink_ref[j, g]
            s = _qkt(q, k) * SCALE
            m = jnp.maximum(jnp.max(s, axis=-1, keepdims=True), sink)
            p = jnp.exp(s - m)
            l = jnp.sum(p, axis=-1, keepdims=True) + jnp.exp(sink - m)
            o = jnp.dot(p.astype(BF16), v, preferred_element_type=F32) / l
            o_ref[:, QA_W + hq * HEAD_DIM:QA_W + (hq + 1) * HEAD_DIM] = o.astype(BF16)


def _attn_ctx(sink, qkv):
    return pl.pallas_call(
        _attn_ctx_kernel,
        grid=(BATCH,),
        in_specs=[
            pl.BlockSpec(memory_space=pltpu.SMEM),
            pl.BlockSpec((SEQ, D_ATT_IN), lambda b: (b, 0)),
        ],
        out_specs=pl.BlockSpec((SEQ, D_MODEL), lambda b: (b, 0)),
        out_shape=jax.ShapeDtypeStruct((N_PROMPT, D_MODEL), BF16),
        compiler_params=_params("arbitrary"),
        name="attn_ctx",
    )(sink, qkv)


NA_Q_ROWS = 4
NA_K_ROWS = 12
NA_Q_CHUNK = NA_Q_ROWS * GRID_W
NA_K_SPAN = NA_K_ROWS * GRID_W
NA_K_ROW0 = (0, 0, 4, 4)
N_RPB_ROWS = 2 * NA_ROWS - 1
N_RPB_COLS = 2 * NA_COLS - 1


def _na_row_start(r):
    return min(max(r - NA_ROWS // 2, 0), GRID_R - NA_ROWS)


def _na_build_bias(rpb_ref, tile_ref, bias_ref):
    h = pl.program_id(0)
    shape = (GRID_W, 2 * GRID_W)
    qc = lax.broadcasted_iota(jnp.int32, shape, 0)
    lane = lax.broadcasted_iota(jnp.int32, shape, 1)
    kc = lane % GRID_W
    start_c = jnp.clip(qc - NA_COLS // 2, 0, GRID_W - NA_COLS)
    in_win = (kc >= start_c) & (kc < start_c + NA_COLS)
    dc = jnp.where(in_win, kc - qc + (NA_COLS - 1), -1)
    for dr in range(N_RPB_ROWS):
        t = jnp.full(shape, NEG, F32)
        for c in range(N_RPB_COLS):
            t = jnp.where(dc == c, rpb_ref[h, dr * N_RPB_COLS + c], t)
        tile_ref[dr] = t
    first_half = lane < GRID_W
    neg = jnp.full(shape, NEG, F32)
    for chunk in range(GRID_R // NA_Q_ROWS):
        for qi in range(NA_Q_ROWS):
            qr = chunk * NA_Q_ROWS + qi
            lo = _na_row_start(qr)
            for m in range(NA_K_ROWS // 2):
                kr = NA_K_ROW0[chunk] + 2 * m
                parts = []
                for r in (kr, kr + 1):
                    parts.append(tile_ref[r - qr + NA_ROWS - 1] if lo <= r < lo + NA_ROWS else neg)
                bias_ref[chunk, qi * GRID_W:(qi + 1) * GRID_W, m * 2 * GRID_W:(m + 1) * 2 * GRID_W] = (
                    jnp.where(first_half, parts[0], parts[1]))


def _attn_na_kernel(rpb_ref, q_ref, k_ref, v_ref, kc_ref, vc_ref, o_ref, tile_ref, bias_ref):
    @pl.when(pl.program_id(1) == 0)
    def _():
        _na_build_bias(rpb_ref, tile_ref, bias_ref)

    kc = kc_ref[...].astype(BF16)
    vc = vc_ref[...].astype(BF16)
    for c in range(DEC_SEQ // NA_Q_CHUNK):
        rows = slice(c * NA_Q_CHUNK, (c + 1) * NA_Q_CHUNK)
        keys = slice(NA_K_ROW0[c] * GRID_W, NA_K_ROW0[c] * GRID_W + NA_K_SPAN)
        q = q_ref[rows, :]
        s = _qkt(q, k_ref[keys, :]) * SCALE + bias_ref[c]
        sc = _qkt(q, kc) * SCALE
        m = jnp.maximum(jnp.max(s, axis=-1, keepdims=True), jnp.max(sc, axis=-1, keepdims=True))
        p = jnp.exp(s - m)
        pc = jnp.exp(sc - m)
        l = jnp.sum(p, axis=-1, keepdims=True) + jnp.sum(pc, axis=-1, keepdims=True)
        o = (jnp.dot(p.astype(BF16), v_ref[keys, :], preferred_element_type=F32)
             + jnp.dot(pc.astype(BF16), vc, preferred_element_type=F32)) / l
        o_ref[rows, :] = o.astype(BF16)


def _attn_na(qkv, rpb, cache_k, cache_v):
    def col(c0):
        return lambda h, b: (b, c0 // HEAD_DIM + h)

    ctx_spec = pl.BlockSpec((None, PAST_LEN, HEAD_DIM), lambda h, b: (b, 0, h))
    n_chunks = DEC_SEQ // NA_Q_CHUNK
    return pl.pallas_call(
        _attn_na_kernel,
        grid=(N_HEADS_A, DEC_BATCH),
        in_specs=[
            pl.BlockSpec(memory_space=pltpu.SMEM),
            pl.BlockSpec((DEC_SEQ, HEAD_DIM), col(COL_QA)),
            pl.BlockSpec((DEC_SEQ, HEAD_DIM), col(COL_KA)),
            pl.BlockSpec((DEC_SEQ, HEAD_DIM), col(COL_VA)),
            ctx_spec,
            ctx_spec,
        ],
        out_specs=pl.BlockSpec((DEC_SEQ, HEAD_DIM), lambda h, b: (b, h)),
        out_shape=jax.ShapeDtypeStruct((N_SAMPLE, QA_W), BF16),
        scratch_shapes=[pltpu.VMEM((N_RPB_ROWS, GRID_W, 2 * GRID_W), F32),
                        pltpu.VMEM((n_chunks, NA_Q_CHUNK, NA_K_SPAN), F32)],
        compiler_params=_params("arbitrary", "arbitrary"),
        name="attn_na",
    )(rpb.reshape(N_HEADS_A, N_RPB_ROWS * N_RPB_COLS), qkv, qkv, qkv, cache_k, cache_v)


WIN_Q_CHUNK = 256
WIN_K_SPAN = WIN_Q_CHUNK + 2 * WIN_B


def _attn_win_kernel(sink_ref, q_ref, k_ref, v_ref, kc_ref, vc_ref, o_ref):
    j = pl.program_id(1)
    kc = kc_ref[...].astype(BF16)
    vc = vc_ref[...].astype(BF16)
    n_rows = G_B * WIN_Q_CHUNK
    grp = lax.broadcasted_iota(jnp.int32, (n_rows, 1), 0) // WIN_Q_CHUNK
    sink = jnp.zeros((n_rows, 1), F32)
    for g in range(G_B):
        sink = jnp.where(grp == g, sink_ref[j, g], sink)
    for c in range(DEC_SEQ // WIN_Q_CHUNK):
        q0 = c * WIN_Q_CHUNK
        k0 = min(max(q0 - WIN_B, 0), DEC_SEQ - WIN_K_SPAN)
        rows = slice(q0, q0 + WIN_Q_CHUNK)
        keys = slice(k0, k0 + WIN_K_SPAN)
        q = jnp.concatenate([q_ref[rows, g * HEAD_DIM:(g + 1) * HEAD_DIM] for g in range(G_B)], axis=0)
        qpos = q0 + lax.broadcasted_iota(jnp.int32, (n_rows, WIN_K_SPAN), 0) % WIN_Q_CHUNK
        kpos = k0 + lax.broadcasted_iota(jnp.int32, (n_rows, WIN_K_SPAN), 1)
        s = jnp.where(jnp.abs(qpos - kpos) <= WIN_B, _qkt(q, k_ref[keys, :]) * SCALE, NEG)
        sc = _qkt(q, kc) * SCALE
        m = jnp.maximum(jnp.maximum(jnp.max(s, axis=-1, keepdims=True),
                                    jnp.max(sc, axis=-1, keepdims=True)), sink)
        p = jnp.exp(s - m)
        pc = jnp.exp(sc - m)
        l = jnp.sum(p, axis=-1, keepdims=True) + jnp.sum(pc, axis=-1, keepdims=True) + jnp.exp(sink - m)
        o = (jnp.dot(p.astype(BF16), v_ref[keys, :], preferred_element_type=F32)
             + jnp.dot(pc.astype(BF16), vc, preferred_element_type=F32)) / l
        for g in range(G_B):
            o_ref[rows, g * HEAD_DIM:(g + 1) * HEAD_DIM] = (
                o[g * WIN_Q_CHUNK:(g + 1) * WIN_Q_CHUNK, :].astype(BF16))


def _attn_win(sink, qkv, cache_k, cache_v):
    gw = G_B * HEAD_DIM
    ctx_spec = pl.BlockSpec((None, PAST_LEN, HEAD_DIM), lambda b, j: (b, 0, j))
    return pl.pallas_call(
        _attn_win_kernel,
        grid=(DEC_BATCH, N_KV_B),
        in_specs=[
            pl.BlockSpec(memory_space=pltpu.SMEM),
            pl.BlockSpec((DEC_SEQ, gw), lambda b, j: (b, COL_QB // gw + j)),
            pl.BlockSpec((DEC_SEQ, HEAD_DIM), lambda b, j: (b, COL_KB // HEAD_DIM + j)),
            pl.BlockSpec((DEC_SEQ, HEAD_DIM), lambda b, j: (b, COL_VB // HEAD_DIM + j)),
            ctx_spec,
            ctx_spec,
        ],
        out_specs=pl.BlockSpec((DEC_SEQ, gw), lambda b, j: (b, j)),
        out_shape=jax.ShapeDtypeStruct((N_SAMPLE, QB_W), BF16),
        compiler_params=_params("arbitrary", "arbitrary"),
        name="attn_win",
    )(sink, qkv, qkv, qkv, cache_k, cache_v)


REC_CB = 512
REC_PAD = 8
REC_ROWS = 256


def _softplus(x):
    return jnp.maximum(x, 0.0) + jnp.log1p(jnp.exp(-jnp.abs(x)))


def _gelu_tanh(x):
    return 0.5 * x * (1.0 + jnp.tanh(np.sqrt(2.0 / np.pi) * (x + 0.044715 * (x * x * x))))


def _sigmoid(x):
    return 0.5 * jnp.tanh(0.5 * x) + 0.5


def _sqrt_nonneg(u):
    return jnp.where(u > 0.0, u * lax.rsqrt(u), 0.0)


def _rec_kernel(x_ref, g_ref, cw_ref, cb_ref, wg_ref, bg_ref, lam_ref, h0_ref, y_ref, st_ref,
                xp_ref, af_ref, bf_ref, ab_ref, bb_ref, *, T):
    a_refs = (af_ref, ab_ref)
    b_refs = (bf_ref, bb_ref)
    zeros = jnp.zeros((REC_PAD, REC_CB), F32)
    xp_ref[0:REC_PAD, :] = zeros
    xp_ref[T + REC_PAD:T + 2 * REC_PAD, :] = zeros
    xp_ref[REC_PAD:T + REC_PAD, :] = x_ref[...]
    spl = _softplus(-lam_ref[...])

    def gate_rows(r, carry):
        r0 = pl.multiple_of(r * REC_ROWS, REC_ROWS)
        rows = pl.ds(r0, REC_ROWS)
        n_win = REC_ROWS + 2 * REC_PAD
        win = xp_ref[pl.ds(r0, n_win), :]
        xc = cb_ref[...] + jnp.zeros((REC_ROWS, REC_CB), F32)
        for k in range(CONV_W):
            tap = pltpu.roll(win, n_win - (REC_PAD - CONV_PAD_L + k), 0)[:REC_ROWS, :]
            xc = xc + cw_ref[k:k + 1, :] * tap
        for n in range(REC_CB // RG_BLOCK):
            cols = slice(n * RG_BLOCK, (n + 1) * RG_BLOCK)
            xn = xc[:, cols]
            gates = jnp.dot(xn.astype(BF16), wg_ref[n].astype(BF16), preferred_element_type=F32)
            for d in range(2):
                ga = gates[:, (2 * d) * RG_BLOCK:(2 * d + 1) * RG_BLOCK] + bg_ref[2 * d:2 * d + 1, cols]
                gx = gates[:, (2 * d + 1) * RG_BLOCK:(2 * d + 2) * RG_BLOCK] + bg_ref[2 * d + 1:2 * d + 2, cols]
                log_a = (-RG_C) * _sigmoid(ga) * spl[d:d + 1, cols]
                a = jnp.exp(log_a)
                a_refs[d][rows, cols] = a
                b_refs[d][rows, cols] = _sqrt_nonneg(1.0 - a * a) * _sigmoid(gx) * xn
        return carry

    lax.fori_loop(0, T // REC_ROWS, gate_rows, 0)

    def scan_step(t, carry):
        hf, hb = carry
        rf = pl.ds(t, 1)
        hf = af_ref[rf, :] * hf + bf_ref[rf, :]
        bf_ref[rf, :] = hf
        rb = pl.ds(T - 1 - t, 1)
        hb = ab_ref[rb, :] * hb + bb_ref[rb, :]
        bb_ref[rb, :] = hb
        return hf, hb

    hf, hb = lax.fori_loop(0, T, scan_step, (h0_ref[0:1, :], h0_ref[1:2, :]), unroll=8)
    st_ref[0:1, :] = hf
    st_ref[1:2, :] = hb

    def out_rows(r, carry):
        rows = pl.ds(pl.multiple_of(r * REC_ROWS, REC_ROWS), REC_ROWS)
        y_ref[rows, :] = ((bf_ref[rows, :] + bb_ref[rows, :]) * _gelu_tanh(g_ref[rows, :])).astype(BF16)
        return carry

    lax.fori_loop(0, T // REC_ROWS, out_rows, 0)


def _rec(xg, n_seq, T, cw, cb, wg, bg, lam, h0):
    nc = D_RNN // REC_CB
    nb = REC_CB // RG_BLOCK
    vec = lambda rows: pl.BlockSpec((rows, REC_CB), lambda s, c: (0, c))
    return pl.pallas_call(
        functools.partial(_rec_kernel, T=T),
        grid=(n_seq, nc),
        in_specs=[
            pl.BlockSpec((T, REC_CB), lambda s, c: (s, c)),
            pl.BlockSpec((T, REC_CB), lambda s, c: (s, nc + c)),
            vec(CONV_W),
            vec(1),
            pl.BlockSpec((nb, RG_BLOCK, 4 * RG_BLOCK), lambda s, c: (c, 0, 0)),
            vec(4),
            vec(2),
            pl.BlockSpec((None, 2, REC_CB), lambda s, c: (s, 0, c)),
        ],
        out_specs=[
            pl.BlockSpec((T, REC_CB), lambda s, c: (s, c)),
            pl.BlockSpec((None, 2, REC_CB), lambda s, c: (s, 0, c)),
        ],
        out_shape=[
            jax.ShapeDtypeStruct((n_seq * T, D_RNN), BF16),
            jax.ShapeDtypeStruct((n_seq, 2, D_RNN), F32),
        ],
        scratch_shapes=[pltpu.VMEM((T + 2 * REC_PAD, REC_CB), F32)] + [pltpu.VMEM((T, REC_CB), F32)] * 4,
        compiler_params=_params("arbitrary", "arbitrary"),
        name="rec",
    )(xg, xg, cw, cb.reshape(1, D_RNN), wg, bg, lam, h0)


def kernel(x_prompt, x_sample, c, cache_a_k, cache_a_v, cache_b_k, cache_b_v, state_rg_fwd, state_rg_bwd, c_ctx, w_ada, b_ada, g_pre_mix, g_post_mix, g_pre_ffn, g_post_ffn, w_att_in, w_att_out, sink_b, rpb_a, w_rec_in, conv_w, conv_b, w_rg_a, b_rg_a, w_rg_x, b_rg_x, rg_lambda, w_rec_out, w_ff1, w_ff2):
    depth = w_ada.shape[0]
    yp = x_prompt.reshape(N_PROMPT, D_MODEL)
    ys = x_sample.reshape(N_SAMPLE, D_MODEL)

    cond8 = jnp.concatenate([c_ctx[None, :], c, jnp.zeros((MOD_ROWS - N_SEG, D_MODEL), F32)], axis=0)
    mod_all = _adaln(cond8, w_ada, b_ada)
    mod_all = mod_all[:, :N_SEG, :].reshape(depth, N_SEG, 6, D_MODEL)
    mod_all = jnp.pad(mod_all, ((0, 0), (0, 0), (0, MOD_ROWS - 6), (0, 0)))

    a_k, a_v, b_k, b_v, s_f, s_b = [], [], [], [], [], []
    for layer in range(depth):
        mod = mod_all[layer]
        li = layer // 2
        g_pre, g_post = g_pre_mix[layer], g_post_mix[layer]
        if layer % 2 == 0:
            qkv_p, ka, va, kb, vb = _proj_att_prompt(yp, g_pre, mod, w_att_in, li)
            qkv_s = _proj_att_sample(ys, g_pre, mod, w_att_in, li)
            a_k.append(ka.reshape(BATCH, SEQ, N_HEADS_A, HEAD_DIM))
            a_v.append(va.reshape(BATCH, SEQ, N_HEADS_A, HEAD_DIM))
            b_k.append(kb.reshape(BATCH, SEQ, N_KV_B, HEAD_DIM))
            b_v.append(vb.reshape(BATCH, SEQ, N_KV_B, HEAD_DIM))
            mix_p = [_attn_ctx(sink_b[li], qkv_p)]
            mix_s = [_attn_na(qkv_s, rpb_a[li],
                              cache_a_k[:, li].reshape(DEC_BATCH, PAST_LEN, QA_W),
                              cache_a_v[:, li].reshape(DEC_BATCH, PAST_LEN, QA_W)),
                     _attn_win(sink_b[li], qkv_s,
                               cache_b_k[:, li].reshape(DEC_BATCH, PAST_LEN, KVB_W),
                               cache_b_v[:, li].reshape(DEC_BATCH, PAST_LEN, KVB_W))]
            w_out = w_att_out
        else:
            xg_p = _proj(PROMPT, yp, g_pre, mod, w_rec_in, li)
            xg_s = _proj(SAMPLE, ys, g_pre, mod, w_rec_in, li)
            wg = jnp.concatenate([w_rg_a[li, 0], w_rg_x[li, 0], w_rg_a[li, 1], w_rg_x[li, 1]], axis=-1)
            bg = jnp.stack([b_rg_a[li, 0], b_rg_x[li, 0], b_rg_a[li, 1], b_rg_x[li, 1]], axis=0)
            rec_args = (conv_w[li], conv_b[li], wg, bg, rg_lambda[li])
            h0_p = jnp.zeros((BATCH, 2, D_RNN), F32)
            h0_s = jnp.stack([state_rg_fwd[:, li], state_rg_bwd[:, li]], axis=1)
            rec_p, st_p = _rec(xg_p, BATCH, SEQ, *rec_args, h0_p)
            rec_s, _ = _rec(xg_s, DEC_BATCH, DEC_SEQ, *rec_args, h0_s)
            mix_p, mix_s = [rec_p], [rec_s]
            s_f.append(st_p[:, 0])
            s_b.append(st_p[:, 1])
            w_out = w_rec_out
        yp = _mixout(PROMPT, mix_p, w_out, li, yp, g_post, mod)
        ys = _mixout(SAMPLE, mix_s, w_out, li, ys, g_post, mod)
        yp = _ffn(PROMPT, yp, g_pre_ffn[layer], g_post_ffn[layer], mod, w_ff1, w_ff2, layer)
        ys = _ffn(SAMPLE, ys, g_pre_ffn[layer], g_post_ffn[layer], mod, w_ff1, w_ff2, layer)

    return (yp.reshape(BATCH, SEQ, D_MODEL), ys.reshape(DEC_BATCH, DEC_SEQ, D_MODEL),
            jnp.stack(a_k, axis=1), jnp.stack(a_v, axis=1), jnp.stack(b_k, axis=1), jnp.stack(b_v, axis=1),
            jnp.stack(s_f, axis=1), jnp.stack(s_b, axis=1))
```

```python
import functools

import jax
import jax.numpy as jnp
import numpy as np
from jax import lax
from jax.experimental import pallas as pl
from jax.experimental.pallas import tpu as pltpu

D_MODEL = 2048
BATCH = 16
SEQ = 256
DEC_BATCH = 2
DEC_SEQ = 1024
PAST_LEN = 256
GRID_W = 64
GRID_R = DEC_SEQ // GRID_W
HEAD_DIM = 128
N_HEADS_A = 8
N_HEADS_B = 8
N_KV_B = 2
G_B = N_HEADS_B // N_KV_B
NA_ROWS = 8
NA_COLS = 16
WIN_B = 128
D_RNN = D_MODEL
N_RG_BLOCKS = 16
RG_BLOCK = D_RNN // N_RG_BLOCKS
CONV_W = 4
CONV_PAD_L = 2
RG_C = 8.0
D_FF = 4 * D_MODEL
ROPE_BASE = 10000.0
EPS = 1e-6
NEG = -1e30
QA_W = N_HEADS_A * HEAD_DIM
QB_W = N_HEADS_B * HEAD_DIM
KVB_W = N_KV_B * HEAD_DIM
D_ATT_IN = 3 * QA_W + QB_W + 2 * KVB_W
SCALE = HEAD_DIM ** -0.5

N_PROMPT = BATCH * SEQ
N_SAMPLE = DEC_BATCH * DEC_SEQ
N_SEG = 1 + DEC_BATCH
MOD_ROWS = 8

COL_QA, COL_KA, COL_VA = 0, QA_W, 2 * QA_W
COL_QB = 3 * QA_W
COL_KB = COL_QB + QB_W
COL_VB = COL_KB + KVB_W

V7X_VMEM_BYTES = 64 * 1024 * 1024
VMEM_LIMIT = V7X_VMEM_BYTES - 4 * 1024 * 1024

TM = 1024
TN = 512
ROW_CHUNK = 16
ROW_UNROLL = 4
RESIDUAL_CHUNK = 128

F32 = jnp.float32
BF16 = jnp.bfloat16


def _params(*sem):
    return pltpu.CompilerParams(dimension_semantics=sem, vmem_limit_bytes=VMEM_LIMIT)


class _Rows:
    def __init__(self, n_rows, seg0, seg_rows):
        self.n = n_rows
        self.seg0 = seg0
        self.seg_rows = seg_rows

    def seg(self, i, tm):
        return self.seg0 + (i * tm) // self.seg_rows


PROMPT = _Rows(N_PROMPT, 0, N_PROMPT)
SAMPLE = _Rows(N_SAMPLE, 1, DEC_SEQ)


def _mod_spec(rows, tm):
    return pl.BlockSpec((None, MOD_ROWS, D_MODEL), lambda i, j: (rows.seg(i, tm), 0, 0))


def _vec_spec():
    return pl.BlockSpec((1, D_MODEL), lambda i, j: (0, 0))


def _rms_scale(x):
    return lax.rsqrt(jnp.mean(x * x, axis=-1, keepdims=True) + EPS)


def _norm_mod_rows(y_ref, g_ref, mod_ref, h_ref, shift_row, tm, h_row0=0):
    shift = mod_ref[shift_row:shift_row + 1, :]
    gain = g_ref[...] * (1.0 + mod_ref[shift_row + 1:shift_row + 2, :])

    def body(r, carry):
        r0 = r * ROW_CHUNK
        y = y_ref[pl.ds(pl.multiple_of(r0, ROW_CHUNK), ROW_CHUNK), :]
        h = (y * _rms_scale(y)) * gain + shift
        h_ref[pl.ds(pl.multiple_of(h_row0 + r0, ROW_CHUNK), ROW_CHUNK), :] = h.astype(BF16)
        return carry

    lax.fori_loop(0, tm // ROW_CHUNK, body, 0, unroll=ROW_UNROLL)


def _gated_residual_rows(y_ref, o_ref, g_ref, gate, tm):
    gain = gate * g_ref[...]

    def body(r, carry):
        rows = pl.ds(pl.multiple_of(r * RESIDUAL_CHUNK, RESIDUAL_CHUNK), RESIDUAL_CHUNK)
        o = o_ref[rows, :]
        o_ref[rows, :] = y_ref[rows, :] + (o * _rms_scale(o)) * gain
        return carry

    lax.fori_loop(0, tm // RESIDUAL_CHUNK, body, 0)


def _adaln_kernel(cond_ref, w_ref, b_ref, o_ref):
    c = cond_ref[...]
    s = c / (1.0 + jnp.exp(-c))
    o_ref[...] = jnp.dot(s.astype(BF16), w_ref[...].astype(BF16),
                         preferred_element_type=F32) + b_ref[...]


def _adaln(cond8, w_ada, b_ada):
    depth = w_ada.shape[0]
    n = w_ada.shape[2]
    tn = 1024
    return pl.pallas_call(
        _adaln_kernel,
        grid=(depth, n // tn),
        in_specs=[
            pl.BlockSpec((MOD_ROWS, D_MODEL), lambda l, j: (0, 0)),
            pl.BlockSpec((None, D_MODEL, tn), lambda l, j: (l, 0, j)),
            pl.BlockSpec((None, 1, tn), lambda l, j: (l, 0, j)),
        ],
        out_specs=pl.BlockSpec((None, MOD_ROWS, tn), lambda l, j: (l, 0, j)),
        out_shape=jax.ShapeDtypeStruct((depth, MOD_ROWS, n), F32),
        compiler_params=_params("arbitrary", "arbitrary"),
        name="adaln",
    )(cond8, w_ada, b_ada.reshape(depth, 1, n))


def _rope_tables():
    t = np.arange(DEC_SEQ)
    half = HEAD_DIM // 2
    inv = ROPE_BASE ** (-np.arange(0, half, 2, dtype=np.float64) / half)
    ang_r = (t // GRID_W)[:, None] * inv[None, :]
    ang_c = (t % GRID_W)[:, None] * inv[None, :]
    cos = np.concatenate([np.cos(ang_r), np.cos(ang_r), np.cos(ang_c), np.cos(ang_c)], axis=1)
    sin = np.concatenate([-np.sin(ang_r), np.sin(ang_r), -np.sin(ang_c), np.sin(ang_c)], axis=1)
    return jnp.asarray(cos, F32), jnp.asarray(sin, F32)


def _rope(x, cos, sin_signed):
    quarter = HEAD_DIM // 4
    lane = lax.broadcasted_iota(jnp.int32, x.shape, 1)
    first = (lane % (2 * quarter)) < quarter
    partner = jnp.where(first, pltpu.roll(x, HEAD_DIM - quarter, 1), pltpu.roll(x, quarter, 1))
    return x * cos + partner * sin_signed


def _proj_h_tile(y_ref, g_ref, mod_ref, h_ref, tm):
    row0 = pl.multiple_of(pl.program_id(1) * tm, tm)

    @pl.when(pl.program_id(0) == 0)
    def _():
        _norm_mod_rows(y_ref, g_ref, mod_ref, h_ref, 0, tm, row0)

    return h_ref[pl.ds(row0, tm), :]


def _proj_in_specs(rows, tm, layer):
    last = rows.n // tm - 1

    def tile(j, i):
        return jnp.where(j == 0, i, last)

    return [
        pl.BlockSpec((tm, D_MODEL), lambda j, i: (tile(j, i), 0)),
        pl.BlockSpec((1, D_MODEL), lambda j, i: (0, 0)),
        pl.BlockSpec((None, MOD_ROWS, D_MODEL), lambda j, i: (rows.seg(tile(j, i), tm), 0, 0)),
        pl.BlockSpec((None, D_MODEL, TN), lambda j, i: (layer, 0, j)),
    ]


def _proj_kernel(y_ref, g_ref, mod_ref, w_ref, o_ref, h_ref, *, tm):
    h = _proj_h_tile(y_ref, g_ref, mod_ref, h_ref, tm)
    o_ref[...] = jnp.dot(h, w_ref[...].astype(BF16), preferred_element_type=F32)


def _proj(rows, y, g, mod, w, layer):
    n = w.shape[2]
    tm = TM
    return pl.pallas_call(
        functools.partial(_proj_kernel, tm=tm),
        grid=(n // TN, rows.n // tm),
        in_specs=_proj_in_specs(rows, tm, layer),
        out_specs=pl.BlockSpec((tm, TN), lambda j, i: (i, j)),
        out_shape=jax.ShapeDtypeStruct((rows.n, n), F32),
        scratch_shapes=[pltpu.VMEM((rows.n, D_MODEL), BF16)],
        compiler_params=_params("arbitrary", "arbitrary"),
        name="proj",
    )(y, g.reshape(1, D_MODEL), mod, w)


ATT_TILE_KA = COL_KA // TN
ATT_TILE_VA = COL_VA // TN
ATT_TILE_QB = COL_QB // TN
ATT_TILE_KVB = COL_KB // TN
N_ATT_TILES = D_ATT_IN // TN


def _proj_att_prompt_kernel(y_ref, g_ref, mod_ref, w_ref, o_ref, ka_ref, va_ref, kb_ref, vb_ref, h_ref, *, tm):
    j = pl.program_id(0)
    h = _proj_h_tile(y_ref, g_ref, mod_ref, h_ref, tm)
    acc = jnp.dot(h, w_ref[...].astype(BF16), preferred_element_type=F32)
    o_ref[...] = acc.astype(BF16)

    @pl.when((j >= ATT_TILE_KA) & (j < ATT_TILE_VA))
    def _():
        ka_ref[...] = acc

    @pl.when((j >= ATT_TILE_VA) & (j < ATT_TILE_QB))
    def _():
        va_ref[...] = acc

    @pl.when(j == ATT_TILE_KVB)
    def _():
        kb_ref[...] = acc[:, :KVB_W]
        vb_ref[...] = acc[:, KVB_W:]


def _proj_att_prompt(y, g, mod, w, layer):
    tm = TM
    rows = PROMPT
    last = rows.n // tm - 1

    def kv_map(first, count):
        def index(j, i):
            row = jnp.where(j < first, 0, jnp.where(j >= first + count, last, i))
            return row, jnp.clip(j - first, 0, count - 1)
        return index

    per_head_set = QA_W // TN
    return pl.pallas_call(
        functools.partial(_proj_att_prompt_kernel, tm=tm),
        grid=(N_ATT_TILES, rows.n // tm),
        in_specs=_proj_in_specs(rows, tm, layer),
        out_specs=[
            pl.BlockSpec((tm, TN), lambda j, i: (i, j)),
            pl.BlockSpec((tm, TN), kv_map(ATT_TILE_KA, per_head_set)),
            pl.BlockSpec((tm, TN), kv_map(ATT_TILE_VA, per_head_set)),
            pl.BlockSpec((tm, KVB_W), kv_map(ATT_TILE_KVB, 1)),
            pl.BlockSpec((tm, KVB_W), kv_map(ATT_TILE_KVB, 1)),
        ],
        out_shape=[
            jax.ShapeDtypeStruct((rows.n, D_ATT_IN), BF16),
            jax.ShapeDtypeStruct((rows.n, QA_W), F32),
            jax.ShapeDtypeStruct((rows.n, QA_W), F32),
            jax.ShapeDtypeStruct((rows.n, KVB_W), F32),
            jax.ShapeDtypeStruct((rows.n, KVB_W), F32),
        ],
        scratch_shapes=[pltpu.VMEM((rows.n, D_MODEL), BF16)],
        compiler_params=_params("arbitrary", "arbitrary"),
        name="proj_att_prompt",
    )(y, g.reshape(1, D_MODEL), mod, w)


def _proj_att_sample_kernel(y_ref, g_ref, mod_ref, w_ref, cos_ref, sin_ref, o_ref, h_ref, *, tm):
    j = pl.program_id(0)
    h = _proj_h_tile(y_ref, g_ref, mod_ref, h_ref, tm)
    acc = jnp.dot(h, w_ref[...].astype(BF16), preferred_element_type=F32)

    def store(n_rope_heads):
        cos = cos_ref[...]
        sin = sin_ref[...]
        for hd in range(TN // HEAD_DIM):
            cols = slice(hd * HEAD_DIM, (hd + 1) * HEAD_DIM)
            x = acc[:, cols]
            if hd < n_rope_heads:
                x = _rope(x, cos, sin)
            o_ref[:, cols] = x.astype(BF16)

    @pl.when(j < ATT_TILE_QB)
    def _():
        o_ref[...] = acc.astype(BF16)

    @pl.when((j >= ATT_TILE_QB) & (j < ATT_TILE_KVB))
    def _():
        store(TN // HEAD_DIM)

    @pl.when(j == ATT_TILE_KVB)
    def _():
        store(N_KV_B)


def _proj_att_sample(y, g, mod, w, layer):
    tm = TM
    assert tm == DEC_SEQ
    rows = SAMPLE
    cos, sin = _rope_tables()
    tab_spec = pl.BlockSpec((DEC_SEQ, HEAD_DIM), lambda j, i: (0, 0))
    return pl.pallas_call(
        functools.partial(_proj_att_sample_kernel, tm=tm),
        grid=(N_ATT_TILES, rows.n // tm),
        in_specs=_proj_in_specs(rows, tm, layer) + [tab_spec, tab_spec],
        out_specs=pl.BlockSpec((tm, TN), lambda j, i: (i, j)),
        out_shape=jax.ShapeDtypeStruct((rows.n, D_ATT_IN), BF16),
        scratch_shapes=[pltpu.VMEM((rows.n, D_MODEL), BF16)],
        compiler_params=_params("arbitrary", "arbitrary"),
        name="proj_att_sample",
    )(y, g.reshape(1, D_MODEL), mod, w, cos, sin)


MIXOUT_TM = 512


def _mixout_kernel(*refs, tm, n_parts):
    a_refs = refs[:n_parts]
    w_ref, y_ref, g_ref, mod_ref, o_ref = refs[n_parts:]
    kp = a_refs[0].shape[1]
    for c in range(D_MODEL // TN):
        cols = slice(c * TN, (c + 1) * TN)
        acc = None
        for p in range(n_parts):
            part = jnp.dot(a_refs[p][...], w_ref[p * kp:(p + 1) * kp, cols].astype(BF16),
                           preferred_element_type=F32)
            acc = part if acc is None else acc + part
        o_ref[:, cols] = acc
    _gated_residual_rows(y_ref, o_ref, g_ref, mod_ref[2:3, :], tm)


def _mixout(rows, a_parts, w, layer, y, g, mod):
    tm = MIXOUT_TM
    n_parts = len(a_parts)
    kp = a_parts[0].shape[1]
    assert n_parts * kp == w.shape[1]
    return pl.pallas_call(
        functools.partial(_mixout_kernel, tm=tm, n_parts=n_parts),
        grid=(rows.n // tm,),
        in_specs=[pl.BlockSpec((tm, kp), lambda i: (i, 0)) for _ in range(n_parts)] + [
            pl.BlockSpec((None, w.shape[1], D_MODEL), lambda i: (layer, 0, 0), pipeline_mode=pl.Buffered(1)),
            pl.BlockSpec((tm, D_MODEL), lambda i: (i, 0)),
            pl.BlockSpec((1, D_MODEL), lambda i: (0, 0)),
            pl.BlockSpec((None, MOD_ROWS, D_MODEL), lambda i: (rows.seg(i, tm), 0, 0)),
        ],
        out_specs=pl.BlockSpec((tm, D_MODEL), lambda i: (i, 0)),
        out_shape=jax.ShapeDtypeStruct((rows.n, D_MODEL), F32),
        compiler_params=_params("arbitrary"),
        name="mixout",
    )(*a_parts, w, y, g.reshape(1, D_MODEL), mod)


FFN_TF = 256


def _ffn_kernel(y_ref, g1_ref, g2_ref, mod_ref, w1_ref, w2a_ref, w2b_ref, o_ref, h_ref, a0_ref, a1_ref, *, tm, ns):
    s = pl.program_id(1)

    @pl.when(s == 0)
    def _():
        _norm_mod_rows(y_ref, g1_ref, mod_ref, h_ref, 3, tm)
        o_ref[...] = jnp.zeros_like(o_ref)
        a1_ref[...] = jnp.zeros_like(a1_ref)

    def up(half):
        cols = slice(half * FFN_TF, (half + 1) * FFN_TF)
        a = jnp.dot(h_ref[...], w1_ref[:, cols].astype(BF16), preferred_element_type=F32)
        a = jnp.maximum(a, 0.0)
        return (a * a).astype(BF16)

    def down(a_ref, w_ref):
        a = a_ref[...]
        for c in range(D_MODEL // TN):
            cols = slice(c * TN, (c + 1) * TN)
            o_ref[:, cols] += jnp.dot(a, w_ref[:, cols].astype(BF16), preferred_element_type=F32)

    @pl.when(s < ns)
    def _():
        a0_ref[...] = up(0)
        down(a1_ref, w2a_ref)
        a1_next = up(1)
        down(a0_ref, w2b_ref)
        a1_ref[...] = a1_next

    @pl.when(s == ns)
    def _():
        down(a1_ref, w2a_ref)
        _gated_residual_rows(y_ref, o_ref, g2_ref, mod_ref[5:6, :], tm)


def _ffn(rows, y, g1, g2, mod, w1, w2, layer):
    tm = TM
    ns = D_FF // (2 * FFN_TF)
    last = D_FF // FFN_TF - 1
    return pl.pallas_call(
        functools.partial(_ffn_kernel, tm=tm, ns=ns),
        grid=(rows.n // tm, ns + 1),
        in_specs=[
            pl.BlockSpec((tm, D_MODEL), lambda i, s: (i, 0)),
            _vec_spec(),
            _vec_spec(),
            _mod_spec(rows, tm),
            pl.BlockSpec((None, D_MODEL, 2 * FFN_TF), lambda i, s: (layer, 0, jnp.minimum(s, ns - 1))),
            pl.BlockSpec((None, FFN_TF, D_MODEL), lambda i, s: (layer, jnp.maximum(2 * s - 1, 0), 0)),
            pl.BlockSpec((None, FFN_TF, D_MODEL), lambda i, s: (layer, jnp.minimum(2 * s, last), 0)),
        ],
        out_specs=pl.BlockSpec((tm, D_MODEL), lambda i, s: (i, 0)),
        out_shape=jax.ShapeDtypeStruct((rows.n, D_MODEL), F32),
        scratch_shapes=[pltpu.VMEM((tm, D_MODEL), BF16), pltpu.VMEM((tm, FFN_TF), BF16),
                        pltpu.VMEM((tm, FFN_TF), BF16)],
        compiler_params=_params("arbitrary", "arbitrary"),
        name="ffn",
    )(y, g1.reshape(1, D_MODEL), g2.reshape(1, D_MODEL), mod, w1, w2, w2)


def _qkt(q, k):
    return lax.dot_general(q, k, (((1,), (1,)), ((), ())), preferred_element_type=F32)


def _head(ref, col):
    return ref[:, col:col + HEAD_DIM]


def _attn_ctx_kernel(sink_ref, qkv_ref, o_ref):
    for h in range(N_HEADS_A):
        q = _head(qkv_ref, COL_QA + h * HEAD_DIM)
        k = _head(qkv_ref, COL_KA + h * HEAD_DIM)
        v = _head(qkv_ref, COL_VA + h * HEAD_DIM)
        s = _qkt(q, k) * SCALE
        m = jnp.max(s, axis=-1, keepdims=True)
        p = jnp.exp(s - m)
        l = jnp.sum(p, axis=-1, keepdims=True)
        o = jnp.dot(p.astype(BF16), v, preferred_element_type=F32) / l
        o_ref[:, h * HEAD_DIM:(h + 1) * HEAD_DIM] = o.astype(BF16)
    for j in range(N_KV_B):
        k = _head(qkv_ref, COL_KB + j * HEAD_DIM)
        v = _head(qkv_ref, COL_VB + j * HEAD_DIM)
        for g in range(G_B):
            hq = j * G_B + g
            q = _head(qkv_ref, COL_QB + hq * HEAD_DIM)
            sink = sink_ref[j, g]
            s = _qkt(q, k) * SCALE
            m = jnp.maximum(jnp.max(s, axis=-1, keepdims=True), sink)
            p = jnp.exp(s - m)
            l = jnp.sum(p, axis=-1, keepdims=True) + jnp.exp(sink - m)
            o = jnp.dot(p.astype(BF16), v, preferred_element_type=F32) / l
            o_ref[:, QA_W + hq * HEAD_DIM:QA_W + (hq + 1) * HEAD_DIM] = o.astype(BF16)


def _attn_ctx(sink, qkv):
    return pl.pallas_call(
        _attn_ctx_kernel,
        grid=(BATCH,),
        in_specs=[
            pl.BlockSpec(memory_space=pltpu.SMEM),
            pl.BlockSpec((SEQ, D_ATT_IN), lambda b: (b, 0)),
        ],
        out_specs=pl.BlockSpec((SEQ, D_MODEL), lambda b: (b, 0)),
        out_shape=jax.ShapeDtypeStruct((N_PROMPT, D_MODEL), BF16),
        compiler_params=_params("arbitrary"),
        name="attn_ctx",
    )(sink, qkv)


NA_Q_ROWS = 4
NA_K_ROWS = 12
NA_Q_CHUNK = NA_Q_ROWS * GRID_W
NA_K_SPAN = NA_K_ROWS * GRID_W
NA_K_ROW0 = (0, 0, 4, 4)
N_RPB_ROWS = 2 * NA_ROWS - 1
N_RPB_COLS = 2 * NA_COLS - 1


def _na_row_start(r):
    return min(max(r - NA_ROWS // 2, 0), GRID_R - NA_ROWS)


def _na_build_bias(rpb_ref, tile_ref, bias_ref):
    h = pl.program_id(0)
    shape = (GRID_W, 2 * GRID_W)
    qc = lax.broadcasted_iota(jnp.int32, shape, 0)
    lane = lax.broadcasted_iota(jnp.int32, shape, 1)
    kc = lane % GRID_W
    start_c = jnp.clip(qc - NA_COLS // 2, 0, GRID_W - NA_COLS)
    in_win = (kc >= start_c) & (kc < start_c + NA_COLS)
    dc = jnp.where(in_win, kc - qc + (NA_COLS - 1), -1)
    for dr in range(N_RPB_ROWS):
        t = jnp.full(shape, NEG, F32)
        for c in range(N_RPB_COLS):
            t = jnp.where(dc == c, rpb_ref[h, dr * N_RPB_COLS + c], t)
        tile_ref[dr] = t
    first_half = lane < GRID_W
    neg = jnp.full(shape, NEG, F32)
    for chunk in range(GRID_R // NA_Q_ROWS):
        for qi in range(NA_Q_ROWS):
            qr = chunk * NA_Q_ROWS + qi
            lo = _na_row_start(qr)
            for m in range(NA_K_ROWS // 2):
                kr = NA_K_ROW0[chunk] + 2 * m
                parts = []
                for r in (kr, kr + 1):
                    parts.append(tile_ref[r - qr + NA_ROWS - 1] if lo <= r < lo + NA_ROWS else neg)
                bias_ref[chunk, qi * GRID_W:(qi + 1) * GRID_W, m * 2 * GRID_W:(m + 1) * 2 * GRID_W] = (
                    jnp.where(first_half, parts[0], parts[1]))


def _attn_na_kernel(rpb_ref, q_ref, k_ref, v_ref, kc_ref, vc_ref, o_ref, tile_ref, bias_ref):
    @pl.when(pl.program_id(1) == 0)
    def _():
        _na_build_bias(rpb_ref, tile_ref, bias_ref)

    kc = kc_ref[...].astype(BF16)
    vc = vc_ref[...].astype(BF16)
    for c in range(DEC_SEQ // NA_Q_CHUNK):
        rows = slice(c * NA_Q_CHUNK, (c + 1) * NA_Q_CHUNK)
        keys = slice(NA_K_ROW0[c] * GRID_W, NA_K_ROW0[c] * GRID_W + NA_K_SPAN)
        q = q_ref[rows, :]
        s = _qkt(q, k_ref[keys, :]) * SCALE + bias_ref[c]
        sc = _qkt(q, kc) * SCALE
        m = jnp.maximum(jnp.max(s, axis=-1, keepdims=True), jnp.max(sc, axis=-1, keepdims=True))
        p = jnp.exp(s - m)
        pc = jnp.exp(sc - m)
        l = jnp.sum(p, axis=-1, keepdims=True) + jnp.sum(pc, axis=-1, keepdims=True)
        o = (jnp.dot(p.astype(BF16), v_ref[keys, :], preferred_element_type=F32)
             + jnp.dot(pc.astype(BF16), vc, preferred_element_type=F32)) / l
        o_ref[rows, :] = o.astype(BF16)


def _attn_na(qkv, rpb, cache_k, cache_v):
    def col(c0):
        return lambda h, b: (b, c0 // HEAD_DIM + h)

    ctx_spec = pl.BlockSpec((None, PAST_LEN, HEAD_DIM), lambda h, b: (b, 0, h))
    n_chunks = DEC_SEQ // NA_Q_CHUNK
    return pl.pallas_call(
        _attn_na_kernel,
        grid=(N_HEADS_A, DEC_BATCH),
        in_specs=[
            pl.BlockSpec(memory_space=pltpu.SMEM),
            pl.BlockSpec((DEC_SEQ, HEAD_DIM), col(COL_QA)),
            pl.BlockSpec((DEC_SEQ, HEAD_DIM), col(COL_KA)),
            pl.BlockSpec((DEC_SEQ, HEAD_DIM), col(COL_VA)),
            ctx_spec,
            ctx_spec,
        ],
        out_specs=pl.BlockSpec((DEC_SEQ, HEAD_DIM), lambda h, b: (b, h)),
        out_shape=jax.ShapeDtypeStruct((N_SAMPLE, QA_W), BF16),
        scratch_shapes=[pltpu.VMEM((N_RPB_ROWS, GRID_W, 2 * GRID_W), F32),
                        pltpu.VMEM((n_chunks, NA_Q_CHUNK, NA_K_SPAN), F32)],
        compiler_params=_params("arbitrary", "arbitrary"),
        name="attn_na",
    )(rpb.reshape(N_HEADS_A, N_RPB_ROWS * N_RPB_COLS), qkv, qkv, qkv, cache_k, cache_v)


WIN_Q_CHUNK = 256
WIN_K_SPAN = WIN_Q_CHUNK + 2 * WIN_B


def _attn_win_kernel(sink_ref, q_ref, k_ref, v_ref, kc_ref, vc_ref, o_ref):
    j = pl.program_id(1)
    kc = kc_ref[...].astype(BF16)
    vc = vc_ref[...].astype(BF16)
    n_rows = G_B * WIN_Q_CHUNK
    grp = lax.broadcasted_iota(jnp.int32, (n_rows, 1), 0) // WIN_Q_CHUNK
    sink = jnp.zeros((n_rows, 1), F32)
    for g in range(G_B):
        sink = jnp.where(grp == g, sink_ref[j, g], sink)
    for c in range(DEC_SEQ // WIN_Q_CHUNK):
        q0 = c * WIN_Q_CHUNK
        k0 = min(max(q0 - WIN_B, 0), DEC_SEQ - WIN_K_SPAN)
        rows = slice(q0, q0 + WIN_Q_CHUNK)
        keys = slice(k0, k0 + WIN_K_SPAN)
        q = jnp.concatenate([q_ref[rows, g * HEAD_DIM:(g + 1) * HEAD_DIM] for g in range(G_B)], axis=0)
        qpos = q0 + lax.broadcasted_iota(jnp.int32, (n_rows, WIN_K_SPAN), 0) % WIN_Q_CHUNK
        kpos = k0 + lax.broadcasted_iota(jnp.int32, (n_rows, WIN_K_SPAN), 1)
        s = jnp.where(jnp.abs(qpos - kpos) <= WIN_B, _qkt(q, k_ref[keys, :]) * SCALE, NEG)
        sc = _qkt(q, kc) * SCALE
        m = jnp.maximum(jnp.maximum(jnp.max(s, axis=-1, keepdims=True),
                                    jnp.max(sc, axis=-1, keepdims=True)), sink)
        p = jnp.exp(s - m)
        pc = jnp.exp(sc - m)
        l = jnp.sum(p, axis=-1, keepdims=True) + jnp.sum(pc, axis=-1, keepdims=True) + jnp.exp(sink - m)
        o = (jnp.dot(p.astype(BF16), v_ref[keys, :], preferred_element_type=F32)
             + jnp.dot(pc.astype(BF16), vc, preferred_element_type=F32)) / l
        for g in range(G_B):
            o_ref[rows, g * HEAD_DIM:(g + 1) * HEAD_DIM] = (
                o[g * WIN_Q_CHUNK:(g + 1) * WIN_Q_CHUNK, :].astype(BF16))


def _attn_win(sink, qkv, cache_k, cache_v):
    gw = G_B * HEAD_DIM
    ctx_spec = pl.BlockSpec((None, PAST_LEN, HEAD_DIM), lambda b, j: (b, 0, j))
    return pl.pallas_call(
        _attn_win_kernel,
        grid=(DEC_BATCH, N_KV_B),
        in_specs=[
            pl.BlockSpec(memory_space=pltpu.SMEM),
            pl.BlockSpec((DEC_SEQ, gw), lambda b, j: (b, COL_QB // gw + j)),
            pl.BlockSpec((DEC_SEQ, HEAD_DIM), lambda b, j: (b, COL_KB // HEAD_DIM + j)),
            pl.BlockSpec((DEC_SEQ, HEAD_DIM), lambda b, j: (b, COL_VB // HEAD_DIM + j)),
            ctx_spec,
            ctx_spec,
        ],
        out_specs=pl.BlockSpec((DEC_SEQ, gw), lambda b, j: (b, j)),
        out_shape=jax.ShapeDtypeStruct((N_SAMPLE, QB_W), BF16),
        compiler_params=_params("arbitrary", "arbitrary"),
        name="attn_win",
    )(sink, qkv, qkv, qkv, cache_k, cache_v)


REC_CB = 512
REC_SLABS = REC_CB // RG_BLOCK
REC_STREAMS = 8
REC_T = 256
REC_GROUP_ROWS = REC_STREAMS * REC_T
REC_PRE = CONV_PAD_L
REC_POST = CONV_W - 1 - CONV_PAD_L
REC_CHUNK_ROWS = 32 * REC_STREAMS


def _softplus(x):
    return jnp.maximum(x, 0.0) + jnp.log1p(jnp.exp(-jnp.abs(x)))


def _gelu_tanh(x):
    k = np.sqrt(2.0 / np.pi)
    half = 0.5 * x
    return half + half * jnp.tanh(x * (k + (k * 0.044715) * (x * x)))


def _sqrt_nonneg(u):
    return jnp.where(u > 0.0, u * lax.rsqrt(u), 0.0)


def _rec_kernel(x_ref, g_ref, cw_ref, cb_ref, wg_ref, bg_ref, lam_ref, h0_ref, y_ref, st_ref,
                xt_ref, af_ref, bf_ref, ab_ref, bb_ref, *, n_seg):
    S, T = REC_STREAMS, REC_T
    a_refs = (af_ref, ab_ref)
    b_refs = (bf_ref, bb_ref)
    seg = lax.broadcasted_iota(jnp.int32, (S, RG_BLOCK), 0) % n_seg
    slab_cols = [slice(n * RG_BLOCK, (n + 1) * RG_BLOCK) for n in range(REC_SLABS)]

    def t_rows(t):
        return pl.ds(pl.multiple_of(t * S, S), S)

    def from_prev_stream(x):
        return pltpu.roll(x, 1, 0)

    def from_next_stream(x):
        return pltpu.roll(x, S - 1, 0)

    for n, cols in enumerate(slab_cols):
        for s in range(S):
            xt_ref[n, pl.ds(REC_PRE * S + s, T, stride=S), :] = x_ref[s * T:(s + 1) * T, cols]
        for p in range(REC_PRE):
            src = xt_ref[n, (T + p) * S:(T + p + 1) * S, :]
            xt_ref[n, p * S:(p + 1) * S, :] = jnp.where(seg > 0, from_prev_stream(src), 0.0)
        for p in range(REC_POST):
            src = xt_ref[n, (REC_PRE + p) * S:(REC_PRE + p + 1) * S, :]
            xt_ref[n, (REC_PRE + T + p) * S:(REC_PRE + T + p + 1) * S, :] = (
                jnp.where(seg < n_seg - 1, from_next_stream(src), 0.0))

    c_all = (-0.5 * RG_C * np.log2(np.e)) * _softplus(-lam_ref[...])

    def gate_rows(r, carry):
        r0 = pl.multiple_of(r * REC_CHUNK_ROWS, REC_CHUNK_ROWS)
        rows = pl.ds(r0, REC_CHUNK_ROWS)
        for n, cols in enumerate(slab_cols):
            xc = cb_ref[:, cols] + cw_ref[0:1, cols] * xt_ref[n, rows, :]
            for k in range(1, CONV_W):
                tap_rows = pl.ds(pl.multiple_of(r0 + k * S, S), REC_CHUNK_ROWS)
                xc = xc + cw_ref[k:k + 1, cols] * xt_ref[n, tap_rows, :]
            gates = jnp.dot(xc.astype(BF16), wg_ref[n].astype(BF16), preferred_element_type=F32)
            x_half = 0.5 * xc
            for d in range(2):
                ga = gates[:, (2 * d) * RG_BLOCK:(2 * d + 1) * RG_BLOCK] + bg_ref[2 * d:2 * d + 1, cols]
                gx = gates[:, (2 * d + 1) * RG_BLOCK:(2 * d + 2) * RG_BLOCK] + bg_ref[2 * d + 1:2 * d + 2, cols]
                c = c_all[d:d + 1, cols]
                a = jnp.exp2(c * jnp.tanh(0.5 * ga) + c)
                a_refs[d][n, rows, :] = a
                b_refs[d][n, rows, :] = _sqrt_nonneg(1.0 - a * a) * ((1.0 + jnp.tanh(0.5 * gx)) * x_half)
        return carry

    lax.fori_loop(0, T * S // REC_CHUNK_ROWS, gate_rows, 0)

    def scan_step(t, carry):
        hf, hb, pf, pb = carry
        rf, rb = t_rows(t), t_rows(T - 1 - t)
        hf_new, hb_new, pf_new, pb_new = [], [], [], []
        for n in range(REC_SLABS):
            a = af_ref[n, rf, :]
            h = a * hf[n] + bf_ref[n, rf, :]
            bf_ref[n, rf, :] = h
            hf_new.append(h)
            a2 = ab_ref[n, rb, :]
            h2 = a2 * hb[n] + bb_ref[n, rb, :]
            bb_ref[n, rb, :] = h2
            hb_new.append(h2)
            if n_seg > 1:
                p = a * pf[n]
                af_ref[n, rf, :] = p
                pf_new.append(p)
                p2 = a2 * pb[n]
                ab_ref[n, rb, :] = p2
                pb_new.append(p2)
        return tuple(hf_new), tuple(hb_new), tuple(pf_new), tuple(pb_new)

    ones = tuple(jnp.ones((S, RG_BLOCK), F32) for _ in range(REC_SLABS)) if n_seg > 1 else ()
    hf, hb, pf, pb = lax.fori_loop(
        0, T, scan_step,
        (tuple(h0_ref[0, :, cols] for cols in slab_cols), tuple(h0_ref[1, :, cols] for cols in slab_cols),
         ones, ones), unroll=4)

    if n_seg > 1:
        cin_f, cin_b = [], []
        for n in range(REC_SLABS):
            cf = jnp.zeros((S, RG_BLOCK), F32)
            for j in range(1, n_seg):
                cf = jnp.where(seg == j, from_prev_stream(hf[n] + pf[n] * cf), cf)
            cb_in = jnp.zeros((S, RG_BLOCK), F32)
            for j in range(n_seg - 2, -1, -1):
                cb_in = jnp.where(seg == j, from_next_stream(hb[n] + pb[n] * cb_in), cb_in)
            cin_f.append(cf)
            cin_b.append(cb_in)

        def carry_in_step(t, carry):
            rows = t_rows(t)
            for n in range(REC_SLABS):
                bf_ref[n, rows, :] += af_ref[n, rows, :] * cin_f[n]
                bb_ref[n, rows, :] += ab_ref[n, rows, :] * cin_b[n]
            return carry

        lax.fori_loop(0, T, carry_in_step, 0, unroll=4)
        hf = tuple(hf[n] + pf[n] * cin_f[n] for n in range(REC_SLABS))
        hb = tuple(hb[n] + pb[n] * cin_b[n] for n in range(REC_SLABS))

    for n, cols in enumerate(slab_cols):
        st_ref[0, :, cols] = hf[n]
        st_ref[1, :, cols] = hb[n]

    for n, cols in enumerate(slab_cols):
        for s in range(S):
            rows = slice(s * T, (s + 1) * T)
            picked = pl.ds(s, T, stride=S)
            h_sum = bf_ref[n, picked, :] + bb_ref[n, picked, :]
            y_ref[rows, cols] = (h_sum * _gelu_tanh(g_ref[rows, cols])).astype(BF16)


def _rec(xg, n_seg, cw, cb, wg, bg, lam, h0):
    n_grp = xg.shape[0] // REC_GROUP_ROWS
    nc = D_RNN // REC_CB
    vec = lambda rows: pl.BlockSpec((rows, REC_CB), lambda s, c: (0, c))
    state_spec = pl.BlockSpec((2, REC_STREAMS, REC_CB), lambda s, c: (0, s, c))
    slab_scratch = lambda n_t: pltpu.VMEM((REC_SLABS, n_t * REC_STREAMS, RG_BLOCK), F32)
    return pl.pallas_call(
        functools.partial(_rec_kernel, n_seg=n_seg),
        grid=(n_grp, nc),
        in_specs=[
            pl.BlockSpec((REC_GROUP_ROWS, REC_CB), lambda s, c: (s, c)),
            pl.BlockSpec((REC_GROUP_ROWS, REC_CB), lambda s, c: (s, nc + c)),
            vec(CONV_W),
            vec(1),
            pl.BlockSpec((REC_SLABS, RG_BLOCK, 4 * RG_BLOCK), lambda s, c: (c, 0, 0)),
            vec(4),
            vec(2),
            state_spec,
        ],
        out_specs=[
            pl.BlockSpec((REC_GROUP_ROWS, REC_CB), lambda s, c: (s, c)),
            state_spec,
        ],
        out_shape=[
            jax.ShapeDtypeStruct((xg.shape[0], D_RNN), BF16),
            jax.ShapeDtypeStruct((2, n_grp * REC_STREAMS, D_RNN), F32),
        ],
        scratch_shapes=[slab_scratch(REC_PRE + REC_T + REC_POST)] + [slab_scratch(REC_T)] * 4,
        compiler_params=_params("arbitrary", "arbitrary"),
        name="rec",
    )(xg, xg, cw, cb.reshape(1, D_RNN), wg, bg, lam, h0)


def kernel(x_prompt, x_sample, c, cache_a_k, cache_a_v, cache_b_k, cache_b_v, state_rg_fwd, state_rg_bwd, c_ctx, w_ada, b_ada, g_pre_mix, g_post_mix, g_pre_ffn, g_post_ffn, w_att_in, w_att_out, sink_b, rpb_a, w_rec_in, conv_w, conv_b, w_rg_a, b_rg_a, w_rg_x, b_rg_x, rg_lambda, w_rec_out, w_ff1, w_ff2):
    depth = w_ada.shape[0]
    yp = x_prompt.reshape(N_PROMPT, D_MODEL)
    ys = x_sample.reshape(N_SAMPLE, D_MODEL)

    cond8 = jnp.concatenate([c_ctx[None, :], c, jnp.zeros((MOD_ROWS - N_SEG, D_MODEL), F32)], axis=0)
    mod_all = _adaln(cond8, w_ada, b_ada)
    mod_all = mod_all[:, :N_SEG, :].reshape(depth, N_SEG, 6, D_MODEL)
    mod_all = jnp.pad(mod_all, ((0, 0), (0, 0), (0, MOD_ROWS - 6), (0, 0)))

    a_k, a_v, b_k, b_v, s_f, s_b = [], [], [], [], [], []
    for layer in range(depth):
        mod = mod_all[layer]
        li = layer // 2
        g_pre, g_post = g_pre_mix[layer], g_post_mix[layer]
        if layer % 2 == 0:
            qkv_p, ka, va, kb, vb = _proj_att_prompt(yp, g_pre, mod, w_att_in, li)
            qkv_s = _proj_att_sample(ys, g_pre, mod, w_att_in, li)
            a_k.append(ka.reshape(BATCH, SEQ, N_HEADS_A, HEAD_DIM))
            a_v.append(va.reshape(BATCH, SEQ, N_HEADS_A, HEAD_DIM))
            b_k.append(kb.reshape(BATCH, SEQ, N_KV_B, HEAD_DIM))
            b_v.append(vb.reshape(BATCH, SEQ, N_KV_B, HEAD_DIM))
            mix_p = [_attn_ctx(sink_b[li], qkv_p)]
            mix_s = [_attn_na(qkv_s, rpb_a[li],
                              cache_a_k[:, li].reshape(DEC_BATCH, PAST_LEN, QA_W),
                              cache_a_v[:, li].reshape(DEC_BATCH, PAST_LEN, QA_W)),
                     _attn_win(sink_b[li], qkv_s,
                               cache_b_k[:, li].reshape(DEC_BATCH, PAST_LEN, KVB_W),
                               cache_b_v[:, li].reshape(DEC_BATCH, PAST_LEN, KVB_W))]
            w_out = w_att_out
        else:
            xg_p = _proj(PROMPT, yp, g_pre, mod, w_rec_in, li)
            xg_s = _proj(SAMPLE, ys, g_pre, mod, w_rec_in, li)
            wg = jnp.concatenate([w_rg_a[li, 0], w_rg_x[li, 0], w_rg_a[li, 1], w_rg_x[li, 1]], axis=-1)
            bg = jnp.stack([b_rg_a[li, 0], b_rg_x[li, 0], b_rg_a[li, 1], b_rg_x[li, 1]], axis=0)
            rec_args = (conv_w[li], conv_b[li], wg, bg, rg_lambda[li])
            n_seg = DEC_SEQ // REC_T
            assert SEQ == REC_T and BATCH % REC_STREAMS == 0 and DEC_BATCH * n_seg == REC_STREAMS
            h0_p = jnp.zeros((2, BATCH, D_RNN), F32)
            seg_state = jnp.zeros((DEC_BATCH, n_seg, D_RNN), F32)
            h0_s = jnp.stack([seg_state.at[:, 0].set(state_rg_fwd[:, li]).reshape(REC_STREAMS, D_RNN),
                              seg_state.at[:, n_seg - 1].set(state_rg_bwd[:, li]).reshape(REC_STREAMS, D_RNN)])
            rec_p, st_p = _rec(xg_p, 1, *rec_args, h0_p)
            rec_s, _ = _rec(xg_s, n_seg, *rec_args, h0_s)
            mix_p, mix_s = [rec_p], [rec_s]
            s_f.append(st_p[0])
            s_b.append(st_p[1])
            w_out = w_rec_out
        yp = _mixout(PROMPT, mix_p, w_out, li, yp, g_post, mod)
        ys = _mixout(SAMPLE, mix_s, w_out, li, ys, g_post, mod)
        yp = _ffn(PROMPT, yp, g_pre_ffn[layer], g_post_ffn[layer], mod, w_ff1, w_ff2, layer)
        ys = _ffn(SAMPLE, ys, g_pre_ffn[layer], g_post_ffn[layer], mod, w_ff1, w_ff2, layer)

    return (yp.reshape(BATCH, SEQ, D_MODEL), ys.reshape(DEC_BATCH, DEC_SEQ, D_MODEL),
            jnp.stack(a_k, axis=1), jnp.stack(a_v, axis=1), jnp.stack(b_k, axis=1), jnp.stack(b_v, axis=1),
            jnp.stack(s_f, axis=1), jnp.stack(s_b, axis=1))
```

```python
import functools

import jax
import jax.numpy as jnp
import numpy as np
from jax import lax
from jax.experimental import pallas as pl
from jax.experimental.pallas import tpu as pltpu

D_MODEL = 2048
BATCH = 16
SEQ = 256
DEC_BATCH = 2
DEC_SEQ = 1024
PAST_LEN = 256
GRID_W = 64
GRID_R = DEC_SEQ // GRID_W
HEAD_DIM = 128
N_HEADS_A = 8
N_HEADS_B = 8
N_KV_B = 2
G_B = N_HEADS_B // N_KV_B
NA_ROWS = 8
NA_COLS = 16
WIN_B = 128
D_RNN = D_MODEL
N_RG_BLOCKS = 16
RG_BLOCK = D_RNN // N_RG_BLOCKS
CONV_W = 4
CONV_PAD_L = 2
RG_C = 8.0
D_FF = 4 * D_MODEL
ROPE_BASE = 10000.0
EPS = 1e-6
NEG = -1e30
QA_W = N_HEADS_A * HEAD_DIM
QB_W = N_HEADS_B * HEAD_DIM
KVB_W = N_KV_B * HEAD_DIM
D_ATT_IN = 3 * QA_W + QB_W + 2 * KVB_W
SCALE = HEAD_DIM ** -0.5

N_PROMPT = BATCH * SEQ
N_SAMPLE = DEC_BATCH * DEC_SEQ
N_SEG = 1 + DEC_BATCH
MOD_ROWS = 8

COL_QA, COL_KA, COL_VA = 0, QA_W, 2 * QA_W
COL_QB = 3 * QA_W
COL_KB = COL_QB + QB_W
COL_VB = COL_KB + KVB_W

V7X_VMEM_BYTES = 64 * 1024 * 1024
VMEM_LIMIT = V7X_VMEM_BYTES - 4 * 1024 * 1024

TM = 1024
TN = 512
ROW_CHUNK = 16
ROW_UNROLL = 4
RESIDUAL_CHUNK = 128

F32 = jnp.float32
BF16 = jnp.bfloat16


def _params(*sem):
    return pltpu.CompilerParams(dimension_semantics=sem, vmem_limit_bytes=VMEM_LIMIT)


class _Rows:
    def __init__(self, n_rows, seg0, seg_rows):
        self.n = n_rows
        self.seg0 = seg0
        self.seg_rows = seg_rows

    def seg(self, i, tm):
        return self.seg0 + (i * tm) // self.seg_rows


PROMPT = _Rows(N_PROMPT, 0, N_PROMPT)
SAMPLE = _Rows(N_SAMPLE, 1, DEC_SEQ)


def _mod_spec(rows, tm):
    return pl.BlockSpec((None, MOD_ROWS, D_MODEL), lambda i, j: (rows.seg(i, tm), 0, 0))


def _vec_spec():
    return pl.BlockSpec((1, D_MODEL), lambda i, j: (0, 0))


def _rms_scale(x):
    return lax.rsqrt(jnp.mean(x * x, axis=-1, keepdims=True) + EPS)


def _norm_mod_rows(y_ref, g_ref, mod_ref, h_ref, shift_row, tm, h_row0=0):
    shift = mod_ref[shift_row:shift_row + 1, :]
    gain = g_ref[...] * (1.0 + mod_ref[shift_row + 1:shift_row + 2, :])

    def body(r, carry):
        r0 = r * ROW_CHUNK
        y = y_ref[pl.ds(pl.multiple_of(r0, ROW_CHUNK), ROW_CHUNK), :]
        h = (y * _rms_scale(y)) * gain + shift
        h_ref[pl.ds(pl.multiple_of(h_row0 + r0, ROW_CHUNK), ROW_CHUNK), :] = h.astype(BF16)
        return carry

    lax.fori_loop(0, tm // ROW_CHUNK, body, 0, unroll=ROW_UNROLL)


def _gated_residual_rows(y_ref, o_ref, g_ref, gate, tm):
    gain = gate * g_ref[...]

    def body(r, carry):
        rows = pl.ds(pl.multiple_of(r * RESIDUAL_CHUNK, RESIDUAL_CHUNK), RESIDUAL_CHUNK)
        o = o_ref[rows, :]
        o_ref[rows, :] = y_ref[rows, :] + (o * _rms_scale(o)) * gain
        return carry

    lax.fori_loop(0, tm // RESIDUAL_CHUNK, body, 0)


def _adaln_kernel(cond_ref, w_ref, b_ref, o_ref):
    c = cond_ref[...]
    s = c / (1.0 + jnp.exp(-c))
    o_ref[...] = jnp.dot(s.astype(BF16), w_ref[...].astype(BF16),
                         preferred_element_type=F32) + b_ref[...]


def _adaln(cond8, w_ada, b_ada):
    depth = w_ada.shape[0]
    n = w_ada.shape[2]
    tn = 1024
    return pl.pallas_call(
        _adaln_kernel,
        grid=(depth, n // tn),
        in_specs=[
            pl.BlockSpec((MOD_ROWS, D_MODEL), lambda l, j: (0, 0)),
            pl.BlockSpec((None, D_MODEL, tn), lambda l, j: (l, 0, j)),
            pl.BlockSpec((None, 1, tn), lambda l, j: (l, 0, j)),
        ],
        out_specs=pl.BlockSpec((None, MOD_ROWS, tn), lambda l, j: (l, 0, j)),
        out_shape=jax.ShapeDtypeStruct((depth, MOD_ROWS, n), F32),
        compiler_params=_params("arbitrary", "arbitrary"),
        name="adaln",
    )(cond8, w_ada, b_ada.reshape(depth, 1, n))


def _rope_tables():
    t = np.arange(DEC_SEQ)
    half = HEAD_DIM // 2
    inv = ROPE_BASE ** (-np.arange(0, half, 2, dtype=np.float64) / half)
    ang_r = (t // GRID_W)[:, None] * inv[None, :]
    ang_c = (t % GRID_W)[:, None] * inv[None, :]
    cos = np.concatenate([np.cos(ang_r), np.cos(ang_r), np.cos(ang_c), np.cos(ang_c)], axis=1)
    sin = np.concatenate([-np.sin(ang_r), np.sin(ang_r), -np.sin(ang_c), np.sin(ang_c)], axis=1)
    return jnp.asarray(cos, F32), jnp.asarray(sin, F32)


def _rope(x, cos, sin_signed):
    quarter = HEAD_DIM // 4
    lane = lax.broadcasted_iota(jnp.int32, x.shape, 1)
    first = (lane % (2 * quarter)) < quarter
    partner = jnp.where(first, pltpu.roll(x, HEAD_DIM - quarter, 1), pltpu.roll(x, quarter, 1))
    return x * cos + partner * sin_signed


def _proj_h_tile(y_ref, g_ref, mod_ref, h_ref, tm):
    row0 = pl.multiple_of(pl.program_id(1) * tm, tm)

    @pl.when(pl.program_id(0) == 0)
    def _():
        _norm_mod_rows(y_ref, g_ref, mod_ref, h_ref, 0, tm, row0)

    return h_ref[pl.ds(row0, tm), :]


def _proj_in_specs(rows, tm, layer, tn=TN):
    last = rows.n // tm - 1

    def tile(j, i):
        return jnp.where(j == 0, i, last)

    return [
        pl.BlockSpec((tm, D_MODEL), lambda j, i: (tile(j, i), 0)),
        pl.BlockSpec((1, D_MODEL), lambda j, i: (0, 0)),
        pl.BlockSpec((None, MOD_ROWS, D_MODEL), lambda j, i: (rows.seg(tile(j, i), tm), 0, 0)),
        pl.BlockSpec((None, D_MODEL, tn), lambda j, i: (layer, 0, j)),
    ]


def _proj_tn(rows):
    return TN if rows.n > N_SAMPLE else 2 * TN


def _proj_kernel(y_ref, g_ref, mod_ref, w_ref, o_ref, h_ref, *, tm):
    h = _proj_h_tile(y_ref, g_ref, mod_ref, h_ref, tm)
    o_ref[...] = jnp.dot(h, w_ref[...].astype(BF16), preferred_element_type=F32)


def _proj(rows, y, g, mod, w, layer):
    n = w.shape[2]
    tm, tn = TM, _proj_tn(rows)
    return pl.pallas_call(
        functools.partial(_proj_kernel, tm=tm),
        grid=(n // tn, rows.n // tm),
        in_specs=_proj_in_specs(rows, tm, layer, tn),
        out_specs=pl.BlockSpec((tm, tn), lambda j, i: (i, j)),
        out_shape=jax.ShapeDtypeStruct((rows.n, n), F32),
        scratch_shapes=[pltpu.VMEM((rows.n, D_MODEL), BF16)],
        compiler_params=_params("arbitrary", "arbitrary"),
        name="proj",
    )(y, g.reshape(1, D_MODEL), mod, w)


ATT_TILE_KA = COL_KA // TN
ATT_TILE_VA = COL_VA // TN
ATT_TILE_QB = COL_QB // TN
ATT_TILE_KVB = COL_KB // TN
N_ATT_TILES = D_ATT_IN // TN


def _proj_att_prompt_kernel(y_ref, g_ref, mod_ref, w_ref, o_ref, ka_ref, va_ref, kb_ref, vb_ref, h_ref, *, tm):
    j = pl.program_id(0)
    h = _proj_h_tile(y_ref, g_ref, mod_ref, h_ref, tm)
    acc = jnp.dot(h, w_ref[...].astype(BF16), preferred_element_type=F32)
    o_ref[...] = acc.astype(BF16)

    @pl.when((j >= ATT_TILE_KA) & (j < ATT_TILE_VA))
    def _():
        ka_ref[...] = acc

    @pl.when((j >= ATT_TILE_VA) & (j < ATT_TILE_QB))
    def _():
        va_ref[...] = acc

    @pl.when(j == ATT_TILE_KVB)
    def _():
        kb_ref[...] = acc[:, :KVB_W]
        vb_ref[...] = acc[:, KVB_W:]


def _proj_att_prompt(y, g, mod, w, layer):
    tm = TM
    rows = PROMPT
    last = rows.n // tm - 1

    def kv_map(first, count):
        def index(j, i):
            row = jnp.where(j < first, 0, jnp.where(j >= first + count, last, i))
            return row, jnp.clip(j - first, 0, count - 1)
        return index

    per_head_set = QA_W // TN
    return pl.pallas_call(
        functools.partial(_proj_att_prompt_kernel, tm=tm),
        grid=(N_ATT_TILES, rows.n // tm),
        in_specs=_proj_in_specs(rows, tm, layer),
        out_specs=[
            pl.BlockSpec((tm, TN), lambda j, i: (i, j)),
            pl.BlockSpec((tm, TN), kv_map(ATT_TILE_KA, per_head_set)),
            pl.BlockSpec((tm, TN), kv_map(ATT_TILE_VA, per_head_set)),
            pl.BlockSpec((tm, KVB_W), kv_map(ATT_TILE_KVB, 1)),
            pl.BlockSpec((tm, KVB_W), kv_map(ATT_TILE_KVB, 1)),
        ],
        out_shape=[
            jax.ShapeDtypeStruct((rows.n, D_ATT_IN), BF16),
            jax.ShapeDtypeStruct((rows.n, QA_W), F32),
            jax.ShapeDtypeStruct((rows.n, QA_W), F32),
            jax.ShapeDtypeStruct((rows.n, KVB_W), F32),
            jax.ShapeDtypeStruct((rows.n, KVB_W), F32),
        ],
        scratch_shapes=[pltpu.VMEM((rows.n, D_MODEL), BF16)],
        compiler_params=_params("arbitrary", "arbitrary"),
        name="proj_att_prompt",
    )(y, g.reshape(1, D_MODEL), mod, w)


def _proj_att_sample_kernel(y_ref, g_ref, mod_ref, w_ref, cos_ref, sin_ref, o_ref, h_ref, *, tm):
    j = pl.program_id(0)
    h = _proj_h_tile(y_ref, g_ref, mod_ref, h_ref, tm)
    acc = jnp.dot(h, w_ref[...].astype(BF16), preferred_element_type=F32)

    def store(n_rope_heads):
        cos = cos_ref[...]
        sin = sin_ref[...]
        for hd in range(TN // HEAD_DIM):
            cols = slice(hd * HEAD_DIM, (hd + 1) * HEAD_DIM)
            x = acc[:, cols]
            if hd < n_rope_heads:
                x = _rope(x, cos, sin)
            o_ref[:, cols] = x.astype(BF16)

    @pl.when(j < ATT_TILE_QB)
    def _():
        o_ref[...] = acc.astype(BF16)

    @pl.when((j >= ATT_TILE_QB) & (j < ATT_TILE_KVB))
    def _():
        store(TN // HEAD_DIM)

    @pl.when(j == ATT_TILE_KVB)
    def _():
        store(N_KV_B)


def _proj_att_sample(y, g, mod, w, layer):
    tm = TM
    assert tm == DEC_SEQ
    rows = SAMPLE
    cos, sin = _rope_tables()
    tab_spec = pl.BlockSpec((DEC_SEQ, HEAD_DIM), lambda j, i: (0, 0))
    return pl.pallas_call(
        functools.partial(_proj_att_sample_kernel, tm=tm),
        grid=(N_ATT_TILES, rows.n // tm),
        in_specs=_proj_in_specs(rows, tm, layer) + [tab_spec, tab_spec],
        out_specs=pl.BlockSpec((tm, TN), lambda j, i: (i, j)),
        out_shape=jax.ShapeDtypeStruct((rows.n, D_ATT_IN), BF16),
        scratch_shapes=[pltpu.VMEM((rows.n, D_MODEL), BF16)],
        compiler_params=_params("arbitrary", "arbitrary"),
        name="proj_att_sample",
    )(y, g.reshape(1, D_MODEL), mod, w, cos, sin)


MIXOUT_TM = 512


def _mixout_kernel(*refs, tm, n_parts):
    a_refs = refs[:n_parts]
    w_ref, y_ref, g_ref, mod_ref, o_ref = refs[n_parts:]
    kp = a_refs[0].shape[1]
    for c in range(D_MODEL // TN):
        cols = slice(c * TN, (c + 1) * TN)
        acc = None
        for p in range(n_parts):
            part = jnp.dot(a_refs[p][...], w_ref[p * kp:(p + 1) * kp, cols].astype(BF16),
                           preferred_element_type=F32)
            acc = part if acc is None else acc + part
        o_ref[:, cols] = acc
    _gated_residual_rows(y_ref, o_ref, g_ref, mod_ref[2:3, :], tm)


def _mixout(rows, a_parts, w, layer, y, g, mod):
    tm = MIXOUT_TM
    n_parts = len(a_parts)
    kp = a_parts[0].shape[1]
    assert n_parts * kp == w.shape[1]
    return pl.pallas_call(
        functools.partial(_mixout_kernel, tm=tm, n_parts=n_parts),
        grid=(rows.n // tm,),
        in_specs=[pl.BlockSpec((tm, kp), lambda i: (i, 0)) for _ in range(n_parts)] + [
            pl.BlockSpec((None, w.shape[1], D_MODEL), lambda i: (layer, 0, 0), pipeline_mode=pl.Buffered(1)),
            pl.BlockSpec((tm, D_MODEL), lambda i: (i, 0)),
            pl.BlockSpec((1, D_MODEL), lambda i: (0, 0)),
            pl.BlockSpec((None, MOD_ROWS, D_MODEL), lambda i: (rows.seg(i, tm), 0, 0)),
        ],
        out_specs=pl.BlockSpec((tm, D_MODEL), lambda i: (i, 0)),
        out_shape=jax.ShapeDtypeStruct((rows.n, D_MODEL), F32),
        compiler_params=_params("arbitrary"),
        name="mixout",
    )(*a_parts, w, y, g.reshape(1, D_MODEL), mod)


FFN_TF = 256


def _ffn_kernel(y_ref, g1_ref, g2_ref, mod_ref, w1_ref, w2a_ref, w2b_ref, o_ref, h_ref, a0_ref, a1_ref, *, tm, ns):
    s = pl.program_id(1)

    @pl.when(s == 0)
    def _():
        _norm_mod_rows(y_ref, g1_ref, mod_ref, h_ref, 3, tm)
        o_ref[...] = jnp.zeros_like(o_ref)
        a1_ref[...] = jnp.zeros_like(a1_ref)

    def up(half):
        cols = slice(half * FFN_TF, (half + 1) * FFN_TF)
        a = jnp.dot(h_ref[...], w1_ref[:, cols].astype(BF16), preferred_element_type=F32)
        a = jnp.maximum(a, 0.0)
        return (a * a).astype(BF16)

    def down(a_ref, w_ref):
        a = a_ref[...]
        for c in range(D_MODEL // TN):
            cols = slice(c * TN, (c + 1) * TN)
            o_ref[:, cols] += jnp.dot(a, w_ref[:, cols].astype(BF16), preferred_element_type=F32)

    @pl.when(s < ns)
    def _():
        a0_ref[...] = up(0)
        down(a1_ref, w2a_ref)
        a1_next = up(1)
        down(a0_ref, w2b_ref)
        a1_ref[...] = a1_next

    @pl.when(s == ns)
    def _():
        down(a1_ref, w2a_ref)
        _gated_residual_rows(y_ref, o_ref, g2_ref, mod_ref[5:6, :], tm)


def _ffn(rows, y, g1, g2, mod, w1, w2, layer):
    tm = TM
    ns = D_FF // (2 * FFN_TF)
    last = D_FF // FFN_TF - 1
    return pl.pallas_call(
        functools.partial(_ffn_kernel, tm=tm, ns=ns),
        grid=(rows.n // tm, ns + 1),
        in_specs=[
            pl.BlockSpec((tm, D_MODEL), lambda i, s: (i, 0)),
            _vec_spec(),
            _vec_spec(),
            _mod_spec(rows, tm),
            pl.BlockSpec((None, D_MODEL, 2 * FFN_TF), lambda i, s: (layer, 0, jnp.minimum(s, ns - 1))),
            pl.BlockSpec((None, FFN_TF, D_MODEL), lambda i, s: (layer, jnp.maximum(2 * s - 1, 0), 0)),
            pl.BlockSpec((None, FFN_TF, D_MODEL), lambda i, s: (layer, jnp.minimum(2 * s, last), 0)),
        ],
        out_specs=pl.BlockSpec((tm, D_MODEL), lambda i, s: (i, 0)),
        out_shape=jax.ShapeDtypeStruct((rows.n, D_MODEL), F32),
        scratch_shapes=[pltpu.VMEM((tm, D_MODEL), BF16), pltpu.VMEM((tm, FFN_TF), BF16),
                        pltpu.VMEM((tm, FFN_TF), BF16)],
        compiler_params=_params("arbitrary", "arbitrary"),
        name="ffn",
    )(y, g1.reshape(1, D_MODEL), g2.reshape(1, D_MODEL), mod, w1, w2, w2)


def _qkt(q, k):
    return lax.dot_general(q, k, (((1,), (1,)), ((), ())), preferred_element_type=F32)


CTX_SEQ_PER_STEP = 2


def _attn_ctx_kernel(sink_ref, qkv_ref, o_ref):
    n_rows = G_B * SEQ
    grp = lax.broadcasted_iota(jnp.int32, (n_rows, 1), 0) // SEQ
    for b in range(CTX_SEQ_PER_STEP):
        rows = slice(b * SEQ, (b + 1) * SEQ)

        def head(col):
            return qkv_ref[rows, col:col + HEAD_DIM]

        for h in range(N_HEADS_A):
            q = head(COL_QA + h * HEAD_DIM)
            k = head(COL_KA + h * HEAD_DIM)
            v = head(COL_VA + h * HEAD_DIM)
            s = _qkt(q, k) * SCALE
            m = jnp.max(s, axis=-1, keepdims=True)
            p = jnp.exp(s - m)
            l = jnp.sum(p, axis=-1, keepdims=True)
            o = jnp.dot(p.astype(BF16), v, preferred_element_type=F32) / l
            o_ref[rows, h * HEAD_DIM:(h + 1) * HEAD_DIM] = o.astype(BF16)
        for j in range(N_KV_B):
            k = head(COL_KB + j * HEAD_DIM)
            v = head(COL_VB + j * HEAD_DIM)
            q = jnp.concatenate([head(COL_QB + (j * G_B + g) * HEAD_DIM) for g in range(G_B)], axis=0)
            sink = jnp.zeros((n_rows, 1), F32)
            for g in range(G_B):
                sink = jnp.where(grp == g, sink_ref[j, g], sink)
            s = _qkt(q, k) * SCALE
            m = jnp.maximum(jnp.max(s, axis=-1, keepdims=True), sink)
            p = jnp.exp(s - m)
            l = jnp.sum(p, axis=-1, keepdims=True) + jnp.exp(sink - m)
            o = jnp.dot(p.astype(BF16), v, preferred_element_type=F32) / l
            for g in range(G_B):
                c0 = QA_W + (j * G_B + g) * HEAD_DIM
                o_ref[rows, c0:c0 + HEAD_DIM] = o[g * SEQ:(g + 1) * SEQ, :].astype(BF16)


def _attn_ctx(sink, qkv):
    rows = CTX_SEQ_PER_STEP * SEQ
    return pl.pallas_call(
        _attn_ctx_kernel,
        grid=(N_PROMPT // rows,),
        in_specs=[
            pl.BlockSpec(memory_space=pltpu.SMEM),
            pl.BlockSpec((rows, D_ATT_IN), lambda b: (b, 0)),
        ],
        out_specs=pl.BlockSpec((rows, D_MODEL), lambda b: (b, 0)),
        out_shape=jax.ShapeDtypeStruct((N_PROMPT, D_MODEL), BF16),
        compiler_params=_params("arbitrary"),
        name="attn_ctx",
    )(sink, qkv)


NA_Q_ROWS = 4
NA_K_ROWS = 12
NA_Q_CHUNK = NA_Q_ROWS * GRID_W
NA_K_SPAN = NA_K_ROWS * GRID_W
NA_K_ROW0 = (0, 0, 4, 4)
N_RPB_ROWS = 2 * NA_ROWS - 1
N_RPB_COLS = 2 * NA_COLS - 1


def _na_row_start(r):
    return min(max(r - NA_ROWS // 2, 0), GRID_R - NA_ROWS)


def _na_build_bias(rpb_ref, tile_ref, bias_ref):
    h = pl.program_id(0)
    shape = (GRID_W, 2 * GRID_W)
    qc = lax.broadcasted_iota(jnp.int32, shape, 0)
    lane = lax.broadcasted_iota(jnp.int32, shape, 1)
    kc = lane % GRID_W
    start_c = jnp.clip(qc - NA_COLS // 2, 0, GRID_W - NA_COLS)
    in_win = (kc >= start_c) & (kc < start_c + NA_COLS)
    dc = jnp.where(in_win, kc - qc + (NA_COLS - 1), -1)
    for dr in range(N_RPB_ROWS):
        t = jnp.full(shape, NEG, F32)
        for c in range(N_RPB_COLS):
            t = jnp.where(dc == c, rpb_ref[h, dr * N_RPB_COLS + c], t)
        tile_ref[dr] = t
    first_half = lane < GRID_W
    neg = jnp.full(shape, NEG, F32)
    for chunk in range(GRID_R // NA_Q_ROWS):
        for qi in range(NA_Q_ROWS):
            qr = chunk * NA_Q_ROWS + qi
            lo = _na_row_start(qr)
            for m in range(NA_K_ROWS // 2):
                kr = NA_K_ROW0[chunk] + 2 * m
                parts = []
                for r in (kr, kr + 1):
                    parts.append(tile_ref[r - qr + NA_ROWS - 1] if lo <= r < lo + NA_ROWS else neg)
                bias_ref[chunk, qi * GRID_W:(qi + 1) * GRID_W, m * 2 * GRID_W:(m + 1) * 2 * GRID_W] = (
                    jnp.where(first_half, parts[0], parts[1]))


def _attn_na_kernel(rpb_ref, q_ref, k_ref, v_ref, kc_ref, vc_ref, o_ref, tile_ref, bias_ref):
    _na_build_bias(rpb_ref, tile_ref, bias_ref)
    for b in range(DEC_BATCH):
        kc = kc_ref[b].astype(BF16)
        vc = vc_ref[b].astype(BF16)
        for c in range(DEC_SEQ // NA_Q_CHUNK):
            r0 = b * DEC_SEQ + c * NA_Q_CHUNK
            k0 = b * DEC_SEQ + NA_K_ROW0[c] * GRID_W
            rows = slice(r0, r0 + NA_Q_CHUNK)
            keys = slice(k0, k0 + NA_K_SPAN)
            q = q_ref[rows, :]
            s = _qkt(q, k_ref[keys, :]) * SCALE + bias_ref[c]
            sc = _qkt(q, kc) * SCALE
            m = jnp.maximum(jnp.max(s, axis=-1, keepdims=True), jnp.max(sc, axis=-1, keepdims=True))
            p = jnp.exp(s - m)
            pc = jnp.exp(sc - m)
            l = jnp.sum(p, axis=-1, keepdims=True) + jnp.sum(pc, axis=-1, keepdims=True)
            o = (jnp.dot(p.astype(BF16), v_ref[keys, :], preferred_element_type=F32)
                 + jnp.dot(pc.astype(BF16), vc, preferred_element_type=F32)) / l
            o_ref[rows, :] = o.astype(BF16)


def _attn_na(qkv, rpb, cache_k, cache_v, layer):
    def col(c0):
        return lambda h: (0, c0 // HEAD_DIM + h)

    ctx_spec = pl.BlockSpec((DEC_BATCH, None, PAST_LEN, HEAD_DIM), lambda h: (0, layer, 0, h))
    n_chunks = DEC_SEQ // NA_Q_CHUNK
    return pl.pallas_call(
        _attn_na_kernel,
        grid=(N_HEADS_A,),
        in_specs=[
            pl.BlockSpec(memory_space=pltpu.SMEM),
            pl.BlockSpec((N_SAMPLE, HEAD_DIM), col(COL_QA)),
            pl.BlockSpec((N_SAMPLE, HEAD_DIM), col(COL_KA)),
            pl.BlockSpec((N_SAMPLE, HEAD_DIM), col(COL_VA)),
            ctx_spec,
            ctx_spec,
        ],
        out_specs=pl.BlockSpec((N_SAMPLE, HEAD_DIM), lambda h: (0, h)),
        out_shape=jax.ShapeDtypeStruct((N_SAMPLE, QA_W), BF16),
        scratch_shapes=[pltpu.VMEM((N_RPB_ROWS, GRID_W, 2 * GRID_W), F32),
                        pltpu.VMEM((n_chunks, NA_Q_CHUNK, NA_K_SPAN), F32)],
        compiler_params=_params("arbitrary"),
        name="attn_na",
    )(rpb.reshape(N_HEADS_A, N_RPB_ROWS * N_RPB_COLS), qkv, qkv, qkv, cache_k, cache_v)


WIN_Q_CHUNK = 256
WIN_K_SPAN = WIN_Q_CHUNK + 2 * WIN_B


def _attn_win_kernel(sink_ref, q_ref, k_ref, v_ref, kc_ref, vc_ref, o_ref):
    j = pl.program_id(1)
    kc = kc_ref[...].astype(BF16)
    vc = vc_ref[...].astype(BF16)
    n_rows = G_B * WIN_Q_CHUNK
    grp = lax.broadcasted_iota(jnp.int32, (n_rows, 1), 0) // WIN_Q_CHUNK
    sink = jnp.zeros((n_rows, 1), F32)
    for g in range(G_B):
        sink = jnp.where(grp == g, sink_ref[j, g], sink)
    for c in range(DEC_SEQ // WIN_Q_CHUNK):
        q0 = c * WIN_Q_CHUNK
        k0 = min(max(q0 - WIN_B, 0), DEC_SEQ - WIN_K_SPAN)
        rows = slice(q0, q0 + WIN_Q_CHUNK)
        keys = slice(k0, k0 + WIN_K_SPAN)
        q = jnp.concatenate([q_ref[rows, g * HEAD_DIM:(g + 1) * HEAD_DIM] for g in range(G_B)], axis=0)
        qpos = q0 + lax.broadcasted_iota(jnp.int32, (n_rows, WIN_K_SPAN), 0) % WIN_Q_CHUNK
        kpos = k0 + lax.broadcasted_iota(jnp.int32, (n_rows, WIN_K_SPAN), 1)
        s = jnp.where(jnp.abs(qpos - kpos) <= WIN_B, _qkt(q, k_ref[keys, :]) * SCALE, NEG)
        sc = _qkt(q, kc) * SCALE
        m = jnp.maximum(jnp.maximum(jnp.max(s, axis=-1, keepdims=True),
                                    jnp.max(sc, axis=-1, keepdims=True)), sink)
        p = jnp.exp(s - m)
        pc = jnp.exp(sc - m)
        l = jnp.sum(p, axis=-1, keepdims=True) + jnp.sum(pc, axis=-1, keepdims=True) + jnp.exp(sink - m)
        o = (jnp.dot(p.astype(BF16), v_ref[keys, :], preferred_element_type=F32)
             + jnp.dot(pc.astype(BF16), vc, preferred_element_type=F32)) / l
        for g in range(G_B):
            o_ref[rows, g * HEAD_DIM:(g + 1) * HEAD_DIM] = (
                o[g * WIN_Q_CHUNK:(g + 1) * WIN_Q_CHUNK, :].astype(BF16))


def _attn_win(sink, qkv, cache_k, cache_v, layer):
    gw = G_B * HEAD_DIM
    ctx_spec = pl.BlockSpec((None, None, PAST_LEN, HEAD_DIM), lambda b, j: (b, layer, 0, j))
    return pl.pallas_call(
        _attn_win_kernel,
        grid=(DEC_BATCH, N_KV_B),
        in_specs=[
            pl.BlockSpec(memory_space=pltpu.SMEM),
            pl.BlockSpec((DEC_SEQ, gw), lambda b, j: (b, COL_QB // gw + j)),
            pl.BlockSpec((DEC_SEQ, HEAD_DIM), lambda b, j: (b, COL_KB // HEAD_DIM + j)),
            pl.BlockSpec((DEC_SEQ, HEAD_DIM), lambda b, j: (b, COL_VB // HEAD_DIM + j)),
            ctx_spec,
            ctx_spec,
        ],
        out_specs=pl.BlockSpec((DEC_SEQ, gw), lambda b, j: (b, j)),
        out_shape=jax.ShapeDtypeStruct((N_SAMPLE, QB_W), BF16),
        compiler_params=_params("arbitrary", "arbitrary"),
        name="attn_win",
    )(sink, qkv, qkv, qkv, cache_k, cache_v)


REC_CB = 512
REC_SLABS = REC_CB // RG_BLOCK
REC_STREAMS = 8
REC_T = 256
REC_GROUP_ROWS = REC_STREAMS * REC_T
REC_PRE = CONV_PAD_L
REC_POST = CONV_W - 1 - CONV_PAD_L
REC_CHUNK_ROWS = 32 * REC_STREAMS


def _softplus(x):
    return jnp.maximum(x, 0.0) + jnp.log1p(jnp.exp(-jnp.abs(x)))


def _gelu_tanh(x):
    k = np.sqrt(2.0 / np.pi)
    half = 0.5 * x
    return half + half * jnp.tanh(x * (k + (k * 0.044715) * (x * x)))


def _sqrt_nonneg(u):
    return jnp.where(u > 0.0, u * lax.rsqrt(u), 0.0)


def _rec_kernel(x_ref, g_ref, cw_ref, cb_ref, wg_ref, bg_ref, lam_ref, h0_ref, y_ref, st_ref,
                xt_ref, af_ref, bf_ref, ab_ref, bb_ref, *, n_seg):
    S, T = REC_STREAMS, REC_T
    a_refs = (af_ref, ab_ref)
    b_refs = (bf_ref, bb_ref)
    seg = lax.broadcasted_iota(jnp.int32, (S, RG_BLOCK), 0) % n_seg
    slab_cols = [slice(n * RG_BLOCK, (n + 1) * RG_BLOCK) for n in range(REC_SLABS)]

    def t_rows(t):
        return pl.ds(pl.multiple_of(t * S, S), S)

    def from_prev_stream(x):
        return pltpu.roll(x, 1, 0)

    def from_next_stream(x):
        return pltpu.roll(x, S - 1, 0)

    for n, cols in enumerate(slab_cols):
        for s in range(S):
            xt_ref[n, pl.ds(REC_PRE * S + s, T, stride=S), :] = x_ref[s * T:(s + 1) * T, cols]
        for p in range(REC_PRE):
            src = xt_ref[n, (T + p) * S:(T + p + 1) * S, :]
            xt_ref[n, p * S:(p + 1) * S, :] = jnp.where(seg > 0, from_prev_stream(src), 0.0)
        for p in range(REC_POST):
            src = xt_ref[n, (REC_PRE + p) * S:(REC_PRE + p + 1) * S, :]
            xt_ref[n, (REC_PRE + T + p) * S:(REC_PRE + T + p + 1) * S, :] = (
                jnp.where(seg < n_seg - 1, from_next_stream(src), 0.0))

    c_all = (-0.5 * RG_C * np.log2(np.e)) * _softplus(-lam_ref[...])

    def gate_rows(r, carry):
        r0 = pl.multiple_of(r * REC_CHUNK_ROWS, REC_CHUNK_ROWS)
        rows = pl.ds(r0, REC_CHUNK_ROWS)
        for n, cols in enumerate(slab_cols):
            xc = cb_ref[:, cols] + cw_ref[0:1, cols] * xt_ref[n, rows, :]
            for k in range(1, CONV_W):
                tap_rows = pl.ds(pl.multiple_of(r0 + k * S, S), REC_CHUNK_ROWS)
                xc = xc + cw_ref[k:k + 1, cols] * xt_ref[n, tap_rows, :]
            gates = jnp.dot(xc.astype(BF16), wg_ref[n].astype(BF16), preferred_element_type=F32)
            x_half = 0.5 * xc
            for d in range(2):
                ga = gates[:, (2 * d) * RG_BLOCK:(2 * d + 1) * RG_BLOCK] + bg_ref[2 * d:2 * d + 1, cols]
                gx = gates[:, (2 * d + 1) * RG_BLOCK:(2 * d + 2) * RG_BLOCK] + bg_ref[2 * d + 1:2 * d + 2, cols]
                c = c_all[d:d + 1, cols]
                a = jnp.exp2(c * jnp.tanh(0.5 * ga) + c)
                a_refs[d][n, rows, :] = a
                b_refs[d][n, rows, :] = _sqrt_nonneg(1.0 - a * a) * ((1.0 + jnp.tanh(0.5 * gx)) * x_half)
        return carry

    lax.fori_loop(0, T * S // REC_CHUNK_ROWS, gate_rows, 0)

    def scan_step(t, carry):
        hf, hb, pf, pb = carry
        rf, rb = t_rows(t), t_rows(T - 1 - t)
        hf_new, hb_new, pf_new, pb_new = [], [], [], []
        for n in range(REC_SLABS):
            a = af_ref[n, rf, :]
            h = a * hf[n] + bf_ref[n, rf, :]
            bf_ref[n, rf, :] = h
            hf_new.append(h)
            a2 = ab_ref[n, rb, :]
            h2 = a2 * hb[n] + bb_ref[n, rb, :]
            bb_ref[n, rb, :] = h2
            hb_new.append(h2)
            if n_seg > 1:
                p = a * pf[n]
                af_ref[n, rf, :] = p
                pf_new.append(p)
                p2 = a2 * pb[n]
                ab_ref[n, rb, :] = p2
                pb_new.append(p2)
        return tuple(hf_new), tuple(hb_new), tuple(pf_new), tuple(pb_new)

    ones = tuple(jnp.ones((S, RG_BLOCK), F32) for _ in range(REC_SLABS)) if n_seg > 1 else ()
    hf, hb, pf, pb = lax.fori_loop(
        0, T, scan_step,
        (tuple(h0_ref[0, :, cols] for cols in slab_cols), tuple(h0_ref[1, :, cols] for cols in slab_cols),
         ones, ones), unroll=4)

    if n_seg > 1:
        cin_f, cin_b = [], []
        for n in range(REC_SLABS):
            cf = jnp.zeros((S, RG_BLOCK), F32)
            for j in range(1, n_seg):
                cf = jnp.where(seg == j, from_prev_stream(hf[n] + pf[n] * cf), cf)
            cb_in = jnp.zeros((S, RG_BLOCK), F32)
            for j in range(n_seg - 2, -1, -1):
                cb_in = jnp.where(seg == j, from_next_stream(hb[n] + pb[n] * cb_in), cb_in)
            cin_f.append(cf)
            cin_b.append(cb_in)

        def carry_in_step(t, carry):
            rows = t_rows(t)
            for n in range(REC_SLABS):
                bf_ref[n, rows, :] += af_ref[n, rows, :] * cin_f[n]
                bb_ref[n, rows, :] += ab_ref[n, rows, :] * cin_b[n]
            return carry

        lax.fori_loop(0, T, carry_in_step, 0, unroll=4)
        hf = tuple(hf[n] + pf[n] * cin_f[n] for n in range(REC_SLABS))
        hb = tuple(hb[n] + pb[n] * cin_b[n] for n in range(REC_SLABS))

    for n, cols in enumerate(slab_cols):
        st_ref[0, :, cols] = hf[n]
        st_ref[1, :, cols] = hb[n]

    for n, cols in enumerate(slab_cols):
        for s in range(S):
            rows = slice(s * T, (s + 1) * T)
            picked = pl.ds(s, T, stride=S)
            h_sum = bf_ref[n, picked, :] + bb_ref[n, picked, :]
            y_ref[rows, cols] = (h_sum * _gelu_tanh(g_ref[rows, cols])).astype(BF16)


def _rec(xg, n_seg, cw, cb, wg, bg, lam, h0):
    n_grp = xg.shape[0] // REC_GROUP_ROWS
    nc = D_RNN // REC_CB
    vec = lambda rows: pl.BlockSpec((rows, REC_CB), lambda s, c: (0, c))
    state_spec = pl.BlockSpec((2, REC_STREAMS, REC_CB), lambda s, c: (0, s, c))
    slab_scratch = lambda n_t: pltpu.VMEM((REC_SLABS, n_t * REC_STREAMS, RG_BLOCK), F32)
    return pl.pallas_call(
        functools.partial(_rec_kernel, n_seg=n_seg),
        grid=(n_grp, nc),
        in_specs=[
            pl.BlockSpec((REC_GROUP_ROWS, REC_CB), lambda s, c: (s, c)),
            pl.BlockSpec((REC_GROUP_ROWS, REC_CB), lambda s, c: (s, nc + c)),
            vec(CONV_W),
            vec(1),
            pl.BlockSpec((REC_SLABS, RG_BLOCK, 4 * RG_BLOCK), lambda s, c: (c, 0, 0)),
            vec(4),
            vec(2),
            state_spec,
        ],
        out_specs=[
            pl.BlockSpec((REC_GROUP_ROWS, REC_CB), lambda s, c: (s, c)),
            state_spec,
        ],
        out_shape=[
            jax.ShapeDtypeStruct((xg.shape[0], D_RNN), BF16),
            jax.ShapeDtypeStruct((2, n_grp * REC_STREAMS, D_RNN), F32),
        ],
        scratch_shapes=[slab_scratch(REC_PRE + REC_T + REC_POST)] + [slab_scratch(REC_T)] * 4,
        compiler_params=_params("arbitrary", "arbitrary"),
        name="rec",
    )(xg, xg, cw, cb.reshape(1, D_RNN), wg, bg, lam, h0)


def kernel(x_prompt, x_sample, c, cache_a_k, cache_a_v, cache_b_k, cache_b_v, state_rg_fwd, state_rg_bwd, c_ctx, w_ada, b_ada, g_pre_mix, g_post_mix, g_pre_ffn, g_post_ffn, w_att_in, w_att_out, sink_b, rpb_a, w_rec_in, conv_w, conv_b, w_rg_a, b_rg_a, w_rg_x, b_rg_x, rg_lambda, w_rec_out, w_ff1, w_ff2):
    depth = w_ada.shape[0]
    yp = x_prompt.reshape(N_PROMPT, D_MODEL)
    ys = x_sample.reshape(N_SAMPLE, D_MODEL)

    cond8 = jnp.concatenate([c_ctx[None, :], c, jnp.zeros((MOD_ROWS - N_SEG, D_MODEL), F32)], axis=0)
    mod_all = _adaln(cond8, w_ada, b_ada)
    mod_all = mod_all[:, :N_SEG, :].reshape(depth, N_SEG, 6, D_MODEL)
    mod_all = jnp.pad(mod_all, ((0, 0), (0, 0), (0, MOD_ROWS - 6), (0, 0)))

    a_k, a_v, b_k, b_v, s_f, s_b = [], [], [], [], [], []
    for layer in range(depth):
        mod = mod_all[layer]
        li = layer // 2
        g_pre, g_post = g_pre_mix[layer], g_post_mix[layer]
        if layer % 2 == 0:
            qkv_p, ka, va, kb, vb = _proj_att_prompt(yp, g_pre, mod, w_att_in, li)
            qkv_s = _proj_att_sample(ys, g_pre, mod, w_att_in, li)
            a_k.append(ka.reshape(BATCH, SEQ, N_HEADS_A, HEAD_DIM))
            a_v.append(va.reshape(BATCH, SEQ, N_HEADS_A, HEAD_DIM))
            b_k.append(kb.reshape(BATCH, SEQ, N_KV_B, HEAD_DIM))
            b_v.append(vb.reshape(BATCH, SEQ, N_KV_B, HEAD_DIM))
            mix_p = [_attn_ctx(sink_b[li], qkv_p)]
            n_att = cache_a_k.shape[1]
            mix_s = [_attn_na(qkv_s, rpb_a[li],
                              cache_a_k.reshape(DEC_BATCH, n_att, PAST_LEN, QA_W),
                              cache_a_v.reshape(DEC_BATCH, n_att, PAST_LEN, QA_W), li),
                     _attn_win(sink_b[li], qkv_s,
                               cache_b_k.reshape(DEC_BATCH, n_att, PAST_LEN, KVB_W),
                               cache_b_v.reshape(DEC_BATCH, n_att, PAST_LEN, KVB_W), li)]
            w_out = w_att_out
        else:
            xg_p = _proj(PROMPT, yp, g_pre, mod, w_rec_in, li)
            xg_s = _proj(SAMPLE, ys, g_pre, mod, w_rec_in, li)
            wg = jnp.concatenate([w_rg_a[li, 0], w_rg_x[li, 0], w_rg_a[li, 1], w_rg_x[li, 1]], axis=-1)
            bg = jnp.stack([b_rg_a[li, 0], b_rg_x[li, 0], b_rg_a[li, 1], b_rg_x[li, 1]], axis=0)
            rec_args = (conv_w[li], conv_b[li], wg, bg, rg_lambda[li])
            n_seg = DEC_SEQ // REC_T
            assert SEQ == REC_T and BATCH % REC_STREAMS == 0 and DEC_BATCH * n_seg == REC_STREAMS
            h0_p = jnp.zeros((2, BATCH, D_RNN), F32)
            seg_state = jnp.zeros((DEC_BATCH, n_seg, D_RNN), F32)
            h0_s = jnp.stack([seg_state.at[:, 0].set(state_rg_fwd[:, li]).reshape(REC_STREAMS, D_RNN),
                              seg_state.at[:, n_seg - 1].set(state_rg_bwd[:, li]).reshape(REC_STREAMS, D_RNN)])
            rec_p, st_p = _rec(xg_p, 1, *rec_args, h0_p)
            rec_s, _ = _rec(xg_s, n_seg, *rec_args, h0_s)
            mix_p, mix_s = [rec_p], [rec_s]
            s_f.append(st_p[0])
            s_b.append(st_p[1])
            w_out = w_rec_out
        yp = _mixout(PROMPT, mix_p, w_out, li, yp, g_post, mod)
        ys = _mixout(SAMPLE, mix_s, w_out, li, ys, g_post, mod)
        yp = _ffn(PROMPT, yp, g_pre_ffn[layer], g_post_ffn[layer], mod, w_ff1, w_ff2, layer)
        ys = _ffn(SAMPLE, ys, g_pre_ffn[layer], g_post_ffn[layer], mod, w_ff1, w_ff2, layer)

    return (yp.reshape(BATCH, SEQ, D_MODEL), ys.reshape(DEC_BATCH, DEC_SEQ, D_MODEL),
            jnp.stack(a_k, axis=1), jnp.stack(a_v, axis=1), jnp.stack(b_k, axis=1), jnp.stack(b_v, axis=1),
            jnp.stack(s_f, axis=1), jnp.stack(s_b, axis=1))
```

```python
import functools

import jax
import jax.numpy as jnp
import numpy as np
from jax import lax
from jax.experimental import pallas as pl
from jax.experimental.pallas import tpu as pltpu

D_MODEL = 2048
BATCH = 16
SEQ = 256
DEC_BATCH = 2
DEC_SEQ = 1024
PAST_LEN = 256
GRID_W = 64
GRID_R = DEC_SEQ // GRID_W
HEAD_DIM = 128
N_HEADS_A = 8
N_HEADS_B = 8
N_KV_B = 2
G_B = N_HEADS_B // N_KV_B
NA_ROWS = 8
NA_COLS = 16
WIN_B = 128
D_RNN = D_MODEL
N_RG_BLOCKS = 16
RG_BLOCK = D_RNN // N_RG_BLOCKS
CONV_W = 4
CONV_PAD_L = 2
RG_C = 8.0
D_FF = 4 * D_MODEL
ROPE_BASE = 10000.0
EPS = 1e-6
NEG = -1e30
QA_W = N_HEADS_A * HEAD_DIM
QB_W = N_HEADS_B * HEAD_DIM
KVB_W = N_KV_B * HEAD_DIM
D_ATT_IN = 3 * QA_W + QB_W + 2 * KVB_W
SCALE = HEAD_DIM ** -0.5

N_PROMPT = BATCH * SEQ
N_SAMPLE = DEC_BATCH * DEC_SEQ
N_SEG = 1 + DEC_BATCH
MOD_ROWS = 8

COL_QA, COL_KA, COL_VA = 0, QA_W, 2 * QA_W
COL_QB = 3 * QA_W
COL_KB = COL_QB + QB_W
COL_VB = COL_KB + KVB_W

V7X_VMEM_BYTES = 64 * 1024 * 1024
VMEM_LIMIT = V7X_VMEM_BYTES - 4 * 1024 * 1024

TM = 1024
TN = 512
ROW_CHUNK = 16
ROW_GROUP = 16

F32 = jnp.float32
BF16 = jnp.bfloat16


def _params(*sem):
    return pltpu.CompilerParams(dimension_semantics=sem, vmem_limit_bytes=VMEM_LIMIT)


class _Rows:
    def __init__(self, n_rows, seg0, seg_rows):
        self.n = n_rows
        self.seg0 = seg0
        self.seg_rows = seg_rows

    def seg(self, i, tm):
        return self.seg0 + (i * tm) // self.seg_rows


PROMPT = _Rows(N_PROMPT, 0, N_PROMPT)
SAMPLE = _Rows(N_SAMPLE, 1, DEC_SEQ)


def _mod_spec(rows, tm):
    return pl.BlockSpec((None, MOD_ROWS, D_MODEL), lambda i, j: (rows.seg(i, tm), 0, 0))


def _vec_spec():
    return pl.BlockSpec((1, D_MODEL), lambda i, j: (0, 0))


def _rms_scale(x):
    return lax.rsqrt(jnp.mean(x * x, axis=-1, keepdims=True) + EPS)


def _norm_mod_rows(y_ref, g_ref, mod_ref, h_ref, shift_row, tm, h_row0=0):
    shift = mod_ref[shift_row:shift_row + 1, :]
    gain = g_ref[...] * (1.0 + mod_ref[shift_row + 1:shift_row + 2, :])

    def body(r, carry):
        starts = [r * (ROW_GROUP * ROW_CHUNK) + u * ROW_CHUNK for u in range(ROW_GROUP)]
        scales = [_rms_scale(y_ref[pl.ds(pl.multiple_of(r0, ROW_CHUNK), ROW_CHUNK), :]) for r0 in starts]
        for r0, rs in zip(starts, scales):
            y = y_ref[pl.ds(pl.multiple_of(r0, ROW_CHUNK), ROW_CHUNK), :]
            h = (y * rs) * gain + shift
            h_ref[pl.ds(pl.multiple_of(h_row0 + r0, ROW_CHUNK), ROW_CHUNK), :] = h.astype(BF16)
        return carry

    lax.fori_loop(0, tm // (ROW_GROUP * ROW_CHUNK), body, 0)


def _gated_residual_rows(y_ref, o_ref, g_ref, gate, tm):
    gain = gate * g_ref[...]

    def body(r, carry):
        chunks = [pl.ds(pl.multiple_of(r * (ROW_GROUP * ROW_CHUNK) + u * ROW_CHUNK, ROW_CHUNK), ROW_CHUNK)
                  for u in range(ROW_GROUP)]
        scales = [_rms_scale(o_ref[rows, :]) for rows in chunks]
        for rows, rs in zip(chunks, scales):
            o_ref[rows, :] = y_ref[rows, :] + (o_ref[rows, :] * rs) * gain
        return carry

    lax.fori_loop(0, tm // (ROW_GROUP * ROW_CHUNK), body, 0)


def _adaln_kernel(cond_ref, w_ref, b_ref, o_ref):
    c = cond_ref[...]
    s = c / (1.0 + jnp.exp(-c))
    o_ref[...] = jnp.dot(s.astype(BF16), w_ref[...].astype(BF16),
                         preferred_element_type=F32) + b_ref[...]


def _adaln(cond8, w_ada, b_ada):
    depth = w_ada.shape[0]
    n = w_ada.shape[2]
    tn = 1024
    return pl.pallas_call(
        _adaln_kernel,
        grid=(depth, n // tn),
        in_specs=[
            pl.BlockSpec((MOD_ROWS, D_MODEL), lambda l, j: (0, 0)),
            pl.BlockSpec((None, D_MODEL, tn), lambda l, j: (l, 0, j)),
            pl.BlockSpec((None, 1, tn), lambda l, j: (l, 0, j)),
        ],
        out_specs=pl.BlockSpec((None, MOD_ROWS, tn), lambda l, j: (l, 0, j)),
        out_shape=jax.ShapeDtypeStruct((depth, MOD_ROWS, n), F32),
        compiler_params=_params("arbitrary", "arbitrary"),
        name="adaln",
    )(cond8, w_ada, b_ada.reshape(depth, 1, n))


def _rope_tables():
    t = np.arange(DEC_SEQ)
    half = HEAD_DIM // 2
    inv = ROPE_BASE ** (-np.arange(0, half, 2, dtype=np.float64) / half)
    ang_r = (t // GRID_W)[:, None] * inv[None, :]
    ang_c = (t % GRID_W)[:, None] * inv[None, :]
    cos = np.concatenate([np.cos(ang_r), np.cos(ang_r), np.cos(ang_c), np.cos(ang_c)], axis=1)
    sin = np.concatenate([-np.sin(ang_r), np.sin(ang_r), -np.sin(ang_c), np.sin(ang_c)], axis=1)
    return jnp.asarray(cos, F32), jnp.asarray(sin, F32)


def _rope(x, cos, sin_signed):
    quarter = HEAD_DIM // 4
    lane = lax.broadcasted_iota(jnp.int32, x.shape, 1)
    first = (lane % (2 * quarter)) < quarter
    partner = jnp.where(first, pltpu.roll(x, HEAD_DIM - quarter, 1), pltpu.roll(x, quarter, 1))
    return x * cos + partner * sin_signed


def _proj_h_tile(y_ref, g_ref, mod_ref, h_ref, tm):
    row0 = pl.multiple_of(pl.program_id(1) * tm, tm)

    @pl.when(pl.program_id(0) == 0)
    def _():
        _norm_mod_rows(y_ref, g_ref, mod_ref, h_ref, 0, tm, row0)

    return h_ref[pl.ds(row0, tm), :]


def _proj_in_specs(rows, tm, layer, tn=TN):
    last = rows.n // tm - 1

    def tile(j, i):
        return jnp.where(j == 0, i, last)

    return [
        pl.BlockSpec((tm, D_MODEL), lambda j, i: (tile(j, i), 0)),
        pl.BlockSpec((1, D_MODEL), lambda j, i: (0, 0)),
        pl.BlockSpec((None, MOD_ROWS, D_MODEL), lambda j, i: (rows.seg(tile(j, i), tm), 0, 0)),
        pl.BlockSpec((None, D_MODEL, tn), lambda j, i: (layer, 0, j)),
    ]


def _proj_tn(rows):
    return TN if rows.n > N_SAMPLE else 2 * TN


def _proj_kernel(y_ref, g_ref, mod_ref, w_ref, o_ref, h_ref, *, tm):
    h = _proj_h_tile(y_ref, g_ref, mod_ref, h_ref, tm)
    o_ref[...] = jnp.dot(h, w_ref[...].astype(BF16), preferred_element_type=F32)


def _proj(rows, y, g, mod, w, layer):
    n = w.shape[2]
    tm, tn = TM, _proj_tn(rows)
    return pl.pallas_call(
        functools.partial(_proj_kernel, tm=tm),
        grid=(n // tn, rows.n // tm),
        in_specs=_proj_in_specs(rows, tm, layer, tn),
        out_specs=pl.BlockSpec((None, tm, tn), lambda j, i: (j, i, 0)),
        out_shape=jax.ShapeDtypeStruct((n // tn, rows.n, tn), F32),
        scratch_shapes=[pltpu.VMEM((rows.n, D_MODEL), BF16)],
        compiler_params=_params("arbitrary", "arbitrary"),
        name="proj",
    )(y, g.reshape(1, D_MODEL), mod, w)


ATT_TILE_KA = COL_KA // TN
ATT_TILE_VA = COL_VA // TN
ATT_TILE_QB = COL_QB // TN
ATT_TILE_KVB = COL_KB // TN
N_ATT_TILES = D_ATT_IN // TN


def _proj_att_prompt_kernel(y_ref, g_ref, mod_ref, w_ref, o_ref, ka_ref, va_ref, kb_ref, vb_ref, h_ref, *, tm):
    j = pl.program_id(0)
    h = _proj_h_tile(y_ref, g_ref, mod_ref, h_ref, tm)
    acc = jnp.dot(h, w_ref[...].astype(BF16), preferred_element_type=F32)
    o_ref[...] = acc.astype(BF16)

    @pl.when((j >= ATT_TILE_KA) & (j < ATT_TILE_VA))
    def _():
        ka_ref[...] = acc

    @pl.when((j >= ATT_TILE_VA) & (j < ATT_TILE_QB))
    def _():
        va_ref[...] = acc

    @pl.when(j == ATT_TILE_KVB)
    def _():
        kb_ref[...] = acc[:, :KVB_W]
        vb_ref[...] = acc[:, KVB_W:]


def _proj_att_prompt(y, g, mod, w, layer):
    tm = TM
    rows = PROMPT
    last = rows.n // tm - 1

    def kv_map(first, count):
        def index(j, i):
            row = jnp.where(j < first, 0, jnp.where(j >= first + count, last, i))
            return row, jnp.clip(j - first, 0, count - 1)
        return index

    per_head_set = QA_W // TN
    return pl.pallas_call(
        functools.partial(_proj_att_prompt_kernel, tm=tm),
        grid=(N_ATT_TILES, rows.n // tm),
        in_specs=_proj_in_specs(rows, tm, layer),
        out_specs=[
            pl.BlockSpec((None, tm, TN), lambda j, i: (j, i, 0)),
            pl.BlockSpec((tm, TN), kv_map(ATT_TILE_KA, per_head_set)),
            pl.BlockSpec((tm, TN), kv_map(ATT_TILE_VA, per_head_set)),
            pl.BlockSpec((tm, KVB_W), kv_map(ATT_TILE_KVB, 1)),
            pl.BlockSpec((tm, KVB_W), kv_map(ATT_TILE_KVB, 1)),
        ],
        out_shape=[
            jax.ShapeDtypeStruct((N_ATT_TILES, rows.n, TN), BF16),
            jax.ShapeDtypeStruct((rows.n, QA_W), F32),
            jax.ShapeDtypeStruct((rows.n, QA_W), F32),
            jax.ShapeDtypeStruct((rows.n, KVB_W), F32),
            jax.ShapeDtypeStruct((rows.n, KVB_W), F32),
        ],
        scratch_shapes=[pltpu.VMEM((rows.n, D_MODEL), BF16)],
        compiler_params=_params("arbitrary", "arbitrary"),
        name="proj_att_prompt",
    )(y, g.reshape(1, D_MODEL), mod, w)


def _proj_att_sample_kernel(y_ref, g_ref, mod_ref, w_ref, cos_ref, sin_ref, o_ref, h_ref, *, tm):
    j = pl.program_id(0)
    h = _proj_h_tile(y_ref, g_ref, mod_ref, h_ref, tm)
    acc = jnp.dot(h, w_ref[...].astype(BF16), preferred_element_type=F32)

    def store(n_rope_heads):
        cos = cos_ref[...]
        sin = sin_ref[...]
        for hd in range(TN // HEAD_DIM):
            cols = slice(hd * HEAD_DIM, (hd + 1) * HEAD_DIM)
            x = acc[:, cols]
            if hd < n_rope_heads:
                x = _rope(x, cos, sin)
            o_ref[:, cols] = x.astype(BF16)

    @pl.when(j < ATT_TILE_QB)
    def _():
        o_ref[...] = acc.astype(BF16)

    @pl.when((j >= ATT_TILE_QB) & (j < ATT_TILE_KVB))
    def _():
        store(TN // HEAD_DIM)

    @pl.when(j == ATT_TILE_KVB)
    def _():
        store(N_KV_B)


def _proj_att_sample(y, g, mod, w, layer):
    tm = TM
    assert tm == DEC_SEQ
    rows = SAMPLE
    cos, sin = _rope_tables()
    tab_spec = pl.BlockSpec((DEC_SEQ, HEAD_DIM), lambda j, i: (0, 0))
    return pl.pallas_call(
        functools.partial(_proj_att_sample_kernel, tm=tm),
        grid=(N_ATT_TILES, rows.n // tm),
        in_specs=_proj_in_specs(rows, tm, layer) + [tab_spec, tab_spec],
        out_specs=pl.BlockSpec((None, tm, TN), lambda j, i: (j, i, 0)),
        out_shape=jax.ShapeDtypeStruct((N_ATT_TILES, rows.n, TN), BF16),
        scratch_shapes=[pltpu.VMEM((rows.n, D_MODEL), BF16)],
        compiler_params=_params("arbitrary", "arbitrary"),
        name="proj_att_sample",
    )(y, g.reshape(1, D_MODEL), mod, w, cos, sin)


MIXOUT_TM = 512


def _mixout_kernel(*refs, tm, n_parts):
    a_refs = refs[:n_parts]
    w_ref, y_ref, g_ref, mod_ref, o_ref = refs[n_parts:]
    kp = a_refs[0].shape[1]
    for c in range(D_MODEL // TN):
        cols = slice(c * TN, (c + 1) * TN)
        acc = None
        for p in range(n_parts):
            part = jnp.dot(a_refs[p][...], w_ref[p * kp:(p + 1) * kp, cols].astype(BF16),
                           preferred_element_type=F32)
            acc = part if acc is None else acc + part
        o_ref[:, cols] = acc
    _gated_residual_rows(y_ref, o_ref, g_ref, mod_ref[2:3, :], tm)


def _mixout(rows, a_parts, w, layer, y, g, mod):
    tm = MIXOUT_TM
    if isinstance(a_parts, (list, tuple)):
        n_parts, kp = len(a_parts), a_parts[0].shape[1]
        part_specs = [pl.BlockSpec((tm, kp), lambda i: (i, 0)) for _ in range(n_parts)]
    else:
        n_parts, kp = a_parts.shape[0], a_parts.shape[2]
        part_specs = [pl.BlockSpec((None, tm, kp), lambda i, p=p: (p, i, 0)) for p in range(n_parts)]
        a_parts = [a_parts] * n_parts
    assert n_parts * kp == w.shape[1]
    return pl.pallas_call(
        functools.partial(_mixout_kernel, tm=tm, n_parts=n_parts),
        grid=(rows.n // tm,),
        in_specs=part_specs + [
            pl.BlockSpec((None, w.shape[1], D_MODEL), lambda i: (layer, 0, 0), pipeline_mode=pl.Buffered(1)),
            pl.BlockSpec((tm, D_MODEL), lambda i: (i, 0)),
            pl.BlockSpec((1, D_MODEL), lambda i: (0, 0)),
            pl.BlockSpec((None, MOD_ROWS, D_MODEL), lambda i: (rows.seg(i, tm), 0, 0)),
        ],
        out_specs=pl.BlockSpec((tm, D_MODEL), lambda i: (i, 0)),
        out_shape=jax.ShapeDtypeStruct((rows.n, D_MODEL), F32),
        compiler_params=_params("arbitrary"),
        name="mixout",
    )(*a_parts, w, y, g.reshape(1, D_MODEL), mod)


FFN_TF = 256


FFN_PAIR = 2 * FFN_TF


FFN_N_PAIRS = D_FF // FFN_PAIR
FFN_GROUP = 4
FFN_N_GROUPS = FFN_N_PAIRS // FFN_GROUP


def _ffn_kernel(y_ref, g1_ref, g2_ref, mod_ref, w1_hbm, w2_hbm, o_ref, h_ref, a_ref, w1_buf, w2_buf, sem,
                *, tm, layer, n_tiles):
    i = pl.program_id(0)
    g = pl.program_id(1)
    last_tile = n_tiles - 1

    def w1_copy(pair, slot):
        cols = pl.ds(pl.multiple_of(pair * FFN_PAIR, FFN_PAIR), FFN_PAIR)
        return pltpu.make_async_copy(w1_hbm.at[layer, :, cols], w1_buf.at[slot], sem.at[0, slot])

    def w2_copy(pair, slot):
        rows = pl.ds(pl.multiple_of(pair * FFN_PAIR, FFN_PAIR), FFN_PAIR)
        return pltpu.make_async_copy(w2_hbm.at[layer, rows, :], w2_buf.at[slot], sem.at[1, slot])

    def up(slot):
        for half in range(FFN_PAIR // FFN_TF):
            cols = slice(half * FFN_TF, (half + 1) * FFN_TF)
            a = jnp.dot(h_ref[...], w1_buf[slot, :, cols].astype(BF16), preferred_element_type=F32)
            a = jnp.maximum(a, 0.0)
            a_ref[slot, :, cols] = (a * a).astype(BF16)

    def down(a_slot, w_slot):
        a = a_ref[a_slot]
        for c in range(D_MODEL // TN):
            cols = slice(c * TN, (c + 1) * TN)
            o_ref[:, cols] += jnp.dot(a, w2_buf[w_slot, :, cols].astype(BF16), preferred_element_type=F32)

    @pl.when(g == 0)
    def _():
        @pl.when(i == 0)
        def _():
            w1_copy(0, 0).start()
            w2_copy(0, 0).start()

        _norm_mod_rows(y_ref, g1_ref, mod_ref, h_ref, 3, tm)
        o_ref[...] = jnp.zeros_like(o_ref)
        a_ref[1] = jnp.zeros((tm, FFN_PAIR), BF16)

    for u in range(FFN_GROUP):
        q = g * FFN_GROUP + u
        slot = u % 2
        w1_copy(q, slot).wait()
        w2_copy(jnp.maximum(q - 1, 0), slot).wait()
        w1_copy((q + 1) % FFN_N_PAIRS, 1 - slot).start()
        w2_copy(q, 1 - slot).start()
        up(slot)
        down(1 - slot, slot)

    @pl.when(g == FFN_N_GROUPS - 1)
    def _():
        w2_copy(FFN_N_PAIRS - 1, 0).wait()
        down((FFN_N_PAIRS - 1) % 2, 0)

        @pl.when(i < last_tile)
        def _():
            w2_copy(0, 0).start()

        _gated_residual_rows(y_ref, o_ref, g2_ref, mod_ref[5:6, :], tm)

        @pl.when(i == last_tile)
        def _():
            w1_copy(0, 0).wait()


def _ffn(rows, y, g1, g2, mod, w1, w2, layer):
    tm = TM
    n_tiles = rows.n // tm
    assert FFN_GROUP % 2 == 0 and FFN_N_PAIRS % FFN_GROUP == 0
    return pl.pallas_call(
        functools.partial(_ffn_kernel, tm=tm, layer=layer, n_tiles=n_tiles),
        grid=(n_tiles, FFN_N_GROUPS),
        in_specs=[
            pl.BlockSpec((tm, D_MODEL), lambda i, s: (i, 0)),
            _vec_spec(),
            _vec_spec(),
            _mod_spec(rows, tm),
            pl.BlockSpec(memory_space=pl.ANY),
            pl.BlockSpec(memory_space=pl.ANY),
        ],
        out_specs=pl.BlockSpec((tm, D_MODEL), lambda i, s: (i, 0)),
        out_shape=jax.ShapeDtypeStruct((rows.n, D_MODEL), F32),
        scratch_shapes=[
            pltpu.VMEM((tm, D_MODEL), BF16),
            pltpu.VMEM((2, tm, FFN_PAIR), BF16),
            pltpu.VMEM((2, D_MODEL, FFN_PAIR), F32),
            pltpu.VMEM((2, FFN_PAIR, D_MODEL), F32),
            pltpu.SemaphoreType.DMA((2, 2)),
        ],
        compiler_params=_params("arbitrary", "arbitrary"),
        name="ffn",
    )(y, g1.reshape(1, D_MODEL), g2.reshape(1, D_MODEL), mod, w1, w2)


def _qkt(q, k):
    return lax.dot_general(q, k, (((1,), (1,)), ((), ())), preferred_element_type=F32)


CTX_SEQ_PER_STEP = 2


def _attn_ctx_kernel(sink_ref, qkv_ref, o_ref):
    n_rows = G_B * SEQ
    grp = lax.broadcasted_iota(jnp.int32, (n_rows, 1), 0) // SEQ
    for b in range(CTX_SEQ_PER_STEP):
        rows = slice(b * SEQ, (b + 1) * SEQ)

        def head(col):
            tile, off = divmod(col, TN)
            return qkv_ref[tile, rows, off:off + HEAD_DIM]

        for h in range(N_HEADS_A):
            q = head(COL_QA + h * HEAD_DIM)
            k = head(COL_KA + h * HEAD_DIM)
            v = head(COL_VA + h * HEAD_DIM)
            s = _qkt(q, k) * SCALE
            m = jnp.max(s, axis=-1, keepdims=True)
            p = jnp.exp(s - m)
            l = jnp.sum(p, axis=-1, keepdims=True)
            o = jnp.dot(p.astype(BF16), v, preferred_element_type=F32) / l
            o_ref[rows, h * HEAD_DIM:(h + 1) * HEAD_DIM] = o.astype(BF16)
        for j in range(N_KV_B):
            k = head(COL_KB + j * HEAD_DIM)
            v = head(COL_VB + j * HEAD_DIM)
            q = jnp.concatenate([head(COL_QB + (j * G_B + g) * HEAD_DIM) for g in range(G_B)], axis=0)
            sink = jnp.zeros((n_rows, 1), F32)
            for g in range(G_B):
                sink = jnp.where(grp == g, sink_ref[j, g], sink)
            s = _qkt(q, k) * SCALE
            m = jnp.maximum(jnp.max(s, axis=-1, keepdims=True), sink)
            p = jnp.exp(s - m)
            l = jnp.sum(p, axis=-1, keepdims=True) + jnp.exp(sink - m)
            o = jnp.dot(p.astype(BF16), v, preferred_element_type=F32) / l
            for g in range(G_B):
                c0 = QA_W + (j * G_B + g) * HEAD_DIM
                o_ref[rows, c0:c0 + HEAD_DIM] = o[g * SEQ:(g + 1) * SEQ, :].astype(BF16)


def _attn_ctx(sink, qkv):
    rows = CTX_SEQ_PER_STEP * SEQ
    return pl.pallas_call(
        _attn_ctx_kernel,
        grid=(N_PROMPT // rows,),
        in_specs=[
            pl.BlockSpec(memory_space=pltpu.SMEM),
            pl.BlockSpec((N_ATT_TILES, rows, TN), lambda b: (0, b, 0)),
        ],
        out_specs=pl.BlockSpec((rows, D_MODEL), lambda b: (b, 0)),
        out_shape=jax.ShapeDtypeStruct((N_PROMPT, D_MODEL), BF16),
        compiler_params=_params("arbitrary"),
        name="attn_ctx",
    )(sink, qkv)


NA_Q_ROWS = 4
NA_K_ROWS = 12
NA_Q_CHUNK = NA_Q_ROWS * GRID_W
NA_K_SPAN = NA_K_ROWS * GRID_W
NA_K_ROW0 = (0, 0, 4, 4)
N_RPB_ROWS = 2 * NA_ROWS - 1
N_RPB_COLS = 2 * NA_COLS - 1


def _na_row_start(r):
    return min(max(r - NA_ROWS // 2, 0), GRID_R - NA_ROWS)


def _na_build_bias(rpb_ref, tile_ref, bias_ref):
    h = pl.program_id(0)
    shape = (GRID_W, 2 * GRID_W)
    qc = lax.broadcasted_iota(jnp.int32, shape, 0)
    lane = lax.broadcasted_iota(jnp.int32, shape, 1)
    kc = lane % GRID_W
    start_c = jnp.clip(qc - NA_COLS // 2, 0, GRID_W - NA_COLS)
    in_win = (kc >= start_c) & (kc < start_c + NA_COLS)
    dc = jnp.where(in_win, kc - qc + (NA_COLS - 1), -1)
    for dr in range(N_RPB_ROWS):
        t = jnp.full(shape, NEG, F32)
        for c in range(N_RPB_COLS):
            t = jnp.where(dc == c, rpb_ref[h, dr * N_RPB_COLS + c], t)
        tile_ref[dr] = t
    first_half = lane < GRID_W
    neg = jnp.full(shape, NEG, F32)
    for chunk in range(GRID_R // NA_Q_ROWS):
        for qi in range(NA_Q_ROWS):
            qr = chunk * NA_Q_ROWS + qi
            lo = _na_row_start(qr)
            for m in range(NA_K_ROWS // 2):
                kr = NA_K_ROW0[chunk] + 2 * m
                parts = []
                for r in (kr, kr + 1):
                    parts.append(tile_ref[r - qr + NA_ROWS - 1] if lo <= r < lo + NA_ROWS else neg)
                bias_ref[chunk, qi * GRID_W:(qi + 1) * GRID_W, m * 2 * GRID_W:(m + 1) * 2 * GRID_W] = (
                    jnp.where(first_half, parts[0], parts[1]))


def _attn_na_kernel(rpb_ref, q_ref, k_ref, v_ref, kc_ref, vc_ref, o_ref, tile_ref, bias_ref):
    _na_build_bias(rpb_ref, tile_ref, bias_ref)
    for b in range(DEC_BATCH):
        kc = kc_ref[b].astype(BF16)
        vc = vc_ref[b].astype(BF16)
        for c in range(DEC_SEQ // NA_Q_CHUNK):
            r0 = b * DEC_SEQ + c * NA_Q_CHUNK
            k0 = b * DEC_SEQ + NA_K_ROW0[c] * GRID_W
            rows = slice(r0, r0 + NA_Q_CHUNK)
            keys = slice(k0, k0 + NA_K_SPAN)
            q = q_ref[rows, :]
            s = _qkt(q, k_ref[keys, :]) * SCALE + bias_ref[c]
            sc = _qkt(q, kc) * SCALE
            m = jnp.maximum(jnp.max(s, axis=-1, keepdims=True), jnp.max(sc, axis=-1, keepdims=True))
            p = jnp.exp(s - m)
            pc = jnp.exp(sc - m)
            l = jnp.sum(p, axis=-1, keepdims=True) + jnp.sum(pc, axis=-1, keepdims=True)
            o = (jnp.dot(p.astype(BF16), v_ref[keys, :], preferred_element_type=F32)
                 + jnp.dot(pc.astype(BF16), vc, preferred_element_type=F32)) / l
            o_ref[rows, :] = o.astype(BF16)


def _attn_na(qkv, rpb, cache_k, cache_v, layer):
    heads_per_tile = TN // HEAD_DIM

    def head_spec(c0):
        def index(h):
            head = c0 // HEAD_DIM + h
            return head // heads_per_tile, 0, head % heads_per_tile
        return pl.BlockSpec((None, N_SAMPLE, HEAD_DIM), index)

    ctx_spec = pl.BlockSpec((DEC_BATCH, None, PAST_LEN, HEAD_DIM), lambda h: (0, layer, 0, h))
    n_chunks = DEC_SEQ // NA_Q_CHUNK
    return pl.pallas_call(
        _attn_na_kernel,
        grid=(N_HEADS_A,),
        in_specs=[
            pl.BlockSpec(memory_space=pltpu.SMEM),
            head_spec(COL_QA),
            head_spec(COL_KA),
            head_spec(COL_VA),
            ctx_spec,
            ctx_spec,
        ],
        out_specs=pl.BlockSpec((N_SAMPLE, HEAD_DIM), lambda h: (0, h)),
        out_shape=jax.ShapeDtypeStruct((N_SAMPLE, QA_W), BF16),
        scratch_shapes=[pltpu.VMEM((N_RPB_ROWS, GRID_W, 2 * GRID_W), F32),
                        pltpu.VMEM((n_chunks, NA_Q_CHUNK, NA_K_SPAN), F32)],
        compiler_params=_params("arbitrary"),
        name="attn_na",
    )(rpb.reshape(N_HEADS_A, N_RPB_ROWS * N_RPB_COLS), qkv, qkv, qkv, cache_k, cache_v)


WIN_Q_CHUNK = 256
WIN_K_SPAN = WIN_Q_CHUNK + 2 * WIN_B


def _attn_win_kernel(sink_ref, q_ref, k_ref, v_ref, kc_ref, vc_ref, o_ref):
    j = pl.program_id(1)
    kc = kc_ref[...].astype(BF16)
    vc = vc_ref[...].astype(BF16)
    n_rows = G_B * WIN_Q_CHUNK
    grp = lax.broadcasted_iota(jnp.int32, (n_rows, 1), 0) // WIN_Q_CHUNK
    sink = jnp.zeros((n_rows, 1), F32)
    for g in range(G_B):
        sink = jnp.where(grp == g, sink_ref[j, g], sink)
    for c in range(DEC_SEQ // WIN_Q_CHUNK):
        q0 = c * WIN_Q_CHUNK
        k0 = min(max(q0 - WIN_B, 0), DEC_SEQ - WIN_K_SPAN)
        rows = slice(q0, q0 + WIN_Q_CHUNK)
        keys = slice(k0, k0 + WIN_K_SPAN)
        q = jnp.concatenate([q_ref[rows, g * HEAD_DIM:(g + 1) * HEAD_DIM] for g in range(G_B)], axis=0)
        qpos = q0 + lax.broadcasted_iota(jnp.int32, (n_rows, WIN_K_SPAN), 0) % WIN_Q_CHUNK
        kpos = k0 + lax.broadcasted_iota(jnp.int32, (n_rows, WIN_K_SPAN), 1)
        s = jnp.where(jnp.abs(qpos - kpos) <= WIN_B, _qkt(q, k_ref[keys, :]) * SCALE, NEG)
        sc = _qkt(q, kc) * SCALE
        m = jnp.maximum(jnp.maximum(jnp.max(s, axis=-1, keepdims=True),
                                    jnp.max(sc, axis=-1, keepdims=True)), sink)
        p = jnp.exp(s - m)
        pc = jnp.exp(sc - m)
        l = jnp.sum(p, axis=-1, keepdims=True) + jnp.sum(pc, axis=-1, keepdims=True) + jnp.exp(sink - m)
        o = (jnp.dot(p.astype(BF16), v_ref[keys, :], preferred_element_type=F32)
             + jnp.dot(pc.astype(BF16), vc, preferred_element_type=F32)) / l
        for g in range(G_B):
            o_ref[rows, g * HEAD_DIM:(g + 1) * HEAD_DIM] = (
                o[g * WIN_Q_CHUNK:(g + 1) * WIN_Q_CHUNK, :].astype(BF16))


def _attn_win(sink, qkv, cache_k, cache_v, layer):
    gw = G_B * HEAD_DIM
    ctx_spec = pl.BlockSpec((None, None, PAST_LEN, HEAD_DIM), lambda b, j: (b, layer, 0, j))
    return pl.pallas_call(
        _attn_win_kernel,
        grid=(DEC_BATCH, N_KV_B),
        in_specs=[
            pl.BlockSpec(memory_space=pltpu.SMEM),
            pl.BlockSpec((None, DEC_SEQ, gw), lambda b, j: (ATT_TILE_QB + j, b, 0)),
            pl.BlockSpec((None, DEC_SEQ, HEAD_DIM), lambda b, j: (ATT_TILE_KVB, b, j)),
            pl.BlockSpec((None, DEC_SEQ, HEAD_DIM), lambda b, j: (ATT_TILE_KVB, b, N_KV_B + j)),
            ctx_spec,
            ctx_spec,
        ],
        out_specs=pl.BlockSpec((DEC_SEQ, gw), lambda b, j: (b, j)),
        out_shape=jax.ShapeDtypeStruct((N_SAMPLE, QB_W), BF16),
        compiler_params=_params("arbitrary", "arbitrary"),
        name="attn_win",
    )(sink, qkv, qkv, qkv, cache_k, cache_v)


REC_CB = 512
REC_SLABS = REC_CB // RG_BLOCK
REC_STREAMS = 8
REC_T = 256
REC_GROUP_ROWS = REC_STREAMS * REC_T
REC_PRE = CONV_PAD_L
REC_POST = CONV_W - 1 - CONV_PAD_L
REC_CHUNK_ROWS = 32 * REC_STREAMS


def _softplus(x):
    return jnp.maximum(x, 0.0) + jnp.log1p(jnp.exp(-jnp.abs(x)))


def _gelu_tanh(x):
    k = np.sqrt(2.0 / np.pi)
    half = 0.5 * x
    return half + half * jnp.tanh(x * (k + (k * 0.044715) * (x * x)))


def _sqrt_nonneg(u):
    return jnp.where(u > 0.0, u * lax.rsqrt(u), 0.0)


def _rec_kernel(x_ref, g_ref, cw_ref, cb_ref, wg_ref, bg_ref, lam_ref, h0_ref, y_ref, st_ref,
                xt_ref, af_ref, bf_ref, ab_ref, bb_ref, *, n_seg):
    S, T = REC_STREAMS, REC_T
    a_refs = (af_ref, ab_ref)
    b_refs = (bf_ref, bb_ref)
    seg = lax.broadcasted_iota(jnp.int32, (S, RG_BLOCK), 0) % n_seg
    slab_cols = [slice(n * RG_BLOCK, (n + 1) * RG_BLOCK) for n in range(REC_SLABS)]

    def t_rows(t):
        return pl.ds(pl.multiple_of(t * S, S), S)

    def from_prev_stream(x):
        return pltpu.roll(x, 1, 0)

    def from_next_stream(x):
        return pltpu.roll(x, S - 1, 0)

    for n, cols in enumerate(slab_cols):
        for s in range(S):
            xt_ref[n, pl.ds(REC_PRE * S + s, T, stride=S), :] = x_ref[s * T:(s + 1) * T, cols]
        for p in range(REC_PRE):
            src = xt_ref[n, (T + p) * S:(T + p + 1) * S, :]
            xt_ref[n, p * S:(p + 1) * S, :] = jnp.where(seg > 0, from_prev_stream(src), 0.0)
        for p in range(REC_POST):
            src = xt_ref[n, (REC_PRE + p) * S:(REC_PRE + p + 1) * S, :]
            xt_ref[n, (REC_PRE + T + p) * S:(REC_PRE + T + p + 1) * S, :] = (
                jnp.where(seg < n_seg - 1, from_next_stream(src), 0.0))

    c_all = (-0.5 * RG_C * np.log2(np.e)) * _softplus(-lam_ref[...])

    def gate_rows(r, carry):
        r0 = pl.multiple_of(r * REC_CHUNK_ROWS, REC_CHUNK_ROWS)
        rows = pl.ds(r0, REC_CHUNK_ROWS)
        for n, cols in enumerate(slab_cols):
            xc = cb_ref[:, cols] + cw_ref[0:1, cols] * xt_ref[n, rows, :]
            for k in range(1, CONV_W):
                tap_rows = pl.ds(pl.multiple_of(r0 + k * S, S), REC_CHUNK_ROWS)
                xc = xc + cw_ref[k:k + 1, cols] * xt_ref[n, tap_rows, :]
            gates = jnp.dot(xc.astype(BF16), wg_ref[n].astype(BF16), preferred_element_type=F32)
            x_half = 0.5 * xc
            for d in range(2):
                ga = gates[:, (2 * d) * RG_BLOCK:(2 * d + 1) * RG_BLOCK] + bg_ref[2 * d:2 * d + 1, cols]
                gx = gates[:, (2 * d + 1) * RG_BLOCK:(2 * d + 2) * RG_BLOCK] + bg_ref[2 * d + 1:2 * d + 2, cols]
                c = c_all[d:d + 1, cols]
                a = jnp.exp2(c * jnp.tanh(0.5 * ga) + c)
                a_refs[d][n, rows, :] = a
                b_refs[d][n, rows, :] = _sqrt_nonneg(1.0 - a * a) * ((1.0 + jnp.tanh(0.5 * gx)) * x_half)
        return carry

    lax.fori_loop(0, T * S // REC_CHUNK_ROWS, gate_rows, 0)

    def scan_step(t, carry):
        hf, hb, pf, pb = carry
        rf, rb = t_rows(t), t_rows(T - 1 - t)
        hf_new, hb_new, pf_new, pb_new = [], [], [], []
        for n in range(REC_SLABS):
            a = af_ref[n, rf, :]
            h = a * hf[n] + bf_ref[n, rf, :]
            bf_ref[n, rf, :] = h
            hf_new.append(h)
            a2 = ab_ref[n, rb, :]
            h2 = a2 * hb[n] + bb_ref[n, rb, :]
            bb_ref[n, rb, :] = h2
            hb_new.append(h2)
            if n_seg > 1:
                p = a * pf[n]
                af_ref[n, rf, :] = p
                pf_new.append(p)
                p2 = a2 * pb[n]
                ab_ref[n, rb, :] = p2
                pb_new.append(p2)
        return tuple(hf_new), tuple(hb_new), tuple(pf_new), tuple(pb_new)

    ones = tuple(jnp.ones((S, RG_BLOCK), F32) for _ in range(REC_SLABS)) if n_seg > 1 else ()
    hf, hb, pf, pb = lax.fori_loop(
        0, T, scan_step,
        (tuple(h0_ref[0, :, cols] for cols in slab_cols), tuple(h0_ref[1, :, cols] for cols in slab_cols),
         ones, ones), unroll=4)

    if n_seg > 1:
        cin_f, cin_b = [], []
        for n in range(REC_SLABS):
            cf = jnp.zeros((S, RG_BLOCK), F32)
            for j in range(1, n_seg):
                cf = jnp.where(seg == j, from_prev_stream(hf[n] + pf[n] * cf), cf)
            cb_in = jnp.zeros((S, RG_BLOCK), F32)
            for j in range(n_seg - 2, -1, -1):
                cb_in = jnp.where(seg == j, from_next_stream(hb[n] + pb[n] * cb_in), cb_in)
            cin_f.append(cf)
            cin_b.append(cb_in)

        def carry_in_step(t, carry):
            rows = t_rows(t)
            for n in range(REC_SLABS):
                bf_ref[n, rows, :] += af_ref[n, rows, :] * cin_f[n]
                bb_ref[n, rows, :] += ab_ref[n, rows, :] * cin_b[n]
            return carry

        lax.fori_loop(0, T, carry_in_step, 0, unroll=4)
        hf = tuple(hf[n] + pf[n] * cin_f[n] for n in range(REC_SLABS))
        hb = tuple(hb[n] + pb[n] * cin_b[n] for n in range(REC_SLABS))

    for n, cols in enumerate(slab_cols):
        st_ref[0, :, cols] = hf[n]
        st_ref[1, :, cols] = hb[n]

    for n, cols in enumerate(slab_cols):
        for s in range(S):
            rows = slice(s * T, (s + 1) * T)
            picked = pl.ds(s, T, stride=S)
            h_sum = bf_ref[n, picked, :] + bb_ref[n, picked, :]
            y_ref[rows, cols] = (h_sum * _gelu_tanh(g_ref[rows, cols])).astype(BF16)


def _rec(xg, n_seg, cw, cb, wg, bg, lam, h0):
    n_rows, tn = xg.shape[1], xg.shape[2]
    n_grp = n_rows // REC_GROUP_ROWS
    nc = D_RNN // REC_CB
    per_tile = tn // REC_CB

    def branch_spec(first):
        return pl.BlockSpec((None, REC_GROUP_ROWS, REC_CB),
                            lambda s, c: ((first + c) // per_tile, s, (first + c) % per_tile))

    vec = lambda rows: pl.BlockSpec((rows, REC_CB), lambda s, c: (0, c))
    state_spec = pl.BlockSpec((2, REC_STREAMS, REC_CB), lambda s, c: (0, s, c))
    slab_scratch = lambda n_t: pltpu.VMEM((REC_SLABS, n_t * REC_STREAMS, RG_BLOCK), F32)
    return pl.pallas_call(
        functools.partial(_rec_kernel, n_seg=n_seg),
        grid=(n_grp, nc),
        in_specs=[
            branch_spec(0),
            branch_spec(nc),
            vec(CONV_W),
            vec(1),
            pl.BlockSpec((REC_SLABS, RG_BLOCK, 4 * RG_BLOCK), lambda s, c: (c, 0, 0)),
            vec(4),
            vec(2),
            state_spec,
        ],
        out_specs=[
            pl.BlockSpec((None, REC_GROUP_ROWS, REC_CB), lambda s, c: (c, s, 0)),
            state_spec,
        ],
        out_shape=[
            jax.ShapeDtypeStruct((nc, n_rows, REC_CB), BF16),
            jax.ShapeDtypeStruct((2, n_grp * REC_STREAMS, D_RNN), F32),
        ],
        scratch_shapes=[slab_scratch(REC_PRE + REC_T + REC_POST)] + [slab_scratch(REC_T)] * 4,
        compiler_params=_params("arbitrary", "arbitrary"),
        name="rec",
    )(xg, xg, cw, cb.reshape(1, D_RNN), wg, bg, lam, h0)


def kernel(x_prompt, x_sample, c, cache_a_k, cache_a_v, cache_b_k, cache_b_v, state_rg_fwd, state_rg_bwd, c_ctx, w_ada, b_ada, g_pre_mix, g_post_mix, g_pre_ffn, g_post_ffn, w_att_in, w_att_out, sink_b, rpb_a, w_rec_in, conv_w, conv_b, w_rg_a, b_rg_a, w_rg_x, b_rg_x, rg_lambda, w_rec_out, w_ff1, w_ff2):
    depth = w_ada.shape[0]
    yp = x_prompt.reshape(N_PROMPT, D_MODEL)
    ys = x_sample.reshape(N_SAMPLE, D_MODEL)

    cond8 = jnp.concatenate([c_ctx[None, :], c, jnp.zeros((MOD_ROWS - N_SEG, D_MODEL), F32)], axis=0)
    mod_all = _adaln(cond8, w_ada, b_ada)
    mod_all = mod_all[:, :N_SEG, :].reshape(depth, N_SEG, 6, D_MODEL)
    mod_all = jnp.pad(mod_all, ((0, 0), (0, 0), (0, MOD_ROWS - 6), (0, 0)))

    a_k, a_v, b_k, b_v, s_f, s_b = [], [], [], [], [], []
    for layer in range(depth):
        mod = mod_all[layer]
        li = layer // 2
        g_pre, g_post = g_pre_mix[layer], g_post_mix[layer]
        if layer % 2 == 0:
            qkv_p, ka, va, kb, vb = _proj_att_prompt(yp, g_pre, mod, w_att_in, li)
            qkv_s = _proj_att_sample(ys, g_pre, mod, w_att_in, li)
            a_k.append(ka.reshape(BATCH, SEQ, N_HEADS_A, HEAD_DIM))
            a_v.append(va.reshape(BATCH, SEQ, N_HEADS_A, HEAD_DIM))
            b_k.append(kb.reshape(BATCH, SEQ, N_KV_B, HEAD_DIM))
            b_v.append(vb.reshape(BATCH, SEQ, N_KV_B, HEAD_DIM))
            mix_p = [_attn_ctx(sink_b[li], qkv_p)]
            n_att = cache_a_k.shape[1]
            mix_s = [_attn_na(qkv_s, rpb_a[li],
                              cache_a_k.reshape(DEC_BATCH, n_att, PAST_LEN, QA_W),
                              cache_a_v.reshape(DEC_BATCH, n_att, PAST_LEN, QA_W), li),
                     _attn_win(sink_b[li], qkv_s,
                               cache_b_k.reshape(DEC_BATCH, n_att, PAST_LEN, KVB_W),
                               cache_b_v.reshape(DEC_BATCH, n_att, PAST_LEN, KVB_W), li)]
            w_out = w_att_out
        else:
            xg_p = _proj(PROMPT, yp, g_pre, mod, w_rec_in, li)
            xg_s = _proj(SAMPLE, ys, g_pre, mod, w_rec_in, li)
            wg = jnp.concatenate([w_rg_a[li, 0], w_rg_x[li, 0], w_rg_a[li, 1], w_rg_x[li, 1]], axis=-1)
            bg = jnp.stack([b_rg_a[li, 0], b_rg_x[li, 0], b_rg_a[li, 1], b_rg_x[li, 1]], axis=0)
            rec_args = (conv_w[li], conv_b[li], wg, bg, rg_lambda[li])
            n_seg = DEC_SEQ // REC_T
            assert SEQ == REC_T and BATCH % REC_STREAMS == 0 and DEC_BATCH * n_seg == REC_STREAMS
            h0_p = jnp.zeros((2, BATCH, D_RNN), F32)
            seg_state = jnp.zeros((DEC_BATCH, n_seg, D_RNN), F32)
            h0_s = jnp.stack([seg_state.at[:, 0].set(state_rg_fwd[:, li]).reshape(REC_STREAMS, D_RNN),
                              seg_state.at[:, n_seg - 1].set(state_rg_bwd[:, li]).reshape(REC_STREAMS, D_RNN)])
            rec_p, st_p = _rec(xg_p, 1, *rec_args, h0_p)
            rec_s, _ = _rec(xg_s, n_seg, *rec_args, h0_s)
            mix_p, mix_s = rec_p, rec_s
            s_f.append(st_p[0])
            s_b.append(st_p[1])
            w_out = w_rec_out
        yp = _mixout(PROMPT, mix_p, w_out, li, yp, g_post, mod)
        ys = _mixout(SAMPLE, mix_s, w_out, li, ys, g_post, mod)
        yp = _ffn(PROMPT, yp, g_pre_ffn[layer], g_post_ffn[layer], mod, w_ff1, w_ff2, layer)
        ys = _ffn(SAMPLE, ys, g_pre_ffn[layer], g_post_ffn[layer], mod, w_ff1, w_ff2, layer)

    return (yp.reshape(BATCH, SEQ, D_MODEL), ys.reshape(DEC_BATCH, DEC_SEQ, D_MODEL),
            jnp.stack(a_k, axis=1), jnp.stack(a_v, axis=1), jnp.stack(b_k, axis=1), jnp.stack(b_v, axis=1),
            jnp.stack(s_f, axis=1), jnp.stack(s_b, axis=1))
```

```python
import functools

import jax
import jax.numpy as jnp
import numpy as np
from jax import lax
from jax.experimental import pallas as pl
from jax.experimental.pallas import tpu as pltpu

D_MODEL = 2048
BATCH = 16
SEQ = 256
DEC_BATCH = 2
DEC_SEQ = 1024
PAST_LEN = 256
GRID_W = 64
GRID_R = DEC_SEQ // GRID_W
HEAD_DIM = 128
N_HEADS_A = 8
N_HEADS_B = 8
N_KV_B = 2
G_B = N_HEADS_B // N_KV_B
NA_ROWS = 8
NA_COLS = 16
WIN_B = 128
D_RNN = D_MODEL
N_RG_BLOCKS = 16
RG_BLOCK = D_RNN // N_RG_BLOCKS
CONV_W = 4
CONV_PAD_L = 2
RG_C = 8.0
D_FF = 4 * D_MODEL
ROPE_BASE = 10000.0
EPS = 1e-6
NEG = -1e30
QA_W = N_HEADS_A * HEAD_DIM
QB_W = N_HEADS_B * HEAD_DIM
KVB_W = N_KV_B * HEAD_DIM
D_ATT_IN = 3 * QA_W + QB_W + 2 * KVB_W
SCALE = HEAD_DIM ** -0.5

N_PROMPT = BATCH * SEQ
N_SAMPLE = DEC_BATCH * DEC_SEQ
N_SEG = 1 + DEC_BATCH
MOD_ROWS = 8

COL_QA, COL_KA, COL_VA = 0, QA_W, 2 * QA_W
COL_QB = 3 * QA_W
COL_KB = COL_QB + QB_W
COL_VB = COL_KB + KVB_W

V7X_VMEM_BYTES = 64 * 1024 * 1024
VMEM_LIMIT = V7X_VMEM_BYTES - 4 * 1024 * 1024

TM = 1024
TN = 512
ROW_CHUNK = 16
ROW_GROUP = 16

F32 = jnp.float32
BF16 = jnp.bfloat16


def _params(*sem):
    return pltpu.CompilerParams(dimension_semantics=sem, vmem_limit_bytes=VMEM_LIMIT)


class _Rows:
    def __init__(self, n_rows, seg0, seg_rows):
        self.n = n_rows
        self.seg0 = seg0
        self.seg_rows = seg_rows

    def seg(self, i, tm):
        return self.seg0 + (i * tm) // self.seg_rows


PROMPT = _Rows(N_PROMPT, 0, N_PROMPT)
SAMPLE = _Rows(N_SAMPLE, 1, DEC_SEQ)


def _mod_spec(rows, tm):
    return pl.BlockSpec((None, MOD_ROWS, D_MODEL), lambda i, j: (rows.seg(i, tm), 0, 0))


def _vec_spec():
    return pl.BlockSpec((1, D_MODEL), lambda i, j: (0, 0))


def _rms_scale(x):
    return lax.rsqrt(jnp.mean(x * x, axis=-1, keepdims=True) + EPS)


def _norm_mod_rows(y_ref, g_ref, mod_ref, h_ref, shift_row, tm, h_row0=0):
    shift = mod_ref[shift_row:shift_row + 1, :]
    gain = g_ref[...] * (1.0 + mod_ref[shift_row + 1:shift_row + 2, :])

    def body(r, carry):
        starts = [r * (ROW_GROUP * ROW_CHUNK) + u * ROW_CHUNK for u in range(ROW_GROUP)]
        scales = [_rms_scale(y_ref[pl.ds(pl.multiple_of(r0, ROW_CHUNK), ROW_CHUNK), :]) for r0 in starts]
        for r0, rs in zip(starts, scales):
            y = y_ref[pl.ds(pl.multiple_of(r0, ROW_CHUNK), ROW_CHUNK), :]
            h = (y * rs) * gain + shift
            h_ref[pl.ds(pl.multiple_of(h_row0 + r0, ROW_CHUNK), ROW_CHUNK), :] = h.astype(BF16)
        return carry

    lax.fori_loop(0, tm // (ROW_GROUP * ROW_CHUNK), body, 0)


def _gated_residual_rows(y_ref, o_ref, g_ref, gate, tm):
    gain = gate * g_ref[...]

    def body(r, carry):
        chunks = [pl.ds(pl.multiple_of(r * (ROW_GROUP * ROW_CHUNK) + u * ROW_CHUNK, ROW_CHUNK), ROW_CHUNK)
                  for u in range(ROW_GROUP)]
        scales = [_rms_scale(o_ref[rows, :]) for rows in chunks]
        for rows, rs in zip(chunks, scales):
            o_ref[rows, :] = y_ref[rows, :] + (o_ref[rows, :] * rs) * gain
        return carry

    lax.fori_loop(0, tm // (ROW_GROUP * ROW_CHUNK), body, 0)


def _adaln_kernel(cond_ref, w_ref, b_ref, o_ref):
    c = cond_ref[...]
    s = c / (1.0 + jnp.exp(-c))
    o_ref[...] = jnp.dot(s.astype(BF16), w_ref[...].astype(BF16),
                         preferred_element_type=F32) + b_ref[...]


def _adaln(cond8, w_ada, b_ada):
    depth = w_ada.shape[0]
    n = w_ada.shape[2]
    tn = 1024
    return pl.pallas_call(
        _adaln_kernel,
        grid=(depth, n // tn),
        in_specs=[
            pl.BlockSpec((MOD_ROWS, D_MODEL), lambda l, j: (0, 0)),
            pl.BlockSpec((None, D_MODEL, tn), lambda l, j: (l, 0, j)),
            pl.BlockSpec((None, 1, tn), lambda l, j: (l, 0, j)),
        ],
        out_specs=pl.BlockSpec((None, MOD_ROWS, tn), lambda l, j: (l, 0, j)),
        out_shape=jax.ShapeDtypeStruct((depth, MOD_ROWS, n), F32),
        compiler_params=_params("arbitrary", "arbitrary"),
        name="adaln",
    )(cond8, w_ada, b_ada.reshape(depth, 1, n))


def _rope_tables():
    t = np.arange(DEC_SEQ)
    half = HEAD_DIM // 2
    inv = ROPE_BASE ** (-np.arange(0, half, 2, dtype=np.float64) / half)
    ang_r = (t // GRID_W)[:, None] * inv[None, :]
    ang_c = (t % GRID_W)[:, None] * inv[None, :]
    cos = np.concatenate([np.cos(ang_r), np.cos(ang_r), np.cos(ang_c), np.cos(ang_c)], axis=1)
    sin = np.concatenate([-np.sin(ang_r), np.sin(ang_r), -np.sin(ang_c), np.sin(ang_c)], axis=1)
    return jnp.asarray(cos, F32), jnp.asarray(sin, F32)


def _rope(x, cos, sin_signed):
    quarter = HEAD_DIM // 4
    lane = lax.broadcasted_iota(jnp.int32, x.shape, 1)
    first = (lane % (2 * quarter)) < quarter
    partner = jnp.where(first, pltpu.roll(x, HEAD_DIM - quarter, 1), pltpu.roll(x, quarter, 1))
    return x * cos + partner * sin_signed


def _proj_h_tile(y_ref, g_ref, mod_ref, h_ref, tm):
    row0 = pl.multiple_of(pl.program_id(1) * tm, tm)

    @pl.when(pl.program_id(0) == 0)
    def _():
        _norm_mod_rows(y_ref, g_ref, mod_ref, h_ref, 0, tm, row0)

    return h_ref[pl.ds(row0, tm), :]


def _proj_in_specs(rows, tm, layer, tn=TN):
    last = rows.n // tm - 1

    def tile(j, i):
        return jnp.where(j == 0, i, last)

    return [
        pl.BlockSpec((tm, D_MODEL), lambda j, i: (tile(j, i), 0)),
        pl.BlockSpec((1, D_MODEL), lambda j, i: (0, 0)),
        pl.BlockSpec((None, MOD_ROWS, D_MODEL), lambda j, i: (rows.seg(tile(j, i), tm), 0, 0)),
        pl.BlockSpec((None, D_MODEL, tn), lambda j, i: (layer, 0, j)),
    ]


def _proj_tn(rows):
    return TN if rows.n > N_SAMPLE else 2 * TN


def _proj_kernel(y_ref, g_ref, mod_ref, w_ref, o_ref, h_ref, *, tm):
    h = _proj_h_tile(y_ref, g_ref, mod_ref, h_ref, tm)
    o_ref[...] = jnp.dot(h, w_ref[...].astype(BF16), preferred_element_type=F32)


def _proj(rows, y, g, mod, w, layer):
    n = w.shape[2]
    tm, tn = TM, _proj_tn(rows)
    return pl.pallas_call(
        functools.partial(_proj_kernel, tm=tm),
        grid=(n // tn, rows.n // tm),
        in_specs=_proj_in_specs(rows, tm, layer, tn),
        out_specs=pl.BlockSpec((None, tm, tn), lambda j, i: (j, i, 0)),
        out_shape=jax.ShapeDtypeStruct((n // tn, rows.n, tn), F32),
        scratch_shapes=[pltpu.VMEM((rows.n, D_MODEL), BF16)],
        compiler_params=_params("arbitrary", "arbitrary"),
        name="proj",
    )(y, g.reshape(1, D_MODEL), mod, w)


ATT_TILE_KA = COL_KA // TN
ATT_TILE_VA = COL_VA // TN
ATT_TILE_QB = COL_QB // TN
ATT_TILE_KVB = COL_KB // TN
N_ATT_TILES = D_ATT_IN // TN


def _proj_att_prompt_kernel(y_ref, g_ref, mod_ref, w_ref, o_ref, ka_ref, va_ref, kb_ref, vb_ref, h_ref, *, tm):
    j = pl.program_id(0)
    h = _proj_h_tile(y_ref, g_ref, mod_ref, h_ref, tm)
    acc = jnp.dot(h, w_ref[...].astype(BF16), preferred_element_type=F32)
    o_ref[...] = acc.astype(BF16)

    @pl.when((j >= ATT_TILE_KA) & (j < ATT_TILE_VA))
    def _():
        ka_ref[...] = acc

    @pl.when((j >= ATT_TILE_VA) & (j < ATT_TILE_QB))
    def _():
        va_ref[...] = acc

    @pl.when(j == ATT_TILE_KVB)
    def _():
        kb_ref[...] = acc[:, :KVB_W]
        vb_ref[...] = acc[:, KVB_W:]


def _proj_att_prompt(y, g, mod, w, layer):
    tm = TM
    rows = PROMPT
    last = rows.n // tm - 1

    def kv_map(first, count):
        def index(j, i):
            row = jnp.where(j < first, 0, jnp.where(j >= first + count, last, i))
            return row, jnp.clip(j - first, 0, count - 1)
        return index

    per_head_set = QA_W // TN
    return pl.pallas_call(
        functools.partial(_proj_att_prompt_kernel, tm=tm),
        grid=(N_ATT_TILES, rows.n // tm),
        in_specs=_proj_in_specs(rows, tm, layer),
        out_specs=[
            pl.BlockSpec((None, tm, TN), lambda j, i: (j, i, 0)),
            pl.BlockSpec((tm, TN), kv_map(ATT_TILE_KA, per_head_set)),
            pl.BlockSpec((tm, TN), kv_map(ATT_TILE_VA, per_head_set)),
            pl.BlockSpec((tm, KVB_W), kv_map(ATT_TILE_KVB, 1)),
            pl.BlockSpec((tm, KVB_W), kv_map(ATT_TILE_KVB, 1)),
        ],
        out_shape=[
            jax.ShapeDtypeStruct((N_ATT_TILES, rows.n, TN), BF16),
            jax.ShapeDtypeStruct((rows.n, QA_W), F32),
            jax.ShapeDtypeStruct((rows.n, QA_W), F32),
            jax.ShapeDtypeStruct((rows.n, KVB_W), F32),
            jax.ShapeDtypeStruct((rows.n, KVB_W), F32),
        ],
        scratch_shapes=[pltpu.VMEM((rows.n, D_MODEL), BF16)],
        compiler_params=_params("arbitrary", "arbitrary"),
        name="proj_att_prompt",
    )(y, g.reshape(1, D_MODEL), mod, w)


def _proj_att_sample_kernel(y_ref, g_ref, mod_ref, w_ref, cos_ref, sin_ref, o_ref, h_ref, *, tm):
    j = pl.program_id(0)
    h = _proj_h_tile(y_ref, g_ref, mod_ref, h_ref, tm)
    acc = jnp.dot(h, w_ref[...].astype(BF16), preferred_element_type=F32)

    def store(n_rope_heads):
        cos = cos_ref[...]
        sin = sin_ref[...]
        for hd in range(TN // HEAD_DIM):
            cols = slice(hd * HEAD_DIM, (hd + 1) * HEAD_DIM)
            x = acc[:, cols]
            if hd < n_rope_heads:
                x = _rope(x, cos, sin)
            o_ref[:, cols] = x.astype(BF16)

    @pl.when(j < ATT_TILE_QB)
    def _():
        o_ref[...] = acc.astype(BF16)

    @pl.when((j >= ATT_TILE_QB) & (j < ATT_TILE_KVB))
    def _():
        store(TN // HEAD_DIM)

    @pl.when(j == ATT_TILE_KVB)
    def _():
        store(N_KV_B)


def _proj_att_sample(y, g, mod, w, layer):
    tm = TM
    assert tm == DEC_SEQ
    rows = SAMPLE
    cos, sin = _rope_tables()
    tab_spec = pl.BlockSpec((DEC_SEQ, HEAD_DIM), lambda j, i: (0, 0))
    return pl.pallas_call(
        functools.partial(_proj_att_sample_kernel, tm=tm),
        grid=(N_ATT_TILES, rows.n // tm),
        in_specs=_proj_in_specs(rows, tm, layer) + [tab_spec, tab_spec],
        out_specs=pl.BlockSpec((None, tm, TN), lambda j, i: (j, i, 0)),
        out_shape=jax.ShapeDtypeStruct((N_ATT_TILES, rows.n, TN), BF16),
        scratch_shapes=[pltpu.VMEM((rows.n, D_MODEL), BF16)],
        compiler_params=_params("arbitrary", "arbitrary"),
        name="proj_att_sample",
    )(y, g.reshape(1, D_MODEL), mod, w, cos, sin)


MIXOUT_TM = 512


def _mixout_kernel(*refs, tm, n_parts):
    a_refs = refs[:n_parts]
    w_ref, y_ref, g_ref, mod_ref, o_ref = refs[n_parts:]
    kp = a_refs[0].shape[1]
    for c in range(D_MODEL // TN):
        cols = slice(c * TN, (c + 1) * TN)
        acc = None
        for p in range(n_parts):
            part = jnp.dot(a_refs[p][...], w_ref[p * kp:(p + 1) * kp, cols].astype(BF16),
                           preferred_element_type=F32)
            acc = part if acc is None else acc + part
        o_ref[:, cols] = acc
    _gated_residual_rows(y_ref, o_ref, g_ref, mod_ref[2:3, :], tm)


def _mixout(rows, a_parts, w, layer, y, g, mod):
    tm = MIXOUT_TM
    if isinstance(a_parts, (list, tuple)):
        n_parts, kp = len(a_parts), a_parts[0].shape[1]
        part_specs = [pl.BlockSpec((tm, kp), lambda i: (i, 0)) for _ in range(n_parts)]
    else:
        n_parts, kp = a_parts.shape[0], a_parts.shape[2]
        part_specs = [pl.BlockSpec((None, tm, kp), lambda i, p=p: (p, i, 0)) for p in range(n_parts)]
        a_parts = [a_parts] * n_parts
    assert n_parts * kp == w.shape[1]
    return pl.pallas_call(
        functools.partial(_mixout_kernel, tm=tm, n_parts=n_parts),
        grid=(rows.n // tm,),
        in_specs=part_specs + [
            pl.BlockSpec((None, w.shape[1], D_MODEL), lambda i: (layer, 0, 0), pipeline_mode=pl.Buffered(1)),
            pl.BlockSpec((tm, D_MODEL), lambda i: (i, 0)),
            pl.BlockSpec((1, D_MODEL), lambda i: (0, 0)),
            pl.BlockSpec((None, MOD_ROWS, D_MODEL), lambda i: (rows.seg(i, tm), 0, 0)),
        ],
        out_specs=pl.BlockSpec((tm, D_MODEL), lambda i: (i, 0)),
        out_shape=jax.ShapeDtypeStruct((rows.n, D_MODEL), F32),
        compiler_params=_params("arbitrary"),
        name="mixout",
    )(*a_parts, w, y, g.reshape(1, D_MODEL), mod)


FFN_TF = 256


FFN_PAIR = 2 * FFN_TF


FFN_N_PAIRS = D_FF // FFN_PAIR
FFN_GROUP = 4
FFN_N_GROUPS = FFN_N_PAIRS // FFN_GROUP


def _ffn_kernel(y_ref, g1_ref, g2_ref, mod_ref, w1_hbm, w2_hbm, o_ref, h_ref, a_ref, w1_buf, w2_buf, sem,
                *, tm, layer, n_tiles):
    i = pl.program_id(0)
    g = pl.program_id(1)
    last_tile = n_tiles - 1

    def w1_copy(pair, slot):
        cols = pl.ds(pl.multiple_of(pair * FFN_PAIR, FFN_PAIR), FFN_PAIR)
        return pltpu.make_async_copy(w1_hbm.at[layer, :, cols], w1_buf.at[slot], sem.at[0, slot])

    def w2_copy(pair, slot):
        rows = pl.ds(pl.multiple_of(pair * FFN_PAIR, FFN_PAIR), FFN_PAIR)
        return pltpu.make_async_copy(w2_hbm.at[layer, rows, :], w2_buf.at[slot], sem.at[1, slot])

    def up(slot):
        for half in range(FFN_PAIR // FFN_TF):
            cols = slice(half * FFN_TF, (half + 1) * FFN_TF)
            a = jnp.dot(h_ref[...], w1_buf[slot, :, cols].astype(BF16), preferred_element_type=F32)
            a = jnp.maximum(a, 0.0)
            a_ref[slot, :, cols] = (a * a).astype(BF16)

    def down(a_slot, w_slot):
        a = a_ref[a_slot]
        for c in range(D_MODEL // TN):
            cols = slice(c * TN, (c + 1) * TN)
            o_ref[:, cols] += jnp.dot(a, w2_buf[w_slot, :, cols].astype(BF16), preferred_element_type=F32)

    @pl.when(g == 0)
    def _():
        @pl.when(i == 0)
        def _():
            w1_copy(0, 0).start()
            w2_copy(0, 0).start()

        _norm_mod_rows(y_ref, g1_ref, mod_ref, h_ref, 3, tm)
        o_ref[...] = jnp.zeros_like(o_ref)
        a_ref[1] = jnp.zeros((tm, FFN_PAIR), BF16)

    for u in range(FFN_GROUP):
        q = g * FFN_GROUP + u
        slot = u % 2
        w1_copy(q, slot).wait()
        w2_copy(jnp.maximum(q - 1, 0), slot).wait()
        w1_copy((q + 1) % FFN_N_PAIRS, 1 - slot).start()
        w2_copy(q, 1 - slot).start()
        up(slot)
        down(1 - slot, slot)

    @pl.when(g == FFN_N_GROUPS - 1)
    def _():
        w2_copy(FFN_N_PAIRS - 1, 0).wait()
        down((FFN_N_PAIRS - 1) % 2, 0)

        @pl.when(i < last_tile)
        def _():
            w2_copy(0, 0).start()

        _gated_residual_rows(y_ref, o_ref, g2_ref, mod_ref[5:6, :], tm)

        @pl.when(i == last_tile)
        def _():
            w1_copy(0, 0).wait()


def _ffn(rows, y, g1, g2, mod, w1, w2, layer):
    tm = TM
    n_tiles = rows.n // tm
    assert FFN_GROUP % 2 == 0 and FFN_N_PAIRS % FFN_GROUP == 0
    return pl.pallas_call(
        functools.partial(_ffn_kernel, tm=tm, layer=layer, n_tiles=n_tiles),
        grid=(n_tiles, FFN_N_GROUPS),
        in_specs=[
            pl.BlockSpec((tm, D_MODEL), lambda i, s: (i, 0)),
            _vec_spec(),
            _vec_spec(),
            _mod_spec(rows, tm),
            pl.BlockSpec(memory_space=pl.ANY),
            pl.BlockSpec(memory_space=pl.ANY),
        ],
        out_specs=pl.BlockSpec((tm, D_MODEL), lambda i, s: (i, 0)),
        out_shape=jax.ShapeDtypeStruct((rows.n, D_MODEL), F32),
        scratch_shapes=[
            pltpu.VMEM((tm, D_MODEL), BF16),
            pltpu.VMEM((2, tm, FFN_PAIR), BF16),
            pltpu.VMEM((2, D_MODEL, FFN_PAIR), F32),
            pltpu.VMEM((2, FFN_PAIR, D_MODEL), F32),
            pltpu.SemaphoreType.DMA((2, 2)),
        ],
        compiler_params=_params("arbitrary", "arbitrary"),
        name="ffn",
    )(y, g1.reshape(1, D_MODEL), g2.reshape(1, D_MODEL), mod, w1, w2)


def _qkt(q, k):
    return lax.dot_general(q, k, (((1,), (1,)), ((), ())), preferred_element_type=F32)


CTX_SEQ_PER_STEP = 2


def _attn_ctx_kernel(sink_ref, qkv_ref, o_ref):
    n_rows = G_B * SEQ
    grp = lax.broadcasted_iota(jnp.int32, (n_rows, 1), 0) // SEQ
    for b in range(CTX_SEQ_PER_STEP):
        rows = slice(b * SEQ, (b + 1) * SEQ)

        def head(col):
            tile, off = divmod(col, TN)
            return qkv_ref[tile, rows, off:off + HEAD_DIM]

        for h in range(N_HEADS_A):
            q = head(COL_QA + h * HEAD_DIM)
            k = head(COL_KA + h * HEAD_DIM)
            v = head(COL_VA + h * HEAD_DIM)
            s = _qkt(q, k) * SCALE
            m = jnp.max(s, axis=-1, keepdims=True)
            p = jnp.exp(s - m)
            l = jnp.sum(p, axis=-1, keepdims=True)
            o = jnp.dot(p.astype(BF16), v, preferred_element_type=F32) / l
            o_ref[rows, h * HEAD_DIM:(h + 1) * HEAD_DIM] = o.astype(BF16)
        for j in range(N_KV_B):
            k = head(COL_KB + j * HEAD_DIM)
            v = head(COL_VB + j * HEAD_DIM)
            q = jnp.concatenate([head(COL_QB + (j * G_B + g) * HEAD_DIM) for g in range(G_B)], axis=0)
            sink = jnp.zeros((n_rows, 1), F32)
            for g in range(G_B):
                sink = jnp.where(grp == g, sink_ref[j, g], sink)
            s = _qkt(q, k) * SCALE
            m = jnp.maximum(jnp.max(s, axis=-1, keepdims=True), sink)
            p = jnp.exp(s - m)
            l = jnp.sum(p, axis=-1, keepdims=True) + jnp.exp(sink - m)
            o = jnp.dot(p.astype(BF16), v, preferred_element_type=F32) / l
            for g in range(G_B):
                c0 = QA_W + (j * G_B + g) * HEAD_DIM
                o_ref[rows, c0:c0 + HEAD_DIM] = o[g * SEQ:(g + 1) * SEQ, :].astype(BF16)


def _attn_ctx(sink, qkv):
    rows = CTX_SEQ_PER_STEP * SEQ
    return pl.pallas_call(
        _attn_ctx_kernel,
        grid=(N_PROMPT // rows,),
        in_specs=[
            pl.BlockSpec(memory_space=pltpu.SMEM),
            pl.BlockSpec((N_ATT_TILES, rows, TN), lambda b: (0, b, 0)),
        ],
        out_specs=pl.BlockSpec((rows, D_MODEL), lambda b: (b, 0)),
        out_shape=jax.ShapeDtypeStruct((N_PROMPT, D_MODEL), BF16),
        compiler_params=_params("arbitrary"),
        name="attn_ctx",
    )(sink, qkv)


NA_Q_ROWS = 4
NA_K_ROWS = 12
NA_Q_CHUNK = NA_Q_ROWS * GRID_W
NA_K_SPAN = NA_K_ROWS * GRID_W
NA_K_ROW0 = (0, 0, 4, 4)
N_RPB_ROWS = 2 * NA_ROWS - 1
N_RPB_COLS = 2 * NA_COLS - 1


def _na_row_start(r):
    return min(max(r - NA_ROWS // 2, 0), GRID_R - NA_ROWS)


for _chunk, _k0 in enumerate(NA_K_ROW0):
    for _r in range(_chunk * NA_Q_ROWS, (_chunk + 1) * NA_Q_ROWS):
        assert _k0 % 2 == 0 and _k0 <= _na_row_start(_r)
        assert _na_row_start(_r) + NA_ROWS <= _k0 + NA_K_ROWS <= GRID_R


def _na_build_bias(rpb_ref, tile_ref, bias_ref):
    shape = (GRID_W, 2 * GRID_W)
    qc = lax.broadcasted_iota(jnp.int32, shape, 0)
    lane = lax.broadcasted_iota(jnp.int32, shape, 1)
    kc = lane % GRID_W
    start_c = jnp.clip(qc - NA_COLS // 2, 0, GRID_W - NA_COLS)
    in_win = (kc >= start_c) & (kc < start_c + NA_COLS)
    for dr in range(N_RPB_ROWS):
        rows = jnp.broadcast_to(rpb_ref[dr:dr + 1, :], shape)
        shifted = pltpu.roll(rows, 2 * GRID_W - (NA_COLS - 1), 1, stride=1, stride_axis=0)
        tile_ref[dr] = jnp.where(in_win, shifted, NEG)
    first_half = lane < GRID_W
    neg = jnp.full(shape, NEG, F32)
    for chunk in range(GRID_R // NA_Q_ROWS):
        for qi in range(NA_Q_ROWS):
            qr = chunk * NA_Q_ROWS + qi
            lo = _na_row_start(qr)
            for m in range(NA_K_ROWS // 2):
                kr = NA_K_ROW0[chunk] + 2 * m
                parts = []
                for r in (kr, kr + 1):
                    parts.append(tile_ref[r - qr + NA_ROWS - 1] if lo <= r < lo + NA_ROWS else neg)
                bias_ref[chunk, qi * GRID_W:(qi + 1) * GRID_W, m * 2 * GRID_W:(m + 1) * 2 * GRID_W] = (
                    jnp.where(first_half, parts[0], parts[1]))


def _attn_na_kernel(rpb_ref, q_ref, k_ref, v_ref, kc_ref, vc_ref, o_ref, tile_ref, bias_ref):
    _na_build_bias(rpb_ref, tile_ref, bias_ref)
    for b in range(DEC_BATCH):
        kc = kc_ref[b].astype(BF16)
        vc = vc_ref[b].astype(BF16)
        for c in range(DEC_SEQ // NA_Q_CHUNK):
            r0 = b * DEC_SEQ + c * NA_Q_CHUNK
            k0 = b * DEC_SEQ + NA_K_ROW0[c] * GRID_W
            rows = slice(r0, r0 + NA_Q_CHUNK)
            keys = slice(k0, k0 + NA_K_SPAN)
            q = q_ref[rows, :]
            s = _qkt(q, k_ref[keys, :]) * SCALE + bias_ref[c]
            sc = _qkt(q, kc) * SCALE
            m = jnp.maximum(jnp.max(s, axis=-1, keepdims=True), jnp.max(sc, axis=-1, keepdims=True))
            p = jnp.exp(s - m)
            pc = jnp.exp(sc - m)
            l = jnp.sum(p, axis=-1, keepdims=True) + jnp.sum(pc, axis=-1, keepdims=True)
            o = (jnp.dot(p.astype(BF16), v_ref[keys, :], preferred_element_type=F32)
                 + jnp.dot(pc.astype(BF16), vc, preferred_element_type=F32)) / l
            o_ref[rows, :] = o.astype(BF16)


def _attn_na(qkv, rpb, cache_k, cache_v, layer):
    heads_per_tile = TN // HEAD_DIM

    def head_spec(c0):
        def index(h):
            head = c0 // HEAD_DIM + h
            return head // heads_per_tile, 0, head % heads_per_tile
        return pl.BlockSpec((None, N_SAMPLE, HEAD_DIM), index)

    ctx_spec = pl.BlockSpec((DEC_BATCH, None, PAST_LEN, HEAD_DIM), lambda h: (0, layer, 0, h))
    n_chunks = DEC_SEQ // NA_Q_CHUNK
    pad_rows = -N_RPB_ROWS % 8
    half = jnp.pad(rpb.astype(F32), ((0, 0), (0, pad_rows), (0, GRID_W - N_RPB_COLS)), constant_values=NEG)
    rpb_rows = jnp.concatenate([half, half], axis=-1)
    return pl.pallas_call(
        _attn_na_kernel,
        grid=(N_HEADS_A,),
        in_specs=[
            pl.BlockSpec((None, rpb_rows.shape[1], 2 * GRID_W), lambda h: (h, 0, 0)),
            head_spec(COL_QA),
            head_spec(COL_KA),
            head_spec(COL_VA),
            ctx_spec,
            ctx_spec,
        ],
        out_specs=pl.BlockSpec((N_SAMPLE, HEAD_DIM), lambda h: (0, h)),
        out_shape=jax.ShapeDtypeStruct((N_SAMPLE, QA_W), BF16),
        scratch_shapes=[pltpu.VMEM((N_RPB_ROWS, GRID_W, 2 * GRID_W), F32),
                        pltpu.VMEM((n_chunks, NA_Q_CHUNK, NA_K_SPAN), F32)],
        compiler_params=_params("arbitrary"),
        name="attn_na",
    )(rpb_rows, qkv, qkv, qkv, cache_k, cache_v)


WIN_Q_CHUNK = 128
WIN_K_SPAN = WIN_Q_CHUNK + 2 * WIN_B


def _attn_win_kernel(sink_ref, q_ref, k_ref, v_ref, kc_ref, vc_ref, o_ref):
    j = pl.program_id(1)
    kc = kc_ref[...].astype(BF16)
    vc = vc_ref[...].astype(BF16)
    n_rows = G_B * WIN_Q_CHUNK
    grp = lax.broadcasted_iota(jnp.int32, (n_rows, 1), 0) // WIN_Q_CHUNK
    sink = jnp.zeros((n_rows, 1), F32)
    for g in range(G_B):
        sink = jnp.where(grp == g, sink_ref[j, g], sink)
    for c in range(DEC_SEQ // WIN_Q_CHUNK):
        q0 = c * WIN_Q_CHUNK
        k0 = min(max(q0 - WIN_B, 0), DEC_SEQ - WIN_K_SPAN)
        rows = slice(q0, q0 + WIN_Q_CHUNK)
        keys = slice(k0, k0 + WIN_K_SPAN)
        q = jnp.concatenate([q_ref[rows, g * HEAD_DIM:(g + 1) * HEAD_DIM] for g in range(G_B)], axis=0)
        qpos = q0 + lax.broadcasted_iota(jnp.int32, (n_rows, WIN_K_SPAN), 0) % WIN_Q_CHUNK
        kpos = k0 + lax.broadcasted_iota(jnp.int32, (n_rows, WIN_K_SPAN), 1)
        s = jnp.where(jnp.abs(qpos - kpos) <= WIN_B, _qkt(q, k_ref[keys, :]) * SCALE, NEG)
        sc = _qkt(q, kc) * SCALE
        m = jnp.maximum(jnp.maximum(jnp.max(s, axis=-1, keepdims=True),
                                    jnp.max(sc, axis=-1, keepdims=True)), sink)
        p = jnp.exp(s - m)
        pc = jnp.exp(sc - m)
        l = jnp.sum(p, axis=-1, keepdims=True) + jnp.sum(pc, axis=-1, keepdims=True) + jnp.exp(sink - m)
        o = (jnp.dot(p.astype(BF16), v_ref[keys, :], preferred_element_type=F32)
             + jnp.dot(pc.astype(BF16), vc, preferred_element_type=F32)) / l
        for g in range(G_B):
            o_ref[rows, g * HEAD_DIM:(g + 1) * HEAD_DIM] = (
                o[g * WIN_Q_CHUNK:(g + 1) * WIN_Q_CHUNK, :].astype(BF16))


def _attn_win(sink, qkv, cache_k, cache_v, layer):
    gw = G_B * HEAD_DIM
    ctx_spec = pl.BlockSpec((None, None, PAST_LEN, HEAD_DIM), lambda b, j: (b, layer, 0, j))
    return pl.pallas_call(
        _attn_win_kernel,
        grid=(DEC_BATCH, N_KV_B),
        in_specs=[
            pl.BlockSpec(memory_space=pltpu.SMEM),
            pl.BlockSpec((None, DEC_SEQ, gw), lambda b, j: (ATT_TILE_QB + j, b, 0)),
            pl.BlockSpec((None, DEC_SEQ, HEAD_DIM), lambda b, j: (ATT_TILE_KVB, b, j)),
            pl.BlockSpec((None, DEC_SEQ, HEAD_DIM), lambda b, j: (ATT_TILE_KVB, b, N_KV_B + j)),
            ctx_spec,
            ctx_spec,
        ],
        out_specs=pl.BlockSpec((DEC_SEQ, gw), lambda b, j: (b, j)),
        out_shape=jax.ShapeDtypeStruct((N_SAMPLE, QB_W), BF16),
        compiler_params=_params("arbitrary", "arbitrary"),
        name="attn_win",
    )(sink, qkv, qkv, qkv, cache_k, cache_v)


REC_CB = 512
REC_SLABS = REC_CB // RG_BLOCK
REC_STREAMS = 8
REC_T = 256
REC_GROUP_ROWS = REC_STREAMS * REC_T
REC_PRE = CONV_PAD_L
REC_POST = CONV_W - 1 - CONV_PAD_L
REC_CHUNK_ROWS = 32 * REC_STREAMS


def _softplus(x):
    return jnp.maximum(x, 0.0) + jnp.log1p(jnp.exp(-jnp.abs(x)))


def _gelu_tanh(x):
    k = np.sqrt(2.0 / np.pi)
    half = 0.5 * x
    return half + half * jnp.tanh(x * (k + (k * 0.044715) * (x * x)))


def _sqrt_nonneg(u):
    return jnp.where(u > 0.0, u * lax.rsqrt(u), 0.0)


def _rec_kernel(x_ref, g_ref, cw_ref, cb_ref, wg_ref, bg_ref, lam_ref, h0_ref, y_ref, st_ref,
                xt_ref, af_ref, bf_ref, ab_ref, bb_ref, *, n_seg):
    S, T = REC_STREAMS, REC_T
    a_refs = (af_ref, ab_ref)
    b_refs = (bf_ref, bb_ref)
    seg = lax.broadcasted_iota(jnp.int32, (S, RG_BLOCK), 0) % n_seg
    slab_cols = [slice(n * RG_BLOCK, (n + 1) * RG_BLOCK) for n in range(REC_SLABS)]

    def t_rows(t):
        return pl.ds(pl.multiple_of(t * S, S), S)

    def from_prev_stream(x):
        return pltpu.roll(x, 1, 0)

    def from_next_stream(x):
        return pltpu.roll(x, S - 1, 0)

    for n, cols in enumerate(slab_cols):
        for s in range(S):
            xt_ref[n, pl.ds(REC_PRE * S + s, T, stride=S), :] = x_ref[s * T:(s + 1) * T, cols]
        for p in range(REC_PRE):
            src = xt_ref[n, (T + p) * S:(T + p + 1) * S, :]
            xt_ref[n, p * S:(p + 1) * S, :] = jnp.where(seg > 0, from_prev_stream(src), 0.0)
        for p in range(REC_POST):
            src = xt_ref[n, (REC_PRE + p) * S:(REC_PRE + p + 1) * S, :]
            xt_ref[n, (REC_PRE + T + p) * S:(REC_PRE + T + p + 1) * S, :] = (
                jnp.where(seg < n_seg - 1, from_next_stream(src), 0.0))

    c_all = (-0.5 * RG_C * np.log2(np.e)) * _softplus(-lam_ref[...])

    def gate_rows(r, carry):
        r0 = pl.multiple_of(r * REC_CHUNK_ROWS, REC_CHUNK_ROWS)
        rows = pl.ds(r0, REC_CHUNK_ROWS)
        for n, cols in enumerate(slab_cols):
            xc = cb_ref[:, cols] + cw_ref[0:1, cols] * xt_ref[n, rows, :]
            for k in range(1, CONV_W):
                tap_rows = pl.ds(pl.multiple_of(r0 + k * S, S), REC_CHUNK_ROWS)
                xc = xc + cw_ref[k:k + 1, cols] * xt_ref[n, tap_rows, :]
            gates = jnp.dot(xc.astype(BF16), wg_ref[n].astype(BF16), preferred_element_type=F32)
            x_half = 0.5 * xc
            for d in range(2):
                ga_half = gates[:, (2 * d) * RG_BLOCK:(2 * d + 1) * RG_BLOCK] + bg_ref[2 * d:2 * d + 1, cols]
                gx_half = (gates[:, (2 * d + 1) * RG_BLOCK:(2 * d + 2) * RG_BLOCK]
                           + bg_ref[2 * d + 1:2 * d + 2, cols])
                c = c_all[d:d + 1, cols]
                a = jnp.exp2(c * jnp.tanh(ga_half) + c)
                a_refs[d][n, rows, :] = a
                b_refs[d][n, rows, :] = _sqrt_nonneg(1.0 - a * a) * ((1.0 + jnp.tanh(gx_half)) * x_half)
        return carry

    lax.fori_loop(0, T * S // REC_CHUNK_ROWS, gate_rows, 0)

    def scan_step(t, carry):
        hf, hb, pf, pb = carry
        rf, rb = t_rows(t), t_rows(T - 1 - t)
        hf_new, hb_new, pf_new, pb_new = [], [], [], []
        for n in range(REC_SLABS):
            a = af_ref[n, rf, :]
            h = a * hf[n] + bf_ref[n, rf, :]
            bf_ref[n, rf, :] = h
            hf_new.append(h)
            a2 = ab_ref[n, rb, :]
            h2 = a2 * hb[n] + bb_ref[n, rb, :]
            bb_ref[n, rb, :] = h2
            hb_new.append(h2)
            if n_seg > 1:
                p = a * pf[n]
                af_ref[n, rf, :] = p
                pf_new.append(p)
                p2 = a2 * pb[n]
                ab_ref[n, rb, :] = p2
                pb_new.append(p2)
        return tuple(hf_new), tuple(hb_new), tuple(pf_new), tuple(pb_new)

    ones = tuple(jnp.ones((S, RG_BLOCK), F32) for _ in range(REC_SLABS)) if n_seg > 1 else ()
    hf, hb, pf, pb = lax.fori_loop(
        0, T, scan_step,
        (tuple(h0_ref[0, :, cols] for cols in slab_cols), tuple(h0_ref[1, :, cols] for cols in slab_cols),
         ones, ones), unroll=4)

    if n_seg > 1:
        cin_f, cin_b = [], []
        for n in range(REC_SLABS):
            cf = jnp.zeros((S, RG_BLOCK), F32)
            for j in range(1, n_seg):
                cf = jnp.where(seg == j, from_prev_stream(hf[n] + pf[n] * cf), cf)
            cb_in = jnp.zeros((S, RG_BLOCK), F32)
            for j in range(n_seg - 2, -1, -1):
                cb_in = jnp.where(seg == j, from_next_stream(hb[n] + pb[n] * cb_in), cb_in)
            cin_f.append(cf)
            cin_b.append(cb_in)

        def carry_in_step(t, carry):
            rows = t_rows(t)
            for n in range(REC_SLABS):
                bf_ref[n, rows, :] += af_ref[n, rows, :] * cin_f[n]
                bb_ref[n, rows, :] += ab_ref[n, rows, :] * cin_b[n]
            return carry

        lax.fori_loop(0, T, carry_in_step, 0, unroll=4)
        hf = tuple(hf[n] + pf[n] * cin_f[n] for n in range(REC_SLABS))
        hb = tuple(hb[n] + pb[n] * cin_b[n] for n in range(REC_SLABS))

    for n, cols in enumerate(slab_cols):
        st_ref[0, :, cols] = hf[n]
        st_ref[1, :, cols] = hb[n]

    for n, cols in enumerate(slab_cols):
        for s in range(S):
            rows = slice(s * T, (s + 1) * T)
            picked = pl.ds(s, T, stride=S)
            h_sum = bf_ref[n, picked, :] + bb_ref[n, picked, :]
            y_ref[rows, cols] = (h_sum * _gelu_tanh(g_ref[rows, cols])).astype(BF16)


def _rec(xg, n_seg, cw, cb, wg, bg, lam, h0):
    n_rows, tn = xg.shape[1], xg.shape[2]
    n_grp = n_rows // REC_GROUP_ROWS
    nc = D_RNN // REC_CB
    per_tile = tn // REC_CB

    def branch_spec(first):
        return pl.BlockSpec((None, REC_GROUP_ROWS, REC_CB),
                            lambda s, c: ((first + c) // per_tile, s, (first + c) % per_tile))

    vec = lambda rows: pl.BlockSpec((rows, REC_CB), lambda s, c: (0, c))
    state_spec = pl.BlockSpec((2, REC_STREAMS, REC_CB), lambda s, c: (0, s, c))
    slab_scratch = lambda n_t: pltpu.VMEM((REC_SLABS, n_t * REC_STREAMS, RG_BLOCK), F32)
    return pl.pallas_call(
        functools.partial(_rec_kernel, n_seg=n_seg),
        grid=(n_grp, nc),
        in_specs=[
            branch_spec(0),
            branch_spec(nc),
            vec(CONV_W),
            vec(1),
            pl.BlockSpec((REC_SLABS, RG_BLOCK, 4 * RG_BLOCK), lambda s, c: (c, 0, 0)),
            vec(4),
            vec(2),
            state_spec,
        ],
        out_specs=[
            pl.BlockSpec((None, REC_GROUP_ROWS, REC_CB), lambda s, c: (c, s, 0)),
            state_spec,
        ],
        out_shape=[
            jax.ShapeDtypeStruct((nc, n_rows, REC_CB), BF16),
            jax.ShapeDtypeStruct((2, n_grp * REC_STREAMS, D_RNN), F32),
        ],
        scratch_shapes=[slab_scratch(REC_PRE + REC_T + REC_POST)] + [slab_scratch(REC_T)] * 4,
        compiler_params=_params("arbitrary", "arbitrary"),
        name="rec",
    )(xg, xg, cw, cb.reshape(1, D_RNN), wg, bg, lam, h0)


def kernel(x_prompt, x_sample, c, cache_a_k, cache_a_v, cache_b_k, cache_b_v, state_rg_fwd, state_rg_bwd, c_ctx, w_ada, b_ada, g_pre_mix, g_post_mix, g_pre_ffn, g_post_ffn, w_att_in, w_att_out, sink_b, rpb_a, w_rec_in, conv_w, conv_b, w_rg_a, b_rg_a, w_rg_x, b_rg_x, rg_lambda, w_rec_out, w_ff1, w_ff2):
    depth = w_ada.shape[0]
    yp = x_prompt.reshape(N_PROMPT, D_MODEL)
    ys = x_sample.reshape(N_SAMPLE, D_MODEL)

    cond8 = jnp.concatenate([c_ctx[None, :], c, jnp.zeros((MOD_ROWS - N_SEG, D_MODEL), F32)], axis=0)
    mod_all = _adaln(cond8, w_ada, b_ada)
    mod_all = mod_all[:, :N_SEG, :].reshape(depth, N_SEG, 6, D_MODEL)
    mod_all = jnp.pad(mod_all, ((0, 0), (0, 0), (0, MOD_ROWS - 6), (0, 0)))

    a_k, a_v, b_k, b_v, s_f, s_b = [], [], [], [], [], []
    for layer in range(depth):
        mod = mod_all[layer]
        li = layer // 2
        g_pre, g_post = g_pre_mix[layer], g_post_mix[layer]
        if layer % 2 == 0:
            qkv_p, ka, va, kb, vb = _proj_att_prompt(yp, g_pre, mod, w_att_in, li)
            qkv_s = _proj_att_sample(ys, g_pre, mod, w_att_in, li)
            a_k.append(ka.reshape(BATCH, SEQ, N_HEADS_A, HEAD_DIM))
            a_v.append(va.reshape(BATCH, SEQ, N_HEADS_A, HEAD_DIM))
            b_k.append(kb.reshape(BATCH, SEQ, N_KV_B, HEAD_DIM))
            b_v.append(vb.reshape(BATCH, SEQ, N_KV_B, HEAD_DIM))
            mix_p = [_attn_ctx(sink_b[li], qkv_p)]
            n_att = cache_a_k.shape[1]
            mix_s = [_attn_na(qkv_s, rpb_a[li],
                              cache_a_k.reshape(DEC_BATCH, n_att, PAST_LEN, QA_W),
                              cache_a_v.reshape(DEC_BATCH, n_att, PAST_LEN, QA_W), li),
                     _attn_win(sink_b[li], qkv_s,
                               cache_b_k.reshape(DEC_BATCH, n_att, PAST_LEN, KVB_W),
                               cache_b_v.reshape(DEC_BATCH, n_att, PAST_LEN, KVB_W), li)]
            w_out = w_att_out
        else:
            xg_p = _proj(PROMPT, yp, g_pre, mod, w_rec_in, li)
            xg_s = _proj(SAMPLE, ys, g_pre, mod, w_rec_in, li)
            wg = 0.5 * jnp.concatenate([w_rg_a[li, 0], w_rg_x[li, 0], w_rg_a[li, 1], w_rg_x[li, 1]], axis=-1)
            bg = 0.5 * jnp.stack([b_rg_a[li, 0], b_rg_x[li, 0], b_rg_a[li, 1], b_rg_x[li, 1]], axis=0)
            rec_args = (conv_w[li], conv_b[li], wg, bg, rg_lambda[li])
            n_seg = DEC_SEQ // REC_T
            assert SEQ == REC_T and BATCH % REC_STREAMS == 0 and DEC_BATCH * n_seg == REC_STREAMS
            h0_p = jnp.zeros((2, BATCH, D_RNN), F32)
            seg_state = jnp.zeros((DEC_BATCH, n_seg, D_RNN), F32)
            h0_s = jnp.stack([seg_state.at[:, 0].set(state_rg_fwd[:, li]).reshape(REC_STREAMS, D_RNN),
                              seg_state.at[:, n_seg - 1].set(state_rg_bwd[:, li]).reshape(REC_STREAMS, D_RNN)])
            rec_p, st_p = _rec(xg_p, 1, *rec_args, h0_p)
            rec_s, _ = _rec(xg_s, n_seg, *rec_args, h0_s)
            mix_p, mix_s = rec_p, rec_s
            s_f.append(st_p[0])
            s_b.append(st_p[1])
            w_out = w_rec_out
        yp = _mixout(PROMPT, mix_p, w_out, li, yp, g_post, mod)
        ys = _mixout(SAMPLE, mix_s, w_out, li, ys, g_post, mod)
        yp = _ffn(PROMPT, yp, g_pre_ffn[layer], g_post_ffn[layer], mod, w_ff1, w_ff2, layer)
        ys = _ffn(SAMPLE, ys, g_pre_ffn[layer], g_post_ffn[layer], mod, w_ff1, w_ff2, layer)

    return (yp.reshape(BATCH, SEQ, D_MODEL), ys.reshape(DEC_BATCH, DEC_SEQ, D_MODEL),
            jnp.stack(a_k, axis=1), jnp.stack(a_v, axis=1), jnp.stack(b_k, axis=1), jnp.stack(b_v, axis=1),
            jnp.stack(s_f, axis=1), jnp.stack(s_b, axis=1))
```

```python
import functools

import jax
import jax.numpy as jnp
import numpy as np
from jax import lax
from jax.experimental import pallas as pl
from jax.experimental.pallas import tpu as pltpu

D_MODEL = 2048
BATCH = 16
SEQ = 256
DEC_BATCH = 2
DEC_SEQ = 1024
PAST_LEN = 256
GRID_W = 64
GRID_R = DEC_SEQ // GRID_W
HEAD_DIM = 128
N_HEADS_A = 8
N_HEADS_B = 8
N_KV_B = 2
G_B = N_HEADS_B // N_KV_B
NA_ROWS = 8
NA_COLS = 16
WIN_B = 128
D_RNN = D_MODEL
N_RG_BLOCKS = 16
RG_BLOCK = D_RNN // N_RG_BLOCKS
CONV_W = 4
CONV_PAD_L = 2
RG_C = 8.0
D_FF = 4 * D_MODEL
ROPE_BASE = 10000.0
EPS = 1e-6
NEG = -1e30
QA_W = N_HEADS_A * HEAD_DIM
QB_W = N_HEADS_B * HEAD_DIM
KVB_W = N_KV_B * HEAD_DIM
D_ATT_IN = 3 * QA_W + QB_W + 2 * KVB_W
SCALE = HEAD_DIM ** -0.5

N_PROMPT = BATCH * SEQ
N_SAMPLE = DEC_BATCH * DEC_SEQ
N_SEG = 1 + DEC_BATCH
N_MOD = 6
MOD_ROWS = 8

COL_QA, COL_KA, COL_VA = 0, QA_W, 2 * QA_W
COL_QB = 3 * QA_W
COL_KB = COL_QB + QB_W
COL_VB = COL_KB + KVB_W

V7X_VMEM_BYTES = 64 * 1024 * 1024
VMEM_LIMIT = V7X_VMEM_BYTES - 4 * 1024 * 1024

TM = 1024
TN = 512
ROW_CHUNK = 16
ROW_GROUP = 16

F32 = jnp.float32
BF16 = jnp.bfloat16


def _params(*sem):
    return pltpu.CompilerParams(dimension_semantics=sem, vmem_limit_bytes=VMEM_LIMIT)


class _Rows:
    def __init__(self, n_rows, seg0, seg_rows):
        self.n = n_rows
        self.seg0 = seg0
        self.seg_rows = seg_rows

    def seg(self, i, tm):
        return self.seg0 + (i * tm) // self.seg_rows


PROMPT = _Rows(N_PROMPT, 0, N_PROMPT)
SAMPLE = _Rows(N_SAMPLE, 1, DEC_SEQ)


def _mod_spec(rows, tm):
    return pl.BlockSpec((None, MOD_ROWS, D_MODEL), lambda i, j: (rows.seg(i, tm), 0, 0))


def _vec_spec():
    return pl.BlockSpec((1, D_MODEL), lambda i, j: (0, 0))


def _rms_scale(x):
    return lax.rsqrt(jnp.mean(x * x, axis=-1, keepdims=True) + EPS)


def _norm_mod_rows(y_ref, g_ref, mod_ref, h_ref, shift_row, tm, h_row0=0):
    shift = mod_ref[shift_row:shift_row + 1, :]
    gain = g_ref[...] * (1.0 + mod_ref[shift_row + 1:shift_row + 2, :])

    def body(r, carry):
        starts = [r * (ROW_GROUP * ROW_CHUNK) + u * ROW_CHUNK for u in range(ROW_GROUP)]
        scales = [_rms_scale(y_ref[pl.ds(pl.multiple_of(r0, ROW_CHUNK), ROW_CHUNK), :]) for r0 in starts]
        for r0, rs in zip(starts, scales):
            y = y_ref[pl.ds(pl.multiple_of(r0, ROW_CHUNK), ROW_CHUNK), :]
            h = (y * rs) * gain + shift
            h_ref[pl.ds(pl.multiple_of(h_row0 + r0, ROW_CHUNK), ROW_CHUNK), :] = h.astype(BF16)
        return carry

    lax.fori_loop(0, tm // (ROW_GROUP * ROW_CHUNK), body, 0)


def _gated_residual_rows(y_ref, o_ref, g_ref, gate, tm):
    gain = gate * g_ref[...]

    def body(r, carry):
        chunks = [pl.ds(pl.multiple_of(r * (ROW_GROUP * ROW_CHUNK) + u * ROW_CHUNK, ROW_CHUNK), ROW_CHUNK)
                  for u in range(ROW_GROUP)]
        scales = [_rms_scale(o_ref[rows, :]) for rows in chunks]
        for rows, rs in zip(chunks, scales):
            o_ref[rows, :] = y_ref[rows, :] + (o_ref[rows, :] * rs) * gain
        return carry

    lax.fori_loop(0, tm // (ROW_GROUP * ROW_CHUNK), body, 0)


def _adaln_kernel(cond_ref, w_ref, b_ref, o_ref):
    c = cond_ref[...]
    s = c / (1.0 + jnp.exp(-c))
    o_ref[...] = jnp.dot(s.astype(BF16), w_ref[...].astype(BF16),
                         preferred_element_type=F32) + b_ref[...]


def _adaln(cond8, w_ada, b_ada):
    depth = w_ada.shape[0]
    n = w_ada.shape[2]
    tn = 1024
    return pl.pallas_call(
        _adaln_kernel,
        grid=(depth, n // tn),
        in_specs=[
            pl.BlockSpec((MOD_ROWS, D_MODEL), lambda l, j: (0, 0)),
            pl.BlockSpec((None, D_MODEL, tn), lambda l, j: (l, 0, j)),
            pl.BlockSpec((None, 1, tn), lambda l, j: (l, 0, j)),
        ],
        out_specs=pl.BlockSpec((None, MOD_ROWS, tn), lambda l, j: (l, 0, j)),
        out_shape=jax.ShapeDtypeStruct((depth, MOD_ROWS, n), F32),
        compiler_params=_params("arbitrary", "arbitrary"),
        name="adaln",
    )(cond8, w_ada, b_ada.reshape(depth, 1, n))


def _rope_tables():
    t = np.arange(DEC_SEQ)
    half = HEAD_DIM // 2
    inv = ROPE_BASE ** (-np.arange(0, half, 2, dtype=np.float64) / half)
    ang_r = (t // GRID_W)[:, None] * inv[None, :]
    ang_c = (t % GRID_W)[:, None] * inv[None, :]
    cos = np.concatenate([np.cos(ang_r), np.cos(ang_r), np.cos(ang_c), np.cos(ang_c)], axis=1)
    sin = np.concatenate([-np.sin(ang_r), np.sin(ang_r), -np.sin(ang_c), np.sin(ang_c)], axis=1)
    return jnp.asarray(cos, F32), jnp.asarray(sin, F32)


def _rope(x, cos, sin_signed):
    quarter = HEAD_DIM // 4
    lane = lax.broadcasted_iota(jnp.int32, x.shape, 1)
    first = (lane % (2 * quarter)) < quarter
    partner = jnp.where(first, pltpu.roll(x, HEAD_DIM - quarter, 1), pltpu.roll(x, quarter, 1))
    return x * cos + partner * sin_signed


def _proj_h_tile(y_ref, g_ref, mod_ref, h_ref, tm):
    row0 = pl.multiple_of(pl.program_id(1) * tm, tm)

    @pl.when(pl.program_id(0) == 0)
    def _():
        _norm_mod_rows(y_ref, g_ref, mod_ref, h_ref, 0, tm, row0)

    return h_ref[pl.ds(row0, tm), :]


def _proj_in_specs(rows, tm, layer, tn=TN):
    last = rows.n // tm - 1

    def tile(j, i):
        return jnp.where(j == 0, i, last)

    return [
        pl.BlockSpec((tm, D_MODEL), lambda j, i: (tile(j, i), 0)),
        pl.BlockSpec((1, D_MODEL), lambda j, i: (0, 0)),
        pl.BlockSpec((None, MOD_ROWS, D_MODEL), lambda j, i: (rows.seg(tile(j, i), tm), 0, 0)),
        pl.BlockSpec((None, D_MODEL, tn), lambda j, i: (layer, 0, j)),
    ]


def _proj_tn(rows):
    return TN if rows.n > N_SAMPLE else 2 * TN


def _proj_kernel(y_ref, g_ref, mod_ref, w_ref, o_ref, h_ref, *, tm):
    h = _proj_h_tile(y_ref, g_ref, mod_ref, h_ref, tm)
    o_ref[...] = jnp.dot(h, w_ref[...].astype(BF16), preferred_element_type=F32)


def _proj(rows, y, g, mod, w, layer):
    n = w.shape[2]
    tm, tn = TM, _proj_tn(rows)
    return pl.pallas_call(
        functools.partial(_proj_kernel, tm=tm),
        grid=(n // tn, rows.n // tm),
        in_specs=_proj_in_specs(rows, tm, layer, tn),
        out_specs=pl.BlockSpec((None, tm, tn), lambda j, i: (j, i, 0)),
        out_shape=jax.ShapeDtypeStruct((n // tn, rows.n, tn), F32),
        scratch_shapes=[pltpu.VMEM((rows.n, D_MODEL), BF16)],
        compiler_params=_params("arbitrary", "arbitrary"),
        name="proj",
    )(y, g.reshape(1, D_MODEL), mod, w)


ATT_TILE_KA = COL_KA // TN
ATT_TILE_VA = COL_VA // TN
ATT_TILE_QB = COL_QB // TN
ATT_TILE_KVB = COL_KB // TN
N_ATT_TILES = D_ATT_IN // TN


def _proj_att_prompt_kernel(y_ref, g_ref, mod_ref, w_ref, o_ref, ka_ref, va_ref, kb_ref, vb_ref, h_ref, *, tm):
    j = pl.program_id(0)
    h = _proj_h_tile(y_ref, g_ref, mod_ref, h_ref, tm)
    acc = jnp.dot(h, w_ref[...].astype(BF16), preferred_element_type=F32)
    o_ref[...] = acc.astype(BF16)

    @pl.when((j >= ATT_TILE_KA) & (j < ATT_TILE_VA))
    def _():
        ka_ref[...] = acc

    @pl.when((j >= ATT_TILE_VA) & (j < ATT_TILE_QB))
    def _():
        va_ref[...] = acc

    @pl.when(j == ATT_TILE_KVB)
    def _():
        kb_ref[...] = acc[:, :KVB_W]
        vb_ref[...] = acc[:, KVB_W:]


def _proj_att_prompt(y, g, mod, w, layer):
    tm = TM
    rows = PROMPT
    last = rows.n // tm - 1

    def kv_map(first, count):
        def index(j, i):
            row = jnp.where(j < first, 0, jnp.where(j >= first + count, last, i))
            return row, jnp.clip(j - first, 0, count - 1)
        return index

    per_head_set = QA_W // TN
    return pl.pallas_call(
        functools.partial(_proj_att_prompt_kernel, tm=tm),
        grid=(N_ATT_TILES, rows.n // tm),
        in_specs=_proj_in_specs(rows, tm, layer),
        out_specs=[
            pl.BlockSpec((None, tm, TN), lambda j, i: (j, i, 0)),
            pl.BlockSpec((tm, TN), kv_map(ATT_TILE_KA, per_head_set)),
            pl.BlockSpec((tm, TN), kv_map(ATT_TILE_VA, per_head_set)),
            pl.BlockSpec((tm, KVB_W), kv_map(ATT_TILE_KVB, 1)),
            pl.BlockSpec((tm, KVB_W), kv_map(ATT_TILE_KVB, 1)),
        ],
        out_shape=[
            jax.ShapeDtypeStruct((N_ATT_TILES, rows.n, TN), BF16),
            jax.ShapeDtypeStruct((rows.n, QA_W), F32),
            jax.ShapeDtypeStruct((rows.n, QA_W), F32),
            jax.ShapeDtypeStruct((rows.n, KVB_W), F32),
            jax.ShapeDtypeStruct((rows.n, KVB_W), F32),
        ],
        scratch_shapes=[pltpu.VMEM((rows.n, D_MODEL), BF16)],
        compiler_params=_params("arbitrary", "arbitrary"),
        name="proj_att_prompt",
    )(y, g.reshape(1, D_MODEL), mod, w)


def _proj_att_sample_kernel(y_ref, g_ref, mod_ref, w_ref, cos_ref, sin_ref, o_ref, h_ref, *, tm):
    j = pl.program_id(0)
    h = _proj_h_tile(y_ref, g_ref, mod_ref, h_ref, tm)
    acc = jnp.dot(h, w_ref[...].astype(BF16), preferred_element_type=F32)

    def store(n_rope_heads):
        cos = cos_ref[...]
        sin = sin_ref[...]
        for hd in range(TN // HEAD_DIM):
            cols = slice(hd * HEAD_DIM, (hd + 1) * HEAD_DIM)
            x = acc[:, cols]
            if hd < n_rope_heads:
                x = _rope(x, cos, sin)
            o_ref[:, cols] = x.astype(BF16)

    @pl.when(j < ATT_TILE_QB)
    def _():
        o_ref[...] = acc.astype(BF16)

    @pl.when((j >= ATT_TILE_QB) & (j < ATT_TILE_KVB))
    def _():
        store(TN // HEAD_DIM)

    @pl.when(j == ATT_TILE_KVB)
    def _():
        store(N_KV_B)


def _proj_att_sample(y, g, mod, w, layer):
    tm = TM
    assert tm == DEC_SEQ
    rows = SAMPLE
    cos, sin = _rope_tables()
    tab_spec = pl.BlockSpec((DEC_SEQ, HEAD_DIM), lambda j, i: (0, 0))
    return pl.pallas_call(
        functools.partial(_proj_att_sample_kernel, tm=tm),
        grid=(N_ATT_TILES, rows.n // tm),
        in_specs=_proj_in_specs(rows, tm, layer) + [tab_spec, tab_spec],
        out_specs=pl.BlockSpec((None, tm, TN), lambda j, i: (j, i, 0)),
        out_shape=jax.ShapeDtypeStruct((N_ATT_TILES, rows.n, TN), BF16),
        scratch_shapes=[pltpu.VMEM((rows.n, D_MODEL), BF16)],
        compiler_params=_params("arbitrary", "arbitrary"),
        name="proj_att_sample",
    )(y, g.reshape(1, D_MODEL), mod, w, cos, sin)


MIXOUT_TM = 512


def _mixout_kernel(*refs, tm, n_parts):
    a_refs = refs[:n_parts]
    w_ref, y_ref, g_ref, mod_ref, o_ref = refs[n_parts:]
    kp = a_refs[0].shape[1]
    for c in range(D_MODEL // TN):
        cols = slice(c * TN, (c + 1) * TN)
        acc = None
        for p in range(n_parts):
            part = jnp.dot(a_refs[p][...], w_ref[p * kp:(p + 1) * kp, cols].astype(BF16),
                           preferred_element_type=F32)
            acc = part if acc is None else acc + part
        o_ref[:, cols] = acc
    _gated_residual_rows(y_ref, o_ref, g_ref, mod_ref[2:3, :], tm)


def _mixout(rows, a_parts, w, layer, y, g, mod):
    tm = MIXOUT_TM
    if isinstance(a_parts, (list, tuple)):
        n_parts, kp = len(a_parts), a_parts[0].shape[1]
        part_specs = [pl.BlockSpec((tm, kp), lambda i: (i, 0)) for _ in range(n_parts)]
    else:
        n_parts, kp = a_parts.shape[0], a_parts.shape[2]
        part_specs = [pl.BlockSpec((None, tm, kp), lambda i, p=p: (p, i, 0)) for p in range(n_parts)]
        a_parts = [a_parts] * n_parts
    assert n_parts * kp == w.shape[1]
    return pl.pallas_call(
        functools.partial(_mixout_kernel, tm=tm, n_parts=n_parts),
        grid=(rows.n // tm,),
        in_specs=part_specs + [
            pl.BlockSpec((None, w.shape[1], D_MODEL), lambda i: (layer, 0, 0), pipeline_mode=pl.Buffered(1)),
            pl.BlockSpec((tm, D_MODEL), lambda i: (i, 0)),
            pl.BlockSpec((1, D_MODEL), lambda i: (0, 0)),
            pl.BlockSpec((None, MOD_ROWS, D_MODEL), lambda i: (rows.seg(i, tm), 0, 0)),
        ],
        out_specs=pl.BlockSpec((tm, D_MODEL), lambda i: (i, 0)),
        out_shape=jax.ShapeDtypeStruct((rows.n, D_MODEL), F32),
        compiler_params=_params("arbitrary"),
        name="mixout",
    )(*a_parts, w, y, g.reshape(1, D_MODEL), mod)


FFN_TF = 256
FFN_PAIR = 2 * FFN_TF
FFN_N_PAIRS = D_FF // FFN_PAIR
FFN_GROUP = 4
FFN_N_GROUPS = FFN_N_PAIRS // FFN_GROUP


def _ffn_kernel(y_ref, g1_ref, g2_ref, mod_ref, w1_hbm, w2_hbm, o_ref, h_ref, a_ref, w1_buf, w2_buf, sem,
                *, tm, layer, n_tiles):
    i = pl.program_id(0)
    g = pl.program_id(1)
    last_tile = n_tiles - 1

    def w1_copy(pair, slot):
        cols = pl.ds(pl.multiple_of(pair * FFN_PAIR, FFN_PAIR), FFN_PAIR)
        return pltpu.make_async_copy(w1_hbm.at[layer, :, cols], w1_buf.at[slot], sem.at[0, slot])

    def w2_copy(pair, slot):
        rows = pl.ds(pl.multiple_of(pair * FFN_PAIR, FFN_PAIR), FFN_PAIR)
        return pltpu.make_async_copy(w2_hbm.at[layer, rows, :], w2_buf.at[slot], sem.at[1, slot])

    def up(slot):
        for half in range(FFN_PAIR // FFN_TF):
            cols = slice(half * FFN_TF, (half + 1) * FFN_TF)
            a = jnp.dot(h_ref[...], w1_buf[slot, :, cols].astype(BF16), preferred_element_type=F32)
            a = jnp.maximum(a, 0.0)
            a_ref[slot, :, cols] = (a * a).astype(BF16)

    def down(a_slot, w_slot):
        a = a_ref[a_slot]
        for c in range(D_MODEL // TN):
            cols = slice(c * TN, (c + 1) * TN)
            o_ref[:, cols] += jnp.dot(a, w2_buf[w_slot, :, cols].astype(BF16), preferred_element_type=F32)

    @pl.when(g == 0)
    def _():
        @pl.when(i == 0)
        def _():
            w1_copy(0, 0).start()
            w2_copy(0, 0).start()

        _norm_mod_rows(y_ref, g1_ref, mod_ref, h_ref, 3, tm)
        o_ref[...] = jnp.zeros_like(o_ref)
        a_ref[1] = jnp.zeros((tm, FFN_PAIR), BF16)

    for u in range(FFN_GROUP):
        q = g * FFN_GROUP + u
        slot = u % 2
        w1_copy(q, slot).wait()
        w2_copy(jnp.maximum(q - 1, 0), slot).wait()
        w1_copy((q + 1) % FFN_N_PAIRS, 1 - slot).start()
        w2_copy(q, 1 - slot).start()
        up(slot)
        down(1 - slot, slot)

    @pl.when(g == FFN_N_GROUPS - 1)
    def _():
        w2_copy(FFN_N_PAIRS - 1, 0).wait()
        down((FFN_N_PAIRS - 1) % 2, 0)

        @pl.when(i < last_tile)
        def _():
            w2_copy(0, 0).start()

        _gated_residual_rows(y_ref, o_ref, g2_ref, mod_ref[5:6, :], tm)

        @pl.when(i == last_tile)
        def _():
            w1_copy(0, 0).wait()


def _ffn(rows, y, g1, g2, mod, w1, w2, layer):
    tm = TM
    n_tiles = rows.n // tm
    assert FFN_GROUP % 2 == 0 and FFN_N_PAIRS % FFN_GROUP == 0
    return pl.pallas_call(
        functools.partial(_ffn_kernel, tm=tm, layer=layer, n_tiles=n_tiles),
        grid=(n_tiles, FFN_N_GROUPS),
        in_specs=[
            pl.BlockSpec((tm, D_MODEL), lambda i, s: (i, 0)),
            _vec_spec(),
            _vec_spec(),
            _mod_spec(rows, tm),
            pl.BlockSpec(memory_space=pl.ANY),
            pl.BlockSpec(memory_space=pl.ANY),
        ],
        out_specs=pl.BlockSpec((tm, D_MODEL), lambda i, s: (i, 0)),
        out_shape=jax.ShapeDtypeStruct((rows.n, D_MODEL), F32),
        scratch_shapes=[
            pltpu.VMEM((tm, D_MODEL), BF16),
            pltpu.VMEM((2, tm, FFN_PAIR), BF16),
            pltpu.VMEM((2, D_MODEL, FFN_PAIR), F32),
            pltpu.VMEM((2, FFN_PAIR, D_MODEL), F32),
            pltpu.SemaphoreType.DMA((2, 2)),
        ],
        compiler_params=_params("arbitrary", "arbitrary"),
        name="ffn",
    )(y, g1.reshape(1, D_MODEL), g2.reshape(1, D_MODEL), mod, w1, w2)


def _qkt(q, k):
    return lax.dot_general(q, k, (((1,), (1,)), ((), ())), preferred_element_type=F32)


CTX_SEQ_PER_STEP = 2


def _attn_ctx_kernel(sink_ref, qkv_ref, o_ref):
    n_rows = G_B * SEQ
    grp = lax.broadcasted_iota(jnp.int32, (n_rows, 1), 0) // SEQ
    for b in range(CTX_SEQ_PER_STEP):
        rows = slice(b * SEQ, (b + 1) * SEQ)

        def head(col):
            tile, off = divmod(col, TN)
            return qkv_ref[tile, rows, off:off + HEAD_DIM]

        for h in range(N_HEADS_A):
            q = head(COL_QA + h * HEAD_DIM)
            k = head(COL_KA + h * HEAD_DIM)
            v = head(COL_VA + h * HEAD_DIM)
            s = _qkt(q, k) * SCALE
            m = jnp.max(s, axis=-1, keepdims=True)
            p = jnp.exp(s - m)
            l = jnp.sum(p, axis=-1, keepdims=True)
            o = jnp.dot(p.astype(BF16), v, preferred_element_type=F32) / l
            o_ref[rows, h * HEAD_DIM:(h + 1) * HEAD_DIM] = o.astype(BF16)
        for j in range(N_KV_B):
            k = head(COL_KB + j * HEAD_DIM)
            v = head(COL_VB + j * HEAD_DIM)
            q = jnp.concatenate([head(COL_QB + (j * G_B + g) * HEAD_DIM) for g in range(G_B)], axis=0)
            sink = jnp.zeros((n_rows, 1), F32)
            for g in range(G_B):
                sink = jnp.where(grp == g, sink_ref[j, g], sink)
            s = _qkt(q, k) * SCALE
            m = jnp.maximum(jnp.max(s, axis=-1, keepdims=True), sink)
            p = jnp.exp(s - m)
            l = jnp.sum(p, axis=-1, keepdims=True) + jnp.exp(sink - m)
            o = jnp.dot(p.astype(BF16), v, preferred_element_type=F32) / l
            for g in range(G_B):
                c0 = QA_W + (j * G_B + g) * HEAD_DIM
                o_ref[rows, c0:c0 + HEAD_DIM] = o[g * SEQ:(g + 1) * SEQ, :].astype(BF16)


def _attn_ctx(sink, qkv):
    rows = CTX_SEQ_PER_STEP * SEQ
    return pl.pallas_call(
        _attn_ctx_kernel,
        grid=(N_PROMPT // rows,),
        in_specs=[
            pl.BlockSpec(memory_space=pltpu.SMEM),
            pl.BlockSpec((N_ATT_TILES, rows, TN), lambda b: (0, b, 0)),
        ],
        out_specs=pl.BlockSpec((rows, D_MODEL), lambda b: (b, 0)),
        out_shape=jax.ShapeDtypeStruct((N_PROMPT, D_MODEL), BF16),
        compiler_params=_params("arbitrary"),
        name="attn_ctx",
    )(sink, qkv)


NA_Q_ROWS = 4
NA_K_ROWS = 12
NA_Q_CHUNK = NA_Q_ROWS * GRID_W
NA_K_SPAN = NA_K_ROWS * GRID_W
NA_K_ROW0 = (0, 0, 4, 4)
N_RPB_ROWS = 2 * NA_ROWS - 1
N_RPB_COLS = 2 * NA_COLS - 1


def _na_row_start(r):
    return min(max(r - NA_ROWS // 2, 0), GRID_R - NA_ROWS)


for _chunk, _k0 in enumerate(NA_K_ROW0):
    for _r in range(_chunk * NA_Q_ROWS, (_chunk + 1) * NA_Q_ROWS):
        assert _k0 % 2 == 0 and _k0 <= _na_row_start(_r)
        assert _na_row_start(_r) + NA_ROWS <= _k0 + NA_K_ROWS <= GRID_R


def _na_build_bias(rpb_ref, tile_ref, bias_ref):
    shape = (GRID_W, 2 * GRID_W)
    qc = lax.broadcasted_iota(jnp.int32, shape, 0)
    lane = lax.broadcasted_iota(jnp.int32, shape, 1)
    kc = lane % GRID_W
    start_c = jnp.clip(qc - NA_COLS // 2, 0, GRID_W - NA_COLS)
    in_win = (kc >= start_c) & (kc < start_c + NA_COLS)
    for dr in range(N_RPB_ROWS):
        rows = jnp.broadcast_to(rpb_ref[dr:dr + 1, :], shape)
        shifted = pltpu.roll(rows, 2 * GRID_W - (NA_COLS - 1), 1, stride=1, stride_axis=0)
        tile_ref[dr] = jnp.where(in_win, shifted, NEG)
    first_half = lane < GRID_W
    neg = jnp.full(shape, NEG, F32)
    for chunk in range(GRID_R // NA_Q_ROWS):
        for qi in range(NA_Q_ROWS):
            qr = chunk * NA_Q_ROWS + qi
            lo = _na_row_start(qr)
            for m in range(NA_K_ROWS // 2):
                kr = NA_K_ROW0[chunk] + 2 * m
                parts = []
                for r in (kr, kr + 1):
                    parts.append(tile_ref[r - qr + NA_ROWS - 1] if lo <= r < lo + NA_ROWS else neg)
                bias_ref[chunk, qi * GRID_W:(qi + 1) * GRID_W, m * 2 * GRID_W:(m + 1) * 2 * GRID_W] = (
                    jnp.where(first_half, parts[0], parts[1]))


def _attn_na_kernel(rpb_ref, q_ref, k_ref, v_ref, kc_ref, vc_ref, o_ref, tile_ref, bias_ref):
    _na_build_bias(rpb_ref, tile_ref, bias_ref)
    for b in range(DEC_BATCH):
        kc = kc_ref[b].astype(BF16)
        vc = vc_ref[b].astype(BF16)
        for c in range(DEC_SEQ // NA_Q_CHUNK):
            r0 = b * DEC_SEQ + c * NA_Q_CHUNK
            k0 = b * DEC_SEQ + NA_K_ROW0[c] * GRID_W
            rows = slice(r0, r0 + NA_Q_CHUNK)
            keys = slice(k0, k0 + NA_K_SPAN)
            q = q_ref[rows, :]
            s = _qkt(q, k_ref[keys, :]) * SCALE + bias_ref[c]
            sc = _qkt(q, kc) * SCALE
            m = jnp.maximum(jnp.max(s, axis=-1, keepdims=True), jnp.max(sc, axis=-1, keepdims=True))
            p = jnp.exp(s - m)
            pc = jnp.exp(sc - m)
            l = jnp.sum(p, axis=-1, keepdims=True) + jnp.sum(pc, axis=-1, keepdims=True)
            o = (jnp.dot(p.astype(BF16), v_ref[keys, :], preferred_element_type=F32)
                 + jnp.dot(pc.astype(BF16), vc, preferred_element_type=F32)) / l
            o_ref[rows, :] = o.astype(BF16)


def _attn_na(qkv, rpb, cache_k, cache_v, layer):
    heads_per_tile = TN // HEAD_DIM

    def head_spec(c0):
        def index(h):
            head = c0 // HEAD_DIM + h
            return head // heads_per_tile, 0, head % heads_per_tile
        return pl.BlockSpec((None, N_SAMPLE, HEAD_DIM), index)

    ctx_spec = pl.BlockSpec((DEC_BATCH, None, PAST_LEN, HEAD_DIM), lambda h: (0, layer, 0, h))
    n_chunks = DEC_SEQ // NA_Q_CHUNK
    pad_rows = -N_RPB_ROWS % 8
    half = jnp.pad(rpb.astype(F32), ((0, 0), (0, pad_rows), (0, GRID_W - N_RPB_COLS)), constant_values=NEG)
    rpb_rows = jnp.concatenate([half, half], axis=-1)
    return pl.pallas_call(
        _attn_na_kernel,
        grid=(N_HEADS_A,),
        in_specs=[
            pl.BlockSpec((None, rpb_rows.shape[1], 2 * GRID_W), lambda h: (h, 0, 0)),
            head_spec(COL_QA),
            head_spec(COL_KA),
            head_spec(COL_VA),
            ctx_spec,
            ctx_spec,
        ],
        out_specs=pl.BlockSpec((N_SAMPLE, HEAD_DIM), lambda h: (0, h)),
        out_shape=jax.ShapeDtypeStruct((N_SAMPLE, QA_W), BF16),
        scratch_shapes=[pltpu.VMEM((N_RPB_ROWS, GRID_W, 2 * GRID_W), F32),
                        pltpu.VMEM((n_chunks, NA_Q_CHUNK, NA_K_SPAN), F32)],
        compiler_params=_params("arbitrary"),
        name="attn_na",
    )(rpb_rows, qkv, qkv, qkv, cache_k, cache_v)


WIN_Q_CHUNK = 128
WIN_K_SPAN = WIN_Q_CHUNK + 2 * WIN_B


def _attn_win_kernel(sink_ref, q_ref, k_ref, v_ref, kc_ref, vc_ref, o_ref):
    j = pl.program_id(1)
    kc = kc_ref[...].astype(BF16)
    vc = vc_ref[...].astype(BF16)
    n_rows = G_B * WIN_Q_CHUNK
    grp = lax.broadcasted_iota(jnp.int32, (n_rows, 1), 0) // WIN_Q_CHUNK
    sink = jnp.zeros((n_rows, 1), F32)
    for g in range(G_B):
        sink = jnp.where(grp == g, sink_ref[j, g], sink)
    for c in range(DEC_SEQ // WIN_Q_CHUNK):
        q0 = c * WIN_Q_CHUNK
        k0 = min(max(q0 - WIN_B, 0), DEC_SEQ - WIN_K_SPAN)
        rows = slice(q0, q0 + WIN_Q_CHUNK)
        keys = slice(k0, k0 + WIN_K_SPAN)
        q = jnp.concatenate([q_ref[rows, g * HEAD_DIM:(g + 1) * HEAD_DIM] for g in range(G_B)], axis=0)
        qpos = q0 + lax.broadcasted_iota(jnp.int32, (n_rows, WIN_K_SPAN), 0) % WIN_Q_CHUNK
        kpos = k0 + lax.broadcasted_iota(jnp.int32, (n_rows, WIN_K_SPAN), 1)
        s = jnp.where(jnp.abs(qpos - kpos) <= WIN_B, _qkt(q, k_ref[keys, :]) * SCALE, NEG)
        sc = _qkt(q, kc) * SCALE
        m = jnp.maximum(jnp.maximum(jnp.max(s, axis=-1, keepdims=True),
                                    jnp.max(sc, axis=-1, keepdims=True)), sink)
        p = jnp.exp(s - m)
        pc = jnp.exp(sc - m)
        l = jnp.sum(p, axis=-1, keepdims=True) + jnp.sum(pc, axis=-1, keepdims=True) + jnp.exp(sink - m)
        o = (jnp.dot(p.astype(BF16), v_ref[keys, :], preferred_element_type=F32)
             + jnp.dot(pc.astype(BF16), vc, preferred_element_type=F32)) / l
        for g in range(G_B):
            o_ref[rows, g * HEAD_DIM:(g + 1) * HEAD_DIM] = (
                o[g * WIN_Q_CHUNK:(g + 1) * WIN_Q_CHUNK, :].astype(BF16))


def _attn_win(sink, qkv, cache_k, cache_v, layer):
    gw = G_B * HEAD_DIM
    ctx_spec = pl.BlockSpec((None, None, PAST_LEN, HEAD_DIM), lambda b, j: (b, layer, 0, j))
    return pl.pallas_call(
        _attn_win_kernel,
        grid=(DEC_BATCH, N_KV_B),
        in_specs=[
            pl.BlockSpec(memory_space=pltpu.SMEM),
            pl.BlockSpec((None, DEC_SEQ, gw), lambda b, j: (ATT_TILE_QB + j, b, 0)),
            pl.BlockSpec((None, DEC_SEQ, HEAD_DIM), lambda b, j: (ATT_TILE_KVB, b, j)),
            pl.BlockSpec((None, DEC_SEQ, HEAD_DIM), lambda b, j: (ATT_TILE_KVB, b, N_KV_B + j)),
            ctx_spec,
            ctx_spec,
        ],
        out_specs=pl.BlockSpec((DEC_SEQ, gw), lambda b, j: (b, j)),
        out_shape=jax.ShapeDtypeStruct((N_SAMPLE, QB_W), BF16),
        compiler_params=_params("arbitrary", "arbitrary"),
        name="attn_win",
    )(sink, qkv, qkv, qkv, cache_k, cache_v)


REC_CB = 512
REC_SLABS = REC_CB // RG_BLOCK
REC_STREAMS = 8
REC_T = 256
REC_GROUP_ROWS = REC_STREAMS * REC_T
REC_PRE = CONV_PAD_L
REC_POST = CONV_W - 1 - CONV_PAD_L
REC_CHUNK_ROWS = 32 * REC_STREAMS


def _softplus(x):
    return jnp.maximum(x, 0.0) + jnp.log1p(jnp.exp(-jnp.abs(x)))


def _gelu_tanh(x):
    k = np.sqrt(2.0 / np.pi)
    half = 0.5 * x
    return half + half * jnp.tanh(x * (k + (k * 0.044715) * (x * x)))


def _sqrt_nonneg(u):
    return jnp.where(u > 0.0, u * lax.rsqrt(u), 0.0)


def _rec_kernel(x_ref, g_ref, cw_ref, cb_ref, wg_ref, bg_ref, lam_ref, h0_ref, y_ref, st_ref,
                xt_ref, af_ref, bf_ref, ab_ref, bb_ref, *, n_seg):
    S, T = REC_STREAMS, REC_T
    a_refs = (af_ref, ab_ref)
    b_refs = (bf_ref, bb_ref)
    seg = lax.broadcasted_iota(jnp.int32, (S, RG_BLOCK), 0) % n_seg
    slab_cols = [slice(n * RG_BLOCK, (n + 1) * RG_BLOCK) for n in range(REC_SLABS)]

    def t_rows(t):
        return pl.ds(pl.multiple_of(t * S, S), S)

    def from_prev_stream(x):
        return pltpu.roll(x, 1, 0)

    def from_next_stream(x):
        return pltpu.roll(x, S - 1, 0)

    for n, cols in enumerate(slab_cols):
        for s in range(S):
            xt_ref[n, pl.ds(REC_PRE * S + s, T, stride=S), :] = x_ref[s * T:(s + 1) * T, cols]
        for p in range(REC_PRE):
            src = xt_ref[n, (T + p) * S:(T + p + 1) * S, :]
            xt_ref[n, p * S:(p + 1) * S, :] = jnp.where(seg > 0, from_prev_stream(src), 0.0)
        for p in range(REC_POST):
            src = xt_ref[n, (REC_PRE + p) * S:(REC_PRE + p + 1) * S, :]
            xt_ref[n, (REC_PRE + T + p) * S:(REC_PRE + T + p + 1) * S, :] = (
                jnp.where(seg < n_seg - 1, from_next_stream(src), 0.0))

    c_all = (-0.5 * RG_C * np.log2(np.e)) * _softplus(-lam_ref[...])

    def gate_rows(r, carry):
        r0 = pl.multiple_of(r * REC_CHUNK_ROWS, REC_CHUNK_ROWS)
        rows = pl.ds(r0, REC_CHUNK_ROWS)
        for n, cols in enumerate(slab_cols):
            xc = cb_ref[:, cols] + cw_ref[0:1, cols] * xt_ref[n, rows, :]
            for k in range(1, CONV_W):
                tap_rows = pl.ds(pl.multiple_of(r0 + k * S, S), REC_CHUNK_ROWS)
                xc = xc + cw_ref[k:k + 1, cols] * xt_ref[n, tap_rows, :]
            gates = jnp.dot(xc.astype(BF16), wg_ref[n].astype(BF16), preferred_element_type=F32)
            x_half = 0.5 * xc
            for d in range(2):
                ga_half = gates[:, (2 * d) * RG_BLOCK:(2 * d + 1) * RG_BLOCK] + bg_ref[2 * d:2 * d + 1, cols]
                gx_half = (gates[:, (2 * d + 1) * RG_BLOCK:(2 * d + 2) * RG_BLOCK]
                           + bg_ref[2 * d + 1:2 * d + 2, cols])
                c = c_all[d:d + 1, cols]
                a = jnp.exp2(c * jnp.tanh(ga_half) + c)
                a_refs[d][n, rows, :] = a
                b_refs[d][n, rows, :] = _sqrt_nonneg(1.0 - a * a) * ((1.0 + jnp.tanh(gx_half)) * x_half)
        return carry

    lax.fori_loop(0, T * S // REC_CHUNK_ROWS, gate_rows, 0)

    def scan_step(t, carry):
        hf, hb, pf, pb = carry
        rf, rb = t_rows(t), t_rows(T - 1 - t)
        hf_new, hb_new, pf_new, pb_new = [], [], [], []
        for n in range(REC_SLABS):
            a = af_ref[n, rf, :]
            h = a * hf[n] + bf_ref[n, rf, :]
            bf_ref[n, rf, :] = h
            hf_new.append(h)
            a2 = ab_ref[n, rb, :]
            h2 = a2 * hb[n] + bb_ref[n, rb, :]
            bb_ref[n, rb, :] = h2
            hb_new.append(h2)
            if n_seg > 1:
                p = a * pf[n]
                af_ref[n, rf, :] = p
                pf_new.append(p)
                p2 = a2 * pb[n]
                ab_ref[n, rb, :] = p2
                pb_new.append(p2)
        return tuple(hf_new), tuple(hb_new), tuple(pf_new), tuple(pb_new)

    ones = tuple(jnp.ones((S, RG_BLOCK), F32) for _ in range(REC_SLABS)) if n_seg > 1 else ()
    hf, hb, pf, pb = lax.fori_loop(
        0, T, scan_step,
        (tuple(h0_ref[0, :, cols] for cols in slab_cols), tuple(h0_ref[1, :, cols] for cols in slab_cols),
         ones, ones), unroll=4)

    if n_seg > 1:
        cin_f, cin_b = [], []
        for n in range(REC_SLABS):
            cf = jnp.zeros((S, RG_BLOCK), F32)
            for j in range(1, n_seg):
                cf = jnp.where(seg == j, from_prev_stream(hf[n] + pf[n] * cf), cf)
            cb_in = jnp.zeros((S, RG_BLOCK), F32)
            for j in range(n_seg - 2, -1, -1):
                cb_in = jnp.where(seg == j, from_next_stream(hb[n] + pb[n] * cb_in), cb_in)
            cin_f.append(cf)
            cin_b.append(cb_in)

        def carry_in_step(t, carry):
            rows = t_rows(t)
            for n in range(REC_SLABS):
                bf_ref[n, rows, :] += af_ref[n, rows, :] * cin_f[n]
                bb_ref[n, rows, :] += ab_ref[n, rows, :] * cin_b[n]
            return carry

        lax.fori_loop(0, T, carry_in_step, 0, unroll=4)
        hf = tuple(hf[n] + pf[n] * cin_f[n] for n in range(REC_SLABS))
        hb = tuple(hb[n] + pb[n] * cin_b[n] for n in range(REC_SLABS))

    for n, cols in enumerate(slab_cols):
        st_ref[0, :, cols] = hf[n]
        st_ref[1, :, cols] = hb[n]

    for n, cols in enumerate(slab_cols):
        for s in range(S):
            rows = slice(s * T, (s + 1) * T)
            picked = pl.ds(s, T, stride=S)
            h_sum = bf_ref[n, picked, :] + bb_ref[n, picked, :]
            y_ref[rows, cols] = (h_sum * _gelu_tanh(g_ref[rows, cols])).astype(BF16)


def _rec(xg, n_seg, cw, cb, wg, bg, lam, h0):
    n_rows, tn = xg.shape[1], xg.shape[2]
    n_grp = n_rows // REC_GROUP_ROWS
    nc = D_RNN // REC_CB
    per_tile = tn // REC_CB

    def branch_spec(first):
        return pl.BlockSpec((None, REC_GROUP_ROWS, REC_CB),
                            lambda s, c: ((first + c) // per_tile, s, (first + c) % per_tile))

    vec = lambda rows: pl.BlockSpec((rows, REC_CB), lambda s, c: (0, c))
    state_spec = pl.BlockSpec((2, REC_STREAMS, REC_CB), lambda s, c: (0, s, c))
    slab_scratch = lambda n_t: pltpu.VMEM((REC_SLABS, n_t * REC_STREAMS, RG_BLOCK), F32)
    return pl.pallas_call(
        functools.partial(_rec_kernel, n_seg=n_seg),
        grid=(n_grp, nc),
        in_specs=[
            branch_spec(0),
            branch_spec(nc),
            vec(CONV_W),
            vec(1),
            pl.BlockSpec((REC_SLABS, RG_BLOCK, 4 * RG_BLOCK), lambda s, c: (c, 0, 0)),
            vec(4),
            vec(2),
            state_spec,
        ],
        out_specs=[
            pl.BlockSpec((None, REC_GROUP_ROWS, REC_CB), lambda s, c: (c, s, 0)),
            state_spec,
        ],
        out_shape=[
            jax.ShapeDtypeStruct((nc, n_rows, REC_CB), BF16),
            jax.ShapeDtypeStruct((2, n_grp * REC_STREAMS, D_RNN), F32),
        ],
        scratch_shapes=[slab_scratch(REC_PRE + REC_T + REC_POST)] + [slab_scratch(REC_T)] * 4,
        compiler_params=_params("arbitrary", "arbitrary"),
        name="rec",
    )(xg, xg, cw, cb.reshape(1, D_RNN), wg, bg, lam, h0)


def kernel(x_prompt, x_sample, c, cache_a_k, cache_a_v, cache_b_k, cache_b_v, state_rg_fwd, state_rg_bwd, c_ctx, w_ada, b_ada, g_pre_mix, g_post_mix, g_pre_ffn, g_post_ffn, w_att_in, w_att_out, sink_b, rpb_a, w_rec_in, conv_w, conv_b, w_rg_a, b_rg_a, w_rg_x, b_rg_x, rg_lambda, w_rec_out, w_ff1, w_ff2):
    depth = w_ada.shape[0]
    yp = x_prompt.reshape(N_PROMPT, D_MODEL)
    ys = x_sample.reshape(N_SAMPLE, D_MODEL)

    cond8 = jnp.concatenate([c_ctx[None, :], c, jnp.zeros((MOD_ROWS - N_SEG, D_MODEL), F32)], axis=0)
    mod_all = _adaln(cond8, w_ada, b_ada)
    mod_all = mod_all[:, :N_SEG, :].reshape(depth, N_SEG, N_MOD, D_MODEL)
    mod_all = jnp.pad(mod_all, ((0, 0), (0, 0), (0, MOD_ROWS - N_MOD), (0, 0)))

    a_k, a_v, b_k, b_v, s_f, s_b = [], [], [], [], [], []
    for layer in range(depth):
        mod = mod_all[layer]
        li = layer // 2
        g_pre, g_post = g_pre_mix[layer], g_post_mix[layer]
        if layer % 2 == 0:
            qkv_p, ka, va, kb, vb = _proj_att_prompt(yp, g_pre, mod, w_att_in, li)
            qkv_s = _proj_att_sample(ys, g_pre, mod, w_att_in, li)
            a_k.append(ka.reshape(BATCH, SEQ, N_HEADS_A, HEAD_DIM))
            a_v.append(va.reshape(BATCH, SEQ, N_HEADS_A, HEAD_DIM))
            b_k.append(kb.reshape(BATCH, SEQ, N_KV_B, HEAD_DIM))
            b_v.append(vb.reshape(BATCH, SEQ, N_KV_B, HEAD_DIM))
            mix_p = [_attn_ctx(sink_b[li], qkv_p)]
            n_att = cache_a_k.shape[1]
            mix_s = [_attn_na(qkv_s, rpb_a[li],
                              cache_a_k.reshape(DEC_BATCH, n_att, PAST_LEN, QA_W),
                              cache_a_v.reshape(DEC_BATCH, n_att, PAST_LEN, QA_W), li),
                     _attn_win(sink_b[li], qkv_s,
                               cache_b_k.reshape(DEC_BATCH, n_att, PAST_LEN, KVB_W),
                               cache_b_v.reshape(DEC_BATCH, n_att, PAST_LEN, KVB_W), li)]
            w_out = w_att_out
        else:
            xg_p = _proj(PROMPT, yp, g_pre, mod, w_rec_in, li)
            xg_s = _proj(SAMPLE, ys, g_pre, mod, w_rec_in, li)
            wg = 0.5 * jnp.concatenate([w_rg_a[li, 0], w_rg_x[li, 0], w_rg_a[li, 1], w_rg_x[li, 1]], axis=-1)
            bg = 0.5 * jnp.stack([b_rg_a[li, 0], b_rg_x[li, 0], b_rg_a[li, 1], b_rg_x[li, 1]], axis=0)
            rec_args = (conv_w[li], conv_b[li], wg, bg, rg_lambda[li])
            n_seg = DEC_SEQ // REC_T
            assert SEQ == REC_T and BATCH % REC_STREAMS == 0 and DEC_BATCH * n_seg == REC_STREAMS
            h0_p = jnp.zeros((2, BATCH, D_RNN), F32)
            seg_state = jnp.zeros((DEC_BATCH, n_seg, D_RNN), F32)
            h0_s = jnp.stack([seg_state.at[:, 0].set(state_rg_fwd[:, li]).reshape(REC_STREAMS, D_RNN),
                              seg_state.at[:, n_seg - 1].set(state_rg_bwd[:, li]).reshape(REC_STREAMS, D_RNN)])
            rec_p, st_p = _rec(xg_p, 1, *rec_args, h0_p)
            rec_s, _ = _rec(xg_s, n_seg, *rec_args, h0_s)
            mix_p, mix_s = rec_p, rec_s
            s_f.append(st_p[0])
            s_b.append(st_p[1])
            w_out = w_rec_out
        yp = _mixout(PROMPT, mix_p, w_out, li, yp, g_post, mod)
        ys = _mixout(SAMPLE, mix_s, w_out, li, ys, g_post, mod)
        yp = _ffn(PROMPT, yp, g_pre_ffn[layer], g_post_ffn[layer], mod, w_ff1, w_ff2, layer)
        ys = _ffn(SAMPLE, ys, g_pre_ffn[layer], g_post_ffn[layer], mod, w_ff1, w_ff2, layer)

    return (yp.reshape(BATCH, SEQ, D_MODEL), ys.reshape(DEC_BATCH, DEC_SEQ, D_MODEL),
            jnp.stack(a_k, axis=1), jnp.stack(a_v, axis=1), jnp.stack(b_k, axis=1), jnp.stack(b_v, axis=1),
            jnp.stack(s_f, axis=1), jnp.stack(s_b, axis=1))
```

```python
import functools

import jax
import jax.numpy as jnp
import numpy as np
from jax import lax
from jax.experimental import pallas as pl
from jax.experimental.pallas import tpu as pltpu

D_MODEL = 2048
BATCH = 16
SEQ = 256
DEC_BATCH = 2
DEC_SEQ = 1024
PAST_LEN = 256
GRID_W = 64
GRID_R = DEC_SEQ // GRID_W
HEAD_DIM = 128
N_HEADS_A = 8
N_HEADS_B = 8
N_KV_B = 2
G_B = N_HEADS_B // N_KV_B
NA_ROWS = 8
NA_COLS = 16
WIN_B = 128
D_RNN = D_MODEL
N_RG_BLOCKS = 16
RG_BLOCK = D_RNN // N_RG_BLOCKS
CONV_W = 4
CONV_PAD_L = 2
RG_C = 8.0
D_FF = 4 * D_MODEL
ROPE_BASE = 10000.0
EPS = 1e-6
NEG = -1e30
QA_W = N_HEADS_A * HEAD_DIM
QB_W = N_HEADS_B * HEAD_DIM
KVB_W = N_KV_B * HEAD_DIM
D_ATT_IN = 3 * QA_W + QB_W + 2 * KVB_W
SCALE = HEAD_DIM ** -0.5

N_PROMPT = BATCH * SEQ
N_SAMPLE = DEC_BATCH * DEC_SEQ
N_SEG = 1 + DEC_BATCH
N_MOD = 6
MOD_ROWS = 8

COL_QA, COL_KA, COL_VA = 0, QA_W, 2 * QA_W
COL_QB = 3 * QA_W
COL_KB = COL_QB + QB_W
COL_VB = COL_KB + KVB_W

V7X_VMEM_BYTES = 64 * 1024 * 1024
VMEM_LIMIT = V7X_VMEM_BYTES - 4 * 1024 * 1024

TM = 1024
TN = 512
ROW_CHUNK = 16
ROW_GROUP = 16

F32 = jnp.float32
BF16 = jnp.bfloat16


def _params(*sem):
    return pltpu.CompilerParams(dimension_semantics=sem, vmem_limit_bytes=VMEM_LIMIT)


class _Rows:
    def __init__(self, n_rows, seg0, seg_rows):
        self.n = n_rows
        self.seg0 = seg0
        self.seg_rows = seg_rows

    def seg(self, i, tm):
        return self.seg0 + (i * tm) // self.seg_rows


PROMPT = _Rows(N_PROMPT, 0, N_PROMPT)
SAMPLE = _Rows(N_SAMPLE, 1, DEC_SEQ)


def _mod_spec(rows, tm):
    return pl.BlockSpec((None, MOD_ROWS, D_MODEL), lambda i, j: (rows.seg(i, tm), 0, 0))


def _vec_spec():
    return pl.BlockSpec((1, D_MODEL), lambda i, j: (0, 0))


def _rms_scale(x):
    return lax.rsqrt(jnp.mean(x * x, axis=-1, keepdims=True) + EPS)


def _norm_mod_rows(y_ref, g_ref, mod_ref, h_ref, shift_row, tm, h_row0=0):
    shift = mod_ref[shift_row:shift_row + 1, :]
    gain = g_ref[...] * (1.0 + mod_ref[shift_row + 1:shift_row + 2, :])

    def body(r, carry):
        starts = [r * (ROW_GROUP * ROW_CHUNK) + u * ROW_CHUNK for u in range(ROW_GROUP)]
        scales = [_rms_scale(y_ref[pl.ds(pl.multiple_of(r0, ROW_CHUNK), ROW_CHUNK), :]) for r0 in starts]
        for r0, rs in zip(starts, scales):
            y = y_ref[pl.ds(pl.multiple_of(r0, ROW_CHUNK), ROW_CHUNK), :]
            h = (y * rs) * gain + shift
            h_ref[pl.ds(pl.multiple_of(h_row0 + r0, ROW_CHUNK), ROW_CHUNK), :] = h.astype(BF16)
        return carry

    lax.fori_loop(0, tm // (ROW_GROUP * ROW_CHUNK), body, 0)


def _gated_residual_rows(y_ref, o_ref, g_ref, gate, tm):
    gain = gate * g_ref[...]

    def body(r, carry):
        chunks = [pl.ds(pl.multiple_of(r * (ROW_GROUP * ROW_CHUNK) + u * ROW_CHUNK, ROW_CHUNK), ROW_CHUNK)
                  for u in range(ROW_GROUP)]
        scales = [_rms_scale(o_ref[rows, :]) for rows in chunks]
        for rows, rs in zip(chunks, scales):
            o_ref[rows, :] = y_ref[rows, :] + (o_ref[rows, :] * rs) * gain
        return carry

    lax.fori_loop(0, tm // (ROW_GROUP * ROW_CHUNK), body, 0)


def _adaln_kernel(cond_ref, w_ref, b_ref, o_ref):
    c = cond_ref[...]
    s = c / (1.0 + jnp.exp(-c))
    o_ref[...] = jnp.dot(s.astype(BF16), w_ref[...].astype(BF16),
                         preferred_element_type=F32) + b_ref[...]


def _adaln(cond8, w_ada, b_ada):
    depth = w_ada.shape[0]
    n = w_ada.shape[2]
    tn = 1024
    return pl.pallas_call(
        _adaln_kernel,
        grid=(depth, n // tn),
        in_specs=[
            pl.BlockSpec((MOD_ROWS, D_MODEL), lambda l, j: (0, 0)),
            pl.BlockSpec((None, D_MODEL, tn), lambda l, j: (l, 0, j)),
            pl.BlockSpec((None, 1, tn), lambda l, j: (l, 0, j)),
        ],
        out_specs=pl.BlockSpec((None, MOD_ROWS, tn), lambda l, j: (l, 0, j)),
        out_shape=jax.ShapeDtypeStruct((depth, MOD_ROWS, n), F32),
        compiler_params=_params("arbitrary", "arbitrary"),
        name="adaln",
    )(cond8, w_ada, b_ada.reshape(depth, 1, n))


def _rope_tables():
    t = np.arange(DEC_SEQ)
    half = HEAD_DIM // 2
    inv = ROPE_BASE ** (-np.arange(0, half, 2, dtype=np.float64) / half)
    ang_r = (t // GRID_W)[:, None] * inv[None, :]
    ang_c = (t % GRID_W)[:, None] * inv[None, :]
    cos = np.concatenate([np.cos(ang_r), np.cos(ang_r), np.cos(ang_c), np.cos(ang_c)], axis=1)
    sin = np.concatenate([-np.sin(ang_r), np.sin(ang_r), -np.sin(ang_c), np.sin(ang_c)], axis=1)
    return jnp.asarray(cos, F32), jnp.asarray(sin, F32)


def _rope(x, cos, sin_signed):
    quarter = HEAD_DIM // 4
    lane = lax.broadcasted_iota(jnp.int32, x.shape, 1)
    first = (lane % (2 * quarter)) < quarter
    partner = jnp.where(first, pltpu.roll(x, HEAD_DIM - quarter, 1), pltpu.roll(x, quarter, 1))
    return x * cos + partner * sin_signed


def _proj_h_tile(y_ref, g_ref, mod_ref, h_ref, tm):
    row0 = pl.multiple_of(pl.program_id(1) * tm, tm)

    @pl.when(pl.program_id(0) == 0)
    def _():
        _norm_mod_rows(y_ref, g_ref, mod_ref, h_ref, 0, tm, row0)

    return h_ref[pl.ds(row0, tm), :]


def _proj_in_specs(rows, tm, layer, tn=TN):
    last = rows.n // tm - 1

    def tile(j, i):
        return jnp.where(j == 0, i, last)

    return [
        pl.BlockSpec((tm, D_MODEL), lambda j, i: (tile(j, i), 0)),
        pl.BlockSpec((1, D_MODEL), lambda j, i: (0, 0)),
        pl.BlockSpec((None, MOD_ROWS, D_MODEL), lambda j, i: (rows.seg(tile(j, i), tm), 0, 0)),
        pl.BlockSpec((None, D_MODEL, tn), lambda j, i: (layer, 0, j)),
    ]


def _proj_tn(rows):
    return TN if rows.n > N_SAMPLE else 2 * TN


def _proj_kernel(y_ref, g_ref, mod_ref, w_ref, o_ref, h_ref, *, tm):
    h = _proj_h_tile(y_ref, g_ref, mod_ref, h_ref, tm)
    o_ref[...] = jnp.dot(h, w_ref[...].astype(BF16), preferred_element_type=F32)


def _proj(rows, y, g, mod, w, layer):
    n = w.shape[2]
    tm, tn = TM, _proj_tn(rows)
    return pl.pallas_call(
        functools.partial(_proj_kernel, tm=tm),
        grid=(n // tn, rows.n // tm),
        in_specs=_proj_in_specs(rows, tm, layer, tn),
        out_specs=pl.BlockSpec((None, tm, tn), lambda j, i: (j, i, 0)),
        out_shape=jax.ShapeDtypeStruct((n // tn, rows.n, tn), F32),
        scratch_shapes=[pltpu.VMEM((rows.n, D_MODEL), BF16)],
        compiler_params=_params("arbitrary", "arbitrary"),
        name="proj",
    )(y, g.reshape(1, D_MODEL), mod, w)


ATT_TILE_KA = COL_KA // TN
ATT_TILE_VA = COL_VA // TN
ATT_TILE_QB = COL_QB // TN
ATT_TILE_KVB = COL_KB // TN
N_ATT_TILES = D_ATT_IN // TN

LOG2E = float(np.log2(np.e))
Q_FACTOR = SCALE * LOG2E


def _q_factor(j):
    is_q = (j < ATT_TILE_KA) | ((j >= ATT_TILE_QB) & (j < ATT_TILE_KVB))
    return jnp.where(is_q, Q_FACTOR, 1.0).astype(F32)


def _proj_att_prompt_kernel(y_ref, g_ref, mod_ref, w_ref, o_ref, ka_ref, va_ref, kb_ref, vb_ref, h_ref, *, tm):
    j = pl.program_id(0)
    h = _proj_h_tile(y_ref, g_ref, mod_ref, h_ref, tm)
    acc = jnp.dot(h, w_ref[...].astype(BF16), preferred_element_type=F32)
    o_ref[...] = (acc * _q_factor(j)).astype(BF16)

    @pl.when((j >= ATT_TILE_KA) & (j < ATT_TILE_VA))
    def _():
        ka_ref[...] = acc

    @pl.when((j >= ATT_TILE_VA) & (j < ATT_TILE_QB))
    def _():
        va_ref[...] = acc

    @pl.when(j == ATT_TILE_KVB)
    def _():
        kb_ref[...] = acc[:, :KVB_W]
        vb_ref[...] = acc[:, KVB_W:]


def _proj_att_prompt(y, g, mod, w, layer):
    tm = TM
    rows = PROMPT
    last = rows.n // tm - 1

    def kv_map(first, count):
        def index(j, i):
            row = jnp.where(j < first, 0, jnp.where(j >= first + count, last, i))
            return row, jnp.clip(j - first, 0, count - 1)
        return index

    per_head_set = QA_W // TN
    return pl.pallas_call(
        functools.partial(_proj_att_prompt_kernel, tm=tm),
        grid=(N_ATT_TILES, rows.n // tm),
        in_specs=_proj_in_specs(rows, tm, layer),
        out_specs=[
            pl.BlockSpec((None, tm, TN), lambda j, i: (j, i, 0)),
            pl.BlockSpec((tm, TN), kv_map(ATT_TILE_KA, per_head_set)),
            pl.BlockSpec((tm, TN), kv_map(ATT_TILE_VA, per_head_set)),
            pl.BlockSpec((tm, KVB_W), kv_map(ATT_TILE_KVB, 1)),
            pl.BlockSpec((tm, KVB_W), kv_map(ATT_TILE_KVB, 1)),
        ],
        out_shape=[
            jax.ShapeDtypeStruct((N_ATT_TILES, rows.n, TN), BF16),
            jax.ShapeDtypeStruct((rows.n, QA_W), F32),
            jax.ShapeDtypeStruct((rows.n, QA_W), F32),
            jax.ShapeDtypeStruct((rows.n, KVB_W), F32),
            jax.ShapeDtypeStruct((rows.n, KVB_W), F32),
        ],
        scratch_shapes=[pltpu.VMEM((rows.n, D_MODEL), BF16)],
        compiler_params=_params("arbitrary", "arbitrary"),
        name="proj_att_prompt",
    )(y, g.reshape(1, D_MODEL), mod, w)


def _proj_att_sample_kernel(y_ref, g_ref, mod_ref, w_ref, cos_ref, sin_ref, o_ref, h_ref, *, tm):
    j = pl.program_id(0)
    h = _proj_h_tile(y_ref, g_ref, mod_ref, h_ref, tm)
    acc = jnp.dot(h, w_ref[...].astype(BF16), preferred_element_type=F32)

    def store(n_rope_heads):
        cos = cos_ref[...]
        sin = sin_ref[...]
        for hd in range(TN // HEAD_DIM):
            cols = slice(hd * HEAD_DIM, (hd + 1) * HEAD_DIM)
            x = acc[:, cols]
            if hd < n_rope_heads:
                x = _rope(x, cos, sin)
            o_ref[:, cols] = (x * q_factor).astype(BF16)

    q_factor = _q_factor(j)

    @pl.when(j < ATT_TILE_QB)
    def _():
        o_ref[...] = (acc * q_factor).astype(BF16)

    @pl.when((j >= ATT_TILE_QB) & (j < ATT_TILE_KVB))
    def _():
        store(TN // HEAD_DIM)

    @pl.when(j == ATT_TILE_KVB)
    def _():
        store(N_KV_B)


def _proj_att_sample(y, g, mod, w, layer):
    tm = TM
    assert tm == DEC_SEQ
    rows = SAMPLE
    cos, sin = _rope_tables()
    tab_spec = pl.BlockSpec((DEC_SEQ, HEAD_DIM), lambda j, i: (0, 0))
    return pl.pallas_call(
        functools.partial(_proj_att_sample_kernel, tm=tm),
        grid=(N_ATT_TILES, rows.n // tm),
        in_specs=_proj_in_specs(rows, tm, layer) + [tab_spec, tab_spec],
        out_specs=pl.BlockSpec((None, tm, TN), lambda j, i: (j, i, 0)),
        out_shape=jax.ShapeDtypeStruct((N_ATT_TILES, rows.n, TN), BF16),
        scratch_shapes=[pltpu.VMEM((rows.n, D_MODEL), BF16)],
        compiler_params=_params("arbitrary", "arbitrary"),
        name="proj_att_sample",
    )(y, g.reshape(1, D_MODEL), mod, w, cos, sin)


MIXOUT_TM = 512


def _mixout_kernel(*refs, tm, n_parts):
    a_refs = refs[:n_parts]
    w_ref, y_ref, g_ref, mod_ref, o_ref = refs[n_parts:]
    kp = a_refs[0].shape[1]
    for c in range(D_MODEL // TN):
        cols = slice(c * TN, (c + 1) * TN)
        acc = None
        for p in range(n_parts):
            part = jnp.dot(a_refs[p][...], w_ref[p * kp:(p + 1) * kp, cols].astype(BF16),
                           preferred_element_type=F32)
            acc = part if acc is None else acc + part
        o_ref[:, cols] = acc
    _gated_residual_rows(y_ref, o_ref, g_ref, mod_ref[2:3, :], tm)


def _mixout(rows, a_parts, w, layer, y, g, mod):
    tm = MIXOUT_TM
    if isinstance(a_parts, (list, tuple)):
        n_parts, kp = len(a_parts), a_parts[0].shape[1]
        part_specs = [pl.BlockSpec((tm, kp), lambda i: (i, 0)) for _ in range(n_parts)]
    else:
        n_parts, kp = a_parts.shape[0], a_parts.shape[2]
        part_specs = [pl.BlockSpec((None, tm, kp), lambda i, p=p: (p, i, 0)) for p in range(n_parts)]
        a_parts = [a_parts] * n_parts
    assert n_parts * kp == w.shape[1]
    return pl.pallas_call(
        functools.partial(_mixout_kernel, tm=tm, n_parts=n_parts),
        grid=(rows.n // tm,),
        in_specs=part_specs + [
            pl.BlockSpec((None, w.shape[1], D_MODEL), lambda i: (layer, 0, 0), pipeline_mode=pl.Buffered(1)),
            pl.BlockSpec((tm, D_MODEL), lambda i: (i, 0)),
            pl.BlockSpec((1, D_MODEL), lambda i: (0, 0)),
            pl.BlockSpec((None, MOD_ROWS, D_MODEL), lambda i: (rows.seg(i, tm), 0, 0)),
        ],
        out_specs=pl.BlockSpec((tm, D_MODEL), lambda i: (i, 0)),
        out_shape=jax.ShapeDtypeStruct((rows.n, D_MODEL), F32),
        compiler_params=_params("arbitrary"),
        name="mixout",
    )(*a_parts, w, y, g.reshape(1, D_MODEL), mod)


FFN_TF = 256
FFN_PAIR = 2 * FFN_TF
FFN_N_PAIRS = D_FF // FFN_PAIR
FFN_GROUP = 4
FFN_N_GROUPS = FFN_N_PAIRS // FFN_GROUP


def _ffn_kernel(y_ref, g1_ref, g2_ref, mod_ref, w1_hbm, w2_hbm, o_ref, h_ref, a_ref, w1_buf, w2_buf, sem,
                *, tm, layer, n_tiles):
    i = pl.program_id(0)
    g = pl.program_id(1)
    last_tile = n_tiles - 1

    def w1_copy(pair, slot):
        cols = pl.ds(pl.multiple_of(pair * FFN_PAIR, FFN_PAIR), FFN_PAIR)
        return pltpu.make_async_copy(w1_hbm.at[layer, :, cols], w1_buf.at[slot], sem.at[0, slot])

    def w2_copy(pair, slot):
        rows = pl.ds(pl.multiple_of(pair * FFN_PAIR, FFN_PAIR), FFN_PAIR)
        return pltpu.make_async_copy(w2_hbm.at[layer, rows, :], w2_buf.at[slot], sem.at[1, slot])

    def up(slot):
        for half in range(FFN_PAIR // FFN_TF):
            cols = slice(half * FFN_TF, (half + 1) * FFN_TF)
            a = jnp.dot(h_ref[...], w1_buf[slot, :, cols].astype(BF16), preferred_element_type=F32)
            a = jnp.maximum(a, 0.0)
            a_ref[slot, :, cols] = (a * a).astype(BF16)

    def down(a_slot, w_slot):
        a = a_ref[a_slot]
        for c in range(D_MODEL // TN):
            cols = slice(c * TN, (c + 1) * TN)
            o_ref[:, cols] += jnp.dot(a, w2_buf[w_slot, :, cols].astype(BF16), preferred_element_type=F32)

    @pl.when(g == 0)
    def _():
        @pl.when(i == 0)
        def _():
            w1_copy(0, 0).start()
            w2_copy(0, 0).start()

        _norm_mod_rows(y_ref, g1_ref, mod_ref, h_ref, 3, tm)
        o_ref[...] = jnp.zeros_like(o_ref)
        a_ref[1] = jnp.zeros((tm, FFN_PAIR), BF16)

    for u in range(FFN_GROUP):
        q = g * FFN_GROUP + u
        slot = u % 2
        w1_copy(q, slot).wait()
        w2_copy(jnp.maximum(q - 1, 0), slot).wait()
        w1_copy((q + 1) % FFN_N_PAIRS, 1 - slot).start()
        w2_copy(q, 1 - slot).start()
        up(slot)
        down(1 - slot, slot)

    @pl.when(g == FFN_N_GROUPS - 1)
    def _():
        w2_copy(FFN_N_PAIRS - 1, 0).wait()
        down((FFN_N_PAIRS - 1) % 2, 0)

        @pl.when(i < last_tile)
        def _():
            w2_copy(0, 0).start()

        _gated_residual_rows(y_ref, o_ref, g2_ref, mod_ref[5:6, :], tm)

        @pl.when(i == last_tile)
        def _():
            w1_copy(0, 0).wait()


def _ffn(rows, y, g1, g2, mod, w1, w2, layer):
    tm = TM
    n_tiles = rows.n // tm
    assert FFN_GROUP % 2 == 0 and FFN_N_PAIRS % FFN_GROUP == 0
    return pl.pallas_call(
        functools.partial(_ffn_kernel, tm=tm, layer=layer, n_tiles=n_tiles),
        grid=(n_tiles, FFN_N_GROUPS),
        in_specs=[
            pl.BlockSpec((tm, D_MODEL), lambda i, s: (i, 0)),
            _vec_spec(),
            _vec_spec(),
            _mod_spec(rows, tm),
            pl.BlockSpec(memory_space=pl.ANY),
            pl.BlockSpec(memory_space=pl.ANY),
        ],
        out_specs=pl.BlockSpec((tm, D_MODEL), lambda i, s: (i, 0)),
        out_shape=jax.ShapeDtypeStruct((rows.n, D_MODEL), F32),
        scratch_shapes=[
            pltpu.VMEM((tm, D_MODEL), BF16),
            pltpu.VMEM((2, tm, FFN_PAIR), BF16),
            pltpu.VMEM((2, D_MODEL, FFN_PAIR), F32),
            pltpu.VMEM((2, FFN_PAIR, D_MODEL), F32),
            pltpu.SemaphoreType.DMA((2, 2)),
        ],
        compiler_params=_params("arbitrary", "arbitrary"),
        name="ffn",
    )(y, g1.reshape(1, D_MODEL), g2.reshape(1, D_MODEL), mod, w1, w2)


def _qkt(q, k):
    return lax.dot_general(q, k, (((1,), (1,)), ((), ())), preferred_element_type=F32)


def _with_ones(v):
    return jnp.concatenate([v, jnp.ones_like(v)], axis=1)


def _pv(p, v_ext):
    o = jnp.dot(p, v_ext, preferred_element_type=F32)
    return o[:, :HEAD_DIM], o[:, HEAD_DIM:]


CTX_SEQ_PER_STEP = 2


def _attn_ctx_kernel(sink_ref, qkv_ref, o_ref):
    n_rows = G_B * SEQ
    grp = lax.broadcasted_iota(jnp.int32, (n_rows, 1), 0) // SEQ
    for b in range(CTX_SEQ_PER_STEP):
        rows = slice(b * SEQ, (b + 1) * SEQ)

        def head(col):
            tile, off = divmod(col, TN)
            return qkv_ref[tile, rows, off:off + HEAD_DIM]

        for h in range(N_HEADS_A):
            q = head(COL_QA + h * HEAD_DIM)
            k = head(COL_KA + h * HEAD_DIM)
            v = head(COL_VA + h * HEAD_DIM)
            s = _qkt(q, k)
            m = jnp.max(s, axis=-1, keepdims=True)
            num, den = _pv(jnp.exp2(s - m).astype(BF16), _with_ones(v))
            o_ref[rows, h * HEAD_DIM:(h + 1) * HEAD_DIM] = (num / den).astype(BF16)
        for j in range(N_KV_B):
            k = head(COL_KB + j * HEAD_DIM)
            v = head(COL_VB + j * HEAD_DIM)
            q = jnp.concatenate([head(COL_QB + (j * G_B + g) * HEAD_DIM) for g in range(G_B)], axis=0)
            sink = jnp.zeros((n_rows, 1), F32)
            for g in range(G_B):
                sink = jnp.where(grp == g, sink_ref[j, g] * LOG2E, sink)
            s = _qkt(q, k)
            m = jnp.maximum(jnp.max(s, axis=-1, keepdims=True), sink)
            num, den = _pv(jnp.exp2(s - m).astype(BF16), _with_ones(v))
            o = num / (den + jnp.exp2(sink - m))
            for g in range(G_B):
                c0 = QA_W + (j * G_B + g) * HEAD_DIM
                o_ref[rows, c0:c0 + HEAD_DIM] = o[g * SEQ:(g + 1) * SEQ, :].astype(BF16)


def _attn_ctx(sink, qkv):
    rows = CTX_SEQ_PER_STEP * SEQ
    return pl.pallas_call(
        _attn_ctx_kernel,
        grid=(N_PROMPT // rows,),
        in_specs=[
            pl.BlockSpec(memory_space=pltpu.SMEM),
            pl.BlockSpec((N_ATT_TILES, rows, TN), lambda b: (0, b, 0)),
        ],
        out_specs=pl.BlockSpec((rows, D_MODEL), lambda b: (b, 0)),
        out_shape=jax.ShapeDtypeStruct((N_PROMPT, D_MODEL), BF16),
        compiler_params=_params("arbitrary"),
        name="attn_ctx",
    )(sink, qkv)


NA_Q_ROWS = 4
NA_K_ROWS = 12
NA_Q_CHUNK = NA_Q_ROWS * GRID_W
NA_K_SPAN = NA_K_ROWS * GRID_W
NA_K_ROW0 = (0, 0, 4, 4)
N_RPB_ROWS = 2 * NA_ROWS - 1
N_RPB_COLS = 2 * NA_COLS - 1


def _na_row_start(r):
    return min(max(r - NA_ROWS // 2, 0), GRID_R - NA_ROWS)


for _chunk, _k0 in enumerate(NA_K_ROW0):
    for _r in range(_chunk * NA_Q_ROWS, (_chunk + 1) * NA_Q_ROWS):
        assert _k0 % 2 == 0 and _k0 <= _na_row_start(_r)
        assert _na_row_start(_r) + NA_ROWS <= _k0 + NA_K_ROWS <= GRID_R


def _na_build_bias(rpb_ref, tile_ref, bias_ref):
    shape = (GRID_W, 2 * GRID_W)
    qc = lax.broadcasted_iota(jnp.int32, shape, 0)
    lane = lax.broadcasted_iota(jnp.int32, shape, 1)
    kc = lane % GRID_W
    start_c = jnp.clip(qc - NA_COLS // 2, 0, GRID_W - NA_COLS)
    in_win = (kc >= start_c) & (kc < start_c + NA_COLS)
    for dr in range(N_RPB_ROWS):
        rows = jnp.broadcast_to(rpb_ref[dr:dr + 1, :], shape)
        shifted = pltpu.roll(rows, 2 * GRID_W - (NA_COLS - 1), 1, stride=1, stride_axis=0)
        tile_ref[dr] = jnp.where(in_win, shifted * LOG2E, NEG)
    first_half = lane < GRID_W
    neg = jnp.full(shape, NEG, F32)
    for chunk in range(GRID_R // NA_Q_ROWS):
        for qi in range(NA_Q_ROWS):
            qr = chunk * NA_Q_ROWS + qi
            lo = _na_row_start(qr)
            for m in range(NA_K_ROWS // 2):
                kr = NA_K_ROW0[chunk] + 2 * m
                parts = []
                for r in (kr, kr + 1):
                    parts.append(tile_ref[r - qr + NA_ROWS - 1] if lo <= r < lo + NA_ROWS else neg)
                bias_ref[chunk, qi * GRID_W:(qi + 1) * GRID_W, m * 2 * GRID_W:(m + 1) * 2 * GRID_W] = (
                    jnp.where(first_half, parts[0], parts[1]))


def _attn_na_kernel(rpb_ref, q_ref, k_ref, v_ref, kc_ref, vc_ref, o_ref, tile_ref, bias_ref):
    _na_build_bias(rpb_ref, tile_ref, bias_ref)
    for b in range(DEC_BATCH):
        kc = kc_ref[b].astype(BF16)
        vc = _with_ones(vc_ref[b].astype(BF16))
        for c in range(DEC_SEQ // NA_Q_CHUNK):
            r0 = b * DEC_SEQ + c * NA_Q_CHUNK
            k0 = b * DEC_SEQ + NA_K_ROW0[c] * GRID_W
            rows = slice(r0, r0 + NA_Q_CHUNK)
            keys = slice(k0, k0 + NA_K_SPAN)
            q = q_ref[rows, :]
            s = _qkt(q, k_ref[keys, :]) + bias_ref[c]
            sc = _qkt(q, kc)
            m = jnp.maximum(jnp.max(s, axis=-1, keepdims=True), jnp.max(sc, axis=-1, keepdims=True))
            num, den = _pv(jnp.exp2(s - m).astype(BF16), _with_ones(v_ref[keys, :]))
            num_c, den_c = _pv(jnp.exp2(sc - m).astype(BF16), vc)
            o_ref[rows, :] = ((num + num_c) / (den + den_c)).astype(BF16)


def _attn_na(qkv, rpb, cache_k, cache_v, layer):
    heads_per_tile = TN // HEAD_DIM

    def head_spec(c0):
        def index(h):
            head = c0 // HEAD_DIM + h
            return head // heads_per_tile, 0, head % heads_per_tile
        return pl.BlockSpec((None, N_SAMPLE, HEAD_DIM), index)

    ctx_spec = pl.BlockSpec((DEC_BATCH, None, PAST_LEN, HEAD_DIM), lambda h: (0, layer, 0, h))
    n_chunks = DEC_SEQ // NA_Q_CHUNK
    pad_rows = -N_RPB_ROWS % 8
    half = jnp.pad(rpb.astype(F32), ((0, 0), (0, pad_rows), (0, GRID_W - N_RPB_COLS)), constant_values=NEG)
    rpb_rows = jnp.concatenate([half, half], axis=-1)
    return pl.pallas_call(
        _attn_na_kernel,
        grid=(N_HEADS_A,),
        in_specs=[
            pl.BlockSpec((None, rpb_rows.shape[1], 2 * GRID_W), lambda h: (h, 0, 0)),
            head_spec(COL_QA),
            head_spec(COL_KA),
            head_spec(COL_VA),
            ctx_spec,
            ctx_spec,
        ],
        out_specs=pl.BlockSpec((N_SAMPLE, HEAD_DIM), lambda h: (0, h)),
        out_shape=jax.ShapeDtypeStruct((N_SAMPLE, QA_W), BF16),
        scratch_shapes=[pltpu.VMEM((N_RPB_ROWS, GRID_W, 2 * GRID_W), F32),
                        pltpu.VMEM((n_chunks, NA_Q_CHUNK, NA_K_SPAN), F32)],
        compiler_params=_params("arbitrary"),
        name="attn_na",
    )(rpb_rows, qkv, qkv, qkv, cache_k, cache_v)


WIN_Q_CHUNK = 128
WIN_K_SPAN = WIN_Q_CHUNK + 2 * WIN_B


def _attn_win_kernel(sink_ref, q_ref, k_ref, v_ref, kc_ref, vc_ref, o_ref):
    j = pl.program_id(1)
    kc = kc_ref[...].astype(BF16)
    vc = _with_ones(vc_ref[...].astype(BF16))
    n_rows = G_B * WIN_Q_CHUNK
    grp = lax.broadcasted_iota(jnp.int32, (n_rows, 1), 0) // WIN_Q_CHUNK
    sink = jnp.zeros((n_rows, 1), F32)
    for g in range(G_B):
        sink = jnp.where(grp == g, sink_ref[j, g] * LOG2E, sink)
    for c in range(DEC_SEQ // WIN_Q_CHUNK):
        q0 = c * WIN_Q_CHUNK
        k0 = min(max(q0 - WIN_B, 0), DEC_SEQ - WIN_K_SPAN)
        rows = slice(q0, q0 + WIN_Q_CHUNK)
        keys = slice(k0, k0 + WIN_K_SPAN)
        q = jnp.concatenate([q_ref[rows, g * HEAD_DIM:(g + 1) * HEAD_DIM] for g in range(G_B)], axis=0)
        qpos = q0 + lax.broadcasted_iota(jnp.int32, (n_rows, WIN_K_SPAN), 0) % WIN_Q_CHUNK
        kpos = k0 + lax.broadcasted_iota(jnp.int32, (n_rows, WIN_K_SPAN), 1)
        s = jnp.where(jnp.abs(qpos - kpos) <= WIN_B, _qkt(q, k_ref[keys, :]), NEG)
        sc = _qkt(q, kc)
        m = jnp.maximum(jnp.maximum(jnp.max(s, axis=-1, keepdims=True),
                                    jnp.max(sc, axis=-1, keepdims=True)), sink)
        num, den = _pv(jnp.exp2(s - m).astype(BF16), _with_ones(v_ref[keys, :]))
        num_c, den_c = _pv(jnp.exp2(sc - m).astype(BF16), vc)
        o = (num + num_c) / (den + den_c + jnp.exp2(sink - m))
        for g in range(G_B):
            o_ref[rows, g * HEAD_DIM:(g + 1) * HEAD_DIM] = (
                o[g * WIN_Q_CHUNK:(g + 1) * WIN_Q_CHUNK, :].astype(BF16))


def _attn_win(sink, qkv, cache_k, cache_v, layer):
    gw = G_B * HEAD_DIM
    ctx_spec = pl.BlockSpec((None, None, PAST_LEN, HEAD_DIM), lambda b, j: (b, layer, 0, j))
    return pl.pallas_call(
        _attn_win_kernel,
        grid=(DEC_BATCH, N_KV_B),
        in_specs=[
            pl.BlockSpec(memory_space=pltpu.SMEM),
            pl.BlockSpec((None, DEC_SEQ, gw), lambda b, j: (ATT_TILE_QB + j, b, 0)),
            pl.BlockSpec((None, DEC_SEQ, HEAD_DIM), lambda b, j: (ATT_TILE_KVB, b, j)),
            pl.BlockSpec((None, DEC_SEQ, HEAD_DIM), lambda b, j: (ATT_TILE_KVB, b, N_KV_B + j)),
            ctx_spec,
            ctx_spec,
        ],
        out_specs=pl.BlockSpec((DEC_SEQ, gw), lambda b, j: (b, j)),
        out_shape=jax.ShapeDtypeStruct((N_SAMPLE, QB_W), BF16),
        compiler_params=_params("arbitrary", "arbitrary"),
        name="attn_win",
    )(sink, qkv, qkv, qkv, cache_k, cache_v)


REC_CB = 512
REC_SLABS = REC_CB // RG_BLOCK
REC_STREAMS = 8
REC_T = 256
REC_GROUP_ROWS = REC_STREAMS * REC_T
REC_PRE = CONV_PAD_L
REC_POST = CONV_W - 1 - CONV_PAD_L
REC_CHUNK_ROWS = 32 * REC_STREAMS


def _softplus(x):
    return jnp.maximum(x, 0.0) + jnp.log1p(jnp.exp(-jnp.abs(x)))


def _gelu_tanh(x):
    k = np.sqrt(2.0 / np.pi)
    half = 0.5 * x
    return half + half * jnp.tanh(x * (k + (k * 0.044715) * (x * x)))


def _sqrt_nonneg(u):
    return jnp.where(u > 0.0, u * lax.rsqrt(u), 0.0)


def _rec_kernel(x_ref, g_ref, cw_ref, cb_ref, wg_ref, bg_ref, lam_ref, h0_ref, y_ref, st_ref,
                xt_ref, af_ref, bf_ref, ab_ref, bb_ref, *, n_seg):
    S, T = REC_STREAMS, REC_T
    a_refs = (af_ref, ab_ref)
    b_refs = (bf_ref, bb_ref)
    seg = lax.broadcasted_iota(jnp.int32, (S, RG_BLOCK), 0) % n_seg
    slab_cols = [slice(n * RG_BLOCK, (n + 1) * RG_BLOCK) for n in range(REC_SLABS)]

    def t_rows(t):
        return pl.ds(pl.multiple_of(t * S, S), S)

    def from_prev_stream(x):
        return pltpu.roll(x, 1, 0)

    def from_next_stream(x):
        return pltpu.roll(x, S - 1, 0)

    for n, cols in enumerate(slab_cols):
        for s in range(S):
            xt_ref[n, pl.ds(REC_PRE * S + s, T, stride=S), :] = x_ref[s * T:(s + 1) * T, cols]
        for p in range(REC_PRE):
            src = xt_ref[n, (T + p) * S:(T + p + 1) * S, :]
            xt_ref[n, p * S:(p + 1) * S, :] = jnp.where(seg > 0, from_prev_stream(src), 0.0)
        for p in range(REC_POST):
            src = xt_ref[n, (REC_PRE + p) * S:(REC_PRE + p + 1) * S, :]
            xt_ref[n, (REC_PRE + T + p) * S:(REC_PRE + T + p + 1) * S, :] = (
                jnp.where(seg < n_seg - 1, from_next_stream(src), 0.0))

    c_all = (-0.5 * RG_C * np.log2(np.e)) * _softplus(-lam_ref[...])

    def gate_rows(r, carry):
        r0 = pl.multiple_of(r * REC_CHUNK_ROWS, REC_CHUNK_ROWS)
        rows = pl.ds(r0, REC_CHUNK_ROWS)
        for n, cols in enumerate(slab_cols):
            xc = cb_ref[:, cols] + cw_ref[0:1, cols] * xt_ref[n, rows, :]
            for k in range(1, CONV_W):
                tap_rows = pl.ds(pl.multiple_of(r0 + k * S, S), REC_CHUNK_ROWS)
                xc = xc + cw_ref[k:k + 1, cols] * xt_ref[n, tap_rows, :]
            gates = jnp.dot(xc.astype(BF16), wg_ref[n].astype(BF16), preferred_element_type=F32)
            x_half = 0.5 * xc
            for d in range(2):
                ga_half = gates[:, (2 * d) * RG_BLOCK:(2 * d + 1) * RG_BLOCK] + bg_ref[2 * d:2 * d + 1, cols]
                gx_half = (gates[:, (2 * d + 1) * RG_BLOCK:(2 * d + 2) * RG_BLOCK]
                           + bg_ref[2 * d + 1:2 * d + 2, cols])
                c = c_all[d:d + 1, cols]
                a = jnp.exp2(c * jnp.tanh(ga_half) + c)
                a_refs[d][n, rows, :] = a
                b_refs[d][n, rows, :] = _sqrt_nonneg(1.0 - a * a) * ((1.0 + jnp.tanh(gx_half)) * x_half)
        return carry

    lax.fori_loop(0, T * S // REC_CHUNK_ROWS, gate_rows, 0)

    def scan_step(t, carry):
        hf, hb, pf, pb = carry
        rf, rb = t_rows(t), t_rows(T - 1 - t)
        hf_new, hb_new, pf_new, pb_new = [], [], [], []
        for n in range(REC_SLABS):
            a = af_ref[n, rf, :]
            h = a * hf[n] + bf_ref[n, rf, :]
            bf_ref[n, rf, :] = h
            hf_new.append(h)
            a2 = ab_ref[n, rb, :]
            h2 = a2 * hb[n] + bb_ref[n, rb, :]
            bb_ref[n, rb, :] = h2
            hb_new.append(h2)
            if n_seg > 1:
                p = a * pf[n]
                af_ref[n, rf, :] = p
                pf_new.append(p)
                p2 = a2 * pb[n]
                ab_ref[n, rb, :] = p2
                pb_new.append(p2)
        return tuple(hf_new), tuple(hb_new), tuple(pf_new), tuple(pb_new)

    ones = tuple(jnp.ones((S, RG_BLOCK), F32) for _ in range(REC_SLABS)) if n_seg > 1 else ()
    hf, hb, pf, pb = lax.fori_loop(
        0, T, scan_step,
        (tuple(h0_ref[0, :, cols] for cols in slab_cols), tuple(h0_ref[1, :, cols] for cols in slab_cols),
         ones, ones), unroll=4)

    if n_seg > 1:
        cin_f, cin_b = [], []
        for n in range(REC_SLABS):
            cf = jnp.zeros((S, RG_BLOCK), F32)
            for j in range(1, n_seg):
                cf = jnp.where(seg == j, from_prev_stream(hf[n] + pf[n] * cf), cf)
            cb_in = jnp.zeros((S, RG_BLOCK), F32)
            for j in range(n_seg - 2, -1, -1):
                cb_in = jnp.where(seg == j, from_next_stream(hb[n] + pb[n] * cb_in), cb_in)
            cin_f.append(cf)
            cin_b.append(cb_in)

        def carry_in_step(t, carry):
            rows = t_rows(t)
            for n in range(REC_SLABS):
                bf_ref[n, rows, :] += af_ref[n, rows, :] * cin_f[n]
                bb_ref[n, rows, :] += ab_ref[n, rows, :] * cin_b[n]
            return carry

        lax.fori_loop(0, T, carry_in_step, 0, unroll=4)
        hf = tuple(hf[n] + pf[n] * cin_f[n] for n in range(REC_SLABS))
        hb = tuple(hb[n] + pb[n] * cin_b[n] for n in range(REC_SLABS))

    for n, cols in enumerate(slab_cols):
        st_ref[0, :, cols] = hf[n]
        st_ref[1, :, cols] = hb[n]

    for n, cols in enumerate(slab_cols):
        for s in range(S):
            rows = slice(s * T, (s + 1) * T)
            picked = pl.ds(s, T, stride=S)
            h_sum = bf_ref[n, picked, :] + bb_ref[n, picked, :]
            y_ref[rows, cols] = (h_sum * _gelu_tanh(g_ref[rows, cols])).astype(BF16)


def _rec(xg, n_seg, cw, cb, wg, bg, lam, h0):
    n_rows, tn = xg.shape[1], xg.shape[2]
    n_grp = n_rows // REC_GROUP_ROWS
    nc = D_RNN // REC_CB
    per_tile = tn // REC_CB

    def branch_spec(first):
        return pl.BlockSpec((None, REC_GROUP_ROWS, REC_CB),
                            lambda s, c: ((first + c) // per_tile, s, (first + c) % per_tile))

    vec = lambda rows: pl.BlockSpec((rows, REC_CB), lambda s, c: (0, c))
    state_spec = pl.BlockSpec((2, REC_STREAMS, REC_CB), lambda s, c: (0, s, c))
    slab_scratch = lambda n_t: pltpu.VMEM((REC_SLABS, n_t * REC_STREAMS, RG_BLOCK), F32)
    return pl.pallas_call(
        functools.partial(_rec_kernel, n_seg=n_seg),
        grid=(n_grp, nc),
        in_specs=[
            branch_spec(0),
            branch_spec(nc),
            vec(CONV_W),
            vec(1),
            pl.BlockSpec((REC_SLABS, RG_BLOCK, 4 * RG_BLOCK), lambda s, c: (c, 0, 0)),
            vec(4),
            vec(2),
            state_spec,
        ],
        out_specs=[
            pl.BlockSpec((None, REC_GROUP_ROWS, REC_CB), lambda s, c: (c, s, 0)),
            state_spec,
        ],
        out_shape=[
            jax.ShapeDtypeStruct((nc, n_rows, REC_CB), BF16),
            jax.ShapeDtypeStruct((2, n_grp * REC_STREAMS, D_RNN), F32),
        ],
        scratch_shapes=[slab_scratch(REC_PRE + REC_T + REC_POST)] + [slab_scratch(REC_T)] * 4,
        compiler_params=_params("arbitrary", "arbitrary"),
        name="rec",
    )(xg, xg, cw, cb.reshape(1, D_RNN), wg, bg, lam, h0)


def kernel(x_prompt, x_sample, c, cache_a_k, cache_a_v, cache_b_k, cache_b_v, state_rg_fwd, state_rg_bwd, c_ctx, w_ada, b_ada, g_pre_mix, g_post_mix, g_pre_ffn, g_post_ffn, w_att_in, w_att_out, sink_b, rpb_a, w_rec_in, conv_w, conv_b, w_rg_a, b_rg_a, w_rg_x, b_rg_x, rg_lambda, w_rec_out, w_ff1, w_ff2):
    depth = w_ada.shape[0]
    yp = x_prompt.reshape(N_PROMPT, D_MODEL)
    ys = x_sample.reshape(N_SAMPLE, D_MODEL)

    cond8 = jnp.concatenate([c_ctx[None, :], c, jnp.zeros((MOD_ROWS - N_SEG, D_MODEL), F32)], axis=0)
    mod_all = _adaln(cond8, w_ada, b_ada)
    mod_all = mod_all[:, :N_SEG, :].reshape(depth, N_SEG, N_MOD, D_MODEL)
    mod_all = jnp.pad(mod_all, ((0, 0), (0, 0), (0, MOD_ROWS - N_MOD), (0, 0)))

    a_k, a_v, b_k, b_v, s_f, s_b = [], [], [], [], [], []
    for layer in range(depth):
        mod = mod_all[layer]
        li = layer // 2
        g_pre, g_post = g_pre_mix[layer], g_post_mix[layer]
        if layer % 2 == 0:
            qkv_p, ka, va, kb, vb = _proj_att_prompt(yp, g_pre, mod, w_att_in, li)
            qkv_s = _proj_att_sample(ys, g_pre, mod, w_att_in, li)
            a_k.append(ka.reshape(BATCH, SEQ, N_HEADS_A, HEAD_DIM))
            a_v.append(va.reshape(BATCH, SEQ, N_HEADS_A, HEAD_DIM))
            b_k.append(kb.reshape(BATCH, SEQ, N_KV_B, HEAD_DIM))
            b_v.append(vb.reshape(BATCH, SEQ, N_KV_B, HEAD_DIM))
            mix_p = [_attn_ctx(sink_b[li], qkv_p)]
            n_att = cache_a_k.shape[1]
            mix_s = [_attn_na(qkv_s, rpb_a[li],
                              cache_a_k.reshape(DEC_BATCH, n_att, PAST_LEN, QA_W),
                              cache_a_v.reshape(DEC_BATCH, n_att, PAST_LEN, QA_W), li),
                     _attn_win(sink_b[li], qkv_s,
                               cache_b_k.reshape(DEC_BATCH, n_att, PAST_LEN, KVB_W),
                               cache_b_v.reshape(DEC_BATCH, n_att, PAST_LEN, KVB_W), li)]
            w_out = w_att_out
        else:
            xg_p = _proj(PROMPT, yp, g_pre, mod, w_rec_in, li)
            xg_s = _proj(SAMPLE, ys, g_pre, mod, w_rec_in, li)
            wg = 0.5 * jnp.concatenate([w_rg_a[li, 0], w_rg_x[li, 0], w_rg_a[li, 1], w_rg_x[li, 1]], axis=-1)
            bg = 0.5 * jnp.stack([b_rg_a[li, 0], b_rg_x[li, 0], b_rg_a[li, 1], b_rg_x[li, 1]], axis=0)
            rec_args = (conv_w[li], conv_b[li], wg, bg, rg_lambda[li])
            n_seg = DEC_SEQ // REC_T
            assert SEQ == REC_T and BATCH % REC_STREAMS == 0 and DEC_BATCH * n_seg == REC_STREAMS
            h0_p = jnp.zeros((2, BATCH, D_RNN), F32)
            seg_state = jnp.zeros((DEC_BATCH, n_seg, D_RNN), F32)
            h0_s = jnp.stack([seg_state.at[:, 0].set(state_rg_fwd[:, li]).reshape(REC_STREAMS, D_RNN),
                              seg_state.at[:, n_seg - 1].set(state_rg_bwd[:, li]).reshape(REC_STREAMS, D_RNN)])
            rec_p, st_p = _rec(xg_p, 1, *rec_args, h0_p)
            rec_s, _ = _rec(xg_s, n_seg, *rec_args, h0_s)
            mix_p, mix_s = rec_p, rec_s
            s_f.append(st_p[0])
            s_b.append(st_p[1])
            w_out = w_rec_out
        yp = _mixout(PROMPT, mix_p, w_out, li, yp, g_post, mod)
        ys = _mixout(SAMPLE, mix_s, w_out, li, ys, g_post, mod)
        yp = _ffn(PROMPT, yp, g_pre_ffn[layer], g_post_ffn[layer], mod, w_ff1, w_ff2, layer)
        ys = _ffn(SAMPLE, ys, g_pre_ffn[layer], g_post_ffn[layer], mod, w_ff1, w_ff2, layer)

    return (yp.reshape(BATCH, SEQ, D_MODEL), ys.reshape(DEC_BATCH, DEC_SEQ, D_MODEL),
            jnp.stack(a_k, axis=1), jnp.stack(a_v, axis=1), jnp.stack(b_k, axis=1), jnp.stack(b_v, axis=1),
            jnp.stack(s_f, axis=1), jnp.stack(s_b, axis=1))
```

```python
import functools

import jax
import jax.numpy as jnp
import numpy as np
from jax import lax
from jax.experimental import pallas as pl
from jax.experimental.pallas import tpu as pltpu

D_MODEL = 2048
BATCH = 16
SEQ = 256
DEC_BATCH = 2
DEC_SEQ = 1024
PAST_LEN = 256
GRID_W = 64
GRID_R = DEC_SEQ // GRID_W
HEAD_DIM = 128
N_HEADS_A = 8
N_HEADS_B = 8
N_KV_B = 2
G_B = N_HEADS_B // N_KV_B
NA_ROWS = 8
NA_COLS = 16
WIN_B = 128
D_RNN = D_MODEL
N_RG_BLOCKS = 16
RG_BLOCK = D_RNN // N_RG_BLOCKS
CONV_W = 4
CONV_PAD_L = 2
RG_C = 8.0
D_FF = 4 * D_MODEL
ROPE_BASE = 10000.0
EPS = 1e-6
NEG = -1e30
QA_W = N_HEADS_A * HEAD_DIM
QB_W = N_HEADS_B * HEAD_DIM
KVB_W = N_KV_B * HEAD_DIM
D_ATT_IN = 3 * QA_W + QB_W + 2 * KVB_W
SCALE = HEAD_DIM ** -0.5

N_PROMPT = BATCH * SEQ
N_SAMPLE = DEC_BATCH * DEC_SEQ
N_SEG = 1 + DEC_BATCH
N_MOD = 6
MOD_ROWS = 8

COL_QA, COL_KA, COL_VA = 0, QA_W, 2 * QA_W
COL_QB = 3 * QA_W
COL_KB = COL_QB + QB_W
COL_VB = COL_KB + KVB_W

V7X_VMEM_BYTES = 64 * 1024 * 1024
VMEM_LIMIT = V7X_VMEM_BYTES - 4 * 1024 * 1024

TM = 1024
TN = 512
ROW_CHUNK = 16
ROW_GROUP = 16

F32 = jnp.float32
BF16 = jnp.bfloat16


def _params(*sem):
    return pltpu.CompilerParams(dimension_semantics=sem, vmem_limit_bytes=VMEM_LIMIT)


class _Rows:
    def __init__(self, n_rows, seg0, seg_rows):
        self.n = n_rows
        self.seg0 = seg0
        self.seg_rows = seg_rows

    def seg(self, i, tm):
        return self.seg0 + (i * tm) // self.seg_rows


PROMPT = _Rows(N_PROMPT, 0, N_PROMPT)
SAMPLE = _Rows(N_SAMPLE, 1, DEC_SEQ)


def _mod_spec(rows, tm):
    return pl.BlockSpec((None, MOD_ROWS, D_MODEL), lambda i, j: (rows.seg(i, tm), 0, 0))


def _vec_spec():
    return pl.BlockSpec((1, D_MODEL), lambda i, j: (0, 0))


def _rms_scale(x):
    return lax.rsqrt(jnp.mean(x * x, axis=-1, keepdims=True) + EPS)


def _norm_mod_rows(y_ref, g_ref, mod_ref, h_ref, shift_row, tm, h_row0=0):
    shift = mod_ref[shift_row:shift_row + 1, :]
    gain = g_ref[...] * (1.0 + mod_ref[shift_row + 1:shift_row + 2, :])

    def body(r, carry):
        starts = [r * (ROW_GROUP * ROW_CHUNK) + u * ROW_CHUNK for u in range(ROW_GROUP)]
        scales = [_rms_scale(y_ref[pl.ds(pl.multiple_of(r0, ROW_CHUNK), ROW_CHUNK), :]) for r0 in starts]
        for r0, rs in zip(starts, scales):
            y = y_ref[pl.ds(pl.multiple_of(r0, ROW_CHUNK), ROW_CHUNK), :]
            h = (y * rs) * gain + shift
            h_ref[pl.ds(pl.multiple_of(h_row0 + r0, ROW_CHUNK), ROW_CHUNK), :] = h.astype(BF16)
        return carry

    lax.fori_loop(0, tm // (ROW_GROUP * ROW_CHUNK), body, 0)


def _gated_residual_rows(y_ref, o_ref, g_ref, gate, tm):
    gain = gate * g_ref[...]

    def body(r, carry):
        chunks = [pl.ds(pl.multiple_of(r * (ROW_GROUP * ROW_CHUNK) + u * ROW_CHUNK, ROW_CHUNK), ROW_CHUNK)
                  for u in range(ROW_GROUP)]
        scales = [_rms_scale(o_ref[rows, :]) for rows in chunks]
        for rows, rs in zip(chunks, scales):
            o_ref[rows, :] = y_ref[rows, :] + (o_ref[rows, :] * rs) * gain
        return carry

    lax.fori_loop(0, tm // (ROW_GROUP * ROW_CHUNK), body, 0)


def _adaln_kernel(cond_ref, w_ref, b_ref, o_ref):
    c = cond_ref[...]
    s = c / (1.0 + jnp.exp(-c))
    o_ref[...] = jnp.dot(s.astype(BF16), w_ref[...].astype(BF16),
                         preferred_element_type=F32) + b_ref[...]


def _adaln(cond8, w_ada, b_ada):
    depth = w_ada.shape[0]
    n = w_ada.shape[2]
    tn = 1024
    return pl.pallas_call(
        _adaln_kernel,
        grid=(depth, n // tn),
        in_specs=[
            pl.BlockSpec((MOD_ROWS, D_MODEL), lambda l, j: (0, 0)),
            pl.BlockSpec((None, D_MODEL, tn), lambda l, j: (l, 0, j)),
            pl.BlockSpec((None, 1, tn), lambda l, j: (l, 0, j)),
        ],
        out_specs=pl.BlockSpec((None, MOD_ROWS, tn), lambda l, j: (l, 0, j)),
        out_shape=jax.ShapeDtypeStruct((depth, MOD_ROWS, n), F32),
        compiler_params=_params("arbitrary", "arbitrary"),
        name="adaln",
    )(cond8, w_ada, b_ada.reshape(depth, 1, n))


def _rope_tables():
    t = np.arange(DEC_SEQ)
    half = HEAD_DIM // 2
    inv = ROPE_BASE ** (-np.arange(0, half, 2, dtype=np.float64) / half)
    ang_r = (t // GRID_W)[:, None] * inv[None, :]
    ang_c = (t % GRID_W)[:, None] * inv[None, :]
    cos = np.concatenate([np.cos(ang_r), np.cos(ang_r), np.cos(ang_c), np.cos(ang_c)], axis=1)
    sin = np.concatenate([-np.sin(ang_r), np.sin(ang_r), -np.sin(ang_c), np.sin(ang_c)], axis=1)
    return jnp.asarray(cos, F32), jnp.asarray(sin, F32)


def _rope(x, cos, sin_signed):
    quarter = HEAD_DIM // 4
    lane = lax.broadcasted_iota(jnp.int32, x.shape, 1)
    first = (lane % (2 * quarter)) < quarter
    partner = jnp.where(first, pltpu.roll(x, HEAD_DIM - quarter, 1), pltpu.roll(x, quarter, 1))
    return x * cos + partner * sin_signed


def _proj_h_tile(y_ref, g_ref, mod_ref, h_ref, tm):
    row0 = pl.multiple_of(pl.program_id(1) * tm, tm)

    @pl.when(pl.program_id(0) == 0)
    def _():
        _norm_mod_rows(y_ref, g_ref, mod_ref, h_ref, 0, tm, row0)

    return h_ref[pl.ds(row0, tm), :]


def _proj_in_specs(rows, tm, layer, tn=TN):
    last = rows.n // tm - 1

    def tile(j, i):
        return jnp.where(j == 0, i, last)

    return [
        pl.BlockSpec((tm, D_MODEL), lambda j, i: (tile(j, i), 0)),
        pl.BlockSpec((1, D_MODEL), lambda j, i: (0, 0)),
        pl.BlockSpec((None, MOD_ROWS, D_MODEL), lambda j, i: (rows.seg(tile(j, i), tm), 0, 0)),
        pl.BlockSpec((None, D_MODEL, tn), lambda j, i: (layer, 0, j)),
    ]


def _proj_tn(rows):
    return TN if rows.n > N_SAMPLE else 2 * TN


def _proj_kernel(y_ref, g_ref, mod_ref, w_ref, o_ref, h_ref, *, tm):
    h = _proj_h_tile(y_ref, g_ref, mod_ref, h_ref, tm)
    o_ref[...] = jnp.dot(h, w_ref[...].astype(BF16), preferred_element_type=F32)


def _proj(rows, y, g, mod, w, layer):
    n = w.shape[2]
    tm, tn = TM, _proj_tn(rows)
    return pl.pallas_call(
        functools.partial(_proj_kernel, tm=tm),
        grid=(n // tn, rows.n // tm),
        in_specs=_proj_in_specs(rows, tm, layer, tn),
        out_specs=pl.BlockSpec((None, tm, tn), lambda j, i: (j, i, 0)),
        out_shape=jax.ShapeDtypeStruct((n // tn, rows.n, tn), F32),
        scratch_shapes=[pltpu.VMEM((rows.n, D_MODEL), BF16)],
        compiler_params=_params("arbitrary", "arbitrary"),
        name="proj",
    )(y, g.reshape(1, D_MODEL), mod, w)


ATT_TILE_KA = COL_KA // TN
ATT_TILE_VA = COL_VA // TN
ATT_TILE_QB = COL_QB // TN
ATT_TILE_KVB = COL_KB // TN
N_ATT_TILES = D_ATT_IN // TN

LOG2E = float(np.log2(np.e))
Q_FACTOR = SCALE * LOG2E


def _q_factor(j):
    is_q = (j < ATT_TILE_KA) | ((j >= ATT_TILE_QB) & (j < ATT_TILE_KVB))
    return jnp.where(is_q, Q_FACTOR, 1.0).astype(F32)


def _proj_att_prompt_kernel(y_ref, g_ref, mod_ref, w_ref, o_ref, ka_ref, va_ref, kb_ref, vb_ref, h_ref, *, tm):
    j = pl.program_id(0)
    h = _proj_h_tile(y_ref, g_ref, mod_ref, h_ref, tm)
    acc = jnp.dot(h, w_ref[...].astype(BF16), preferred_element_type=F32)
    o_ref[...] = (acc * _q_factor(j)).astype(BF16)

    @pl.when((j >= ATT_TILE_KA) & (j < ATT_TILE_VA))
    def _():
        ka_ref[...] = acc

    @pl.when((j >= ATT_TILE_VA) & (j < ATT_TILE_QB))
    def _():
        va_ref[...] = acc

    @pl.when(j == ATT_TILE_KVB)
    def _():
        kb_ref[...] = acc[:, :KVB_W]
        vb_ref[...] = acc[:, KVB_W:]


def _proj_att_prompt(y, g, mod, w, layer):
    tm = TM
    rows = PROMPT
    last = rows.n // tm - 1

    def kv_map(first, count):
        def index(j, i):
            row = jnp.where(j < first, 0, jnp.where(j >= first + count, last, i))
            return row, jnp.clip(j - first, 0, count - 1)
        return index

    per_head_set = QA_W // TN
    return pl.pallas_call(
        functools.partial(_proj_att_prompt_kernel, tm=tm),
        grid=(N_ATT_TILES, rows.n // tm),
        in_specs=_proj_in_specs(rows, tm, layer),
        out_specs=[
            pl.BlockSpec((None, tm, TN), lambda j, i: (j, i, 0)),
            pl.BlockSpec((tm, TN), kv_map(ATT_TILE_KA, per_head_set)),
            pl.BlockSpec((tm, TN), kv_map(ATT_TILE_VA, per_head_set)),
            pl.BlockSpec((tm, KVB_W), kv_map(ATT_TILE_KVB, 1)),
            pl.BlockSpec((tm, KVB_W), kv_map(ATT_TILE_KVB, 1)),
        ],
        out_shape=[
            jax.ShapeDtypeStruct((N_ATT_TILES, rows.n, TN), BF16),
            jax.ShapeDtypeStruct((rows.n, QA_W), F32),
            jax.ShapeDtypeStruct((rows.n, QA_W), F32),
            jax.ShapeDtypeStruct((rows.n, KVB_W), F32),
            jax.ShapeDtypeStruct((rows.n, KVB_W), F32),
        ],
        scratch_shapes=[pltpu.VMEM((rows.n, D_MODEL), BF16)],
        compiler_params=_params("arbitrary", "arbitrary"),
        name="proj_att_prompt",
    )(y, g.reshape(1, D_MODEL), mod, w)


def _proj_att_sample_kernel(y_ref, g_ref, mod_ref, w_ref, cos_ref, sin_ref, o_ref, h_ref, *, tm):
    j = pl.program_id(0)
    h = _proj_h_tile(y_ref, g_ref, mod_ref, h_ref, tm)
    acc = jnp.dot(h, w_ref[...].astype(BF16), preferred_element_type=F32)

    def store(n_rope_heads):
        cos = cos_ref[...]
        sin = sin_ref[...]
        for hd in range(TN // HEAD_DIM):
            cols = slice(hd * HEAD_DIM, (hd + 1) * HEAD_DIM)
            x = acc[:, cols]
            if hd < n_rope_heads:
                x = _rope(x, cos, sin)
            o_ref[:, cols] = (x * q_factor).astype(BF16)

    q_factor = _q_factor(j)

    @pl.when(j < ATT_TILE_QB)
    def _():
        o_ref[...] = (acc * q_factor).astype(BF16)

    @pl.when((j >= ATT_TILE_QB) & (j < ATT_TILE_KVB))
    def _():
        store(TN // HEAD_DIM)

    @pl.when(j == ATT_TILE_KVB)
    def _():
        store(N_KV_B)


def _proj_att_sample(y, g, mod, w, layer):
    tm = TM
    assert tm == DEC_SEQ
    rows = SAMPLE
    cos, sin = _rope_tables()
    tab_spec = pl.BlockSpec((DEC_SEQ, HEAD_DIM), lambda j, i: (0, 0))
    return pl.pallas_call(
        functools.partial(_proj_att_sample_kernel, tm=tm),
        grid=(N_ATT_TILES, rows.n // tm),
        in_specs=_proj_in_specs(rows, tm, layer) + [tab_spec, tab_spec],
        out_specs=pl.BlockSpec((None, tm, TN), lambda j, i: (j, i, 0)),
        out_shape=jax.ShapeDtypeStruct((N_ATT_TILES, rows.n, TN), BF16),
        scratch_shapes=[pltpu.VMEM((rows.n, D_MODEL), BF16)],
        compiler_params=_params("arbitrary", "arbitrary"),
        name="proj_att_sample",
    )(y, g.reshape(1, D_MODEL), mod, w, cos, sin)


MIXOUT_TM = 512


def _mixout_kernel(*refs, tm, n_parts):
    a_refs = refs[:n_parts]
    w_ref, y_ref, g_ref, mod_ref, o_ref = refs[n_parts:]
    kp = a_refs[0].shape[1]
    for c in range(D_MODEL // TN):
        cols = slice(c * TN, (c + 1) * TN)
        acc = None
        for p in range(n_parts):
            part = jnp.dot(a_refs[p][...], w_ref[p * kp:(p + 1) * kp, cols].astype(BF16),
                           preferred_element_type=F32)
            acc = part if acc is None else acc + part
        o_ref[:, cols] = acc
    _gated_residual_rows(y_ref, o_ref, g_ref, mod_ref[2:3, :], tm)


def _mixout(rows, a_parts, w, layer, y, g, mod):
    tm = MIXOUT_TM
    if isinstance(a_parts, (list, tuple)):
        n_parts, kp = len(a_parts), a_parts[0].shape[1]
        part_specs = [pl.BlockSpec((tm, kp), lambda i: (i, 0)) for _ in range(n_parts)]
    else:
        n_parts, kp = a_parts.shape[0], a_parts.shape[2]
        part_specs = [pl.BlockSpec((None, tm, kp), lambda i, p=p: (p, i, 0)) for p in range(n_parts)]
        a_parts = [a_parts] * n_parts
    assert n_parts * kp == w.shape[1]
    return pl.pallas_call(
        functools.partial(_mixout_kernel, tm=tm, n_parts=n_parts),
        grid=(rows.n // tm,),
        in_specs=part_specs + [
            pl.BlockSpec((None, w.shape[1], D_MODEL), lambda i: (layer, 0, 0), pipeline_mode=pl.Buffered(1)),
            pl.BlockSpec((tm, D_MODEL), lambda i: (i, 0)),
            pl.BlockSpec((1, D_MODEL), lambda i: (0, 0)),
            pl.BlockSpec((None, MOD_ROWS, D_MODEL), lambda i: (rows.seg(i, tm), 0, 0)),
        ],
        out_specs=pl.BlockSpec((tm, D_MODEL), lambda i: (i, 0)),
        out_shape=jax.ShapeDtypeStruct((rows.n, D_MODEL), F32),
        compiler_params=_params("arbitrary"),
        name="mixout",
    )(*a_parts, w, y, g.reshape(1, D_MODEL), mod)


FFN_TF = 256
FFN_PAIR = 2 * FFN_TF
FFN_N_PAIRS = D_FF // FFN_PAIR
FFN_GROUP = 2
FFN_N_GROUPS = FFN_N_PAIRS // FFN_GROUP


def _ffn_kernel(y_ref, g1_ref, g2_ref, mod_ref, w1_hbm, w2_hbm, o_ref, h_ref, a_ref, w1_buf, w2_buf, sem,
                *, tm, layer, n_tiles):
    i = pl.program_id(0)
    g = pl.program_id(1)
    last_tile = n_tiles - 1

    def block(pair):
        start = pair * FFN_PAIR
        return pl.ds(start if isinstance(start, int) else pl.multiple_of(start, FFN_PAIR), FFN_PAIR)

    def w1_copy(pair, slot):
        return pltpu.make_async_copy(w1_hbm.at[layer, :, block(pair)], w1_buf.at[slot], sem.at[0, slot])

    def w2_copy(pair, slot):
        return pltpu.make_async_copy(w2_hbm.at[layer, block(pair), :], w2_buf.at[slot], sem.at[1, slot])

    def up(slot):
        for half in range(FFN_PAIR // FFN_TF):
            cols = slice(half * FFN_TF, (half + 1) * FFN_TF)
            a = jnp.dot(h_ref[...], w1_buf[slot, :, cols].astype(BF16), preferred_element_type=F32)
            a = jnp.maximum(a, 0.0)
            a_ref[slot, :, cols] = (a * a).astype(BF16)

    def down(a_slot, w_slot, first=False):
        a = a_ref[a_slot]
        for c in range(D_MODEL // TN):
            cols = slice(c * TN, (c + 1) * TN)
            part = jnp.dot(a, w2_buf[w_slot, :, cols].astype(BF16), preferred_element_type=F32)
            if first:
                o_ref[:, cols] = part
            else:
                o_ref[:, cols] += part

    def pair_step(q, u):
        slot = u % 2
        w1_copy(q, slot).wait()
        w1_copy((q + 1) % FFN_N_PAIRS, 1 - slot).start()
        w2_copy(q, 1 - slot).start()
        return slot

    @pl.when(g == 0)
    def _():
        @pl.when(i == 0)
        def _():
            w1_copy(0, 0).start()

        _norm_mod_rows(y_ref, g1_ref, mod_ref, h_ref, 3, tm)
        for u in range(FFN_GROUP):
            slot = pair_step(u, u)
            if u > 0:
                w2_copy(u - 1, slot).wait()
            up(slot)
            if u > 0:
                down(1 - slot, slot, first=(u == 1))

    @pl.when(g > 0)
    def _():
        for u in range(FFN_GROUP):
            q = g * FFN_GROUP + u
            slot = pair_step(q, u)
            w2_copy(q - 1, slot).wait()
            up(slot)
            down(1 - slot, slot)

    @pl.when(g == FFN_N_GROUPS - 1)
    def _():
        w2_copy(FFN_N_PAIRS - 1, 0).wait()
        down((FFN_N_PAIRS - 1) % 2, 0)
        _gated_residual_rows(y_ref, o_ref, g2_ref, mod_ref[5:6, :], tm)

        @pl.when(i == last_tile)
        def _():
            w1_copy(0, 0).wait()


def _ffn(rows, y, g1, g2, mod, w1, w2, layer):
    tm = TM
    n_tiles = rows.n // tm
    assert FFN_GROUP % 2 == 0 and FFN_N_PAIRS % FFN_GROUP == 0
    return pl.pallas_call(
        functools.partial(_ffn_kernel, tm=tm, layer=layer, n_tiles=n_tiles),
        grid=(n_tiles, FFN_N_GROUPS),
        in_specs=[
            pl.BlockSpec((tm, D_MODEL), lambda i, s: (i, 0)),
            _vec_spec(),
            _vec_spec(),
            _mod_spec(rows, tm),
            pl.BlockSpec(memory_space=pl.ANY),
            pl.BlockSpec(memory_space=pl.ANY),
        ],
        out_specs=pl.BlockSpec((tm, D_MODEL), lambda i, s: (i, 0)),
        out_shape=jax.ShapeDtypeStruct((rows.n, D_MODEL), F32),
        scratch_shapes=[
            pltpu.VMEM((tm, D_MODEL), BF16),
            pltpu.VMEM((2, tm, FFN_PAIR), BF16),
            pltpu.VMEM((2, D_MODEL, FFN_PAIR), F32),
            pltpu.VMEM((2, FFN_PAIR, D_MODEL), F32),
            pltpu.SemaphoreType.DMA((2, 2)),
        ],
        compiler_params=_params("arbitrary", "arbitrary"),
        name="ffn",
    )(y, g1.reshape(1, D_MODEL), g2.reshape(1, D_MODEL), mod, w1, w2)


def _qkt(q, k):
    return lax.dot_general(q, k, (((1,), (1,)), ((), ())), preferred_element_type=F32)


def _with_ones(v):
    return jnp.concatenate([v, jnp.ones_like(v)], axis=1)


def _pv(p, v_ext):
    o = jnp.dot(p, v_ext, preferred_element_type=F32)
    return o[:, :HEAD_DIM], o[:, HEAD_DIM:]


CTX_SEQ_PER_STEP = 2


def _attn_ctx_kernel(sink_ref, qkv_ref, o_ref):
    n_rows = G_B * SEQ
    grp = lax.broadcasted_iota(jnp.int32, (n_rows, 1), 0) // SEQ
    for b in range(CTX_SEQ_PER_STEP):
        rows = slice(b * SEQ, (b + 1) * SEQ)

        def head(col):
            tile, off = divmod(col, TN)
            return qkv_ref[tile, rows, off:off + HEAD_DIM]

        for h in range(N_HEADS_A):
            q = head(COL_QA + h * HEAD_DIM)
            k = head(COL_KA + h * HEAD_DIM)
            v = head(COL_VA + h * HEAD_DIM)
            s = _qkt(q, k)
            m = jnp.max(s, axis=-1, keepdims=True)
            num, den = _pv(jnp.exp2(s - m).astype(BF16), _with_ones(v))
            o_ref[rows, h * HEAD_DIM:(h + 1) * HEAD_DIM] = (num / den).astype(BF16)
        for j in range(N_KV_B):
            k = head(COL_KB + j * HEAD_DIM)
            v = head(COL_VB + j * HEAD_DIM)
            q = jnp.concatenate([head(COL_QB + (j * G_B + g) * HEAD_DIM) for g in range(G_B)], axis=0)
            sink = jnp.zeros((n_rows, 1), F32)
            for g in range(G_B):
                sink = jnp.where(grp == g, sink_ref[j, g] * LOG2E, sink)
            s = _qkt(q, k)
            m = jnp.maximum(jnp.max(s, axis=-1, keepdims=True), sink)
            num, den = _pv(jnp.exp2(s - m).astype(BF16), _with_ones(v))
            o = num / (den + jnp.exp2(sink - m))
            for g in range(G_B):
                c0 = QA_W + (j * G_B + g) * HEAD_DIM
                o_ref[rows, c0:c0 + HEAD_DIM] = o[g * SEQ:(g + 1) * SEQ, :].astype(BF16)


def _attn_ctx(sink, qkv):
    rows = CTX_SEQ_PER_STEP * SEQ
    return pl.pallas_call(
        _attn_ctx_kernel,
        grid=(N_PROMPT // rows,),
        in_specs=[
            pl.BlockSpec(memory_space=pltpu.SMEM),
            pl.BlockSpec((N_ATT_TILES, rows, TN), lambda b: (0, b, 0)),
        ],
        out_specs=pl.BlockSpec((rows, D_MODEL), lambda b: (b, 0)),
        out_shape=jax.ShapeDtypeStruct((N_PROMPT, D_MODEL), BF16),
        compiler_params=_params("arbitrary"),
        name="attn_ctx",
    )(sink, qkv)


NA_Q_ROWS = 4
NA_K_ROWS = 12
NA_Q_CHUNK = NA_Q_ROWS * GRID_W
NA_K_SPAN = NA_K_ROWS * GRID_W
NA_K_ROW0 = (0, 0, 4, 4)
N_RPB_ROWS = 2 * NA_ROWS - 1
N_RPB_COLS = 2 * NA_COLS - 1


def _na_row_start(r):
    return min(max(r - NA_ROWS // 2, 0), GRID_R - NA_ROWS)


for _chunk, _k0 in enumerate(NA_K_ROW0):
    for _r in range(_chunk * NA_Q_ROWS, (_chunk + 1) * NA_Q_ROWS):
        assert _k0 % 2 == 0 and _k0 <= _na_row_start(_r)
        assert _na_row_start(_r) + NA_ROWS <= _k0 + NA_K_ROWS <= GRID_R


def _na_build_bias(rpb_ref, tile_ref, bias_ref):
    shape = (GRID_W, 2 * GRID_W)
    qc = lax.broadcasted_iota(jnp.int32, shape, 0)
    lane = lax.broadcasted_iota(jnp.int32, shape, 1)
    kc = lane % GRID_W
    start_c = jnp.clip(qc - NA_COLS // 2, 0, GRID_W - NA_COLS)
    in_win = (kc >= start_c) & (kc < start_c + NA_COLS)
    for dr in range(N_RPB_ROWS):
        rows = jnp.broadcast_to(rpb_ref[dr:dr + 1, :], shape)
        shifted = pltpu.roll(rows, 2 * GRID_W - (NA_COLS - 1), 1, stride=1, stride_axis=0)
        tile_ref[dr] = jnp.where(in_win, shifted * LOG2E, NEG)
    first_half = lane < GRID_W
    neg = jnp.full(shape, NEG, F32)
    for chunk in range(GRID_R // NA_Q_ROWS):
        for qi in range(NA_Q_ROWS):
            qr = chunk * NA_Q_ROWS + qi
            lo = _na_row_start(qr)
            for m in range(NA_K_ROWS // 2):
                kr = NA_K_ROW0[chunk] + 2 * m
                parts = []
                for r in (kr, kr + 1):
                    parts.append(tile_ref[r - qr + NA_ROWS - 1] if lo <= r < lo + NA_ROWS else neg)
                bias_ref[chunk, qi * GRID_W:(qi + 1) * GRID_W, m * 2 * GRID_W:(m + 1) * 2 * GRID_W] = (
                    jnp.where(first_half, parts[0], parts[1]))


def _attn_na_kernel(rpb_ref, q_ref, k_ref, v_ref, kc_ref, vc_ref, o_ref, tile_ref, bias_ref):
    _na_build_bias(rpb_ref, tile_ref, bias_ref)
    for b in range(DEC_BATCH):
        kc = kc_ref[b].astype(BF16)
        vc = _with_ones(vc_ref[b].astype(BF16))
        for c in range(DEC_SEQ // NA_Q_CHUNK):
            r0 = b * DEC_SEQ + c * NA_Q_CHUNK
            k0 = b * DEC_SEQ + NA_K_ROW0[c] * GRID_W
            rows = slice(r0, r0 + NA_Q_CHUNK)
            keys = slice(k0, k0 + NA_K_SPAN)
            q = q_ref[rows, :]
            s = _qkt(q, k_ref[keys, :]) + bias_ref[c]
            sc = _qkt(q, kc)
            m = jnp.maximum(jnp.max(s, axis=-1, keepdims=True), jnp.max(sc, axis=-1, keepdims=True))
            num, den = _pv(jnp.exp2(s - m).astype(BF16), _with_ones(v_ref[keys, :]))
            num_c, den_c = _pv(jnp.exp2(sc - m).astype(BF16), vc)
            o_ref[rows, :] = ((num + num_c) / (den + den_c)).astype(BF16)


def _attn_na(qkv, rpb, cache_k, cache_v, layer):
    heads_per_tile = TN // HEAD_DIM

    def head_spec(c0):
        def index(h):
            head = c0 // HEAD_DIM + h
            return head // heads_per_tile, 0, head % heads_per_tile
        return pl.BlockSpec((None, N_SAMPLE, HEAD_DIM), index)

    ctx_spec = pl.BlockSpec((DEC_BATCH, None, PAST_LEN, HEAD_DIM), lambda h: (0, layer, 0, h))
    n_chunks = DEC_SEQ // NA_Q_CHUNK
    pad_rows = -N_RPB_ROWS % 8
    half = jnp.pad(rpb.astype(F32), ((0, 0), (0, pad_rows), (0, GRID_W - N_RPB_COLS)), constant_values=NEG)
    rpb_rows = jnp.concatenate([half, half], axis=-1)
    return pl.pallas_call(
        _attn_na_kernel,
        grid=(N_HEADS_A,),
        in_specs=[
            pl.BlockSpec((None, rpb_rows.shape[1], 2 * GRID_W), lambda h: (h, 0, 0)),
            head_spec(COL_QA),
            head_spec(COL_KA),
            head_spec(COL_VA),
            ctx_spec,
            ctx_spec,
        ],
        out_specs=pl.BlockSpec((N_SAMPLE, HEAD_DIM), lambda h: (0, h)),
        out_shape=jax.ShapeDtypeStruct((N_SAMPLE, QA_W), BF16),
        scratch_shapes=[pltpu.VMEM((N_RPB_ROWS, GRID_W, 2 * GRID_W), F32),
                        pltpu.VMEM((n_chunks, NA_Q_CHUNK, NA_K_SPAN), F32)],
        compiler_params=_params("arbitrary"),
        name="attn_na",
    )(rpb_rows, qkv, qkv, qkv, cache_k, cache_v)


WIN_Q_CHUNK = 128
WIN_K_SPAN = WIN_Q_CHUNK + 2 * WIN_B


def _attn_win_kernel(sink_ref, q_ref, k_ref, v_ref, kc_ref, vc_ref, o_ref):
    j = pl.program_id(1)
    kc = kc_ref[...].astype(BF16)
    vc = _with_ones(vc_ref[...].astype(BF16))
    n_rows = G_B * WIN_Q_CHUNK
    grp = lax.broadcasted_iota(jnp.int32, (n_rows, 1), 0) // WIN_Q_CHUNK
    sink = jnp.zeros((n_rows, 1), F32)
    for g in range(G_B):
        sink = jnp.where(grp == g, sink_ref[j, g] * LOG2E, sink)
    for c in range(DEC_SEQ // WIN_Q_CHUNK):
        q0 = c * WIN_Q_CHUNK
        k0 = min(max(q0 - WIN_B, 0), DEC_SEQ - WIN_K_SPAN)
        rows = slice(q0, q0 + WIN_Q_CHUNK)
        keys = slice(k0, k0 + WIN_K_SPAN)
        q = jnp.concatenate([q_ref[rows, g * HEAD_DIM:(g + 1) * HEAD_DIM] for g in range(G_B)], axis=0)
        qpos = q0 + lax.broadcasted_iota(jnp.int32, (n_rows, WIN_K_SPAN), 0) % WIN_Q_CHUNK
        kpos = k0 + lax.broadcasted_iota(jnp.int32, (n_rows, WIN_K_SPAN), 1)
        s = jnp.where(jnp.abs(qpos - kpos) <= WIN_B, _qkt(q, k_ref[keys, :]), NEG)
        sc = _qkt(q, kc)
        m = jnp.maximum(jnp.maximum(jnp.max(s, axis=-1, keepdims=True),
                                    jnp.max(sc, axis=-1, keepdims=True)), sink)
        num, den = _pv(jnp.exp2(s - m).astype(BF16), _with_ones(v_ref[keys, :]))
        num_c, den_c = _pv(jnp.exp2(sc - m).astype(BF16), vc)
        o = (num + num_c) / (den + den_c + jnp.exp2(sink - m))
        for g in range(G_B):
            o_ref[rows, g * HEAD_DIM:(g + 1) * HEAD_DIM] = (
                o[g * WIN_Q_CHUNK:(g + 1) * WIN_Q_CHUNK, :].astype(BF16))


def _attn_win(sink, qkv, cache_k, cache_v, layer):
    gw = G_B * HEAD_DIM
    ctx_spec = pl.BlockSpec((None, None, PAST_LEN, HEAD_DIM), lambda b, j: (b, layer, 0, j))
    return pl.pallas_call(
        _attn_win_kernel,
        grid=(DEC_BATCH, N_KV_B),
        in_specs=[
            pl.BlockSpec(memory_space=pltpu.SMEM),
            pl.BlockSpec((None, DEC_SEQ, gw), lambda b, j: (ATT_TILE_QB + j, b, 0)),
            pl.BlockSpec((None, DEC_SEQ, HEAD_DIM), lambda b, j: (ATT_TILE_KVB, b, j)),
            pl.BlockSpec((None, DEC_SEQ, HEAD_DIM), lambda b, j: (ATT_TILE_KVB, b, N_KV_B + j)),
            ctx_spec,
            ctx_spec,
        ],
        out_specs=pl.BlockSpec((DEC_SEQ, gw), lambda b, j: (b, j)),
        out_shape=jax.ShapeDtypeStruct((N_SAMPLE, QB_W), BF16),
        compiler_params=_params("arbitrary", "arbitrary"),
        name="attn_win",
    )(sink, qkv, qkv, qkv, cache_k, cache_v)


REC_CB = 512
REC_SLABS = REC_CB // RG_BLOCK
REC_STREAMS = 8
REC_T = 256
REC_GROUP_ROWS = REC_STREAMS * REC_T
REC_PRE = CONV_PAD_L
REC_POST = CONV_W - 1 - CONV_PAD_L
REC_CHUNK_ROWS = 32 * REC_STREAMS


def _softplus(x):
    return jnp.maximum(x, 0.0) + jnp.log1p(jnp.exp(-jnp.abs(x)))


def _gelu_tanh(x):
    k = np.sqrt(2.0 / np.pi)
    half = 0.5 * x
    return half + half * jnp.tanh(x * (k + (k * 0.044715) * (x * x)))


def _sqrt_nonneg(u):
    return jnp.where(u > 0.0, u * lax.rsqrt(u), 0.0)


def _rec_kernel(x_ref, g_ref, cw_ref, cb_ref, wg_ref, bg_ref, lam_ref, h0_ref, y_ref, st_ref,
                xt_ref, af_ref, bf_ref, ab_ref, bb_ref, *, n_seg):
    S, T = REC_STREAMS, REC_T
    a_refs = (af_ref, ab_ref)
    b_refs = (bf_ref, bb_ref)
    seg = lax.broadcasted_iota(jnp.int32, (S, RG_BLOCK), 0) % n_seg
    slab_cols = [slice(n * RG_BLOCK, (n + 1) * RG_BLOCK) for n in range(REC_SLABS)]

    def t_rows(t):
        return pl.ds(pl.multiple_of(t * S, S), S)

    def from_prev_stream(x):
        return pltpu.roll(x, 1, 0)

    def from_next_stream(x):
        return pltpu.roll(x, S - 1, 0)

    for n, cols in enumerate(slab_cols):
        for s in range(S):
            xt_ref[n, pl.ds(REC_PRE * S + s, T, stride=S), :] = x_ref[s * T:(s + 1) * T, cols]
        for p in range(REC_PRE):
            src = xt_ref[n, (T + p) * S:(T + p + 1) * S, :]
            xt_ref[n, p * S:(p + 1) * S, :] = jnp.where(seg > 0, from_prev_stream(src), 0.0)
        for p in range(REC_POST):
            src = xt_ref[n, (REC_PRE + p) * S:(REC_PRE + p + 1) * S, :]
            xt_ref[n, (REC_PRE + T + p) * S:(REC_PRE + T + p + 1) * S, :] = (
                jnp.where(seg < n_seg - 1, from_next_stream(src), 0.0))

    c_all = (-0.5 * RG_C * np.log2(np.e)) * _softplus(-lam_ref[...])

    def gate_rows(r, carry):
        r0 = pl.multiple_of(r * REC_CHUNK_ROWS, REC_CHUNK_ROWS)
        rows = pl.ds(r0, REC_CHUNK_ROWS)
        for n, cols in enumerate(slab_cols):
            xc = cb_ref[:, cols] + cw_ref[0:1, cols] * xt_ref[n, rows, :]
            for k in range(1, CONV_W):
                tap_rows = pl.ds(pl.multiple_of(r0 + k * S, S), REC_CHUNK_ROWS)
                xc = xc + cw_ref[k:k + 1, cols] * xt_ref[n, tap_rows, :]
            gates = jnp.dot(xc.astype(BF16), wg_ref[n].astype(BF16), preferred_element_type=F32)
            x_half = 0.5 * xc
            for d in range(2):
                ga_half = gates[:, (2 * d) * RG_BLOCK:(2 * d + 1) * RG_BLOCK] + bg_ref[2 * d:2 * d + 1, cols]
                gx_half = (gates[:, (2 * d + 1) * RG_BLOCK:(2 * d + 2) * RG_BLOCK]
                           + bg_ref[2 * d + 1:2 * d + 2, cols])
                c = c_all[d:d + 1, cols]
                a = jnp.exp2(c * jnp.tanh(ga_half) + c)
                a_refs[d][n, rows, :] = a
                b_refs[d][n, rows, :] = _sqrt_nonneg(1.0 - a * a) * ((1.0 + jnp.tanh(gx_half)) * x_half)
        return carry

    lax.fori_loop(0, T * S // REC_CHUNK_ROWS, gate_rows, 0)

    def scan_step(t, carry):
        hf, hb, pf, pb = carry
        rf, rb = t_rows(t), t_rows(T - 1 - t)
        hf_new, hb_new, pf_new, pb_new = [], [], [], []
        for n in range(REC_SLABS):
            a = af_ref[n, rf, :]
            h = a * hf[n] + bf_ref[n, rf, :]
            bf_ref[n, rf, :] = h
            hf_new.append(h)
            a2 = ab_ref[n, rb, :]
            h2 = a2 * hb[n] + bb_ref[n, rb, :]
            bb_ref[n, rb, :] = h2
            hb_new.append(h2)
            if n_seg > 1:
                p = a * pf[n]
                af_ref[n, rf, :] = p
                pf_new.append(p)
                p2 = a2 * pb[n]
                ab_ref[n, rb, :] = p2
                pb_new.append(p2)
        return tuple(hf_new), tuple(hb_new), tuple(pf_new), tuple(pb_new)

    ones = tuple(jnp.ones((S, RG_BLOCK), F32) for _ in range(REC_SLABS)) if n_seg > 1 else ()
    hf, hb, pf, pb = lax.fori_loop(
        0, T, scan_step,
        (tuple(h0_ref[0, :, cols] for cols in slab_cols), tuple(h0_ref[1, :, cols] for cols in slab_cols),
         ones, ones), unroll=4)

    if n_seg > 1:
        cin_f, cin_b = [], []
        for n in range(REC_SLABS):
            cf = jnp.zeros((S, RG_BLOCK), F32)
            for j in range(1, n_seg):
                cf = jnp.where(seg == j, from_prev_stream(hf[n] + pf[n] * cf), cf)
            cb_in = jnp.zeros((S, RG_BLOCK), F32)
            for j in range(n_seg - 2, -1, -1):
                cb_in = jnp.where(seg == j, from_next_stream(hb[n] + pb[n] * cb_in), cb_in)
            cin_f.append(cf)
            cin_b.append(cb_in)

        def carry_in_step(t, carry):
            rows = t_rows(t)
            for n in range(REC_SLABS):
                bf_ref[n, rows, :] += af_ref[n, rows, :] * cin_f[n]
                bb_ref[n, rows, :] += ab_ref[n, rows, :] * cin_b[n]
            return carry

        lax.fori_loop(0, T, carry_in_step, 0, unroll=4)
        hf = tuple(hf[n] + pf[n] * cin_f[n] for n in range(REC_SLABS))
        hb = tuple(hb[n] + pb[n] * cin_b[n] for n in range(REC_SLABS))

    for n, cols in enumerate(slab_cols):
        st_ref[0, :, cols] = hf[n]
        st_ref[1, :, cols] = hb[n]

    for n, cols in enumerate(slab_cols):
        for s in range(S):
            rows = slice(s * T, (s + 1) * T)
            picked = pl.ds(s, T, stride=S)
            h_sum = bf_ref[n, picked, :] + bb_ref[n, picked, :]
            y_ref[rows, cols] = (h_sum * _gelu_tanh(g_ref[rows, cols])).astype(BF16)


def _rec(xg, n_seg, cw, cb, wg, bg, lam, h0):
    n_rows, tn = xg.shape[1], xg.shape[2]
    n_grp = n_rows // REC_GROUP_ROWS
    nc = D_RNN // REC_CB
    per_tile = tn // REC_CB

    def branch_spec(first):
        return pl.BlockSpec((None, REC_GROUP_ROWS, REC_CB),
                            lambda s, c: ((first + c) // per_tile, s, (first + c) % per_tile))

    vec = lambda rows: pl.BlockSpec((rows, REC_CB), lambda s, c: (0, c))
    state_spec = pl.BlockSpec((2, REC_STREAMS, REC_CB), lambda s, c: (0, s, c))
    slab_scratch = lambda n_t: pltpu.VMEM((REC_SLABS, n_t * REC_STREAMS, RG_BLOCK), F32)
    return pl.pallas_call(
        functools.partial(_rec_kernel, n_seg=n_seg),
        grid=(n_grp, nc),
        in_specs=[
            branch_spec(0),
            branch_spec(nc),
            vec(CONV_W),
            vec(1),
            pl.BlockSpec((REC_SLABS, RG_BLOCK, 4 * RG_BLOCK), lambda s, c: (c, 0, 0)),
            vec(4),
            vec(2),
            state_spec,
        ],
        out_specs=[
            pl.BlockSpec((None, REC_GROUP_ROWS, REC_CB), lambda s, c: (c, s, 0)),
            state_spec,
        ],
        out_shape=[
            jax.ShapeDtypeStruct((nc, n_rows, REC_CB), BF16),
            jax.ShapeDtypeStruct((2, n_grp * REC_STREAMS, D_RNN), F32),
        ],
        scratch_shapes=[slab_scratch(REC_PRE + REC_T + REC_POST)] + [slab_scratch(REC_T)] * 4,
        compiler_params=_params("arbitrary", "arbitrary"),
        name="rec",
    )(xg, xg, cw, cb.reshape(1, D_RNN), wg, bg, lam, h0)


def kernel(x_prompt, x_sample, c, cache_a_k, cache_a_v, cache_b_k, cache_b_v, state_rg_fwd, state_rg_bwd, c_ctx, w_ada, b_ada, g_pre_mix, g_post_mix, g_pre_ffn, g_post_ffn, w_att_in, w_att_out, sink_b, rpb_a, w_rec_in, conv_w, conv_b, w_rg_a, b_rg_a, w_rg_x, b_rg_x, rg_lambda, w_rec_out, w_ff1, w_ff2):
    depth = w_ada.shape[0]
    yp = x_prompt.reshape(N_PROMPT, D_MODEL)
    ys = x_sample.reshape(N_SAMPLE, D_MODEL)

    cond8 = jnp.concatenate([c_ctx[None, :], c, jnp.zeros((MOD_ROWS - N_SEG, D_MODEL), F32)], axis=0)
    mod_all = _adaln(cond8, w_ada, b_ada)
    mod_all = mod_all[:, :N_SEG, :].reshape(depth, N_SEG, N_MOD, D_MODEL)
    mod_all = jnp.pad(mod_all, ((0, 0), (0, 0), (0, MOD_ROWS - N_MOD), (0, 0)))

    a_k, a_v, b_k, b_v, s_f, s_b = [], [], [], [], [], []
    for layer in range(depth):
        mod = mod_all[layer]
        li = layer // 2
        g_pre, g_post = g_pre_mix[layer], g_post_mix[layer]
        if layer % 2 == 0:
            qkv_p, ka, va, kb, vb = _proj_att_prompt(yp, g_pre, mod, w_att_in, li)
            qkv_s = _proj_att_sample(ys, g_pre, mod, w_att_in, li)
            a_k.append(ka.reshape(BATCH, SEQ, N_HEADS_A, HEAD_DIM))
            a_v.append(va.reshape(BATCH, SEQ, N_HEADS_A, HEAD_DIM))
            b_k.append(kb.reshape(BATCH, SEQ, N_KV_B, HEAD_DIM))
            b_v.append(vb.reshape(BATCH, SEQ, N_KV_B, HEAD_DIM))
            mix_p = [_attn_ctx(sink_b[li], qkv_p)]
            n_att = cache_a_k.shape[1]
            mix_s = [_attn_na(qkv_s, rpb_a[li],
                              cache_a_k.reshape(DEC_BATCH, n_att, PAST_LEN, QA_W),
                              cache_a_v.reshape(DEC_BATCH, n_att, PAST_LEN, QA_W), li),
                     _attn_win(sink_b[li], qkv_s,
                               cache_b_k.reshape(DEC_BATCH, n_att, PAST_LEN, KVB_W),
                               cache_b_v.reshape(DEC_BATCH, n_att, PAST_LEN, KVB_W), li)]
            w_out = w_att_out
        else:
            xg_p = _proj(PROMPT, yp, g_pre, mod, w_rec_in, li)
            xg_s = _proj(SAMPLE, ys, g_pre, mod, w_rec_in, li)
            wg = 0.5 * jnp.concatenate([w_rg_a[li, 0], w_rg_x[li, 0], w_rg_a[li, 1], w_rg_x[li, 1]], axis=-1)
            bg = 0.5 * jnp.stack([b_rg_a[li, 0], b_rg_x[li, 0], b_rg_a[li, 1], b_rg_x[li, 1]], axis=0)
            rec_args = (conv_w[li], conv_b[li], wg, bg, rg_lambda[li])
            n_seg = DEC_SEQ // REC_T
            assert SEQ == REC_T and BATCH % REC_STREAMS == 0 and DEC_BATCH * n_seg == REC_STREAMS
            h0_p = jnp.zeros((2, BATCH, D_RNN), F32)
            seg_state = jnp.zeros((DEC_BATCH, n_seg, D_RNN), F32)
            h0_s = jnp.stack([seg_state.at[:, 0].set(state_rg_fwd[:, li]).reshape(REC_STREAMS, D_RNN),
                              seg_state.at[:, n_seg - 1].set(state_rg_bwd[:, li]).reshape(REC_STREAMS, D_RNN)])
            rec_p, st_p = _rec(xg_p, 1, *rec_args, h0_p)
            rec_s, _ = _rec(xg_s, n_seg, *rec_args, h0_s)
            mix_p, mix_s = rec_p, rec_s
            s_f.append(st_p[0])
            s_b.append(st_p[1])
            w_out = w_rec_out
        yp = _mixout(PROMPT, mix_p, w_out, li, yp, g_post, mod)
        ys = _mixout(SAMPLE, mix_s, w_out, li, ys, g_post, mod)
        yp = _ffn(PROMPT, yp, g_pre_ffn[layer], g_post_ffn[layer], mod, w_ff1, w_ff2, layer)
        ys = _ffn(SAMPLE, ys, g_pre_ffn[layer], g_post_ffn[layer], mod, w_ff1, w_ff2, layer)

    return (yp.reshape(BATCH, SEQ, D_MODEL), ys.reshape(DEC_BATCH, DEC_SEQ, D_MODEL),
            jnp.stack(a_k, axis=1), jnp.stack(a_v, axis=1), jnp.stack(b_k, axis=1), jnp.stack(b_v, axis=1),
            jnp.stack(s_f, axis=1), jnp.stack(s_b, axis=1))
```

```python
import functools

import jax
import jax.numpy as jnp
import numpy as np
from jax import lax
from jax.experimental import pallas as pl
from jax.experimental.pallas import tpu as pltpu

D_MODEL = 2048
BATCH = 16
SEQ = 256
DEC_BATCH = 2
DEC_SEQ = 1024
PAST_LEN = 256
GRID_W = 64
GRID_R = DEC_SEQ // GRID_W
HEAD_DIM = 128
N_HEADS_A = 8
N_HEADS_B = 8
N_KV_B = 2
G_B = N_HEADS_B // N_KV_B
NA_ROWS = 8
NA_COLS = 16
WIN_B = 128
D_RNN = D_MODEL
N_RG_BLOCKS = 16
RG_BLOCK = D_RNN // N_RG_BLOCKS
CONV_W = 4
CONV_PAD_L = 2
RG_C = 8.0
D_FF = 4 * D_MODEL
ROPE_BASE = 10000.0
EPS = 1e-6
NEG = -1e30
QA_W = N_HEADS_A * HEAD_DIM
QB_W = N_HEADS_B * HEAD_DIM
KVB_W = N_KV_B * HEAD_DIM
D_ATT_IN = 3 * QA_W + QB_W + 2 * KVB_W
SCALE = HEAD_DIM ** -0.5

N_PROMPT = BATCH * SEQ
N_SAMPLE = DEC_BATCH * DEC_SEQ
N_SEG = 1 + DEC_BATCH
N_MOD = 6
MOD_ROWS = 8

COL_QA, COL_KA, COL_VA = 0, QA_W, 2 * QA_W
COL_QB = 3 * QA_W
COL_KB = COL_QB + QB_W
COL_VB = COL_KB + KVB_W

V7X_VMEM_BYTES = 64 * 1024 * 1024
VMEM_LIMIT = V7X_VMEM_BYTES - 4 * 1024 * 1024

TM = 1024
TN = 512
LANES = 128
ROW_CHUNK = 16
ROW_GROUP = 16

F32 = jnp.float32
BF16 = jnp.bfloat16


def _params(*sem):
    return pltpu.CompilerParams(dimension_semantics=sem, vmem_limit_bytes=VMEM_LIMIT)


class _Rows:
    def __init__(self, n_rows, seg0, seg_rows):
        self.n = n_rows
        self.seg0 = seg0
        self.seg_rows = seg_rows

    def seg(self, i, tm):
        return self.seg0 + (i * tm) // self.seg_rows


PROMPT = _Rows(N_PROMPT, 0, N_PROMPT)
SAMPLE = _Rows(N_SAMPLE, 1, DEC_SEQ)


def _mod_spec(rows, tm):
    return pl.BlockSpec((None, MOD_ROWS, D_MODEL), lambda i, j: (rows.seg(i, tm), 0, 0))


def _vec_spec():
    return pl.BlockSpec((1, D_MODEL), lambda i, j: (0, 0))


def _rms_scale(x):
    return lax.rsqrt(jnp.mean(x * x, axis=-1, keepdims=True) + EPS)


def _norm_mod_rows(y_ref, g_ref, mod_ref, h_ref, shift_row, tm, h_row0=0):
    shift = mod_ref[shift_row:shift_row + 1, :]
    gain = g_ref[...] * (1.0 + mod_ref[shift_row + 1:shift_row + 2, :])

    def body(r, carry):
        starts = [r * (ROW_GROUP * ROW_CHUNK) + u * ROW_CHUNK for u in range(ROW_GROUP)]
        scales = [_rms_scale(y_ref[pl.ds(pl.multiple_of(r0, ROW_CHUNK), ROW_CHUNK), :]) for r0 in starts]
        for r0, rs in zip(starts, scales):
            y = y_ref[pl.ds(pl.multiple_of(r0, ROW_CHUNK), ROW_CHUNK), :]
            h = (y * rs) * gain + shift
            h_ref[pl.ds(pl.multiple_of(h_row0 + r0, ROW_CHUNK), ROW_CHUNK), :] = h.astype(BF16)
        return carry

    lax.fori_loop(0, tm // (ROW_GROUP * ROW_CHUNK), body, 0)


def _store_with_square_sums(o_ref, sq_ref, cols, x, first):
    o_ref[:, cols] = x
    sq = x * x
    folded = sq[:, :LANES]
    for k in range(1, sq.shape[1] // LANES):
        folded = folded + sq[:, k * LANES:(k + 1) * LANES]
    if first:
        sq_ref[...] = folded
    else:
        sq_ref[...] += folded


def _gated_residual_rows(y_ref, o_ref, sq_ref, g_ref, gate, tm):
    gain = gate * g_ref[...]
    scale = lax.rsqrt(jnp.sum(sq_ref[...], axis=-1, keepdims=True) / D_MODEL + EPS)
    sq_ref[...] = jnp.broadcast_to(scale, sq_ref.shape)

    def body(r, carry):
        for u in range(ROW_GROUP):
            rows = pl.ds(pl.multiple_of(r * (ROW_GROUP * ROW_CHUNK) + u * ROW_CHUNK, ROW_CHUNK), ROW_CHUNK)
            rs = sq_ref[rows, :]
            for k in range(D_MODEL // LANES):
                c = slice(k * LANES, (k + 1) * LANES)
                o_ref[rows, c] = y_ref[rows, c] + (o_ref[rows, c] * rs) * gain[:, c]
        return carry

    lax.fori_loop(0, tm // (ROW_GROUP * ROW_CHUNK), body, 0)


def _adaln_kernel(cond_ref, w_ref, b_ref, o_ref):
    c = cond_ref[...]
    s = c / (1.0 + jnp.exp(-c))
    o_ref[...] = jnp.dot(s.astype(BF16), w_ref[...].astype(BF16),
                         preferred_element_type=F32) + b_ref[...]


def _adaln(cond8, w_ada, b_ada):
    depth = w_ada.shape[0]
    n = w_ada.shape[2]
    tn = 1024
    return pl.pallas_call(
        _adaln_kernel,
        grid=(depth, n // tn),
        in_specs=[
            pl.BlockSpec((MOD_ROWS, D_MODEL), lambda l, j: (0, 0)),
            pl.BlockSpec((None, D_MODEL, tn), lambda l, j: (l, 0, j)),
            pl.BlockSpec((None, 1, tn), lambda l, j: (l, 0, j)),
        ],
        out_specs=pl.BlockSpec((None, MOD_ROWS, tn), lambda l, j: (l, 0, j)),
        out_shape=jax.ShapeDtypeStruct((depth, MOD_ROWS, n), F32),
        compiler_params=_params("arbitrary", "arbitrary"),
        name="adaln",
    )(cond8, w_ada, b_ada.reshape(depth, 1, n))


def _rope_tables():
    t = np.arange(DEC_SEQ)
    half = HEAD_DIM // 2
    inv = ROPE_BASE ** (-np.arange(0, half, 2, dtype=np.float64) / half)
    ang_r = (t // GRID_W)[:, None] * inv[None, :]
    ang_c = (t % GRID_W)[:, None] * inv[None, :]
    cos = np.concatenate([np.cos(ang_r), np.cos(ang_r), np.cos(ang_c), np.cos(ang_c)], axis=1)
    sin = np.concatenate([-np.sin(ang_r), np.sin(ang_r), -np.sin(ang_c), np.sin(ang_c)], axis=1)
    return jnp.asarray(cos, F32), jnp.asarray(sin, F32)


def _rope(x, cos, sin_signed):
    quarter = HEAD_DIM // 4
    lane = lax.broadcasted_iota(jnp.int32, x.shape, 1)
    first = (lane % (2 * quarter)) < quarter
    partner = jnp.where(first, pltpu.roll(x, HEAD_DIM - quarter, 1), pltpu.roll(x, quarter, 1))
    return x * cos + partner * sin_signed


def _proj_h_tile(y_ref, g_ref, mod_ref, h_ref, tm):
    row0 = pl.multiple_of(pl.program_id(1) * tm, tm)

    @pl.when(pl.program_id(0) == 0)
    def _():
        _norm_mod_rows(y_ref, g_ref, mod_ref, h_ref, 0, tm, row0)

    return h_ref[pl.ds(row0, tm), :]


def _proj_in_specs(rows, tm, layer, tn=TN):
    last = rows.n // tm - 1

    def tile(j, i):
        return jnp.where(j == 0, i, last)

    return [
        pl.BlockSpec((tm, D_MODEL), lambda j, i: (tile(j, i), 0)),
        pl.BlockSpec((1, D_MODEL), lambda j, i: (0, 0)),
        pl.BlockSpec((None, MOD_ROWS, D_MODEL), lambda j, i: (rows.seg(tile(j, i), tm), 0, 0)),
        pl.BlockSpec((None, D_MODEL, tn), lambda j, i: (layer, 0, j)),
    ]


def _proj_tn(rows):
    return TN if rows.n > N_SAMPLE else 2 * TN


def _proj_kernel(y_ref, g_ref, mod_ref, w_ref, o_ref, h_ref, *, tm):
    h = _proj_h_tile(y_ref, g_ref, mod_ref, h_ref, tm)
    o_ref[...] = jnp.dot(h, w_ref[...].astype(BF16), preferred_element_type=F32)


def _proj(rows, y, g, mod, w, layer):
    n = w.shape[2]
    tm, tn = TM, _proj_tn(rows)
    return pl.pallas_call(
        functools.partial(_proj_kernel, tm=tm),
        grid=(n // tn, rows.n // tm),
        in_specs=_proj_in_specs(rows, tm, layer, tn),
        out_specs=pl.BlockSpec((None, tm, tn), lambda j, i: (j, i, 0)),
        out_shape=jax.ShapeDtypeStruct((n // tn, rows.n, tn), F32),
        scratch_shapes=[pltpu.VMEM((rows.n, D_MODEL), BF16)],
        compiler_params=_params("arbitrary", "arbitrary"),
        name="proj",
    )(y, g.reshape(1, D_MODEL), mod, w)


ATT_TILE_KA = COL_KA // TN
ATT_TILE_VA = COL_VA // TN
ATT_TILE_QB = COL_QB // TN
ATT_TILE_KVB = COL_KB // TN
N_ATT_TILES = D_ATT_IN // TN

LOG2E = float(np.log2(np.e))
Q_FACTOR = SCALE * LOG2E


def _q_factor(j):
    is_q = (j < ATT_TILE_KA) | ((j >= ATT_TILE_QB) & (j < ATT_TILE_KVB))
    return jnp.where(is_q, Q_FACTOR, 1.0).astype(F32)


def _proj_att_prompt_kernel(y_ref, g_ref, mod_ref, w_ref, o_ref, ka_ref, va_ref, kb_ref, vb_ref, h_ref, *, tm):
    j = pl.program_id(0)
    h = _proj_h_tile(y_ref, g_ref, mod_ref, h_ref, tm)
    acc = jnp.dot(h, w_ref[...].astype(BF16), preferred_element_type=F32)
    o_ref[...] = (acc * _q_factor(j)).astype(BF16)

    @pl.when((j >= ATT_TILE_KA) & (j < ATT_TILE_VA))
    def _():
        ka_ref[...] = acc

    @pl.when((j >= ATT_TILE_VA) & (j < ATT_TILE_QB))
    def _():
        va_ref[...] = acc

    @pl.when(j == ATT_TILE_KVB)
    def _():
        kb_ref[...] = acc[:, :KVB_W]
        vb_ref[...] = acc[:, KVB_W:]


def _proj_att_prompt(y, g, mod, w, layer):
    tm = TM
    rows = PROMPT
    last = rows.n // tm - 1

    def kv_map(first, count):
        def index(j, i):
            row = jnp.where(j < first, 0, jnp.where(j >= first + count, last, i))
            return row, jnp.clip(j - first, 0, count - 1)
        return index

    per_head_set = QA_W // TN
    return pl.pallas_call(
        functools.partial(_proj_att_prompt_kernel, tm=tm),
        grid=(N_ATT_TILES, rows.n // tm),
        in_specs=_proj_in_specs(rows, tm, layer),
        out_specs=[
            pl.BlockSpec((None, tm, TN), lambda j, i: (j, i, 0)),
            pl.BlockSpec((tm, TN), kv_map(ATT_TILE_KA, per_head_set)),
            pl.BlockSpec((tm, TN), kv_map(ATT_TILE_VA, per_head_set)),
            pl.BlockSpec((tm, KVB_W), kv_map(ATT_TILE_KVB, 1)),
            pl.BlockSpec((tm, KVB_W), kv_map(ATT_TILE_KVB, 1)),
        ],
        out_shape=[
            jax.ShapeDtypeStruct((N_ATT_TILES, rows.n, TN), BF16),
            jax.ShapeDtypeStruct((rows.n, QA_W), F32),
            jax.ShapeDtypeStruct((rows.n, QA_W), F32),
            jax.ShapeDtypeStruct((rows.n, KVB_W), F32),
            jax.ShapeDtypeStruct((rows.n, KVB_W), F32),
        ],
        scratch_shapes=[pltpu.VMEM((rows.n, D_MODEL), BF16)],
        compiler_params=_params("arbitrary", "arbitrary"),
        name="proj_att_prompt",
    )(y, g.reshape(1, D_MODEL), mod, w)


def _proj_att_sample_kernel(y_ref, g_ref, mod_ref, w_ref, cos_ref, sin_ref, o_ref, h_ref, *, tm):
    j = pl.program_id(0)
    h = _proj_h_tile(y_ref, g_ref, mod_ref, h_ref, tm)
    acc = jnp.dot(h, w_ref[...].astype(BF16), preferred_element_type=F32)

    def store(n_rope_heads):
        cos = cos_ref[...]
        sin = sin_ref[...]
        for hd in range(TN // HEAD_DIM):
            cols = slice(hd * HEAD_DIM, (hd + 1) * HEAD_DIM)
            x = acc[:, cols]
            if hd < n_rope_heads:
                x = _rope(x, cos, sin)
            o_ref[:, cols] = (x * q_factor).astype(BF16)

    q_factor = _q_factor(j)

    @pl.when(j < ATT_TILE_QB)
    def _():
        o_ref[...] = (acc * q_factor).astype(BF16)

    @pl.when((j >= ATT_TILE_QB) & (j < ATT_TILE_KVB))
    def _():
        store(TN // HEAD_DIM)

    @pl.when(j == ATT_TILE_KVB)
    def _():
        store(N_KV_B)


def _proj_att_sample(y, g, mod, w, layer):
    tm = TM
    assert tm == DEC_SEQ
    rows = SAMPLE
    cos, sin = _rope_tables()
    tab_spec = pl.BlockSpec((DEC_SEQ, HEAD_DIM), lambda j, i: (0, 0))
    return pl.pallas_call(
        functools.partial(_proj_att_sample_kernel, tm=tm),
        grid=(N_ATT_TILES, rows.n // tm),
        in_specs=_proj_in_specs(rows, tm, layer) + [tab_spec, tab_spec],
        out_specs=pl.BlockSpec((None, tm, TN), lambda j, i: (j, i, 0)),
        out_shape=jax.ShapeDtypeStruct((N_ATT_TILES, rows.n, TN), BF16),
        scratch_shapes=[pltpu.VMEM((rows.n, D_MODEL), BF16)],
        compiler_params=_params("arbitrary", "arbitrary"),
        name="proj_att_sample",
    )(y, g.reshape(1, D_MODEL), mod, w, cos, sin)


MIXOUT_TM = 512


def _mixout_kernel(*refs, tm, n_parts):
    a_refs = refs[:n_parts]
    w_ref, y_ref, g_ref, mod_ref, o_ref, sq_ref = refs[n_parts:]
    kp = a_refs[0].shape[1]
    for c in range(D_MODEL // TN):
        cols = slice(c * TN, (c + 1) * TN)
        acc = None
        for p in range(n_parts):
            part = jnp.dot(a_refs[p][...], w_ref[p * kp:(p + 1) * kp, cols].astype(BF16),
                           preferred_element_type=F32)
            acc = part if acc is None else acc + part
        _store_with_square_sums(o_ref, sq_ref, cols, acc, first=(c == 0))
    _gated_residual_rows(y_ref, o_ref, sq_ref, g_ref, mod_ref[2:3, :], tm)


def _mixout(rows, a_parts, w, layer, y, g, mod):
    tm = MIXOUT_TM
    if isinstance(a_parts, (list, tuple)):
        n_parts, kp = len(a_parts), a_parts[0].shape[1]
        part_specs = [pl.BlockSpec((tm, kp), lambda i: (i, 0)) for _ in range(n_parts)]
    else:
        n_parts, kp = a_parts.shape[0], a_parts.shape[2]
        part_specs = [pl.BlockSpec((None, tm, kp), lambda i, p=p: (p, i, 0)) for p in range(n_parts)]
        a_parts = [a_parts] * n_parts
    assert n_parts * kp == w.shape[1]
    return pl.pallas_call(
        functools.partial(_mixout_kernel, tm=tm, n_parts=n_parts),
        grid=(rows.n // tm,),
        in_specs=part_specs + [
            pl.BlockSpec((None, w.shape[1], D_MODEL), lambda i: (layer, 0, 0), pipeline_mode=pl.Buffered(1)),
            pl.BlockSpec((tm, D_MODEL), lambda i: (i, 0)),
            pl.BlockSpec((1, D_MODEL), lambda i: (0, 0)),
            pl.BlockSpec((None, MOD_ROWS, D_MODEL), lambda i: (rows.seg(i, tm), 0, 0)),
        ],
        out_specs=pl.BlockSpec((tm, D_MODEL), lambda i: (i, 0)),
        out_shape=jax.ShapeDtypeStruct((rows.n, D_MODEL), F32),
        scratch_shapes=[pltpu.VMEM((tm, LANES), F32)],
        compiler_params=_params("arbitrary"),
        name="mixout",
    )(*a_parts, w, y, g.reshape(1, D_MODEL), mod)


FFN_TF = 256
FFN_PAIR = 2 * FFN_TF
FFN_N_PAIRS = D_FF // FFN_PAIR
FFN_GROUP = 2
FFN_N_GROUPS = FFN_N_PAIRS // FFN_GROUP


def _ffn_kernel(y_ref, g1_ref, g2_ref, mod_ref, w1_hbm, w2_hbm, o_ref, h_ref, a_ref, sq_ref, w1_buf, w2_buf,
                sem, *, tm, layer, n_tiles):
    i = pl.program_id(0)
    g = pl.program_id(1)
    last_tile = n_tiles - 1

    def block(pair):
        start = pair * FFN_PAIR
        return pl.ds(start if isinstance(start, int) else pl.multiple_of(start, FFN_PAIR), FFN_PAIR)

    def w1_copy(pair, slot):
        return pltpu.make_async_copy(w1_hbm.at[layer, :, block(pair)], w1_buf.at[slot], sem.at[0, slot])

    def w2_copy(pair, slot):
        return pltpu.make_async_copy(w2_hbm.at[layer, block(pair), :], w2_buf.at[slot], sem.at[1, slot])

    def up(slot):
        for half in range(FFN_PAIR // FFN_TF):
            cols = slice(half * FFN_TF, (half + 1) * FFN_TF)
            a = jnp.dot(h_ref[...], w1_buf[slot, :, cols].astype(BF16), preferred_element_type=F32)
            a = jnp.maximum(a, 0.0)
            a_ref[slot, :, cols] = (a * a).astype(BF16)

    def down(a_slot, w_slot, first=False, last=False):
        a = a_ref[a_slot]
        for c in range(D_MODEL // TN):
            cols = slice(c * TN, (c + 1) * TN)
            part = jnp.dot(a, w2_buf[w_slot, :, cols].astype(BF16), preferred_element_type=F32)
            if first:
                o_ref[:, cols] = part
            elif last:
                _store_with_square_sums(o_ref, sq_ref, cols, o_ref[:, cols] + part, first=(c == 0))
            else:
                o_ref[:, cols] += part

    def pair_step(q, u):
        slot = u % 2
        w1_copy(q, slot).wait()
        w1_copy((q + 1) % FFN_N_PAIRS, 1 - slot).start()
        w2_copy(q, 1 - slot).start()
        return slot

    @pl.when(g == 0)
    def _():
        @pl.when(i == 0)
        def _():
            w1_copy(0, 0).start()

        _norm_mod_rows(y_ref, g1_ref, mod_ref, h_ref, 3, tm)
        for u in range(FFN_GROUP):
            slot = pair_step(u, u)
            if u > 0:
                w2_copy(u - 1, slot).wait()
            up(slot)
            if u > 0:
                down(1 - slot, slot, first=(u == 1))

    @pl.when(g > 0)
    def _():
        for u in range(FFN_GROUP):
            q = g * FFN_GROUP + u
            slot = pair_step(q, u)
            w2_copy(q - 1, slot).wait()
            up(slot)
            down(1 - slot, slot)

    @pl.when(g == FFN_N_GROUPS - 1)
    def _():
        w2_copy(FFN_N_PAIRS - 1, 0).wait()
        down((FFN_N_PAIRS - 1) % 2, 0, last=True)
        _gated_residual_rows(y_ref, o_ref, sq_ref, g2_ref, mod_ref[5:6, :], tm)

        @pl.when(i == last_tile)
        def _():
            w1_copy(0, 0).wait()


def _ffn(rows, y, g1, g2, mod, w1, w2, layer):
    tm = TM
    n_tiles = rows.n // tm
    assert FFN_GROUP % 2 == 0 and FFN_N_PAIRS % FFN_GROUP == 0
    return pl.pallas_call(
        functools.partial(_ffn_kernel, tm=tm, layer=layer, n_tiles=n_tiles),
        grid=(n_tiles, FFN_N_GROUPS),
        in_specs=[
            pl.BlockSpec((tm, D_MODEL), lambda i, s: (i, 0)),
            _vec_spec(),
            _vec_spec(),
            _mod_spec(rows, tm),
            pl.BlockSpec(memory_space=pl.ANY),
            pl.BlockSpec(memory_space=pl.ANY),
        ],
        out_specs=pl.BlockSpec((tm, D_MODEL), lambda i, s: (i, 0)),
        out_shape=jax.ShapeDtypeStruct((rows.n, D_MODEL), F32),
        scratch_shapes=[
            pltpu.VMEM((tm, D_MODEL), BF16),
            pltpu.VMEM((2, tm, FFN_PAIR), BF16),
            pltpu.VMEM((tm, LANES), F32),
            pltpu.VMEM((2, D_MODEL, FFN_PAIR), F32),
            pltpu.VMEM((2, FFN_PAIR, D_MODEL), F32),
            pltpu.SemaphoreType.DMA((2, 2)),
        ],
        compiler_params=_params("arbitrary", "arbitrary"),
        name="ffn",
    )(y, g1.reshape(1, D_MODEL), g2.reshape(1, D_MODEL), mod, w1, w2)


def _qkt(q, k):
    return lax.dot_general(q, k, (((1,), (1,)), ((), ())), preferred_element_type=F32)


def _with_ones(v):
    return jnp.concatenate([v, jnp.ones_like(v)], axis=1)


def _pv(p, v_ext):
    o = jnp.dot(p, v_ext, preferred_element_type=F32)
    return o[:, :HEAD_DIM], o[:, HEAD_DIM:]


CTX_SEQ_PER_STEP = 2


def _attn_ctx_kernel(sink_ref, qkv_ref, o_ref):
    n_rows = G_B * SEQ
    grp = lax.broadcasted_iota(jnp.int32, (n_rows, 1), 0) // SEQ
    for b in range(CTX_SEQ_PER_STEP):
        rows = slice(b * SEQ, (b + 1) * SEQ)

        def head(col):
            tile, off = divmod(col, TN)
            return qkv_ref[tile, rows, off:off + HEAD_DIM]

        for h in range(N_HEADS_A):
            q = head(COL_QA + h * HEAD_DIM)
            k = head(COL_KA + h * HEAD_DIM)
            v = head(COL_VA + h * HEAD_DIM)
            s = _qkt(q, k)
            m = jnp.max(s, axis=-1, keepdims=True)
            num, den = _pv(jnp.exp2(s - m).astype(BF16), _with_ones(v))
            o_ref[rows, h * HEAD_DIM:(h + 1) * HEAD_DIM] = (num / den).astype(BF16)
        for j in range(N_KV_B):
            k = head(COL_KB + j * HEAD_DIM)
            v = head(COL_VB + j * HEAD_DIM)
            q = jnp.concatenate([head(COL_QB + (j * G_B + g) * HEAD_DIM) for g in range(G_B)], axis=0)
            sink = jnp.zeros((n_rows, 1), F32)
            for g in range(G_B):
                sink = jnp.where(grp == g, sink_ref[j, g] * LOG2E, sink)
            s = _qkt(q, k)
            m = jnp.maximum(jnp.max(s, axis=-1, keepdims=True), sink)
            num, den = _pv(jnp.exp2(s - m).astype(BF16), _with_ones(v))
            o = num / (den + jnp.exp2(sink - m))
            for g in range(G_B):
                c0 = QA_W + (j * G_B + g) * HEAD_DIM
                o_ref[rows, c0:c0 + HEAD_DIM] = o[g * SEQ:(g + 1) * SEQ, :].astype(BF16)


def _attn_ctx(sink, qkv):
    rows = CTX_SEQ_PER_STEP * SEQ
    return pl.pallas_call(
        _attn_ctx_kernel,
        grid=(N_PROMPT // rows,),
        in_specs=[
            pl.BlockSpec(memory_space=pltpu.SMEM),
            pl.BlockSpec((N_ATT_TILES, rows, TN), lambda b: (0, b, 0)),
        ],
        out_specs=pl.BlockSpec((rows, D_MODEL), lambda b: (b, 0)),
        out_shape=jax.ShapeDtypeStruct((N_PROMPT, D_MODEL), BF16),
        compiler_params=_params("arbitrary"),
        name="attn_ctx",
    )(sink, qkv)


NA_Q_ROWS = 4
NA_K_ROWS = 12
NA_Q_CHUNK = NA_Q_ROWS * GRID_W
NA_K_SPAN = NA_K_ROWS * GRID_W
NA_K_ROW0 = (0, 0, 4, 4)
N_RPB_ROWS = 2 * NA_ROWS - 1
N_RPB_COLS = 2 * NA_COLS - 1


def _na_row_start(r):
    return min(max(r - NA_ROWS // 2, 0), GRID_R - NA_ROWS)


for _chunk, _k0 in enumerate(NA_K_ROW0):
    for _r in range(_chunk * NA_Q_ROWS, (_chunk + 1) * NA_Q_ROWS):
        assert _k0 % 2 == 0 and _k0 <= _na_row_start(_r)
        assert _na_row_start(_r) + NA_ROWS <= _k0 + NA_K_ROWS <= GRID_R


def _na_build_bias(rpb_ref, tile_ref, bias_ref):
    shape = (GRID_W, 2 * GRID_W)
    qc = lax.broadcasted_iota(jnp.int32, shape, 0)
    lane = lax.broadcasted_iota(jnp.int32, shape, 1)
    kc = lane % GRID_W
    start_c = jnp.clip(qc - NA_COLS // 2, 0, GRID_W - NA_COLS)
    in_win = (kc >= start_c) & (kc < start_c + NA_COLS)
    for dr in range(N_RPB_ROWS):
        rows = jnp.broadcast_to(rpb_ref[dr:dr + 1, :], shape)
        shifted = pltpu.roll(rows, 2 * GRID_W - (NA_COLS - 1), 1, stride=1, stride_axis=0)
        tile_ref[dr] = jnp.where(in_win, shifted * LOG2E, NEG)
    first_half = lane < GRID_W
    neg = jnp.full(shape, NEG, F32)
    for chunk in range(GRID_R // NA_Q_ROWS):
        for qi in range(NA_Q_ROWS):
            qr = chunk * NA_Q_ROWS + qi
            lo = _na_row_start(qr)
            for m in range(NA_K_ROWS // 2):
                kr = NA_K_ROW0[chunk] + 2 * m
                parts = []
                for r in (kr, kr + 1):
                    parts.append(tile_ref[r - qr + NA_ROWS - 1] if lo <= r < lo + NA_ROWS else neg)
                bias_ref[chunk, qi * GRID_W:(qi + 1) * GRID_W, m * 2 * GRID_W:(m + 1) * 2 * GRID_W] = (
                    jnp.where(first_half, parts[0], parts[1]))


def _attn_na_kernel(rpb_ref, q_ref, k_ref, v_ref, kc_ref, vc_ref, o_ref, tile_ref, bias_ref):
    _na_build_bias(rpb_ref, tile_ref, bias_ref)
    for b in range(DEC_BATCH):
        kc = kc_ref[b].astype(BF16)
        vc = _with_ones(vc_ref[b].astype(BF16))
        for c in range(DEC_SEQ // NA_Q_CHUNK):
            r0 = b * DEC_SEQ + c * NA_Q_CHUNK
            k0 = b * DEC_SEQ + NA_K_ROW0[c] * GRID_W
            rows = slice(r0, r0 + NA_Q_CHUNK)
            keys = slice(k0, k0 + NA_K_SPAN)
            q = q_ref[rows, :]
            s = _qkt(q, k_ref[keys, :]) + bias_ref[c]
            sc = _qkt(q, kc)
            m = jnp.maximum(jnp.max(s, axis=-1, keepdims=True), jnp.max(sc, axis=-1, keepdims=True))
            num, den = _pv(jnp.exp2(s - m).astype(BF16), _with_ones(v_ref[keys, :]))
            num_c, den_c = _pv(jnp.exp2(sc - m).astype(BF16), vc)
            o_ref[rows, :] = ((num + num_c) / (den + den_c)).astype(BF16)


def _attn_na(qkv, rpb, cache_k, cache_v, layer):
    heads_per_tile = TN // HEAD_DIM

    def head_spec(c0):
        def index(h):
            head = c0 // HEAD_DIM + h
            return head // heads_per_tile, 0, head % heads_per_tile
        return pl.BlockSpec((None, N_SAMPLE, HEAD_DIM), index)

    ctx_spec = pl.BlockSpec((DEC_BATCH, None, PAST_LEN, HEAD_DIM), lambda h: (0, layer, 0, h))
    n_chunks = DEC_SEQ // NA_Q_CHUNK
    pad_rows = -N_RPB_ROWS % 8
    half = jnp.pad(rpb.astype(F32), ((0, 0), (0, pad_rows), (0, GRID_W - N_RPB_COLS)), constant_values=NEG)
    rpb_rows = jnp.concatenate([half, half], axis=-1)
    return pl.pallas_call(
        _attn_na_kernel,
        grid=(N_HEADS_A,),
        in_specs=[
            pl.BlockSpec((None, rpb_rows.shape[1], 2 * GRID_W), lambda h: (h, 0, 0)),
            head_spec(COL_QA),
            head_spec(COL_KA),
            head_spec(COL_VA),
            ctx_spec,
            ctx_spec,
        ],
        out_specs=pl.BlockSpec((N_SAMPLE, HEAD_DIM), lambda h: (0, h)),
        out_shape=jax.ShapeDtypeStruct((N_SAMPLE, QA_W), BF16),
        scratch_shapes=[pltpu.VMEM((N_RPB_ROWS, GRID_W, 2 * GRID_W), F32),
                        pltpu.VMEM((n_chunks, NA_Q_CHUNK, NA_K_SPAN), F32)],
        compiler_params=_params("arbitrary"),
        name="attn_na",
    )(rpb_rows, qkv, qkv, qkv, cache_k, cache_v)


WIN_Q_CHUNK = 128
WIN_K_SPAN = WIN_Q_CHUNK + 2 * WIN_B


def _attn_win_kernel(sink_ref, q_ref, k_ref, v_ref, kc_ref, vc_ref, o_ref):
    j = pl.program_id(1)
    kc = kc_ref[...].astype(BF16)
    vc = _with_ones(vc_ref[...].astype(BF16))
    n_rows = G_B * WIN_Q_CHUNK
    grp = lax.broadcasted_iota(jnp.int32, (n_rows, 1), 0) // WIN_Q_CHUNK
    sink = jnp.zeros((n_rows, 1), F32)
    for g in range(G_B):
        sink = jnp.where(grp == g, sink_ref[j, g] * LOG2E, sink)
    for c in range(DEC_SEQ // WIN_Q_CHUNK):
        q0 = c * WIN_Q_CHUNK
        k0 = min(max(q0 - WIN_B, 0), DEC_SEQ - WIN_K_SPAN)
        rows = slice(q0, q0 + WIN_Q_CHUNK)
        keys = slice(k0, k0 + WIN_K_SPAN)
        q = jnp.concatenate([q_ref[rows, g * HEAD_DIM:(g + 1) * HEAD_DIM] for g in range(G_B)], axis=0)
        qpos = q0 + lax.broadcasted_iota(jnp.int32, (n_rows, WIN_K_SPAN), 0) % WIN_Q_CHUNK
        kpos = k0 + lax.broadcasted_iota(jnp.int32, (n_rows, WIN_K_SPAN), 1)
        s = jnp.where(jnp.abs(qpos - kpos) <= WIN_B, _qkt(q, k_ref[keys, :]), NEG)
        sc = _qkt(q, kc)
        m = jnp.maximum(jnp.maximum(jnp.max(s, axis=-1, keepdims=True),
                                    jnp.max(sc, axis=-1, keepdims=True)), sink)
        num, den = _pv(jnp.exp2(s - m).astype(BF16), _with_ones(v_ref[keys, :]))
        num_c, den_c = _pv(jnp.exp2(sc - m).astype(BF16), vc)
        o = (num + num_c) / (den + den_c + jnp.exp2(sink - m))
        for g in range(G_B):
            o_ref[rows, g * HEAD_DIM:(g + 1) * HEAD_DIM] = (
                o[g * WIN_Q_CHUNK:(g + 1) * WIN_Q_CHUNK, :].astype(BF16))


def _attn_win(sink, qkv, cache_k, cache_v, layer):
    gw = G_B * HEAD_DIM
    ctx_spec = pl.BlockSpec((None, None, PAST_LEN, HEAD_DIM), lambda b, j: (b, layer, 0, j))
    return pl.pallas_call(
        _attn_win_kernel,
        grid=(DEC_BATCH, N_KV_B),
        in_specs=[
            pl.BlockSpec(memory_space=pltpu.SMEM),
            pl.BlockSpec((None, DEC_SEQ, gw), lambda b, j: (ATT_TILE_QB + j, b, 0)),
            pl.BlockSpec((None, DEC_SEQ, HEAD_DIM), lambda b, j: (ATT_TILE_KVB, b, j)),
            pl.BlockSpec((None, DEC_SEQ, HEAD_DIM), lambda b, j: (ATT_TILE_KVB, b, N_KV_B + j)),
            ctx_spec,
            ctx_spec,
        ],
        out_specs=pl.BlockSpec((DEC_SEQ, gw), lambda b, j: (b, j)),
        out_shape=jax.ShapeDtypeStruct((N_SAMPLE, QB_W), BF16),
        compiler_params=_params("arbitrary", "arbitrary"),
        name="attn_win",
    )(sink, qkv, qkv, qkv, cache_k, cache_v)


REC_CB = 512
REC_SLABS = REC_CB // RG_BLOCK
REC_STREAMS = 8
REC_T = 256
REC_GROUP_ROWS = REC_STREAMS * REC_T
REC_PRE = CONV_PAD_L
REC_POST = CONV_W - 1 - CONV_PAD_L
REC_CHUNK_ROWS = 32 * REC_STREAMS


def _softplus(x):
    return jnp.maximum(x, 0.0) + jnp.log1p(jnp.exp(-jnp.abs(x)))


def _gelu_tanh(x):
    k = np.sqrt(2.0 / np.pi)
    half = 0.5 * x
    return half + half * jnp.tanh(x * (k + (k * 0.044715) * (x * x)))


def _sqrt_nonneg(u):
    return jnp.where(u > 0.0, u * lax.rsqrt(u), 0.0)


def _rec_kernel(x_ref, g_ref, cw_ref, cb_ref, wg_ref, bg_ref, lam_ref, h0_ref, y_ref, st_ref,
                xt_ref, af_ref, bf_ref, ab_ref, bb_ref, *, n_seg):
    S, T = REC_STREAMS, REC_T
    a_refs = (af_ref, ab_ref)
    b_refs = (bf_ref, bb_ref)
    seg = lax.broadcasted_iota(jnp.int32, (S, RG_BLOCK), 0) % n_seg
    slab_cols = [slice(n * RG_BLOCK, (n + 1) * RG_BLOCK) for n in range(REC_SLABS)]

    def t_rows(t):
        return pl.ds(pl.multiple_of(t * S, S), S)

    def from_prev_stream(x):
        return pltpu.roll(x, 1, 0)

    def from_next_stream(x):
        return pltpu.roll(x, S - 1, 0)

    for n, cols in enumerate(slab_cols):
        for s in range(S):
            xt_ref[n, pl.ds(REC_PRE * S + s, T, stride=S), :] = x_ref[s * T:(s + 1) * T, cols]
        for p in range(REC_PRE):
            src = xt_ref[n, (T + p) * S:(T + p + 1) * S, :]
            xt_ref[n, p * S:(p + 1) * S, :] = jnp.where(seg > 0, from_prev_stream(src), 0.0)
        for p in range(REC_POST):
            src = xt_ref[n, (REC_PRE + p) * S:(REC_PRE + p + 1) * S, :]
            xt_ref[n, (REC_PRE + T + p) * S:(REC_PRE + T + p + 1) * S, :] = (
                jnp.where(seg < n_seg - 1, from_next_stream(src), 0.0))

    c_all = (-0.5 * RG_C * np.log2(np.e)) * _softplus(-lam_ref[...])

    def gate_rows(r, carry):
        r0 = pl.multiple_of(r * REC_CHUNK_ROWS, REC_CHUNK_ROWS)
        rows = pl.ds(r0, REC_CHUNK_ROWS)
        for n, cols in enumerate(slab_cols):
            xc = cb_ref[:, cols] + cw_ref[0:1, cols] * xt_ref[n, rows, :]
            for k in range(1, CONV_W):
                tap_rows = pl.ds(pl.multiple_of(r0 + k * S, S), REC_CHUNK_ROWS)
                xc = xc + cw_ref[k:k + 1, cols] * xt_ref[n, tap_rows, :]
            gates = jnp.dot(xc.astype(BF16), wg_ref[n].astype(BF16), preferred_element_type=F32)
            x_half = 0.5 * xc
            for d in range(2):
                ga_half = gates[:, (2 * d) * RG_BLOCK:(2 * d + 1) * RG_BLOCK] + bg_ref[2 * d:2 * d + 1, cols]
                gx_half = (gates[:, (2 * d + 1) * RG_BLOCK:(2 * d + 2) * RG_BLOCK]
                           + bg_ref[2 * d + 1:2 * d + 2, cols])
                c = c_all[d:d + 1, cols]
                a = jnp.exp2(c * jnp.tanh(ga_half) + c)
                a_refs[d][n, rows, :] = a
                b_refs[d][n, rows, :] = _sqrt_nonneg(1.0 - a * a) * ((1.0 + jnp.tanh(gx_half)) * x_half)
        return carry

    lax.fori_loop(0, T * S // REC_CHUNK_ROWS, gate_rows, 0)

    def scan_step(t, carry):
        hf, hb, pf, pb = carry
        rf, rb = t_rows(t), t_rows(T - 1 - t)
        hf_new, hb_new, pf_new, pb_new = [], [], [], []
        for n in range(REC_SLABS):
            a = af_ref[n, rf, :]
            h = a * hf[n] + bf_ref[n, rf, :]
            bf_ref[n, rf, :] = h
            hf_new.append(h)
            a2 = ab_ref[n, rb, :]
            h2 = a2 * hb[n] + bb_ref[n, rb, :]
            bb_ref[n, rb, :] = h2
            hb_new.append(h2)
            if n_seg > 1:
                p = a * pf[n]
                af_ref[n, rf, :] = p
                pf_new.append(p)
                p2 = a2 * pb[n]
                ab_ref[n, rb, :] = p2
                pb_new.append(p2)
        return tuple(hf_new), tuple(hb_new), tuple(pf_new), tuple(pb_new)

    ones = tuple(jnp.ones((S, RG_BLOCK), F32) for _ in range(REC_SLABS)) if n_seg > 1 else ()
    hf, hb, pf, pb = lax.fori_loop(
        0, T, scan_step,
        (tuple(h0_ref[0, :, cols] for cols in slab_cols), tuple(h0_ref[1, :, cols] for cols in slab_cols),
         ones, ones), unroll=4)

    if n_seg > 1:
        cin_f, cin_b = [], []
        for n in range(REC_SLABS):
            cf = jnp.zeros((S, RG_BLOCK), F32)
            for j in range(1, n_seg):
                cf = jnp.where(seg == j, from_prev_stream(hf[n] + pf[n] * cf), cf)
            cb_in = jnp.zeros((S, RG_BLOCK), F32)
            for j in range(n_seg - 2, -1, -1):
                cb_in = jnp.where(seg == j, from_next_stream(hb[n] + pb[n] * cb_in), cb_in)
            cin_f.append(cf)
            cin_b.append(cb_in)

        def carry_in_step(t, carry):
            rows = t_rows(t)
            for n in range(REC_SLABS):
                bf_ref[n, rows, :] += af_ref[n, rows, :] * cin_f[n]
                bb_ref[n, rows, :] += ab_ref[n, rows, :] * cin_b[n]
            return carry

        lax.fori_loop(0, T, carry_in_step, 0, unroll=4)
        hf = tuple(hf[n] + pf[n] * cin_f[n] for n in range(REC_SLABS))
        hb = tuple(hb[n] + pb[n] * cin_b[n] for n in range(REC_SLABS))

    for n, cols in enumerate(slab_cols):
        st_ref[0, :, cols] = hf[n]
        st_ref[1, :, cols] = hb[n]

    for n, cols in enumerate(slab_cols):
        for s in range(S):
            rows = slice(s * T, (s + 1) * T)
            picked = pl.ds(s, T, stride=S)
            h_sum = bf_ref[n, picked, :] + bb_ref[n, picked, :]
            y_ref[rows, cols] = (h_sum * _gelu_tanh(g_ref[rows, cols])).astype(BF16)


def _rec(xg, n_seg, cw, cb, wg, bg, lam, h0):
    n_rows, tn = xg.shape[1], xg.shape[2]
    n_grp = n_rows // REC_GROUP_ROWS
    nc = D_RNN // REC_CB
    per_tile = tn // REC_CB

    def branch_spec(first):
        return pl.BlockSpec((None, REC_GROUP_ROWS, REC_CB),
                            lambda s, c: ((first + c) // per_tile, s, (first + c) % per_tile))

    vec = lambda rows: pl.BlockSpec((rows, REC_CB), lambda s, c: (0, c))
    state_spec = pl.BlockSpec((2, REC_STREAMS, REC_CB), lambda s, c: (0, s, c))
    slab_scratch = lambda n_t: pltpu.VMEM((REC_SLABS, n_t * REC_STREAMS, RG_BLOCK), F32)
    return pl.pallas_call(
        functools.partial(_rec_kernel, n_seg=n_seg),
        grid=(n_grp, nc),
        in_specs=[
            branch_spec(0),
            branch_spec(nc),
            vec(CONV_W),
            vec(1),
            pl.BlockSpec((REC_SLABS, RG_BLOCK, 4 * RG_BLOCK), lambda s, c: (c, 0, 0)),
            vec(4),
            vec(2),
            state_spec,
        ],
        out_specs=[
            pl.BlockSpec((None, REC_GROUP_ROWS, REC_CB), lambda s, c: (c, s, 0)),
            state_spec,
        ],
        out_shape=[
            jax.ShapeDtypeStruct((nc, n_rows, REC_CB), BF16),
            jax.ShapeDtypeStruct((2, n_grp * REC_STREAMS, D_RNN), F32),
        ],
        scratch_shapes=[slab_scratch(REC_PRE + REC_T + REC_POST)] + [slab_scratch(REC_T)] * 4,
        compiler_params=_params("arbitrary", "arbitrary"),
        name="rec",
    )(xg, xg, cw, cb.reshape(1, D_RNN), wg, bg, lam, h0)


def kernel(x_prompt, x_sample, c, cache_a_k, cache_a_v, cache_b_k, cache_b_v, state_rg_fwd, state_rg_bwd, c_ctx, w_ada, b_ada, g_pre_mix, g_post_mix, g_pre_ffn, g_post_ffn, w_att_in, w_att_out, sink_b, rpb_a, w_rec_in, conv_w, conv_b, w_rg_a, b_rg_a, w_rg_x, b_rg_x, rg_lambda, w_rec_out, w_ff1, w_ff2):
    depth = w_ada.shape[0]
    yp = x_prompt.reshape(N_PROMPT, D_MODEL)
    ys = x_sample.reshape(N_SAMPLE, D_MODEL)

    cond8 = jnp.concatenate([c_ctx[None, :], c, jnp.zeros((MOD_ROWS - N_SEG, D_MODEL), F32)], axis=0)
    mod_all = _adaln(cond8, w_ada, b_ada)
    mod_all = mod_all[:, :N_SEG, :].reshape(depth, N_SEG, N_MOD, D_MODEL)
    mod_all = jnp.pad(mod_all, ((0, 0), (0, 0), (0, MOD_ROWS - N_MOD), (0, 0)))

    a_k, a_v, b_k, b_v, s_f, s_b = [], [], [], [], [], []
    for layer in range(depth):
        mod = mod_all[layer]
        li = layer // 2
        g_pre, g_post = g_pre_mix[layer], g_post_mix[layer]
        if layer % 2 == 0:
            qkv_p, ka, va, kb, vb = _proj_att_prompt(yp, g_pre, mod, w_att_in, li)
            qkv_s = _proj_att_sample(ys, g_pre, mod, w_att_in, li)
            a_k.append(ka.reshape(BATCH, SEQ, N_HEADS_A, HEAD_DIM))
            a_v.append(va.reshape(BATCH, SEQ, N_HEADS_A, HEAD_DIM))
            b_k.append(kb.reshape(BATCH, SEQ, N_KV_B, HEAD_DIM))
            b_v.append(vb.reshape(BATCH, SEQ, N_KV_B, HEAD_DIM))
            mix_p = [_attn_ctx(sink_b[li], qkv_p)]
            n_att = cache_a_k.shape[1]
            mix_s = [_attn_na(qkv_s, rpb_a[li],
                              cache_a_k.reshape(DEC_BATCH, n_att, PAST_LEN, QA_W),
                              cache_a_v.reshape(DEC_BATCH, n_att, PAST_LEN, QA_W), li),
                     _attn_win(sink_b[li], qkv_s,
                               cache_b_k.reshape(DEC_BATCH, n_att, PAST_LEN, KVB_W),
                               cache_b_v.reshape(DEC_BATCH, n_att, PAST_LEN, KVB_W), li)]
            w_out = w_att_out
        else:
            xg_p = _proj(PROMPT, yp, g_pre, mod, w_rec_in, li)
            xg_s = _proj(SAMPLE, ys, g_pre, mod, w_rec_in, li)
            wg = 0.5 * jnp.concatenate([w_rg_a[li, 0], w_rg_x[li, 0], w_rg_a[li, 1], w_rg_x[li, 1]], axis=-1)
            bg = 0.5 * jnp.stack([b_rg_a[li, 0], b_rg_x[li, 0], b_rg_a[li, 1], b_rg_x[li, 1]], axis=0)
            rec_args = (conv_w[li], conv_b[li], wg, bg, rg_lambda[li])
            n_seg = DEC_SEQ // REC_T
            assert SEQ == REC_T and BATCH % REC_STREAMS == 0 and DEC_BATCH * n_seg == REC_STREAMS
            h0_p = jnp.zeros((2, BATCH, D_RNN), F32)
            seg_state = jnp.zeros((DEC_BATCH, n_seg, D_RNN), F32)
            h0_s = jnp.stack([seg_state.at[:, 0].set(state_rg_fwd[:, li]).reshape(REC_STREAMS, D_RNN),
                              seg_state.at[:, n_seg - 1].set(state_rg_bwd[:, li]).reshape(REC_STREAMS, D_RNN)])
            rec_p, st_p = _rec(xg_p, 1, *rec_args, h0_p)
            rec_s, _ = _rec(xg_s, n_seg, *rec_args, h0_s)
            mix_p, mix_s = rec_p, rec_s
            s_f.append(st_p[0])
            s_b.append(st_p[1])
            w_out = w_rec_out
        yp = _mixout(PROMPT, mix_p, w_out, li, yp, g_post, mod)
        ys = _mixout(SAMPLE, mix_s, w_out, li, ys, g_post, mod)
        yp = _ffn(PROMPT, yp, g_pre_ffn[layer], g_post_ffn[layer], mod, w_ff1, w_ff2, layer)
        ys = _ffn(SAMPLE, ys, g_pre_ffn[layer], g_post_ffn[layer], mod, w_ff1, w_ff2, layer)

    return (yp.reshape(BATCH, SEQ, D_MODEL), ys.reshape(DEC_BATCH, DEC_SEQ, D_MODEL),
            jnp.stack(a_k, axis=1), jnp.stack(a_v, axis=1), jnp.stack(b_k, axis=1), jnp.stack(b_v, axis=1),
            jnp.stack(s_f, axis=1), jnp.stack(s_b, axis=1))
```

```python
import functools

import jax
import jax.numpy as jnp
import numpy as np
from jax import lax
from jax.experimental import pallas as pl
from jax.experimental.pallas import tpu as pltpu

D_MODEL = 2048
BATCH = 16
SEQ = 256
DEC_BATCH = 2
DEC_SEQ = 1024
PAST_LEN = 256
GRID_W = 64
GRID_R = DEC_SEQ // GRID_W
HEAD_DIM = 128
N_HEADS_A = 8
N_HEADS_B = 8
N_KV_B = 2
G_B = N_HEADS_B // N_KV_B
NA_ROWS = 8
NA_COLS = 16
WIN_B = 128
D_RNN = D_MODEL
N_RG_BLOCKS = 16
RG_BLOCK = D_RNN // N_RG_BLOCKS
CONV_W = 4
CONV_PAD_L = 2
RG_C = 8.0
D_FF = 4 * D_MODEL
ROPE_BASE = 10000.0
EPS = 1e-6
NEG = -1e30
QA_W = N_HEADS_A * HEAD_DIM
QB_W = N_HEADS_B * HEAD_DIM
KVB_W = N_KV_B * HEAD_DIM
D_ATT_IN = 3 * QA_W + QB_W + 2 * KVB_W
SCALE = HEAD_DIM ** -0.5

N_PROMPT = BATCH * SEQ
N_SAMPLE = DEC_BATCH * DEC_SEQ
N_SEG = 1 + DEC_BATCH
N_MOD = 6
MOD_ROWS = 8

COL_QA, COL_KA, COL_VA = 0, QA_W, 2 * QA_W
COL_QB = 3 * QA_W
COL_KB = COL_QB + QB_W
COL_VB = COL_KB + KVB_W

V7X_VMEM_BYTES = 64 * 1024 * 1024
VMEM_LIMIT = V7X_VMEM_BYTES - 4 * 1024 * 1024

TM = 1024
TN = 512
LANES = 128
ROW_CHUNK = 16
ROW_GROUP = 16

F32 = jnp.float32
BF16 = jnp.bfloat16


def _params(*sem):
    return pltpu.CompilerParams(dimension_semantics=sem, vmem_limit_bytes=VMEM_LIMIT)


class _Rows:
    def __init__(self, n_rows, seg0, seg_rows):
        self.n = n_rows
        self.seg0 = seg0
        self.seg_rows = seg_rows

    def seg(self, i, tm):
        return self.seg0 + (i * tm) // self.seg_rows


PROMPT = _Rows(N_PROMPT, 0, N_PROMPT)
SAMPLE = _Rows(N_SAMPLE, 1, DEC_SEQ)


def _mod_spec(rows, tm):
    return pl.BlockSpec((None, MOD_ROWS, D_MODEL), lambda i, j: (rows.seg(i, tm), 0, 0))


def _vec_spec():
    return pl.BlockSpec((1, D_MODEL), lambda i, j: (0, 0))


def _rms_scale(x):
    return lax.rsqrt(jnp.mean(x * x, axis=-1, keepdims=True) + EPS)


def _norm_mod_rows(y_ref, g_ref, mod_ref, h_ref, shift_row, tm, h_row0=0):
    shift = mod_ref[shift_row:shift_row + 1, :]
    gain = g_ref[...] * (1.0 + mod_ref[shift_row + 1:shift_row + 2, :])

    def body(r, carry):
        starts = [r * (ROW_GROUP * ROW_CHUNK) + u * ROW_CHUNK for u in range(ROW_GROUP)]
        scales = [_rms_scale(y_ref[pl.ds(pl.multiple_of(r0, ROW_CHUNK), ROW_CHUNK), :]) for r0 in starts]
        for r0, rs in zip(starts, scales):
            y = y_ref[pl.ds(pl.multiple_of(r0, ROW_CHUNK), ROW_CHUNK), :]
            h = (y * rs) * gain + shift
            h_ref[pl.ds(pl.multiple_of(h_row0 + r0, ROW_CHUNK), ROW_CHUNK), :] = h.astype(BF16)
        return carry

    lax.fori_loop(0, tm // (ROW_GROUP * ROW_CHUNK), body, 0)


def _store_with_square_sums(o_ref, sq_ref, cols, x, first):
    o_ref[:, cols] = x
    sq = x * x
    folded = sq[:, :LANES]
    for k in range(1, sq.shape[1] // LANES):
        folded = folded + sq[:, k * LANES:(k + 1) * LANES]
    if first:
        sq_ref[...] = folded
    else:
        sq_ref[...] += folded


def _gated_residual_rows(y_ref, o_ref, sq_ref, g_ref, gate, tm):
    gain = gate * g_ref[...]
    scale = lax.rsqrt(jnp.sum(sq_ref[...], axis=-1, keepdims=True) / D_MODEL + EPS)
    sq_ref[...] = jnp.broadcast_to(scale, sq_ref.shape)

    def body(r, carry):
        for u in range(ROW_GROUP):
            rows = pl.ds(pl.multiple_of(r * (ROW_GROUP * ROW_CHUNK) + u * ROW_CHUNK, ROW_CHUNK), ROW_CHUNK)
            rs = sq_ref[rows, :]
            for k in range(D_MODEL // LANES):
                c = slice(k * LANES, (k + 1) * LANES)
                o_ref[rows, c] = y_ref[rows, c] + (o_ref[rows, c] * rs) * gain[:, c]
        return carry

    lax.fori_loop(0, tm // (ROW_GROUP * ROW_CHUNK), body, 0)


def _adaln_kernel(cond_ref, w_ref, b_ref, o_ref):
    c = cond_ref[...]
    s = c / (1.0 + jnp.exp(-c))
    o_ref[...] = jnp.dot(s.astype(BF16), w_ref[...].astype(BF16),
                         preferred_element_type=F32) + b_ref[...]


def _adaln(cond8, w_ada, b_ada):
    depth = w_ada.shape[0]
    n = w_ada.shape[2]
    tn = 1024
    return pl.pallas_call(
        _adaln_kernel,
        grid=(depth, n // tn),
        in_specs=[
            pl.BlockSpec((MOD_ROWS, D_MODEL), lambda l, j: (0, 0)),
            pl.BlockSpec((None, D_MODEL, tn), lambda l, j: (l, 0, j)),
            pl.BlockSpec((None, 1, tn), lambda l, j: (l, 0, j)),
        ],
        out_specs=pl.BlockSpec((None, MOD_ROWS, tn), lambda l, j: (l, 0, j)),
        out_shape=jax.ShapeDtypeStruct((depth, MOD_ROWS, n), F32),
        compiler_params=_params("arbitrary", "arbitrary"),
        name="adaln",
    )(cond8, w_ada, b_ada.reshape(depth, 1, n))


def _rope_tables():
    t = np.arange(DEC_SEQ)
    half = HEAD_DIM // 2
    inv = ROPE_BASE ** (-np.arange(0, half, 2, dtype=np.float64) / half)
    ang_r = (t // GRID_W)[:, None] * inv[None, :]
    ang_c = (t % GRID_W)[:, None] * inv[None, :]
    cos = np.concatenate([np.cos(ang_r), np.cos(ang_r), np.cos(ang_c), np.cos(ang_c)], axis=1)
    sin = np.concatenate([-np.sin(ang_r), np.sin(ang_r), -np.sin(ang_c), np.sin(ang_c)], axis=1)
    return jnp.asarray(cos, F32), jnp.asarray(sin, F32)


def _rope(x, cos, sin_signed):
    quarter = HEAD_DIM // 4
    lane = lax.broadcasted_iota(jnp.int32, x.shape, 1)
    first = (lane % (2 * quarter)) < quarter
    partner = jnp.where(first, pltpu.roll(x, HEAD_DIM - quarter, 1), pltpu.roll(x, quarter, 1))
    return x * cos + partner * sin_signed


def _proj_h_rows(y_ref, g_ref, mod_ref, h_ref, tm):
    row0 = pl.multiple_of(pl.program_id(1) * tm, tm)

    @pl.when(pl.program_id(0) == 0)
    def _():
        _norm_mod_rows(y_ref, g_ref, mod_ref, h_ref, 0, tm, row0)

    return pl.ds(row0, tm)


def _proj_in_specs(rows, tm, layer, tn=TN):
    last = rows.n // tm - 1

    def tile(j, i):
        return jnp.where(j == 0, i, last)

    return [
        pl.BlockSpec((tm, D_MODEL), lambda j, i: (tile(j, i), 0)),
        pl.BlockSpec((1, D_MODEL), lambda j, i: (0, 0)),
        pl.BlockSpec((None, MOD_ROWS, D_MODEL), lambda j, i: (rows.seg(tile(j, i), tm), 0, 0)),
        pl.BlockSpec((None, D_MODEL, tn), lambda j, i: (layer, 0, j)),
    ]


def _proj_tn(rows):
    return TN if rows.n > N_SAMPLE else 2 * TN


def _proj_kernel(y_ref, g_ref, mod_ref, w_ref, o_ref, h_ref, *, tm):
    rows = _proj_h_rows(y_ref, g_ref, mod_ref, h_ref, tm)
    o_ref[...] = jnp.dot(h_ref[rows, :], w_ref[...].astype(BF16), preferred_element_type=F32)


def _proj(rows, y, g, mod, w, layer):
    n = w.shape[2]
    tm, tn = TM, _proj_tn(rows)
    return pl.pallas_call(
        functools.partial(_proj_kernel, tm=tm),
        grid=(n // tn, rows.n // tm),
        in_specs=_proj_in_specs(rows, tm, layer, tn),
        out_specs=pl.BlockSpec((None, tm, tn), lambda j, i: (j, i, 0)),
        out_shape=jax.ShapeDtypeStruct((n // tn, rows.n, tn), F32),
        scratch_shapes=[pltpu.VMEM((rows.n, D_MODEL), BF16)],
        compiler_params=_params("arbitrary", "arbitrary"),
        name="proj",
    )(y, g.reshape(1, D_MODEL), mod, w)


ATT_TILE_KA = COL_KA // TN
ATT_TILE_VA = COL_VA // TN
ATT_TILE_QB = COL_QB // TN
ATT_TILE_KVB = COL_KB // TN
N_ATT_TILES = D_ATT_IN // TN
ATT_CHUNK = KVB_W
assert TN == 2 * ATT_CHUNK and ATT_CHUNK % HEAD_DIM == 0

LOG2E = float(np.log2(np.e))
Q_FACTOR = SCALE * LOG2E


def _q_factor(j):
    is_q = (j < ATT_TILE_KA) | ((j >= ATT_TILE_QB) & (j < ATT_TILE_KVB))
    return jnp.where(is_q, Q_FACTOR, 1.0).astype(F32)


def _proj_att_prompt_kernel(y_ref, g_ref, mod_ref, w_ref, o_ref, ka_ref, va_ref, kb_ref, vb_ref, h_ref, *, tm):
    j = pl.program_id(0)
    rows = _proj_h_rows(y_ref, g_ref, mod_ref, h_ref, tm)

    def tile(factor, f32_out):
        for c in range(TN // ATT_CHUNK):
            cols = slice(c * ATT_CHUNK, (c + 1) * ATT_CHUNK)
            acc = jnp.dot(h_ref[rows, :], w_ref[:, cols].astype(BF16), preferred_element_type=F32)
            o_ref[:, cols] = (acc if factor is None else acc * factor).astype(BF16)
            if f32_out is not None:
                f32_out(c, cols, acc)

    def store_ka(c, cols, acc):
        ka_ref[:, cols] = acc

    def store_va(c, cols, acc):
        va_ref[:, cols] = acc

    def store_kvb(c, cols, acc):
        (kb_ref, vb_ref)[c][...] = acc

    @pl.when((j < ATT_TILE_KA) | ((j >= ATT_TILE_QB) & (j < ATT_TILE_KVB)))
    def _():
        tile(Q_FACTOR, None)

    @pl.when((j >= ATT_TILE_KA) & (j < ATT_TILE_VA))
    def _():
        tile(None, store_ka)

    @pl.when((j >= ATT_TILE_VA) & (j < ATT_TILE_QB))
    def _():
        tile(None, store_va)

    @pl.when(j == ATT_TILE_KVB)
    def _():
        tile(None, store_kvb)


def _proj_att_prompt(y, g, mod, w, layer):
    tm = TM
    rows = PROMPT
    last = rows.n // tm - 1

    def kv_map(first, count):
        def index(j, i):
            row = jnp.where(j < first, 0, jnp.where(j >= first + count, last, i))
            return row, jnp.clip(j - first, 0, count - 1)
        return index

    per_head_set = QA_W // TN
    return pl.pallas_call(
        functools.partial(_proj_att_prompt_kernel, tm=tm),
        grid=(N_ATT_TILES, rows.n // tm),
        in_specs=_proj_in_specs(rows, tm, layer),
        out_specs=[
            pl.BlockSpec((None, tm, TN), lambda j, i: (j, i, 0)),
            pl.BlockSpec((tm, TN), kv_map(ATT_TILE_KA, per_head_set)),
            pl.BlockSpec((tm, TN), kv_map(ATT_TILE_VA, per_head_set)),
            pl.BlockSpec((tm, KVB_W), kv_map(ATT_TILE_KVB, 1)),
            pl.BlockSpec((tm, KVB_W), kv_map(ATT_TILE_KVB, 1)),
        ],
        out_shape=[
            jax.ShapeDtypeStruct((N_ATT_TILES, rows.n, TN), BF16),
            jax.ShapeDtypeStruct((rows.n, QA_W), F32),
            jax.ShapeDtypeStruct((rows.n, QA_W), F32),
            jax.ShapeDtypeStruct((rows.n, KVB_W), F32),
            jax.ShapeDtypeStruct((rows.n, KVB_W), F32),
        ],
        scratch_shapes=[pltpu.VMEM((rows.n, D_MODEL), BF16)],
        compiler_params=_params("arbitrary", "arbitrary"),
        name="proj_att_prompt",
    )(y, g.reshape(1, D_MODEL), mod, w)


def _proj_att_sample_kernel(y_ref, g_ref, mod_ref, w_ref, cos_ref, sin_ref, o_ref, h_ref, *, tm):
    j = pl.program_id(0)
    rows = _proj_h_rows(y_ref, g_ref, mod_ref, h_ref, tm)
    q_factor = _q_factor(j)

    def tile(n_rope_heads):
        for c in range(TN // ATT_CHUNK):
            acc = jnp.dot(h_ref[rows, :], w_ref[:, c * ATT_CHUNK:(c + 1) * ATT_CHUNK].astype(BF16),
                          preferred_element_type=F32)
            for hd in range(ATT_CHUNK // HEAD_DIM):
                head = c * (ATT_CHUNK // HEAD_DIM) + hd
                x = acc[:, hd * HEAD_DIM:(hd + 1) * HEAD_DIM]
                if head < n_rope_heads:
                    x = _rope(x, cos_ref[...], sin_ref[...])
                o_ref[:, head * HEAD_DIM:(head + 1) * HEAD_DIM] = (x * q_factor).astype(BF16)

    @pl.when(j < ATT_TILE_QB)
    def _():
        tile(0)

    @pl.when((j >= ATT_TILE_QB) & (j < ATT_TILE_KVB))
    def _():
        tile(TN // HEAD_DIM)

    @pl.when(j == ATT_TILE_KVB)
    def _():
        tile(N_KV_B)


def _proj_att_sample(y, g, mod, w, layer):
    tm = TM
    assert tm == DEC_SEQ
    rows = SAMPLE
    cos, sin = _rope_tables()
    tab_spec = pl.BlockSpec((DEC_SEQ, HEAD_DIM), lambda j, i: (0, 0))
    return pl.pallas_call(
        functools.partial(_proj_att_sample_kernel, tm=tm),
        grid=(N_ATT_TILES, rows.n // tm),
        in_specs=_proj_in_specs(rows, tm, layer) + [tab_spec, tab_spec],
        out_specs=pl.BlockSpec((None, tm, TN), lambda j, i: (j, i, 0)),
        out_shape=jax.ShapeDtypeStruct((N_ATT_TILES, rows.n, TN), BF16),
        scratch_shapes=[pltpu.VMEM((rows.n, D_MODEL), BF16)],
        compiler_params=_params("arbitrary", "arbitrary"),
        name="proj_att_sample",
    )(y, g.reshape(1, D_MODEL), mod, w, cos, sin)


MIXOUT_TM = 512


def _mixout_kernel(*refs, tm, n_parts):
    a_refs = refs[:n_parts]
    w_ref, y_ref, g_ref, mod_ref, o_ref, sq_ref = refs[n_parts:]
    kp = a_refs[0].shape[1]
    for c in range(D_MODEL // TN):
        cols = slice(c * TN, (c + 1) * TN)
        acc = None
        for p in range(n_parts):
            part = jnp.dot(a_refs[p][...], w_ref[p * kp:(p + 1) * kp, cols].astype(BF16),
                           preferred_element_type=F32)
            acc = part if acc is None else acc + part
        _store_with_square_sums(o_ref, sq_ref, cols, acc, first=(c == 0))
    _gated_residual_rows(y_ref, o_ref, sq_ref, g_ref, mod_ref[2:3, :], tm)


def _mixout(rows, a_parts, w, layer, y, g, mod):
    tm = MIXOUT_TM
    if isinstance(a_parts, (list, tuple)):
        n_parts, kp = len(a_parts), a_parts[0].shape[1]
        part_specs = [pl.BlockSpec((tm, kp), lambda i: (i, 0)) for _ in range(n_parts)]
    else:
        n_parts, kp = a_parts.shape[0], a_parts.shape[2]
        part_specs = [pl.BlockSpec((None, tm, kp), lambda i, p=p: (p, i, 0)) for p in range(n_parts)]
        a_parts = [a_parts] * n_parts
    assert n_parts * kp == w.shape[1]
    return pl.pallas_call(
        functools.partial(_mixout_kernel, tm=tm, n_parts=n_parts),
        grid=(rows.n // tm,),
        in_specs=part_specs + [
            pl.BlockSpec((None, w.shape[1], D_MODEL), lambda i: (layer, 0, 0), pipeline_mode=pl.Buffered(1)),
            pl.BlockSpec((tm, D_MODEL), lambda i: (i, 0)),
            pl.BlockSpec((1, D_MODEL), lambda i: (0, 0)),
            pl.BlockSpec((None, MOD_ROWS, D_MODEL), lambda i: (rows.seg(i, tm), 0, 0)),
        ],
        out_specs=pl.BlockSpec((tm, D_MODEL), lambda i: (i, 0)),
        out_shape=jax.ShapeDtypeStruct((rows.n, D_MODEL), F32),
        scratch_shapes=[pltpu.VMEM((tm, LANES), F32)],
        compiler_params=_params("arbitrary"),
        name="mixout",
    )(*a_parts, w, y, g.reshape(1, D_MODEL), mod)


FFN_TF = 256
FFN_PAIR = 2 * FFN_TF
FFN_N_PAIRS = D_FF // FFN_PAIR
FFN_GROUP = 2
FFN_N_GROUPS = FFN_N_PAIRS // FFN_GROUP


def _ffn_kernel(y_ref, g1_ref, g2_ref, mod_ref, w1_hbm, w2_hbm, o_ref, h_ref, a_ref, sq_ref, w1_buf, w2_buf,
                sem, *, tm, layer, n_tiles):
    i = pl.program_id(0)
    g = pl.program_id(1)
    last_tile = n_tiles - 1

    def block(pair):
        start = pair * FFN_PAIR
        return pl.ds(start if isinstance(start, int) else pl.multiple_of(start, FFN_PAIR), FFN_PAIR)

    def w1_copy(pair, slot):
        return pltpu.make_async_copy(w1_hbm.at[layer, :, block(pair)], w1_buf.at[slot], sem.at[0, slot])

    def w2_copy(pair, slot):
        return pltpu.make_async_copy(w2_hbm.at[layer, block(pair), :], w2_buf.at[slot], sem.at[1, slot])

    def up(slot):
        for half in range(FFN_PAIR // FFN_TF):
            cols = slice(half * FFN_TF, (half + 1) * FFN_TF)
            a = jnp.dot(h_ref[...], w1_buf[slot, :, cols].astype(BF16), preferred_element_type=F32)
            a = jnp.maximum(a, 0.0)
            a_ref[slot, :, cols] = (a * a).astype(BF16)

    def down(a_slot, w_slot, first=False, last=False):
        a = a_ref[a_slot]
        for c in range(D_MODEL // TN):
            cols = slice(c * TN, (c + 1) * TN)
            part = jnp.dot(a, w2_buf[w_slot, :, cols].astype(BF16), preferred_element_type=F32)
            if first:
                o_ref[:, cols] = part
            elif last:
                _store_with_square_sums(o_ref, sq_ref, cols, o_ref[:, cols] + part, first=(c == 0))
            else:
                o_ref[:, cols] += part

    def pair_step(q, u):
        slot = u % 2
        w1_copy(q, slot).wait()
        w1_copy((q + 1) % FFN_N_PAIRS, 1 - slot).start()
        w2_copy(q, 1 - slot).start()
        return slot

    @pl.when(g == 0)
    def _():
        @pl.when(i == 0)
        def _():
            w1_copy(0, 0).start()

        _norm_mod_rows(y_ref, g1_ref, mod_ref, h_ref, 3, tm)
        for u in range(FFN_GROUP):
            slot = pair_step(u, u)
            if u > 0:
                w2_copy(u - 1, slot).wait()
            up(slot)
            if u > 0:
                down(1 - slot, slot, first=(u == 1))

    @pl.when(g > 0)
    def _():
        for u in range(FFN_GROUP):
            q = g * FFN_GROUP + u
            slot = pair_step(q, u)
            w2_copy(q - 1, slot).wait()
            up(slot)
            down(1 - slot, slot)

    @pl.when(g == FFN_N_GROUPS - 1)
    def _():
        w2_copy(FFN_N_PAIRS - 1, 0).wait()
        down((FFN_N_PAIRS - 1) % 2, 0, last=True)
        _gated_residual_rows(y_ref, o_ref, sq_ref, g2_ref, mod_ref[5:6, :], tm)

        @pl.when(i == last_tile)
        def _():
            w1_copy(0, 0).wait()


def _ffn(rows, y, g1, g2, mod, w1, w2, layer):
    tm = TM
    n_tiles = rows.n // tm
    assert FFN_GROUP % 2 == 0 and FFN_N_PAIRS % FFN_GROUP == 0
    return pl.pallas_call(
        functools.partial(_ffn_kernel, tm=tm, layer=layer, n_tiles=n_tiles),
        grid=(n_tiles, FFN_N_GROUPS),
        in_specs=[
            pl.BlockSpec((tm, D_MODEL), lambda i, s: (i, 0)),
            _vec_spec(),
            _vec_spec(),
            _mod_spec(rows, tm),
            pl.BlockSpec(memory_space=pl.ANY),
            pl.BlockSpec(memory_space=pl.ANY),
        ],
        out_specs=pl.BlockSpec((tm, D_MODEL), lambda i, s: (i, 0)),
        out_shape=jax.ShapeDtypeStruct((rows.n, D_MODEL), F32),
        scratch_shapes=[
            pltpu.VMEM((tm, D_MODEL), BF16),
            pltpu.VMEM((2, tm, FFN_PAIR), BF16),
            pltpu.VMEM((tm, LANES), F32),
            pltpu.VMEM((2, D_MODEL, FFN_PAIR), F32),
            pltpu.VMEM((2, FFN_PAIR, D_MODEL), F32),
            pltpu.SemaphoreType.DMA((2, 2)),
        ],
        compiler_params=_params("arbitrary", "arbitrary"),
        name="ffn",
    )(y, g1.reshape(1, D_MODEL), g2.reshape(1, D_MODEL), mod, w1, w2)


def _qkt(q, k):
    return lax.dot_general(q, k, (((1,), (1,)), ((), ())), preferred_element_type=F32)


def _with_ones(v):
    return jnp.concatenate([v, jnp.ones_like(v)], axis=1)


def _pv(p, v_ext):
    o = jnp.dot(p, v_ext, preferred_element_type=F32)
    return o[:, :HEAD_DIM], o[:, HEAD_DIM:]


CTX_SEQ_PER_STEP = 2


def _attn_ctx_kernel(sink_ref, qkv_ref, o_ref):
    n_rows = G_B * SEQ
    grp = lax.broadcasted_iota(jnp.int32, (n_rows, 1), 0) // SEQ
    for b in range(CTX_SEQ_PER_STEP):
        rows = slice(b * SEQ, (b + 1) * SEQ)

        def head(col):
            tile, off = divmod(col, TN)
            return qkv_ref[tile, rows, off:off + HEAD_DIM]

        for h in range(N_HEADS_A):
            q = head(COL_QA + h * HEAD_DIM)
            k = head(COL_KA + h * HEAD_DIM)
            v = head(COL_VA + h * HEAD_DIM)
            s = _qkt(q, k)
            m = jnp.max(s, axis=-1, keepdims=True)
            num, den = _pv(jnp.exp2(s - m).astype(BF16), _with_ones(v))
            o_ref[rows, h * HEAD_DIM:(h + 1) * HEAD_DIM] = (num / den).astype(BF16)
        for j in range(N_KV_B):
            k = head(COL_KB + j * HEAD_DIM)
            v = head(COL_VB + j * HEAD_DIM)
            q = jnp.concatenate([head(COL_QB + (j * G_B + g) * HEAD_DIM) for g in range(G_B)], axis=0)
            sink = jnp.zeros((n_rows, 1), F32)
            for g in range(G_B):
                sink = jnp.where(grp == g, sink_ref[j, g] * LOG2E, sink)
            s = _qkt(q, k)
            m = jnp.maximum(jnp.max(s, axis=-1, keepdims=True), sink)
            num, den = _pv(jnp.exp2(s - m).astype(BF16), _with_ones(v))
            o = num / (den + jnp.exp2(sink - m))
            for g in range(G_B):
                c0 = QA_W + (j * G_B + g) * HEAD_DIM
                o_ref[rows, c0:c0 + HEAD_DIM] = o[g * SEQ:(g + 1) * SEQ, :].astype(BF16)


def _attn_ctx(sink, qkv):
    rows = CTX_SEQ_PER_STEP * SEQ
    return pl.pallas_call(
        _attn_ctx_kernel,
        grid=(N_PROMPT // rows,),
        in_specs=[
            pl.BlockSpec(memory_space=pltpu.SMEM),
            pl.BlockSpec((N_ATT_TILES, rows, TN), lambda b: (0, b, 0)),
        ],
        out_specs=pl.BlockSpec((rows, D_MODEL), lambda b: (b, 0)),
        out_shape=jax.ShapeDtypeStruct((N_PROMPT, D_MODEL), BF16),
        compiler_params=_params("arbitrary"),
        name="attn_ctx",
    )(sink, qkv)


NA_Q_ROWS = 4
NA_K_ROWS = 12
NA_Q_CHUNK = NA_Q_ROWS * GRID_W
NA_K_SPAN = NA_K_ROWS * GRID_W
NA_K_ROW0 = (0, 0, 4, 4)
N_RPB_ROWS = 2 * NA_ROWS - 1
N_RPB_COLS = 2 * NA_COLS - 1


def _na_row_start(r):
    return min(max(r - NA_ROWS // 2, 0), GRID_R - NA_ROWS)


for _chunk, _k0 in enumerate(NA_K_ROW0):
    for _r in range(_chunk * NA_Q_ROWS, (_chunk + 1) * NA_Q_ROWS):
        assert _k0 % 2 == 0 and _k0 <= _na_row_start(_r)
        assert _na_row_start(_r) + NA_ROWS <= _k0 + NA_K_ROWS <= GRID_R


def _na_build_bias(rpb_ref, tile_ref, bias_ref):
    shape = (GRID_W, 2 * GRID_W)
    qc = lax.broadcasted_iota(jnp.int32, shape, 0)
    lane = lax.broadcasted_iota(jnp.int32, shape, 1)
    kc = lane % GRID_W
    start_c = jnp.clip(qc - NA_COLS // 2, 0, GRID_W - NA_COLS)
    in_win = (kc >= start_c) & (kc < start_c + NA_COLS)
    for dr in range(N_RPB_ROWS):
        rows = jnp.broadcast_to(rpb_ref[dr:dr + 1, :], shape)
        shifted = pltpu.roll(rows, 2 * GRID_W - (NA_COLS - 1), 1, stride=1, stride_axis=0)
        tile_ref[dr] = jnp.where(in_win, shifted * LOG2E, NEG)
    first_half = lane < GRID_W
    neg = jnp.full(shape, NEG, F32)
    for chunk in range(GRID_R // NA_Q_ROWS):
        for qi in range(NA_Q_ROWS):
            qr = chunk * NA_Q_ROWS + qi
            lo = _na_row_start(qr)
            for m in range(NA_K_ROWS // 2):
                kr = NA_K_ROW0[chunk] + 2 * m
                parts = []
                for r in (kr, kr + 1):
                    parts.append(tile_ref[r - qr + NA_ROWS - 1] if lo <= r < lo + NA_ROWS else neg)
                bias_ref[chunk, qi * GRID_W:(qi + 1) * GRID_W, m * 2 * GRID_W:(m + 1) * 2 * GRID_W] = (
                    jnp.where(first_half, parts[0], parts[1]))


def _attn_na_kernel(rpb_ref, q_ref, k_ref, v_ref, kc_ref, vc_ref, o_ref, tile_ref, bias_ref):
    _na_build_bias(rpb_ref, tile_ref, bias_ref)
    for b in range(DEC_BATCH):
        kc = kc_ref[b].astype(BF16)
        vc = _with_ones(vc_ref[b].astype(BF16))
        for c in range(DEC_SEQ // NA_Q_CHUNK):
            r0 = b * DEC_SEQ + c * NA_Q_CHUNK
            k0 = b * DEC_SEQ + NA_K_ROW0[c] * GRID_W
            rows = slice(r0, r0 + NA_Q_CHUNK)
            keys = slice(k0, k0 + NA_K_SPAN)
            q = q_ref[rows, :]
            s = _qkt(q, k_ref[keys, :]) + bias_ref[c]
            sc = _qkt(q, kc)
            m = jnp.maximum(jnp.max(s, axis=-1, keepdims=True), jnp.max(sc, axis=-1, keepdims=True))
            num, den = _pv(jnp.exp2(s - m).astype(BF16), _with_ones(v_ref[keys, :]))
            num_c, den_c = _pv(jnp.exp2(sc - m).astype(BF16), vc)
            o_ref[rows, :] = ((num + num_c) / (den + den_c)).astype(BF16)


def _attn_na(qkv, rpb, cache_k, cache_v, layer):
    heads_per_tile = TN // HEAD_DIM

    def head_spec(c0):
        def index(h):
            head = c0 // HEAD_DIM + h
            return head // heads_per_tile, 0, head % heads_per_tile
        return pl.BlockSpec((None, N_SAMPLE, HEAD_DIM), index)

    ctx_spec = pl.BlockSpec((DEC_BATCH, None, PAST_LEN, HEAD_DIM), lambda h: (0, layer, 0, h))
    n_chunks = DEC_SEQ // NA_Q_CHUNK
    pad_rows = -N_RPB_ROWS % 8
    half = jnp.pad(rpb.astype(F32), ((0, 0), (0, pad_rows), (0, GRID_W - N_RPB_COLS)), constant_values=NEG)
    rpb_rows = jnp.concatenate([half, half], axis=-1)
    return pl.pallas_call(
        _attn_na_kernel,
        grid=(N_HEADS_A,),
        in_specs=[
            pl.BlockSpec((None, rpb_rows.shape[1], 2 * GRID_W), lambda h: (h, 0, 0)),
            head_spec(COL_QA),
            head_spec(COL_KA),
            head_spec(COL_VA),
            ctx_spec,
            ctx_spec,
        ],
        out_specs=pl.BlockSpec((N_SAMPLE, HEAD_DIM), lambda h: (0, h)),
        out_shape=jax.ShapeDtypeStruct((N_SAMPLE, QA_W), BF16),
        scratch_shapes=[pltpu.VMEM((N_RPB_ROWS, GRID_W, 2 * GRID_W), F32),
                        pltpu.VMEM((n_chunks, NA_Q_CHUNK, NA_K_SPAN), F32)],
        compiler_params=_params("arbitrary"),
        name="attn_na",
    )(rpb_rows, qkv, qkv, qkv, cache_k, cache_v)


WIN_Q_CHUNK = 128
WIN_K_SPAN = WIN_Q_CHUNK + 2 * WIN_B


def _attn_win_kernel(sink_ref, q_ref, k_ref, v_ref, kc_ref, vc_ref, o_ref):
    j = pl.program_id(1)
    kc = kc_ref[...].astype(BF16)
    vc = _with_ones(vc_ref[...].astype(BF16))
    n_rows = G_B * WIN_Q_CHUNK
    grp = lax.broadcasted_iota(jnp.int32, (n_rows, 1), 0) // WIN_Q_CHUNK
    sink = jnp.zeros((n_rows, 1), F32)
    for g in range(G_B):
        sink = jnp.where(grp == g, sink_ref[j, g] * LOG2E, sink)
    for c in range(DEC_SEQ // WIN_Q_CHUNK):
        q0 = c * WIN_Q_CHUNK
        k0 = min(max(q0 - WIN_B, 0), DEC_SEQ - WIN_K_SPAN)
        rows = slice(q0, q0 + WIN_Q_CHUNK)
        keys = slice(k0, k0 + WIN_K_SPAN)
        q = jnp.concatenate([q_ref[rows, g * HEAD_DIM:(g + 1) * HEAD_DIM] for g in range(G_B)], axis=0)
        qpos = q0 + lax.broadcasted_iota(jnp.int32, (n_rows, WIN_K_SPAN), 0) % WIN_Q_CHUNK
        kpos = k0 + lax.broadcasted_iota(jnp.int32, (n_rows, WIN_K_SPAN), 1)
        s = jnp.where(jnp.abs(qpos - kpos) <= WIN_B, _qkt(q, k_ref[keys, :]), NEG)
        sc = _qkt(q, kc)
        m = jnp.maximum(jnp.maximum(jnp.max(s, axis=-1, keepdims=True),
                                    jnp.max(sc, axis=-1, keepdims=True)), sink)
        num, den = _pv(jnp.exp2(s - m).astype(BF16), _with_ones(v_ref[keys, :]))
        num_c, den_c = _pv(jnp.exp2(sc - m).astype(BF16), vc)
        o = (num + num_c) / (den + den_c + jnp.exp2(sink - m))
        for g in range(G_B):
            o_ref[rows, g * HEAD_DIM:(g + 1) * HEAD_DIM] = (
                o[g * WIN_Q_CHUNK:(g + 1) * WIN_Q_CHUNK, :].astype(BF16))


def _attn_win(sink, qkv, cache_k, cache_v, layer):
    gw = G_B * HEAD_DIM
    ctx_spec = pl.BlockSpec((None, None, PAST_LEN, HEAD_DIM), lambda b, j: (b, layer, 0, j))
    return pl.pallas_call(
        _attn_win_kernel,
        grid=(DEC_BATCH, N_KV_B),
        in_specs=[
            pl.BlockSpec(memory_space=pltpu.SMEM),
            pl.BlockSpec((None, DEC_SEQ, gw), lambda b, j: (ATT_TILE_QB + j, b, 0)),
            pl.BlockSpec((None, DEC_SEQ, HEAD_DIM), lambda b, j: (ATT_TILE_KVB, b, j)),
            pl.BlockSpec((None, DEC_SEQ, HEAD_DIM), lambda b, j: (ATT_TILE_KVB, b, N_KV_B + j)),
            ctx_spec,
            ctx_spec,
        ],
        out_specs=pl.BlockSpec((DEC_SEQ, gw), lambda b, j: (b, j)),
        out_shape=jax.ShapeDtypeStruct((N_SAMPLE, QB_W), BF16),
        compiler_params=_params("arbitrary", "arbitrary"),
        name="attn_win",
    )(sink, qkv, qkv, qkv, cache_k, cache_v)


REC_CB = 512
REC_SLABS = REC_CB // RG_BLOCK
REC_STREAMS = 8
REC_T = 256
REC_GROUP_ROWS = REC_STREAMS * REC_T
REC_PRE = CONV_PAD_L
REC_POST = CONV_W - 1 - CONV_PAD_L
REC_CHUNK_ROWS = 32 * REC_STREAMS


def _softplus(x):
    return jnp.maximum(x, 0.0) + jnp.log1p(jnp.exp(-jnp.abs(x)))


def _gelu_tanh(x):
    k = np.sqrt(2.0 / np.pi)
    half = 0.5 * x
    return half + half * jnp.tanh(x * (k + (k * 0.044715) * (x * x)))


def _sqrt_nonneg(u):
    return jnp.where(u > 0.0, u * lax.rsqrt(u), 0.0)


def _rec_kernel(x_ref, g_ref, cw_ref, cb_ref, wg_ref, bg_ref, lam_ref, h0_ref, y_ref, st_ref,
                xt_ref, af_ref, bf_ref, ab_ref, bb_ref, *, n_seg):
    S, T = REC_STREAMS, REC_T
    a_refs = (af_ref, ab_ref)
    b_refs = (bf_ref, bb_ref)
    seg = lax.broadcasted_iota(jnp.int32, (S, RG_BLOCK), 0) % n_seg
    slab_cols = [slice(n * RG_BLOCK, (n + 1) * RG_BLOCK) for n in range(REC_SLABS)]

    def t_rows(t):
        return pl.ds(pl.multiple_of(t * S, S), S)

    def from_prev_stream(x):
        return pltpu.roll(x, 1, 0)

    def from_next_stream(x):
        return pltpu.roll(x, S - 1, 0)

    for n, cols in enumerate(slab_cols):
        for s in range(S):
            xt_ref[n, pl.ds(REC_PRE * S + s, T, stride=S), :] = x_ref[s * T:(s + 1) * T, cols]
        for p in range(REC_PRE):
            src = xt_ref[n, (T + p) * S:(T + p + 1) * S, :]
            xt_ref[n, p * S:(p + 1) * S, :] = jnp.where(seg > 0, from_prev_stream(src), 0.0)
        for p in range(REC_POST):
            src = xt_ref[n, (REC_PRE + p) * S:(REC_PRE + p + 1) * S, :]
            xt_ref[n, (REC_PRE + T + p) * S:(REC_PRE + T + p + 1) * S, :] = (
                jnp.where(seg < n_seg - 1, from_next_stream(src), 0.0))

    c_all = (-0.5 * RG_C * np.log2(np.e)) * _softplus(-lam_ref[...])

    def gate_rows(r, carry):
        r0 = pl.multiple_of(r * REC_CHUNK_ROWS, REC_CHUNK_ROWS)
        rows = pl.ds(r0, REC_CHUNK_ROWS)
        for n, cols in enumerate(slab_cols):
            xc = cb_ref[:, cols] + cw_ref[0:1, cols] * xt_ref[n, rows, :]
            for k in range(1, CONV_W):
                tap_rows = pl.ds(pl.multiple_of(r0 + k * S, S), REC_CHUNK_ROWS)
                xc = xc + cw_ref[k:k + 1, cols] * xt_ref[n, tap_rows, :]
            gates = jnp.dot(xc.astype(BF16), wg_ref[n].astype(BF16), preferred_element_type=F32)
            x_half = 0.5 * xc
            for d in range(2):
                ga_half = gates[:, (2 * d) * RG_BLOCK:(2 * d + 1) * RG_BLOCK] + bg_ref[2 * d:2 * d + 1, cols]
                gx_half = (gates[:, (2 * d + 1) * RG_BLOCK:(2 * d + 2) * RG_BLOCK]
                           + bg_ref[2 * d + 1:2 * d + 2, cols])
                c = c_all[d:d + 1, cols]
                a = jnp.exp2(c * jnp.tanh(ga_half) + c)
                a_refs[d][n, rows, :] = a
                b_refs[d][n, rows, :] = _sqrt_nonneg(1.0 - a * a) * ((1.0 + jnp.tanh(gx_half)) * x_half)
        return carry

    lax.fori_loop(0, T * S // REC_CHUNK_ROWS, gate_rows, 0)

    def scan_step(t, carry):
        hf, hb, pf, pb = carry
        rf, rb = t_rows(t), t_rows(T - 1 - t)
        hf_new, hb_new, pf_new, pb_new = [], [], [], []
        for n in range(REC_SLABS):
            a = af_ref[n, rf, :]
            h = a * hf[n] + bf_ref[n, rf, :]
            bf_ref[n, rf, :] = h
            hf_new.append(h)
            a2 = ab_ref[n, rb, :]
            h2 = a2 * hb[n] + bb_ref[n, rb, :]
            bb_ref[n, rb, :] = h2
            hb_new.append(h2)
            if n_seg > 1:
                p = a * pf[n]
                af_ref[n, rf, :] = p
                pf_new.append(p)
                p2 = a2 * pb[n]
                ab_ref[n, rb, :] = p2
                pb_new.append(p2)
        return tuple(hf_new), tuple(hb_new), tuple(pf_new), tuple(pb_new)

    ones = tuple(jnp.ones((S, RG_BLOCK), F32) for _ in range(REC_SLABS)) if n_seg > 1 else ()
    hf, hb, pf, pb = lax.fori_loop(
        0, T, scan_step,
        (tuple(h0_ref[0, :, cols] for cols in slab_cols), tuple(h0_ref[1, :, cols] for cols in slab_cols),
         ones, ones), unroll=4)

    if n_seg > 1:
        cin_f, cin_b = [], []
        for n in range(REC_SLABS):
            cf = jnp.zeros((S, RG_BLOCK), F32)
            for j in range(1, n_seg):
                cf = jnp.where(seg == j, from_prev_stream(hf[n] + pf[n] * cf), cf)
            cb_in = jnp.zeros((S, RG_BLOCK), F32)
            for j in range(n_seg - 2, -1, -1):
                cb_in = jnp.where(seg == j, from_next_stream(hb[n] + pb[n] * cb_in), cb_in)
            cin_f.append(cf)
            cin_b.append(cb_in)

        def carry_in_step(t, carry):
            rows = t_rows(t)
            for n in range(REC_SLABS):
                bf_ref[n, rows, :] += af_ref[n, rows, :] * cin_f[n]
                bb_ref[n, rows, :] += ab_ref[n, rows, :] * cin_b[n]
            return carry

        lax.fori_loop(0, T, carry_in_step, 0, unroll=4)
        hf = tuple(hf[n] + pf[n] * cin_f[n] for n in range(REC_SLABS))
        hb = tuple(hb[n] + pb[n] * cin_b[n] for n in range(REC_SLABS))

    for n, cols in enumerate(slab_cols):
        st_ref[0, :, cols] = hf[n]
        st_ref[1, :, cols] = hb[n]

    for n, cols in enumerate(slab_cols):
        for s in range(S):
            rows = slice(s * T, (s + 1) * T)
            picked = pl.ds(s, T, stride=S)
            h_sum = bf_ref[n, picked, :] + bb_ref[n, picked, :]
            y_ref[rows, cols] = (h_sum * _gelu_tanh(g_ref[rows, cols])).astype(BF16)


def _rec(xg, n_seg, cw, cb, wg, bg, lam, h0):
    n_rows, tn = xg.shape[1], xg.shape[2]
    n_grp = n_rows // REC_GROUP_ROWS
    nc = D_RNN // REC_CB
    per_tile = tn // REC_CB

    def branch_spec(first):
        return pl.BlockSpec((None, REC_GROUP_ROWS, REC_CB),
                            lambda s, c: ((first + c) // per_tile, s, (first + c) % per_tile))

    vec = lambda rows: pl.BlockSpec((rows, REC_CB), lambda s, c: (0, c))
    state_spec = pl.BlockSpec((2, REC_STREAMS, REC_CB), lambda s, c: (0, s, c))
    slab_scratch = lambda n_t: pltpu.VMEM((REC_SLABS, n_t * REC_STREAMS, RG_BLOCK), F32)
    return pl.pallas_call(
        functools.partial(_rec_kernel, n_seg=n_seg),
        grid=(n_grp, nc),
        in_specs=[
            branch_spec(0),
            branch_spec(nc),
            vec(CONV_W),
            vec(1),
            pl.BlockSpec((REC_SLABS, RG_BLOCK, 4 * RG_BLOCK), lambda s, c: (c, 0, 0)),
            vec(4),
            vec(2),
            state_spec,
        ],
        out_specs=[
            pl.BlockSpec((None, REC_GROUP_ROWS, REC_CB), lambda s, c: (c, s, 0)),
            state_spec,
        ],
        out_shape=[
            jax.ShapeDtypeStruct((nc, n_rows, REC_CB), BF16),
            jax.ShapeDtypeStruct((2, n_grp * REC_STREAMS, D_RNN), F32),
        ],
        scratch_shapes=[slab_scratch(REC_PRE + REC_T + REC_POST)] + [slab_scratch(REC_T)] * 4,
        compiler_params=_params("arbitrary", "arbitrary"),
        name="rec",
    )(xg, xg, cw, cb.reshape(1, D_RNN), wg, bg, lam, h0)


def kernel(x_prompt, x_sample, c, cache_a_k, cache_a_v, cache_b_k, cache_b_v, state_rg_fwd, state_rg_bwd, c_ctx, w_ada, b_ada, g_pre_mix, g_post_mix, g_pre_ffn, g_post_ffn, w_att_in, w_att_out, sink_b, rpb_a, w_rec_in, conv_w, conv_b, w_rg_a, b_rg_a, w_rg_x, b_rg_x, rg_lambda, w_rec_out, w_ff1, w_ff2):
    depth = w_ada.shape[0]
    yp = x_prompt.reshape(N_PROMPT, D_MODEL)
    ys = x_sample.reshape(N_SAMPLE, D_MODEL)

    cond8 = jnp.concatenate([c_ctx[None, :], c, jnp.zeros((MOD_ROWS - N_SEG, D_MODEL), F32)], axis=0)
    mod_all = _adaln(cond8, w_ada, b_ada)
    mod_all = mod_all[:, :N_SEG, :].reshape(depth, N_SEG, N_MOD, D_MODEL)
    mod_all = jnp.pad(mod_all, ((0, 0), (0, 0), (0, MOD_ROWS - N_MOD), (0, 0)))

    a_k, a_v, b_k, b_v, s_f, s_b = [], [], [], [], [], []
    for layer in range(depth):
        mod = mod_all[layer]
        li = layer // 2
        g_pre, g_post = g_pre_mix[layer], g_post_mix[layer]
        if layer % 2 == 0:
            qkv_p, ka, va, kb, vb = _proj_att_prompt(yp, g_pre, mod, w_att_in, li)
            qkv_s = _proj_att_sample(ys, g_pre, mod, w_att_in, li)
            a_k.append(ka.reshape(BATCH, SEQ, N_HEADS_A, HEAD_DIM))
            a_v.append(va.reshape(BATCH, SEQ, N_HEADS_A, HEAD_DIM))
            b_k.append(kb.reshape(BATCH, SEQ, N_KV_B, HEAD_DIM))
            b_v.append(vb.reshape(BATCH, SEQ, N_KV_B, HEAD_DIM))
            mix_p = [_attn_ctx(sink_b[li], qkv_p)]
            n_att = cache_a_k.shape[1]
            mix_s = [_attn_na(qkv_s, rpb_a[li],
                              cache_a_k.reshape(DEC_BATCH, n_att, PAST_LEN, QA_W),
                              cache_a_v.reshape(DEC_BATCH, n_att, PAST_LEN, QA_W), li),
                     _attn_win(sink_b[li], qkv_s,
                               cache_b_k.reshape(DEC_BATCH, n_att, PAST_LEN, KVB_W),
                               cache_b_v.reshape(DEC_BATCH, n_att, PAST_LEN, KVB_W), li)]
            w_out = w_att_out
        else:
            xg_p = _proj(PROMPT, yp, g_pre, mod, w_rec_in, li)
            xg_s = _proj(SAMPLE, ys, g_pre, mod, w_rec_in, li)
            wg = 0.5 * jnp.concatenate([w_rg_a[li, 0], w_rg_x[li, 0], w_rg_a[li, 1], w_rg_x[li, 1]], axis=-1)
            bg = 0.5 * jnp.stack([b_rg_a[li, 0], b_rg_x[li, 0], b_rg_a[li, 1], b_rg_x[li, 1]], axis=0)
            rec_args = (conv_w[li], conv_b[li], wg, bg, rg_lambda[li])
            n_seg = DEC_SEQ // REC_T
            assert SEQ == REC_T and BATCH % REC_STREAMS == 0 and DEC_BATCH * n_seg == REC_STREAMS
            h0_p = jnp.zeros((2, BATCH, D_RNN), F32)
            seg_state = jnp.zeros((DEC_BATCH, n_seg, D_RNN), F32)
            h0_s = jnp.stack([seg_state.at[:, 0].set(state_rg_fwd[:, li]).reshape(REC_STREAMS, D_RNN),
                              seg_state.at[:, n_seg - 1].set(state_rg_bwd[:, li]).reshape(REC_STREAMS, D_RNN)])
            rec_p, st_p = _rec(xg_p, 1, *rec_args, h0_p)
            rec_s, _ = _rec(xg_s, n_seg, *rec_args, h0_s)
            mix_p, mix_s = rec_p, rec_s
            s_f.append(st_p[0])
            s_b.append(st_p[1])
            w_out = w_rec_out
        yp = _mixout(PROMPT, mix_p, w_out, li, yp, g_post, mod)
        ys = _mixout(SAMPLE, mix_s, w_out, li, ys, g_post, mod)
        yp = _ffn(PROMPT, yp, g_pre_ffn[layer], g_post_ffn[layer], mod, w_ff1, w_ff2, layer)
        ys = _ffn(SAMPLE, ys, g_pre_ffn[layer], g_post_ffn[layer], mod, w_ff1, w_ff2, layer)

    return (yp.reshape(BATCH, SEQ, D_MODEL), ys.reshape(DEC_BATCH, DEC_SEQ, D_MODEL),
            jnp.stack(a_k, axis=1), jnp.stack(a_v, axis=1), jnp.stack(b_k, axis=1), jnp.stack(b_v, axis=1),
            jnp.stack(s_f, axis=1), jnp.stack(s_b, axis=1))
```

```python
import functools

import jax
import jax.numpy as jnp
import numpy as np
from jax import lax
from jax.experimental import pallas as pl
from jax.experimental.pallas import tpu as pltpu

D_MODEL = 2048
BATCH = 16
SEQ = 256
DEC_BATCH = 2
DEC_SEQ = 1024
PAST_LEN = 256
GRID_W = 64
GRID_R = DEC_SEQ // GRID_W
HEAD_DIM = 128
N_HEADS_A = 8
N_HEADS_B = 8
N_KV_B = 2
G_B = N_HEADS_B // N_KV_B
NA_ROWS = 8
NA_COLS = 16
WIN_B = 128
D_RNN = D_MODEL
N_RG_BLOCKS = 16
RG_BLOCK = D_RNN // N_RG_BLOCKS
CONV_W = 4
CONV_PAD_L = 2
RG_C = 8.0
D_FF = 4 * D_MODEL
ROPE_BASE = 10000.0
EPS = 1e-6
NEG = -1e30
QA_W = N_HEADS_A * HEAD_DIM
QB_W = N_HEADS_B * HEAD_DIM
KVB_W = N_KV_B * HEAD_DIM
D_ATT_IN = 3 * QA_W + QB_W + 2 * KVB_W
SCALE = HEAD_DIM ** -0.5

N_PROMPT = BATCH * SEQ
N_SAMPLE = DEC_BATCH * DEC_SEQ
N_SEG = 1 + DEC_BATCH
N_MOD = 6
MOD_ROWS = 8

COL_QA, COL_KA, COL_VA = 0, QA_W, 2 * QA_W
COL_QB = 3 * QA_W
COL_KB = COL_QB + QB_W
COL_VB = COL_KB + KVB_W

V7X_VMEM_BYTES = 64 * 1024 * 1024
VMEM_LIMIT = V7X_VMEM_BYTES - 4 * 1024 * 1024

TM = 1024
TN = 512
LANES = 128
ROW_CHUNK = 16
ROW_GROUP = 16

F32 = jnp.float32
BF16 = jnp.bfloat16


def _params(*sem):
    return pltpu.CompilerParams(dimension_semantics=sem, vmem_limit_bytes=VMEM_LIMIT)


class _Rows:
    def __init__(self, n_rows, seg0, seg_rows):
        self.n = n_rows
        self.seg0 = seg0
        self.seg_rows = seg_rows

    def seg(self, i, tm):
        return self.seg0 + (i * tm) // self.seg_rows


PROMPT = _Rows(N_PROMPT, 0, N_PROMPT)
SAMPLE = _Rows(N_SAMPLE, 1, DEC_SEQ)


def _mod_spec(rows, tm):
    return pl.BlockSpec((None, MOD_ROWS, D_MODEL), lambda i, j: (rows.seg(i, tm), 0, 0))


def _vec_spec():
    return pl.BlockSpec((1, D_MODEL), lambda i, j: (0, 0))


def _rms_scale(x):
    return lax.rsqrt(jnp.mean(x * x, axis=-1, keepdims=True) + EPS)


def _norm_mod_rows(y_ref, g_ref, mod_ref, h_ref, shift_row, tm, h_row0=0):
    shift = mod_ref[shift_row:shift_row + 1, :]
    gain = g_ref[...] * (1.0 + mod_ref[shift_row + 1:shift_row + 2, :])

    def body(r, carry):
        starts = [r * (ROW_GROUP * ROW_CHUNK) + u * ROW_CHUNK for u in range(ROW_GROUP)]
        scales = [_rms_scale(y_ref[pl.ds(pl.multiple_of(r0, ROW_CHUNK), ROW_CHUNK), :]) for r0 in starts]
        for r0, rs in zip(starts, scales):
            y = y_ref[pl.ds(pl.multiple_of(r0, ROW_CHUNK), ROW_CHUNK), :]
            h = (y * rs) * gain + shift
            h_ref[pl.ds(pl.multiple_of(h_row0 + r0, ROW_CHUNK), ROW_CHUNK), :] = h.astype(BF16)
        return carry

    lax.fori_loop(0, tm // (ROW_GROUP * ROW_CHUNK), body, 0)


def _store_with_square_sums(o_ref, sq_ref, cols, x, first):
    o_ref[:, cols] = x
    sq = x * x
    folded = sq[:, :LANES]
    for k in range(1, sq.shape[1] // LANES):
        folded = folded + sq[:, k * LANES:(k + 1) * LANES]
    if first:
        sq_ref[...] = folded
    else:
        sq_ref[...] += folded


def _gated_residual_rows(y_ref, o_ref, sq_ref, g_ref, gate, tm):
    gain = gate * g_ref[...]
    scale = lax.rsqrt(jnp.sum(sq_ref[...], axis=-1, keepdims=True) / D_MODEL + EPS)
    sq_ref[...] = jnp.broadcast_to(scale, sq_ref.shape)

    def body(r, carry):
        for u in range(ROW_GROUP):
            rows = pl.ds(pl.multiple_of(r * (ROW_GROUP * ROW_CHUNK) + u * ROW_CHUNK, ROW_CHUNK), ROW_CHUNK)
            rs = sq_ref[rows, :]
            for k in range(D_MODEL // LANES):
                c = slice(k * LANES, (k + 1) * LANES)
                o_ref[rows, c] = y_ref[rows, c] + (o_ref[rows, c] * rs) * gain[:, c]
        return carry

    lax.fori_loop(0, tm // (ROW_GROUP * ROW_CHUNK), body, 0)


def _adaln_kernel(cond_ref, w_ref, b_ref, o_ref):
    c = cond_ref[...]
    s = c / (1.0 + jnp.exp(-c))
    o_ref[...] = jnp.dot(s.astype(BF16), w_ref[...].astype(BF16),
                         preferred_element_type=F32) + b_ref[...]


def _adaln(cond8, w_ada, b_ada):
    depth = w_ada.shape[0]
    n = w_ada.shape[2]
    tn = 1024
    return pl.pallas_call(
        _adaln_kernel,
        grid=(depth, n // tn),
        in_specs=[
            pl.BlockSpec((MOD_ROWS, D_MODEL), lambda l, j: (0, 0)),
            pl.BlockSpec((None, D_MODEL, tn), lambda l, j: (l, 0, j)),
            pl.BlockSpec((None, 1, tn), lambda l, j: (l, 0, j)),
        ],
        out_specs=pl.BlockSpec((None, MOD_ROWS, tn), lambda l, j: (l, 0, j)),
        out_shape=jax.ShapeDtypeStruct((depth, MOD_ROWS, n), F32),
        compiler_params=_params("arbitrary", "arbitrary"),
        name="adaln",
    )(cond8, w_ada, b_ada.reshape(depth, 1, n))


def _rope_tables():
    t = np.arange(DEC_SEQ)
    half = HEAD_DIM // 2
    inv = ROPE_BASE ** (-np.arange(0, half, 2, dtype=np.float64) / half)
    ang_r = (t // GRID_W)[:, None] * inv[None, :]
    ang_c = (t % GRID_W)[:, None] * inv[None, :]
    cos = np.concatenate([np.cos(ang_r), np.cos(ang_r), np.cos(ang_c), np.cos(ang_c)], axis=1)
    sin = np.concatenate([-np.sin(ang_r), np.sin(ang_r), -np.sin(ang_c), np.sin(ang_c)], axis=1)
    return jnp.asarray(cos, F32), jnp.asarray(sin, F32)


def _rope(x, cos, sin_signed):
    quarter = HEAD_DIM // 4
    lane = lax.broadcasted_iota(jnp.int32, x.shape, 1)
    first = (lane % (2 * quarter)) < quarter
    partner = jnp.where(first, pltpu.roll(x, HEAD_DIM - quarter, 1), pltpu.roll(x, quarter, 1))
    return x * cos + partner * sin_signed


def _proj_h_rows(y_ref, g_ref, mod_ref, h_ref, tm):
    row0 = pl.multiple_of(pl.program_id(1) * tm, tm)

    @pl.when(pl.program_id(0) == 0)
    def _():
        _norm_mod_rows(y_ref, g_ref, mod_ref, h_ref, 0, tm, row0)

    return pl.ds(row0, tm)


def _proj_in_specs(rows, tm, layer, tn=TN):
    last = rows.n // tm - 1

    def tile(j, i):
        return jnp.where(j == 0, i, last)

    return [
        pl.BlockSpec((tm, D_MODEL), lambda j, i: (tile(j, i), 0)),
        pl.BlockSpec((1, D_MODEL), lambda j, i: (0, 0)),
        pl.BlockSpec((None, MOD_ROWS, D_MODEL), lambda j, i: (rows.seg(tile(j, i), tm), 0, 0)),
        pl.BlockSpec((None, D_MODEL, tn), lambda j, i: (layer, 0, j)),
    ]


def _proj_tn(rows):
    return TN if rows.n > N_SAMPLE else 2 * TN


def _proj_kernel(y_ref, g_ref, mod_ref, w_hbm, o_ref, h_ref, w_buf, sem, *, tm, tn, layer, n_col):
    j = pl.program_id(0)
    i = pl.program_id(1)

    def w_copy(col, slot):
        start = col * tn
        cols = pl.ds(start if isinstance(start, int) else pl.multiple_of(start, tn), tn)
        return pltpu.make_async_copy(w_hbm.at[layer, :, cols], w_buf.at[slot], sem.at[slot])

    @pl.when((j == 0) & (i == 0))
    def _():
        w_copy(0, 0).start()

    rows = _proj_h_rows(y_ref, g_ref, mod_ref, h_ref, tm)
    for slot in range(2):
        @pl.when(j % 2 == slot)
        def _(slot=slot):
            @pl.when(i == 0)
            def _():
                w_copy(j, slot).wait()

                @pl.when(j + 1 < n_col)
                def _():
                    w_copy(j + 1, 1 - slot).start()

            o_ref[...] = jnp.dot(h_ref[rows, :], w_buf[slot].astype(BF16), preferred_element_type=F32)


def _proj(rows, y, g, mod, w, layer):
    n = w.shape[2]
    tm, tn = TM, _proj_tn(rows)
    return pl.pallas_call(
        functools.partial(_proj_kernel, tm=tm, tn=tn, layer=layer, n_col=n // tn),
        grid=(n // tn, rows.n // tm),
        in_specs=_proj_in_specs(rows, tm, layer, tn)[:3] + [pl.BlockSpec(memory_space=pl.ANY)],
        out_specs=pl.BlockSpec((None, tm, tn), lambda j, i: (j, i, 0)),
        out_shape=jax.ShapeDtypeStruct((n // tn, rows.n, tn), F32),
        scratch_shapes=[
            pltpu.VMEM((rows.n, D_MODEL), BF16),
            pltpu.VMEM((2, D_MODEL, tn), F32),
            pltpu.SemaphoreType.DMA((2,)),
        ],
        compiler_params=_params("arbitrary", "arbitrary"),
        name="proj",
    )(y, g.reshape(1, D_MODEL), mod, w)


ATT_TILE_KA = COL_KA // TN
ATT_TILE_VA = COL_VA // TN
ATT_TILE_QB = COL_QB // TN
ATT_TILE_KVB = COL_KB // TN
N_ATT_TILES = D_ATT_IN // TN
ATT_CHUNK = KVB_W
assert TN == 2 * ATT_CHUNK and ATT_CHUNK % HEAD_DIM == 0

LOG2E = float(np.log2(np.e))
Q_FACTOR = SCALE * LOG2E


def _q_factor(j):
    is_q = (j < ATT_TILE_KA) | ((j >= ATT_TILE_QB) & (j < ATT_TILE_KVB))
    return jnp.where(is_q, Q_FACTOR, 1.0).astype(F32)


def _proj_att_prompt_kernel(y_ref, g_ref, mod_ref, w_ref, o_ref, ka_ref, va_ref, kb_ref, vb_ref, h_ref, *, tm):
    j = pl.program_id(0)
    rows = _proj_h_rows(y_ref, g_ref, mod_ref, h_ref, tm)

    def tile(factor, f32_out):
        for c in range(TN // ATT_CHUNK):
            cols = slice(c * ATT_CHUNK, (c + 1) * ATT_CHUNK)
            acc = jnp.dot(h_ref[rows, :], w_ref[:, cols].astype(BF16), preferred_element_type=F32)
            o_ref[:, cols] = (acc if factor is None else acc * factor).astype(BF16)
            if f32_out is not None:
                f32_out(c, cols, acc)

    def store_ka(c, cols, acc):
        ka_ref[:, cols] = acc

    def store_va(c, cols, acc):
        va_ref[:, cols] = acc

    def store_kvb(c, cols, acc):
        (kb_ref, vb_ref)[c][...] = acc

    @pl.when((j < ATT_TILE_KA) | ((j >= ATT_TILE_QB) & (j < ATT_TILE_KVB)))
    def _():
        tile(Q_FACTOR, None)

    @pl.when((j >= ATT_TILE_KA) & (j < ATT_TILE_VA))
    def _():
        tile(None, store_ka)

    @pl.when((j >= ATT_TILE_VA) & (j < ATT_TILE_QB))
    def _():
        tile(None, store_va)

    @pl.when(j == ATT_TILE_KVB)
    def _():
        tile(None, store_kvb)


def _proj_att_prompt(y, g, mod, w, layer):
    tm = TM
    rows = PROMPT
    last = rows.n // tm - 1

    def kv_map(first, count):
        def index(j, i):
            row = jnp.where(j < first, 0, jnp.where(j >= first + count, last, i))
            return row, jnp.clip(j - first, 0, count - 1)
        return index

    per_head_set = QA_W // TN
    return pl.pallas_call(
        functools.partial(_proj_att_prompt_kernel, tm=tm),
        grid=(N_ATT_TILES, rows.n // tm),
        in_specs=_proj_in_specs(rows, tm, layer),
        out_specs=[
            pl.BlockSpec((None, tm, TN), lambda j, i: (j, i, 0)),
            pl.BlockSpec((tm, TN), kv_map(ATT_TILE_KA, per_head_set)),
            pl.BlockSpec((tm, TN), kv_map(ATT_TILE_VA, per_head_set)),
            pl.BlockSpec((tm, KVB_W), kv_map(ATT_TILE_KVB, 1)),
            pl.BlockSpec((tm, KVB_W), kv_map(ATT_TILE_KVB, 1)),
        ],
        out_shape=[
            jax.ShapeDtypeStruct((N_ATT_TILES, rows.n, TN), BF16),
            jax.ShapeDtypeStruct((rows.n, QA_W), F32),
            jax.ShapeDtypeStruct((rows.n, QA_W), F32),
            jax.ShapeDtypeStruct((rows.n, KVB_W), F32),
            jax.ShapeDtypeStruct((rows.n, KVB_W), F32),
        ],
        scratch_shapes=[pltpu.VMEM((rows.n, D_MODEL), BF16)],
        compiler_params=_params("arbitrary", "arbitrary"),
        name="proj_att_prompt",
    )(y, g.reshape(1, D_MODEL), mod, w)


def _proj_att_sample_kernel(y_ref, g_ref, mod_ref, w_ref, cos_ref, sin_ref, o_ref, h_ref, *, tm):
    j = pl.program_id(0)
    rows = _proj_h_rows(y_ref, g_ref, mod_ref, h_ref, tm)
    q_factor = _q_factor(j)

    def tile(n_rope_heads):
        for c in range(TN // ATT_CHUNK):
            acc = jnp.dot(h_ref[rows, :], w_ref[:, c * ATT_CHUNK:(c + 1) * ATT_CHUNK].astype(BF16),
                          preferred_element_type=F32)
            for hd in range(ATT_CHUNK // HEAD_DIM):
                head = c * (ATT_CHUNK // HEAD_DIM) + hd
                x = acc[:, hd * HEAD_DIM:(hd + 1) * HEAD_DIM]
                if head < n_rope_heads:
                    x = _rope(x, cos_ref[...], sin_ref[...])
                o_ref[:, head * HEAD_DIM:(head + 1) * HEAD_DIM] = (x * q_factor).astype(BF16)

    @pl.when(j < ATT_TILE_QB)
    def _():
        tile(0)

    @pl.when((j >= ATT_TILE_QB) & (j < ATT_TILE_KVB))
    def _():
        tile(TN // HEAD_DIM)

    @pl.when(j == ATT_TILE_KVB)
    def _():
        tile(N_KV_B)


def _proj_att_sample(y, g, mod, w, layer):
    tm = TM
    assert tm == DEC_SEQ
    rows = SAMPLE
    cos, sin = _rope_tables()
    tab_spec = pl.BlockSpec((DEC_SEQ, HEAD_DIM), lambda j, i: (0, 0))
    return pl.pallas_call(
        functools.partial(_proj_att_sample_kernel, tm=tm),
        grid=(N_ATT_TILES, rows.n // tm),
        in_specs=_proj_in_specs(rows, tm, layer) + [tab_spec, tab_spec],
        out_specs=pl.BlockSpec((None, tm, TN), lambda j, i: (j, i, 0)),
        out_shape=jax.ShapeDtypeStruct((N_ATT_TILES, rows.n, TN), BF16),
        scratch_shapes=[pltpu.VMEM((rows.n, D_MODEL), BF16)],
        compiler_params=_params("arbitrary", "arbitrary"),
        name="proj_att_sample",
    )(y, g.reshape(1, D_MODEL), mod, w, cos, sin)


MIXOUT_TM = 512


def _mixout_kernel(*refs, tm, n_parts):
    a_refs = refs[:n_parts]
    w_ref, y_ref, g_ref, mod_ref, o_ref, sq_ref = refs[n_parts:]
    kp = a_refs[0].shape[1]
    for c in range(D_MODEL // TN):
        cols = slice(c * TN, (c + 1) * TN)
        acc = None
        for p in range(n_parts):
            part = jnp.dot(a_refs[p][...], w_ref[p * kp:(p + 1) * kp, cols].astype(BF16),
                           preferred_element_type=F32)
            acc = part if acc is None else acc + part
        _store_with_square_sums(o_ref, sq_ref, cols, acc, first=(c == 0))
    _gated_residual_rows(y_ref, o_ref, sq_ref, g_ref, mod_ref[2:3, :], tm)


def _mixout(rows, a_parts, w, layer, y, g, mod):
    tm = MIXOUT_TM
    if isinstance(a_parts, (list, tuple)):
        n_parts, kp = len(a_parts), a_parts[0].shape[1]
        part_specs = [pl.BlockSpec((tm, kp), lambda i: (i, 0)) for _ in range(n_parts)]
    else:
        n_parts, kp = a_parts.shape[0], a_parts.shape[2]
        part_specs = [pl.BlockSpec((None, tm, kp), lambda i, p=p: (p, i, 0)) for p in range(n_parts)]
        a_parts = [a_parts] * n_parts
    assert n_parts * kp == w.shape[1]
    return pl.pallas_call(
        functools.partial(_mixout_kernel, tm=tm, n_parts=n_parts),
        grid=(rows.n // tm,),
        in_specs=part_specs + [
            pl.BlockSpec((None, w.shape[1], D_MODEL), lambda i: (layer, 0, 0), pipeline_mode=pl.Buffered(1)),
            pl.BlockSpec((tm, D_MODEL), lambda i: (i, 0)),
            pl.BlockSpec((1, D_MODEL), lambda i: (0, 0)),
            pl.BlockSpec((None, MOD_ROWS, D_MODEL), lambda i: (rows.seg(i, tm), 0, 0)),
        ],
        out_specs=pl.BlockSpec((tm, D_MODEL), lambda i: (i, 0)),
        out_shape=jax.ShapeDtypeStruct((rows.n, D_MODEL), F32),
        scratch_shapes=[pltpu.VMEM((tm, LANES), F32)],
        compiler_params=_params("arbitrary"),
        name="mixout",
    )(*a_parts, w, y, g.reshape(1, D_MODEL), mod)


FFN_TF = 256
FFN_PAIR = 2 * FFN_TF
FFN_N_PAIRS = D_FF // FFN_PAIR
FFN_GROUP = 2
FFN_N_GROUPS = FFN_N_PAIRS // FFN_GROUP


def _ffn_kernel(y_ref, g1_ref, g2_ref, mod_ref, w1_hbm, w2_hbm, o_ref, h_ref, a_ref, sq_ref, w1_buf, w2_buf,
                sem, *, tm, layer, n_tiles):
    i = pl.program_id(0)
    g = pl.program_id(1)
    last_tile = n_tiles - 1

    def block(pair):
        start = pair * FFN_PAIR
        return pl.ds(start if isinstance(start, int) else pl.multiple_of(start, FFN_PAIR), FFN_PAIR)

    def w1_copy(pair, slot):
        return pltpu.make_async_copy(w1_hbm.at[layer, :, block(pair)], w1_buf.at[slot], sem.at[0, slot])

    def w2_copy(pair, slot):
        return pltpu.make_async_copy(w2_hbm.at[layer, block(pair), :], w2_buf.at[slot], sem.at[1, slot])

    def up(slot):
        for half in range(FFN_PAIR // FFN_TF):
            cols = slice(half * FFN_TF, (half + 1) * FFN_TF)
            a = jnp.dot(h_ref[...], w1_buf[slot, :, cols].astype(BF16), preferred_element_type=F32)
            a = jnp.maximum(a, 0.0)
            a_ref[slot, :, cols] = (a * a).astype(BF16)

    def down(a_slot, w_slot, first=False, last=False):
        a = a_ref[a_slot]
        for c in range(D_MODEL // TN):
            cols = slice(c * TN, (c + 1) * TN)
            part = jnp.dot(a, w2_buf[w_slot, :, cols].astype(BF16), preferred_element_type=F32)
            if first:
                o_ref[:, cols] = part
            elif last:
                _store_with_square_sums(o_ref, sq_ref, cols, o_ref[:, cols] + part, first=(c == 0))
            else:
                o_ref[:, cols] += part

    def pair_step(q, u):
        slot = u % 2
        w1_copy(q, slot).wait()
        w1_copy((q + 1) % FFN_N_PAIRS, 1 - slot).start()
        w2_copy(q, 1 - slot).start()
        return slot

    @pl.when(g == 0)
    def _():
        @pl.when(i == 0)
        def _():
            w1_copy(0, 0).start()

        _norm_mod_rows(y_ref, g1_ref, mod_ref, h_ref, 3, tm)
        for u in range(FFN_GROUP):
            slot = pair_step(u, u)
            if u > 0:
                w2_copy(u - 1, slot).wait()
            up(slot)
            if u > 0:
                down(1 - slot, slot, first=(u == 1))

    @pl.when(g > 0)
    def _():
        for u in range(FFN_GROUP):
            q = g * FFN_GROUP + u
            slot = pair_step(q, u)
            w2_copy(q - 1, slot).wait()
            up(slot)
            down(1 - slot, slot)

    @pl.when(g == FFN_N_GROUPS - 1)
    def _():
        w2_copy(FFN_N_PAIRS - 1, 0).wait()
        down((FFN_N_PAIRS - 1) % 2, 0, last=True)
        _gated_residual_rows(y_ref, o_ref, sq_ref, g2_ref, mod_ref[5:6, :], tm)

        @pl.when(i == last_tile)
        def _():
            w1_copy(0, 0).wait()


def _ffn(rows, y, g1, g2, mod, w1, w2, layer):
    tm = TM
    n_tiles = rows.n // tm
    assert FFN_GROUP % 2 == 0 and FFN_N_PAIRS % FFN_GROUP == 0
    return pl.pallas_call(
        functools.partial(_ffn_kernel, tm=tm, layer=layer, n_tiles=n_tiles),
        grid=(n_tiles, FFN_N_GROUPS),
        in_specs=[
            pl.BlockSpec((tm, D_MODEL), lambda i, s: (i, 0)),
            _vec_spec(),
            _vec_spec(),
            _mod_spec(rows, tm),
            pl.BlockSpec(memory_space=pl.ANY),
            pl.BlockSpec(memory_space=pl.ANY),
        ],
        out_specs=pl.BlockSpec((tm, D_MODEL), lambda i, s: (i, 0)),
        out_shape=jax.ShapeDtypeStruct((rows.n, D_MODEL), F32),
        scratch_shapes=[
            pltpu.VMEM((tm, D_MODEL), BF16),
            pltpu.VMEM((2, tm, FFN_PAIR), BF16),
            pltpu.VMEM((tm, LANES), F32),
            pltpu.VMEM((2, D_MODEL, FFN_PAIR), F32),
            pltpu.VMEM((2, FFN_PAIR, D_MODEL), F32),
            pltpu.SemaphoreType.DMA((2, 2)),
        ],
        compiler_params=_params("arbitrary", "arbitrary"),
        name="ffn",
    )(y, g1.reshape(1, D_MODEL), g2.reshape(1, D_MODEL), mod, w1, w2)


def _qkt(q, k):
    return lax.dot_general(q, k, (((1,), (1,)), ((), ())), preferred_element_type=F32)


def _with_ones(v):
    return jnp.concatenate([v, jnp.ones_like(v)], axis=1)


def _pv(p, v_ext):
    o = jnp.dot(p, v_ext, preferred_element_type=F32)
    return o[:, :HEAD_DIM], o[:, HEAD_DIM:]


CTX_SEQ_PER_STEP = 2


def _attn_ctx_kernel(sink_ref, qkv_ref, o_ref):
    n_rows = G_B * SEQ
    grp = lax.broadcasted_iota(jnp.int32, (n_rows, 1), 0) // SEQ
    for b in range(CTX_SEQ_PER_STEP):
        rows = slice(b * SEQ, (b + 1) * SEQ)

        def head(col):
            tile, off = divmod(col, TN)
            return qkv_ref[tile, rows, off:off + HEAD_DIM]

        for h in range(N_HEADS_A):
            q = head(COL_QA + h * HEAD_DIM)
            k = head(COL_KA + h * HEAD_DIM)
            v = head(COL_VA + h * HEAD_DIM)
            s = _qkt(q, k)
            m = jnp.max(s, axis=-1, keepdims=True)
            num, den = _pv(jnp.exp2(s - m).astype(BF16), _with_ones(v))
            o_ref[rows, h * HEAD_DIM:(h + 1) * HEAD_DIM] = (num / den).astype(BF16)
        for j in range(N_KV_B):
            k = head(COL_KB + j * HEAD_DIM)
            v = head(COL_VB + j * HEAD_DIM)
            q = jnp.concatenate([head(COL_QB + (j * G_B + g) * HEAD_DIM) for g in range(G_B)], axis=0)
            sink = jnp.zeros((n_rows, 1), F32)
            for g in range(G_B):
                sink = jnp.where(grp == g, sink_ref[j, g] * LOG2E, sink)
            s = _qkt(q, k)
            m = jnp.maximum(jnp.max(s, axis=-1, keepdims=True), sink)
            num, den = _pv(jnp.exp2(s - m).astype(BF16), _with_ones(v))
            o = num / (den + jnp.exp2(sink - m))
            for g in range(G_B):
                c0 = QA_W + (j * G_B + g) * HEAD_DIM
                o_ref[rows, c0:c0 + HEAD_DIM] = o[g * SEQ:(g + 1) * SEQ, :].astype(BF16)


def _attn_ctx(sink, qkv):
    rows = CTX_SEQ_PER_STEP * SEQ
    return pl.pallas_call(
        _attn_ctx_kernel,
        grid=(N_PROMPT // rows,),
        in_specs=[
            pl.BlockSpec(memory_space=pltpu.SMEM),
            pl.BlockSpec((N_ATT_TILES, rows, TN), lambda b: (0, b, 0)),
        ],
        out_specs=pl.BlockSpec((rows, D_MODEL), lambda b: (b, 0)),
        out_shape=jax.ShapeDtypeStruct((N_PROMPT, D_MODEL), BF16),
        compiler_params=_params("arbitrary"),
        name="attn_ctx",
    )(sink, qkv)


NA_Q_ROWS = 4
NA_K_ROWS = 12
NA_Q_CHUNK = NA_Q_ROWS * GRID_W
NA_K_SPAN = NA_K_ROWS * GRID_W
NA_K_ROW0 = (0, 0, 4, 4)
N_RPB_ROWS = 2 * NA_ROWS - 1
N_RPB_COLS = 2 * NA_COLS - 1


def _na_row_start(r):
    return min(max(r - NA_ROWS // 2, 0), GRID_R - NA_ROWS)


for _chunk, _k0 in enumerate(NA_K_ROW0):
    for _r in range(_chunk * NA_Q_ROWS, (_chunk + 1) * NA_Q_ROWS):
        assert _k0 % 2 == 0 and _k0 <= _na_row_start(_r)
        assert _na_row_start(_r) + NA_ROWS <= _k0 + NA_K_ROWS <= GRID_R


def _na_build_bias(rpb_ref, tile_ref, bias_ref):
    shape = (GRID_W, 2 * GRID_W)
    qc = lax.broadcasted_iota(jnp.int32, shape, 0)
    lane = lax.broadcasted_iota(jnp.int32, shape, 1)
    kc = lane % GRID_W
    start_c = jnp.clip(qc - NA_COLS // 2, 0, GRID_W - NA_COLS)
    in_win = (kc >= start_c) & (kc < start_c + NA_COLS)
    for dr in range(N_RPB_ROWS):
        rows = jnp.broadcast_to(rpb_ref[dr:dr + 1, :], shape)
        shifted = pltpu.roll(rows, 2 * GRID_W - (NA_COLS - 1), 1, stride=1, stride_axis=0)
        tile_ref[dr] = jnp.where(in_win, shifted * LOG2E, NEG)
    first_half = lane < GRID_W
    neg = jnp.full(shape, NEG, F32)
    for chunk in range(GRID_R // NA_Q_ROWS):
        for qi in range(NA_Q_ROWS):
            qr = chunk * NA_Q_ROWS + qi
            lo = _na_row_start(qr)
            for m in range(NA_K_ROWS // 2):
                kr = NA_K_ROW0[chunk] + 2 * m
                parts = []
                for r in (kr, kr + 1):
                    parts.append(tile_ref[r - qr + NA_ROWS - 1] if lo <= r < lo + NA_ROWS else neg)
                bias_ref[chunk, qi * GRID_W:(qi + 1) * GRID_W, m * 2 * GRID_W:(m + 1) * 2 * GRID_W] = (
                    jnp.where(first_half, parts[0], parts[1]))


def _attn_na_kernel(rpb_ref, q_ref, k_ref, v_ref, kc_ref, vc_ref, o_ref, tile_ref, bias_ref):
    _na_build_bias(rpb_ref, tile_ref, bias_ref)
    for b in range(DEC_BATCH):
        kc = kc_ref[b].astype(BF16)
        vc = _with_ones(vc_ref[b].astype(BF16))
        for c in range(DEC_SEQ // NA_Q_CHUNK):
            r0 = b * DEC_SEQ + c * NA_Q_CHUNK
            k0 = b * DEC_SEQ + NA_K_ROW0[c] * GRID_W
            rows = slice(r0, r0 + NA_Q_CHUNK)
            keys = slice(k0, k0 + NA_K_SPAN)
            q = q_ref[rows, :]
            s = _qkt(q, k_ref[keys, :]) + bias_ref[c]
            sc = _qkt(q, kc)
            m = jnp.maximum(jnp.max(s, axis=-1, keepdims=True), jnp.max(sc, axis=-1, keepdims=True))
            num, den = _pv(jnp.exp2(s - m).astype(BF16), _with_ones(v_ref[keys, :]))
            num_c, den_c = _pv(jnp.exp2(sc - m).astype(BF16), vc)
            o_ref[rows, :] = ((num + num_c) / (den + den_c)).astype(BF16)


def _attn_na(qkv, rpb, cache_k, cache_v, layer):
    heads_per_tile = TN // HEAD_DIM

    def head_spec(c0):
        def index(h):
            head = c0 // HEAD_DIM + h
            return head // heads_per_tile, 0, head % heads_per_tile
        return pl.BlockSpec((None, N_SAMPLE, HEAD_DIM), index)

    ctx_spec = pl.BlockSpec((DEC_BATCH, None, PAST_LEN, HEAD_DIM), lambda h: (0, layer, 0, h))
    n_chunks = DEC_SEQ // NA_Q_CHUNK
    pad_rows = -N_RPB_ROWS % 8
    half = jnp.pad(rpb.astype(F32), ((0, 0), (0, pad_rows), (0, GRID_W - N_RPB_COLS)), constant_values=NEG)
    rpb_rows = jnp.concatenate([half, half], axis=-1)
    return pl.pallas_call(
        _attn_na_kernel,
        grid=(N_HEADS_A,),
        in_specs=[
            pl.BlockSpec((None, rpb_rows.shape[1], 2 * GRID_W), lambda h: (h, 0, 0)),
            head_spec(COL_QA),
            head_spec(COL_KA),
            head_spec(COL_VA),
            ctx_spec,
            ctx_spec,
        ],
        out_specs=pl.BlockSpec((N_SAMPLE, HEAD_DIM), lambda h: (0, h)),
        out_shape=jax.ShapeDtypeStruct((N_SAMPLE, QA_W), BF16),
        scratch_shapes=[pltpu.VMEM((N_RPB_ROWS, GRID_W, 2 * GRID_W), F32),
                        pltpu.VMEM((n_chunks, NA_Q_CHUNK, NA_K_SPAN), F32)],
        compiler_params=_params("arbitrary"),
        name="attn_na",
    )(rpb_rows, qkv, qkv, qkv, cache_k, cache_v)


WIN_Q_CHUNK = 128
WIN_K_SPAN = WIN_Q_CHUNK + 2 * WIN_B


def _attn_win_kernel(sink_ref, q_ref, k_ref, v_ref, kc_ref, vc_ref, o_ref):
    j = pl.program_id(1)
    kc = kc_ref[...].astype(BF16)
    vc = _with_ones(vc_ref[...].astype(BF16))
    n_rows = G_B * WIN_Q_CHUNK
    grp = lax.broadcasted_iota(jnp.int32, (n_rows, 1), 0) // WIN_Q_CHUNK
    sink = jnp.zeros((n_rows, 1), F32)
    for g in range(G_B):
        sink = jnp.where(grp == g, sink_ref[j, g] * LOG2E, sink)
    for c in range(DEC_SEQ // WIN_Q_CHUNK):
        q0 = c * WIN_Q_CHUNK
        k0 = min(max(q0 - WIN_B, 0), DEC_SEQ - WIN_K_SPAN)
        rows = slice(q0, q0 + WIN_Q_CHUNK)
        keys = slice(k0, k0 + WIN_K_SPAN)
        q = jnp.concatenate([q_ref[rows, g * HEAD_DIM:(g + 1) * HEAD_DIM] for g in range(G_B)], axis=0)
        qpos = q0 + lax.broadcasted_iota(jnp.int32, (n_rows, WIN_K_SPAN), 0) % WIN_Q_CHUNK
        kpos = k0 + lax.broadcasted_iota(jnp.int32, (n_rows, WIN_K_SPAN), 1)
        s = jnp.where(jnp.abs(qpos - kpos) <= WIN_B, _qkt(q, k_ref[keys, :]), NEG)
        sc = _qkt(q, kc)
        m = jnp.maximum(jnp.maximum(jnp.max(s, axis=-1, keepdims=True),
                                    jnp.max(sc, axis=-1, keepdims=True)), sink)
        num, den = _pv(jnp.exp2(s - m).astype(BF16), _with_ones(v_ref[keys, :]))
        num_c, den_c = _pv(jnp.exp2(sc - m).astype(BF16), vc)
        o = (num + num_c) / (den + den_c + jnp.exp2(sink - m))
        for g in range(G_B):
            o_ref[rows, g * HEAD_DIM:(g + 1) * HEAD_DIM] = (
                o[g * WIN_Q_CHUNK:(g + 1) * WIN_Q_CHUNK, :].astype(BF16))


def _attn_win(sink, qkv, cache_k, cache_v, layer):
    gw = G_B * HEAD_DIM
    ctx_spec = pl.BlockSpec((None, None, PAST_LEN, HEAD_DIM), lambda b, j: (b, layer, 0, j))
    return pl.pallas_call(
        _attn_win_kernel,
        grid=(DEC_BATCH, N_KV_B),
        in_specs=[
            pl.BlockSpec(memory_space=pltpu.SMEM),
            pl.BlockSpec((None, DEC_SEQ, gw), lambda b, j: (ATT_TILE_QB + j, b, 0)),
            pl.BlockSpec((None, DEC_SEQ, HEAD_DIM), lambda b, j: (ATT_TILE_KVB, b, j)),
            pl.BlockSpec((None, DEC_SEQ, HEAD_DIM), lambda b, j: (ATT_TILE_KVB, b, N_KV_B + j)),
            ctx_spec,
            ctx_spec,
        ],
        out_specs=pl.BlockSpec((DEC_SEQ, gw), lambda b, j: (b, j)),
        out_shape=jax.ShapeDtypeStruct((N_SAMPLE, QB_W), BF16),
        compiler_params=_params("arbitrary", "arbitrary"),
        name="attn_win",
    )(sink, qkv, qkv, qkv, cache_k, cache_v)


REC_CB = 512
REC_SLABS = REC_CB // RG_BLOCK
REC_STREAMS = 8
REC_T = 256
REC_GROUP_ROWS = REC_STREAMS * REC_T
REC_PRE = CONV_PAD_L
REC_POST = CONV_W - 1 - CONV_PAD_L
REC_CHUNK_ROWS = 32 * REC_STREAMS


def _softplus(x):
    return jnp.maximum(x, 0.0) + jnp.log1p(jnp.exp(-jnp.abs(x)))


def _gelu_tanh(x):
    k = np.sqrt(2.0 / np.pi)
    half = 0.5 * x
    return half + half * jnp.tanh(x * (k + (k * 0.044715) * (x * x)))


def _sqrt_nonneg(u):
    return jnp.where(u > 0.0, u * lax.rsqrt(u), 0.0)


def _rec_kernel(x_ref, g_ref, cw_ref, cb_ref, wg_ref, bg_ref, lam_ref, h0_ref, y_ref, st_ref,
                xt_ref, af_ref, bf_ref, ab_ref, bb_ref, *, n_seg):
    S, T = REC_STREAMS, REC_T
    a_refs = (af_ref, ab_ref)
    b_refs = (bf_ref, bb_ref)
    seg = lax.broadcasted_iota(jnp.int32, (S, RG_BLOCK), 0) % n_seg
    slab_cols = [slice(n * RG_BLOCK, (n + 1) * RG_BLOCK) for n in range(REC_SLABS)]

    def t_rows(t):
        return pl.ds(pl.multiple_of(t * S, S), S)

    def from_prev_stream(x):
        return pltpu.roll(x, 1, 0)

    def from_next_stream(x):
        return pltpu.roll(x, S - 1, 0)

    for n, cols in enumerate(slab_cols):
        for s in range(S):
            xt_ref[n, pl.ds(REC_PRE * S + s, T, stride=S), :] = x_ref[s * T:(s + 1) * T, cols]
        for p in range(REC_PRE):
            src = xt_ref[n, (T + p) * S:(T + p + 1) * S, :]
            xt_ref[n, p * S:(p + 1) * S, :] = jnp.where(seg > 0, from_prev_stream(src), 0.0)
        for p in range(REC_POST):
            src = xt_ref[n, (REC_PRE + p) * S:(REC_PRE + p + 1) * S, :]
            xt_ref[n, (REC_PRE + T + p) * S:(REC_PRE + T + p + 1) * S, :] = (
                jnp.where(seg < n_seg - 1, from_next_stream(src), 0.0))

    c_all = (-0.5 * RG_C * np.log2(np.e)) * _softplus(-lam_ref[...])

    def gate_rows(r, carry):
        r0 = pl.multiple_of(r * REC_CHUNK_ROWS, REC_CHUNK_ROWS)
        rows = pl.ds(r0, REC_CHUNK_ROWS)
        for n, cols in enumerate(slab_cols):
            xc = cb_ref[:, cols] + cw_ref[0:1, cols] * xt_ref[n, rows, :]
            for k in range(1, CONV_W):
                tap_rows = pl.ds(pl.multiple_of(r0 + k * S, S), REC_CHUNK_ROWS)
                xc = xc + cw_ref[k:k + 1, cols] * xt_ref[n, tap_rows, :]
            gates = jnp.dot(xc.astype(BF16), wg_ref[n].astype(BF16), preferred_element_type=F32)
            x_half = 0.5 * xc
            for d in range(2):
                ga_half = gates[:, (2 * d) * RG_BLOCK:(2 * d + 1) * RG_BLOCK] + bg_ref[2 * d:2 * d + 1, cols]
                gx_half = (gates[:, (2 * d + 1) * RG_BLOCK:(2 * d + 2) * RG_BLOCK]
                           + bg_ref[2 * d + 1:2 * d + 2, cols])
                c = c_all[d:d + 1, cols]
                a = jnp.exp2(c * jnp.tanh(ga_half) + c)
                a_refs[d][n, rows, :] = a
                b_refs[d][n, rows, :] = _sqrt_nonneg(1.0 - a * a) * ((1.0 + jnp.tanh(gx_half)) * x_half)
        return carry

    lax.fori_loop(0, T * S // REC_CHUNK_ROWS, gate_rows, 0)

    def scan_step(t, carry):
        hf, hb, pf, pb = carry
        rf, rb = t_rows(t), t_rows(T - 1 - t)
        hf_new, hb_new, pf_new, pb_new = [], [], [], []
        for n in range(REC_SLABS):
            a = af_ref[n, rf, :]
            h = a * hf[n] + bf_ref[n, rf, :]
            bf_ref[n, rf, :] = h
            hf_new.append(h)
            a2 = ab_ref[n, rb, :]
            h2 = a2 * hb[n] + bb_ref[n, rb, :]
            bb_ref[n, rb, :] = h2
            hb_new.append(h2)
            if n_seg > 1:
                p = a * pf[n]
                af_ref[n, rf, :] = p
                pf_new.append(p)
                p2 = a2 * pb[n]
                ab_ref[n, rb, :] = p2
                pb_new.append(p2)
        return tuple(hf_new), tuple(hb_new), tuple(pf_new), tuple(pb_new)

    ones = tuple(jnp.ones((S, RG_BLOCK), F32) for _ in range(REC_SLABS)) if n_seg > 1 else ()
    hf, hb, pf, pb = lax.fori_loop(
        0, T, scan_step,
        (tuple(h0_ref[0, :, cols] for cols in slab_cols), tuple(h0_ref[1, :, cols] for cols in slab_cols),
         ones, ones), unroll=4)

    if n_seg > 1:
        cin_f, cin_b = [], []
        for n in range(REC_SLABS):
            cf = jnp.zeros((S, RG_BLOCK), F32)
            for j in range(1, n_seg):
                cf = jnp.where(seg == j, from_prev_stream(hf[n] + pf[n] * cf), cf)
            cb_in = jnp.zeros((S, RG_BLOCK), F32)
            for j in range(n_seg - 2, -1, -1):
                cb_in = jnp.where(seg == j, from_next_stream(hb[n] + pb[n] * cb_in), cb_in)
            cin_f.append(cf)
            cin_b.append(cb_in)

        def carry_in_step(t, carry):
            rows = t_rows(t)
            for n in range(REC_SLABS):
                bf_ref[n, rows, :] += af_ref[n, rows, :] * cin_f[n]
                bb_ref[n, rows, :] += ab_ref[n, rows, :] * cin_b[n]
            return carry

        lax.fori_loop(0, T, carry_in_step, 0, unroll=4)
        hf = tuple(hf[n] + pf[n] * cin_f[n] for n in range(REC_SLABS))
        hb = tuple(hb[n] + pb[n] * cin_b[n] for n in range(REC_SLABS))

    for n, cols in enumerate(slab_cols):
        st_ref[0, :, cols] = hf[n]
        st_ref[1, :, cols] = hb[n]

    for n, cols in enumerate(slab_cols):
        for s in range(S):
            rows = slice(s * T, (s + 1) * T)
            picked = pl.ds(s, T, stride=S)
            h_sum = bf_ref[n, picked, :] + bb_ref[n, picked, :]
            y_ref[rows, cols] = (h_sum * _gelu_tanh(g_ref[rows, cols])).astype(BF16)


def _rec(xg, n_seg, cw, cb, wg, bg, lam, h0):
    n_rows, tn = xg.shape[1], xg.shape[2]
    n_grp = n_rows // REC_GROUP_ROWS
    nc = D_RNN // REC_CB
    per_tile = tn // REC_CB

    def branch_spec(first):
        return pl.BlockSpec((None, REC_GROUP_ROWS, REC_CB),
                            lambda s, c: ((first + c) // per_tile, s, (first + c) % per_tile))

    vec = lambda rows: pl.BlockSpec((rows, REC_CB), lambda s, c: (0, c))
    state_spec = pl.BlockSpec((2, REC_STREAMS, REC_CB), lambda s, c: (0, s, c))
    slab_scratch = lambda n_t: pltpu.VMEM((REC_SLABS, n_t * REC_STREAMS, RG_BLOCK), F32)
    return pl.pallas_call(
        functools.partial(_rec_kernel, n_seg=n_seg),
        grid=(n_grp, nc),
        in_specs=[
            branch_spec(0),
            branch_spec(nc),
            vec(CONV_W),
            vec(1),
            pl.BlockSpec((REC_SLABS, RG_BLOCK, 4 * RG_BLOCK), lambda s, c: (c, 0, 0)),
            vec(4),
            vec(2),
            state_spec,
        ],
        out_specs=[
            pl.BlockSpec((None, REC_GROUP_ROWS, REC_CB), lambda s, c: (c, s, 0)),
            state_spec,
        ],
        out_shape=[
            jax.ShapeDtypeStruct((nc, n_rows, REC_CB), BF16),
            jax.ShapeDtypeStruct((2, n_grp * REC_STREAMS, D_RNN), F32),
        ],
        scratch_shapes=[slab_scratch(REC_PRE + REC_T + REC_POST)] + [slab_scratch(REC_T)] * 4,
        compiler_params=_params("arbitrary", "arbitrary"),
        name="rec",
    )(xg, xg, cw, cb.reshape(1, D_RNN), wg, bg, lam, h0)


def kernel(x_prompt, x_sample, c, cache_a_k, cache_a_v, cache_b_k, cache_b_v, state_rg_fwd, state_rg_bwd, c_ctx, w_ada, b_ada, g_pre_mix, g_post_mix, g_pre_ffn, g_post_ffn, w_att_in, w_att_out, sink_b, rpb_a, w_rec_in, conv_w, conv_b, w_rg_a, b_rg_a, w_rg_x, b_rg_x, rg_lambda, w_rec_out, w_ff1, w_ff2):
    depth = w_ada.shape[0]
    yp = x_prompt.reshape(N_PROMPT, D_MODEL)
    ys = x_sample.reshape(N_SAMPLE, D_MODEL)

    cond8 = jnp.concatenate([c_ctx[None, :], c, jnp.zeros((MOD_ROWS - N_SEG, D_MODEL), F32)], axis=0)
    mod_all = _adaln(cond8, w_ada, b_ada)
    mod_all = mod_all[:, :N_SEG, :].reshape(depth, N_SEG, N_MOD, D_MODEL)
    mod_all = jnp.pad(mod_all, ((0, 0), (0, 0), (0, MOD_ROWS - N_MOD), (0, 0)))

    a_k, a_v, b_k, b_v, s_f, s_b = [], [], [], [], [], []
    for layer in range(depth):
        mod = mod_all[layer]
        li = layer // 2
        g_pre, g_post = g_pre_mix[layer], g_post_mix[layer]
        if layer % 2 == 0:
            qkv_p, ka, va, kb, vb = _proj_att_prompt(yp, g_pre, mod, w_att_in, li)
            qkv_s = _proj_att_sample(ys, g_pre, mod, w_att_in, li)
            a_k.append(ka.reshape(BATCH, SEQ, N_HEADS_A, HEAD_DIM))
            a_v.append(va.reshape(BATCH, SEQ, N_HEADS_A, HEAD_DIM))
            b_k.append(kb.reshape(BATCH, SEQ, N_KV_B, HEAD_DIM))
            b_v.append(vb.reshape(BATCH, SEQ, N_KV_B, HEAD_DIM))
            mix_p = [_attn_ctx(sink_b[li], qkv_p)]
            n_att = cache_a_k.shape[1]
            mix_s = [_attn_na(qkv_s, rpb_a[li],
                              cache_a_k.reshape(DEC_BATCH, n_att, PAST_LEN, QA_W),
                              cache_a_v.reshape(DEC_BATCH, n_att, PAST_LEN, QA_W), li),
                     _attn_win(sink_b[li], qkv_s,
                               cache_b_k.reshape(DEC_BATCH, n_att, PAST_LEN, KVB_W),
                               cache_b_v.reshape(DEC_BATCH, n_att, PAST_LEN, KVB_W), li)]
            w_out = w_att_out
        else:
            xg_p = _proj(PROMPT, yp, g_pre, mod, w_rec_in, li)
            xg_s = _proj(SAMPLE, ys, g_pre, mod, w_rec_in, li)
            wg = 0.5 * jnp.concatenate([w_rg_a[li, 0], w_rg_x[li, 0], w_rg_a[li, 1], w_rg_x[li, 1]], axis=-1)
            bg = 0.5 * jnp.stack([b_rg_a[li, 0], b_rg_x[li, 0], b_rg_a[li, 1], b_rg_x[li, 1]], axis=0)
            rec_args = (conv_w[li], conv_b[li], wg, bg, rg_lambda[li])
            n_seg = DEC_SEQ // REC_T
            assert SEQ == REC_T and BATCH % REC_STREAMS == 0 and DEC_BATCH * n_seg == REC_STREAMS
            h0_p = jnp.zeros((2, BATCH, D_RNN), F32)
            seg_state = jnp.zeros((DEC_BATCH, n_seg, D_RNN), F32)
            h0_s = jnp.stack([seg_state.at[:, 0].set(state_rg_fwd[:, li]).reshape(REC_STREAMS, D_RNN),
                              seg_state.at[:, n_seg - 1].set(state_rg_bwd[:, li]).reshape(REC_STREAMS, D_RNN)])
            rec_p, st_p = _rec(xg_p, 1, *rec_args, h0_p)
            rec_s, _ = _rec(xg_s, n_seg, *rec_args, h0_s)
            mix_p, mix_s = rec_p, rec_s
            s_f.append(st_p[0])
            s_b.append(st_p[1])
            w_out = w_rec_out
        yp = _mixout(PROMPT, mix_p, w_out, li, yp, g_post, mod)
        ys = _mixout(SAMPLE, mix_s, w_out, li, ys, g_post, mod)
        yp = _ffn(PROMPT, yp, g_pre_ffn[layer], g_post_ffn[layer], mod, w_ff1, w_ff2, layer)
        ys = _ffn(SAMPLE, ys, g_pre_ffn[layer], g_post_ffn[layer], mod, w_ff1, w_ff2, layer)

    return (yp.reshape(BATCH, SEQ, D_MODEL), ys.reshape(DEC_BATCH, DEC_SEQ, D_MODEL),
            jnp.stack(a_k, axis=1), jnp.stack(a_v, axis=1), jnp.stack(b_k, axis=1), jnp.stack(b_v, axis=1),
            jnp.stack(s_f, axis=1), jnp.stack(s_b, axis=1))
```
